```python
import jax, jax.numpy as jnp
from jax import lax
import numpy as np

D_MODEL = 1024
BATCH = 8
SEQ = 8192
DEPTH = 4

N_MIXERS = 3
EXPAND = 2
D_INNER = EXPAND * D_MODEL
CONF_KERNEL = 31
SHORT_KERNEL = 3
FOX_HEADS = 16
FOX_HEAD_DIM = D_INNER // FOX_HEADS
Q_BLOCK = 128
NORM_EPS = 1e-6
N_A = (DEPTH + 2) // 3
N_B = (DEPTH + 1) // 3
N_C = DEPTH // 3

kernel_name = "hybrid_conformer_fox_shortconv_trunk"


def rms_norm(x, g):
    xf = x.astype(jnp.float32)
    y = xf * lax.rsqrt(jnp.mean(xf * xf, axis=-1, keepdims=True) + NORM_EPS)
    return (y * g.astype(jnp.float32)).astype(x.dtype)


def layer_norm(x, g, b):
    xf = x.astype(jnp.float32)
    mu = jnp.mean(xf, axis=-1, keepdims=True)
    var = jnp.mean(jnp.square(xf - mu), axis=-1, keepdims=True)
    y = (xf - mu) * lax.rsqrt(var + NORM_EPS)
    return (y * g.astype(jnp.float32) + b.astype(jnp.float32)).astype(x.dtype)


def causal_depthwise_conv(x, w):
    k_width, channels = w.shape
    kern = w[:, None, :].astype(x.dtype)
    return lax.conv_general_dilated(
        x, kern, window_strides=(1,), padding=[(k_width - 1, 0)],
        dimension_numbers=("NWC", "WIO", "NWC"), feature_group_count=channels)


def conformer_conv_mixer(h, w_in, conv_w, conv_b, ln_g, ln_b, w_out):
    proj = h @ w_in
    val, glu_gate, z = jnp.split(proj, 3, axis=-1)
    u = val * jax.nn.sigmoid(glu_gate)
    u = causal_depthwise_conv(u, conv_w) + conv_b
    u = jax.nn.silu(layer_norm(u, ln_g, ln_b))
    return (u * jax.nn.silu(z)) @ w_out


def blocked_forgetting_attention(q, k, v, c):
    b, s_len, n_h, d_h = q.shape
    n_blocks = s_len // Q_BLOCK
    scale = d_h ** -0.5
    k_pos = jnp.arange(s_len)

    def one_block(i):
        start = i * Q_BLOCK
        q_blk = lax.dynamic_slice_in_dim(q, start, Q_BLOCK, axis=1)
        c_blk = lax.dynamic_slice_in_dim(c, start, Q_BLOCK, axis=2)
        logits = jnp.einsum("bqhd,bkhd->bhqk", q_blk, k,
                            preferred_element_type=jnp.float32) * scale
        logits = logits + c_blk[..., :, None] - c[..., None, :]
        q_pos = start + jnp.arange(Q_BLOCK)
        causal = k_pos[None, :] <= q_pos[:, None]
        logits = jnp.where(causal, logits, -jnp.inf)
        p = jax.nn.softmax(logits, axis=-1)
        return jnp.einsum("bhqk,bkhd->bqhd", p.astype(v.dtype), v)

    out = lax.map(one_block, jnp.arange(n_blocks))
    return out.transpose(1, 0, 2, 3, 4).reshape(b, s_len, n_h, d_h)


def forgetting_attention_mixer(h, w_in, f_bias, q_norm_g, k_norm_g, w_out):
    b, s_len, _ = h.shape
    proj = h @ w_in
    q, k, v, z, f_logit = jnp.split(
        proj, [D_INNER, 2 * D_INNER, 3 * D_INNER, 4 * D_INNER], axis=-1)
    q = rms_norm(q.reshape(b, s_len, FOX_HEADS, FOX_HEAD_DIM), q_norm_g)
    k = rms_norm(k.reshape(b, s_len, FOX_HEADS, FOX_HEAD_DIM), k_norm_g)
    v = v.reshape(b, s_len, FOX_HEADS, FOX_HEAD_DIM)
    log_f = jax.nn.log_sigmoid((f_logit + f_bias).astype(jnp.float32))
    c = jnp.cumsum(log_f, axis=1).transpose(0, 2, 1)
    o = blocked_forgetting_attention(q, k, v, c).reshape(b, s_len, D_INNER)
    return (o * jax.nn.silu(z)) @ w_out


def short_conv_mixer(h, w_in, conv_w, w_out):
    proj = h @ w_in
    u, b_gate, c_gate, z = jnp.split(proj, 4, axis=-1)
    y = b_gate * causal_depthwise_conv(c_gate * u, conv_w)
    return (y * jax.nn.silu(z)) @ w_out


def _fwd_setup_inputs(seed: int = 0) -> dict:
    key = jax.random.key(seed)
    ks = jax.random.split(key, 20)
    f32 = jnp.float32
    nrm = lambda k, shape, s: jax.random.normal(k, shape, f32) * s
    d, e, h = D_MODEL, D_INNER, FOX_HEADS
    return {
        "x": jax.random.normal(ks[0], (BATCH, SEQ, d), f32),
        "a_norm": 1.0 + nrm(ks[1], (N_A, d), 0.05),
        "a_w_in": nrm(ks[2], (N_A, d, 3 * e), d ** -0.5),
        "a_conv_w": nrm(ks[3], (N_A, CONF_KERNEL, e), CONF_KERNEL ** -0.5),
        "a_conv_b": nrm(ks[4], (N_A, e), 0.01),
        "a_ln_g": 1.0 + nrm(ks[5], (N_A, e), 0.05),
        "a_ln_b": nrm(ks[6], (N_A, e), 0.01),
        "a_w_out": nrm(ks[7], (N_A, e, d), e ** -0.5),
        "b_norm": 1.0 + nrm(ks[8], (N_B, d), 0.05),
        "b_w_in": nrm(ks[9], (N_B, d, 4 * e + h), d ** -0.5),
        "b_f_bias": 2.0 + nrm(ks[10], (N_B, h), 0.5),
        "b_q_norm": 1.0 + nrm(ks[11], (N_B, FOX_HEAD_DIM), 0.05),
        "b_k_norm": 1.0 + nrm(ks[12], (N_B, FOX_HEAD_DIM), 0.05),
        "b_w_out": nrm(ks[13], (N_B, e, d), e ** -0.5),
        "c_norm": 1.0 + nrm(ks[14], (N_C, d), 0.05),
        "c_w_in": nrm(ks[15], (N_C, d, 4 * e), d ** -0.5),
        "c_conv_w": nrm(ks[16], (N_C, SHORT_KERNEL, e), SHORT_KERNEL ** -0.5),
        "c_w_out": nrm(ks[17], (N_C, e, d), e ** -0.5),
    }


def _fwd_reference(x, a_norm, a_w_in, a_conv_w, a_conv_b, a_ln_g, a_ln_b, a_w_out,
              b_norm, b_w_in, b_f_bias, b_q_norm, b_k_norm, b_w_out,
              c_norm, c_w_in, c_conv_w, c_w_out):
    for i in range(DEPTH):
        kind, j = i % N_MIXERS, i // N_MIXERS
        if kind == 0:
            hn = rms_norm(x, a_norm[j])
            x = x + conformer_conv_mixer(hn, a_w_in[j], a_conv_w[j], a_conv_b[j],
                                         a_ln_g[j], a_ln_b[j], a_w_out[j])
        elif kind == 1:
            hn = rms_norm(x, b_norm[j])
            x = x + forgetting_attention_mixer(hn, b_w_in[j], b_f_bias[j], b_q_norm[j],
                                               b_k_norm[j], b_w_out[j])
        else:
            hn = rms_norm(x, c_norm[j])
            x = x + short_conv_mixer(hn, c_w_in[j], c_conv_w[j], c_w_out[j])
    return x


import jax as _jax
import jax.numpy as _jnp

TWIN_FORMAT = 'train_step'
FWD_PARAMS = ['x', 'a_norm', 'a_w_in', 'a_conv_w', 'a_conv_b', 'a_ln_g', 'a_ln_b', 'a_w_out', 'b_norm', 'b_w_in', 'b_f_bias', 'b_q_norm', 'b_k_norm', 'b_w_out', 'c_norm', 'c_w_in', 'c_conv_w', 'c_w_out']
TWIN_WEIGHTS = ['a_norm', 'a_w_in', 'a_conv_w', 'a_conv_b', 'a_ln_g', 'a_ln_b', 'a_w_out', 'b_norm', 'b_w_in', 'b_f_bias', 'b_q_norm', 'b_k_norm', 'b_w_out', 'c_norm', 'c_w_in', 'c_conv_w', 'c_w_out']
TWIN_DIFF_INPUT = 'x'
TWIN_INPUTS = ['x', 'a_norm', 'a_w_in', 'a_conv_w', 'a_conv_b', 'a_ln_g', 'a_ln_b', 'a_w_out', 'b_norm', 'b_w_in', 'b_f_bias', 'b_q_norm', 'b_k_norm', 'b_w_out', 'c_norm', 'c_w_in', 'c_conv_w', 'c_w_out', 'loss_target', 'm_a_norm', 'm_a_w_in', 'm_a_conv_w', 'm_a_conv_b', 'm_a_ln_g', 'm_a_ln_b', 'm_a_w_out', 'm_b_norm', 'm_b_w_in', 'm_b_f_bias', 'm_b_q_norm', 'm_b_k_norm', 'm_b_w_out', 'm_c_norm', 'm_c_w_in', 'm_c_conv_w', 'm_c_w_out', 'v_a_norm', 'v_a_w_in', 'v_a_conv_w', 'v_a_conv_b', 'v_a_ln_g', 'v_a_ln_b', 'v_a_w_out', 'v_b_norm', 'v_b_w_in', 'v_b_f_bias', 'v_b_q_norm', 'v_b_k_norm', 'v_b_w_out', 'v_c_norm', 'v_c_w_in', 'v_c_conv_w', 'v_c_w_out']
TWIN_OUTPUTS = ['loss', 'grad_x', 'grad_a_norm', 'grad_a_w_in', 'grad_a_conv_w', 'grad_a_conv_b', 'grad_a_ln_g', 'grad_a_ln_b', 'grad_a_w_out', 'grad_b_norm', 'grad_b_w_in', 'grad_b_f_bias', 'grad_b_q_norm', 'grad_b_k_norm', 'grad_b_w_out', 'grad_c_norm', 'grad_c_w_in', 'grad_c_conv_w', 'grad_c_w_out', 'delta_a_norm', 'delta_a_w_in', 'delta_a_conv_w', 'delta_a_conv_b', 'delta_a_ln_g', 'delta_a_ln_b', 'delta_a_w_out', 'delta_b_norm', 'delta_b_w_in', 'delta_b_f_bias', 'delta_b_q_norm', 'delta_b_k_norm', 'delta_b_w_out', 'delta_c_norm', 'delta_c_w_in', 'delta_c_conv_w', 'delta_c_w_out', 'new_m_a_norm', 'new_m_a_w_in', 'new_m_a_conv_w', 'new_m_a_conv_b', 'new_m_a_ln_g', 'new_m_a_ln_b', 'new_m_a_w_out', 'new_m_b_norm', 'new_m_b_w_in', 'new_m_b_f_bias', 'new_m_b_q_norm', 'new_m_b_k_norm', 'new_m_b_w_out', 'new_m_c_norm', 'new_m_c_w_in', 'new_m_c_conv_w', 'new_m_c_w_out', 'new_v_a_norm', 'new_v_a_w_in', 'new_v_a_conv_w', 'new_v_a_conv_b', 'new_v_a_ln_g', 'new_v_a_ln_b', 'new_v_a_w_out', 'new_v_b_norm', 'new_v_b_w_in', 'new_v_b_f_bias', 'new_v_b_q_norm', 'new_v_b_k_norm', 'new_v_b_w_out', 'new_v_c_norm', 'new_v_c_w_in', 'new_v_c_conv_w', 'new_v_c_w_out']
TWIN_LEAF_KINDS = {'loss': 'loss', 'grad_x': 'grad_x', 'grad_a_norm': 'grad_w', 'grad_a_w_in': 'grad_w', 'grad_a_conv_w': 'grad_w', 'grad_a_conv_b': 'grad_w', 'grad_a_ln_g': 'grad_w', 'grad_a_ln_b': 'grad_w', 'grad_a_w_out': 'grad_w', 'grad_b_norm': 'grad_w', 'grad_b_w_in': 'grad_w', 'grad_b_f_bias': 'grad_w', 'grad_b_q_norm': 'grad_w', 'grad_b_k_norm': 'grad_w', 'grad_b_w_out': 'grad_w', 'grad_c_norm': 'grad_w', 'grad_c_w_in': 'grad_w', 'grad_c_conv_w': 'grad_w', 'grad_c_w_out': 'grad_w', 'delta_a_norm': 'delta_w', 'delta_a_w_in': 'delta_w', 'delta_a_conv_w': 'delta_w', 'delta_a_conv_b': 'delta_w', 'delta_a_ln_g': 'delta_w', 'delta_a_ln_b': 'delta_w', 'delta_a_w_out': 'delta_w', 'delta_b_norm': 'delta_w', 'delta_b_w_in': 'delta_w', 'delta_b_f_bias': 'delta_w', 'delta_b_q_norm': 'delta_w', 'delta_b_k_norm': 'delta_w', 'delta_b_w_out': 'delta_w', 'delta_c_norm': 'delta_w', 'delta_c_w_in': 'delta_w', 'delta_c_conv_w': 'delta_w', 'delta_c_w_out': 'delta_w', 'new_m_a_norm': 'new_m', 'new_m_a_w_in': 'new_m', 'new_m_a_conv_w': 'new_m', 'new_m_a_conv_b': 'new_m', 'new_m_a_ln_g': 'new_m', 'new_m_a_ln_b': 'new_m', 'new_m_a_w_out': 'new_m', 'new_m_b_norm': 'new_m', 'new_m_b_w_in': 'new_m', 'new_m_b_f_bias': 'new_m', 'new_m_b_q_norm': 'new_m', 'new_m_b_k_norm': 'new_m', 'new_m_b_w_out': 'new_m', 'new_m_c_norm': 'new_m', 'new_m_c_w_in': 'new_m', 'new_m_c_conv_w': 'new_m', 'new_m_c_w_out': 'new_m', 'new_v_a_norm': 'new_v', 'new_v_a_w_in': 'new_v', 'new_v_a_conv_w': 'new_v', 'new_v_a_conv_b': 'new_v', 'new_v_a_ln_g': 'new_v', 'new_v_a_ln_b': 'new_v', 'new_v_a_w_out': 'new_v', 'new_v_b_norm': 'new_v', 'new_v_b_w_in': 'new_v', 'new_v_b_f_bias': 'new_v', 'new_v_b_q_norm': 'new_v', 'new_v_b_k_norm': 'new_v', 'new_v_b_w_out': 'new_v', 'new_v_c_norm': 'new_v', 'new_v_c_w_in': 'new_v', 'new_v_c_conv_w': 'new_v', 'new_v_c_w_out': 'new_v'}


def _forward(args):
    return _fwd_reference(*[args[k] for k in FWD_PARAMS])


def _output_shape():
    def fwd():
        inp = _fwd_setup_inputs(0)
        return _fwd_reference(*[inp[k] for k in FWD_PARAMS])
    out = _jax.eval_shape(fwd)
    return out.shape, out.dtype

N_MICROBATCH = 1
ADAM_LR = 0.001
ADAM_B1 = 0.9
ADAM_B2 = 0.999
ADAM_EPS = 1e-08
ADAM_WD = 0.01
ADAM_STEP = 10
PER_EXAMPLE_BATCH_AXIS = {'x': 0, 'loss_target': 0}
SHARED_INPUTS = []
_WEIGHT_DTYPES = {'a_norm': _jnp.float32, 'a_w_in': _jnp.float32, 'a_conv_w': _jnp.float32, 'a_conv_b': _jnp.float32, 'a_ln_g': _jnp.float32, 'a_ln_b': _jnp.float32, 'a_w_out': _jnp.float32, 'b_norm': _jnp.float32, 'b_w_in': _jnp.float32, 'b_f_bias': _jnp.float32, 'b_q_norm': _jnp.float32, 'b_k_norm': _jnp.float32, 'b_w_out': _jnp.float32, 'c_norm': _jnp.float32, 'c_w_in': _jnp.float32, 'c_conv_w': _jnp.float32, 'c_w_out': _jnp.float32}
MOMENT_SCALE = {'a_norm': 9.219879e+00, 'a_w_in': 2.467820e-01, 'a_conv_w': 2.925738e-01, 'a_conv_b': 1.689840e+00, 'a_ln_g': 4.799927e+00, 'a_ln_b': 3.152102e+00, 'a_w_out': 5.286092e-01, 'b_norm': 8.178744e+00, 'b_w_in': 2.202613e-01, 'b_f_bias': 7.345372e+01, 'b_q_norm': 1.099340e+01, 'b_k_norm': 1.096686e+01, 'b_w_out': 3.204875e-01, 'c_norm': 9.679900e+01, 'c_w_in': 5.469896e-01, 'c_conv_w': 6.418379e+00, 'c_w_out': 4.866691e-01}


def _to_microbatches(a, axis):
    t = _jnp.moveaxis(a, axis, 0)
    t = t.reshape((N_MICROBATCH, t.shape[0] // N_MICROBATCH) + t.shape[1:])
    return _jnp.moveaxis(t, 1, axis + 1)


def setup_inputs(seed: int = 0) -> dict:
    inp = _fwd_setup_inputs(seed)
    key = _jax.random.fold_in(_jax.random.key(seed), 7919)
    shape, _ = _output_shape()
    out = dict(inp)
    out["loss_target"] = _jax.random.normal(_jax.random.fold_in(key, 0), shape, _jnp.float32)
    for i, name in enumerate(TWIN_WEIGHTS):
        w = inp[name].astype(_jnp.float32)
        if MOMENT_SCALE is None:
            s = _jnp.sqrt(_jnp.mean(_jnp.square(w)) + 1e-30)
        else:
            s = MOMENT_SCALE[name]
        km, kv = _jax.random.split(_jax.random.fold_in(key, i + 1))
        out[name] = w
        out["m_" + name] = s * _jax.random.normal(km, w.shape, _jnp.float32)
        out["v_" + name] = (s * s) * _jax.random.uniform(kv, w.shape, _jnp.float32, 0.5, 1.5)
    if N_MICROBATCH > 1:
        for name, axis in PER_EXAMPLE_BATCH_AXIS.items():
            out[name] = _to_microbatches(out[name], axis)
    return {'x': out['x'], 'a_norm': out['a_norm'], 'a_w_in': out['a_w_in'], 'a_conv_w': out['a_conv_w'], 'a_conv_b': out['a_conv_b'], 'a_ln_g': out['a_ln_g'], 'a_ln_b': out['a_ln_b'], 'a_w_out': out['a_w_out'], 'b_norm': out['b_norm'], 'b_w_in': out['b_w_in'], 'b_f_bias': out['b_f_bias'], 'b_q_norm': out['b_q_norm'], 'b_k_norm': out['b_k_norm'], 'b_w_out': out['b_w_out'], 'c_norm': out['c_norm'], 'c_w_in': out['c_w_in'], 'c_conv_w': out['c_conv_w'], 'c_w_out': out['c_w_out'], 'loss_target': out['loss_target'], 'm_a_norm': out['m_a_norm'], 'm_a_w_in': out['m_a_w_in'], 'm_a_conv_w': out['m_a_conv_w'], 'm_a_conv_b': out['m_a_conv_b'], 'm_a_ln_g': out['m_a_ln_g'], 'm_a_ln_b': out['m_a_ln_b'], 'm_a_w_out': out['m_a_w_out'], 'm_b_norm': out['m_b_norm'], 'm_b_w_in': out['m_b_w_in'], 'm_b_f_bias': out['m_b_f_bias'], 'm_b_q_norm': out['m_b_q_norm'], 'm_b_k_norm': out['m_b_k_norm'], 'm_b_w_out': out['m_b_w_out'], 'm_c_norm': out['m_c_norm'], 'm_c_w_in': out['m_c_w_in'], 'm_c_conv_w': out['m_c_conv_w'], 'm_c_w_out': out['m_c_w_out'], 'v_a_norm': out['v_a_norm'], 'v_a_w_in': out['v_a_w_in'], 'v_a_conv_w': out['v_a_conv_w'], 'v_a_conv_b': out['v_a_conv_b'], 'v_a_ln_g': out['v_a_ln_g'], 'v_a_ln_b': out['v_a_ln_b'], 'v_a_w_out': out['v_a_w_out'], 'v_b_norm': out['v_b_norm'], 'v_b_w_in': out['v_b_w_in'], 'v_b_f_bias': out['v_b_f_bias'], 'v_b_q_norm': out['v_b_q_norm'], 'v_b_k_norm': out['v_b_k_norm'], 'v_b_w_out': out['v_b_w_out'], 'v_c_norm': out['v_c_norm'], 'v_c_w_in': out['v_c_w_in'], 'v_c_conv_w': out['v_c_conv_w'], 'v_c_w_out': out['v_c_w_out']}


def _loss(weights, diff, rest, loss_target):
    with _jax.named_scope("forward"):
        args = {**rest, TWIN_DIFF_INPUT: diff, **{k: w.astype(_WEIGHT_DTYPES[k]) for k, w in weights.items()}}
        y = _forward(args)
    with _jax.named_scope("loss_head"):
        err = _jnp.square(y.astype(_jnp.float32) - loss_target)
        return 0.5 * _jnp.sum(_jnp.mean(err, axis=-1)) if err.ndim else 0.5 * err


def _adamw(w, g, m, v):
    m = ADAM_B1 * m + (1.0 - ADAM_B1) * g
    v = ADAM_B2 * v + (1.0 - ADAM_B2) * _jnp.square(g)
    m_hat = m / (1.0 - ADAM_B1 ** ADAM_STEP)
    v_hat = v / (1.0 - ADAM_B2 ** ADAM_STEP)
    delta = -ADAM_LR * (m_hat / (_jnp.sqrt(v_hat) + ADAM_EPS) + ADAM_WD * w)
    return delta, m, v


def reference(x, a_norm, a_w_in, a_conv_w, a_conv_b, a_ln_g, a_ln_b, a_w_out, b_norm, b_w_in, b_f_bias, b_q_norm, b_k_norm, b_w_out, c_norm, c_w_in, c_conv_w, c_w_out, loss_target, m_a_norm, m_a_w_in, m_a_conv_w, m_a_conv_b, m_a_ln_g, m_a_ln_b, m_a_w_out, m_b_norm, m_b_w_in, m_b_f_bias, m_b_q_norm, m_b_k_norm, m_b_w_out, m_c_norm, m_c_w_in, m_c_conv_w, m_c_w_out, v_a_norm, v_a_w_in, v_a_conv_w, v_a_conv_b, v_a_ln_g, v_a_ln_b, v_a_w_out, v_b_norm, v_b_w_in, v_b_f_bias, v_b_q_norm, v_b_k_norm, v_b_w_out, v_c_norm, v_c_w_in, v_c_conv_w, v_c_w_out):
    given = dict(x=x, a_norm=a_norm, a_w_in=a_w_in, a_conv_w=a_conv_w, a_conv_b=a_conv_b, a_ln_g=a_ln_g, a_ln_b=a_ln_b, a_w_out=a_w_out, b_norm=b_norm, b_w_in=b_w_in, b_f_bias=b_f_bias, b_q_norm=b_q_norm, b_k_norm=b_k_norm, b_w_out=b_w_out, c_norm=c_norm, c_w_in=c_w_in, c_conv_w=c_conv_w, c_w_out=c_w_out, loss_target=loss_target, m_a_norm=m_a_norm, m_a_w_in=m_a_w_in, m_a_conv_w=m_a_conv_w, m_a_conv_b=m_a_conv_b, m_a_ln_g=m_a_ln_g, m_a_ln_b=m_a_ln_b, m_a_w_out=m_a_w_out, m_b_norm=m_b_norm, m_b_w_in=m_b_w_in, m_b_f_bias=m_b_f_bias, m_b_q_norm=m_b_q_norm, m_b_k_norm=m_b_k_norm, m_b_w_out=m_b_w_out, m_c_norm=m_c_norm, m_c_w_in=m_c_w_in, m_c_conv_w=m_c_conv_w, m_c_w_out=m_c_w_out, v_a_norm=v_a_norm, v_a_w_in=v_a_w_in, v_a_conv_w=v_a_conv_w, v_a_conv_b=v_a_conv_b, v_a_ln_g=v_a_ln_g, v_a_ln_b=v_a_ln_b, v_a_w_out=v_a_w_out, v_b_norm=v_b_norm, v_b_w_in=v_b_w_in, v_b_f_bias=v_b_f_bias, v_b_q_norm=v_b_q_norm, v_b_k_norm=v_b_k_norm, v_b_w_out=v_b_w_out, v_c_norm=v_c_norm, v_c_w_in=v_c_w_in, v_c_conv_w=v_c_conv_w, v_c_w_out=v_c_w_out)
    weights = {n: given[n] for n in TWIN_WEIGHTS}
    shared = {n: given[n] for n in SHARED_INPUTS}
    per_example = {n: given[n] for n in ['x']}
    grad_fn = _jax.value_and_grad(_loss, argnums=(0, 1))

    def one_microbatch(ex, loss_target):
        ex = dict(ex)
        diff = ex.pop(TWIN_DIFF_INPUT)
        return grad_fn(weights, diff, {**shared, **ex}, loss_target)

    if N_MICROBATCH == 1:
        loss, (grad_w, grad_x) = one_microbatch(per_example, given["loss_target"])
    else:
        def body(carry, xs):
            loss_sum, grad_sum = carry
            l_k, (gw_k, gx_k) = one_microbatch(xs[0], xs[1])
            with _jax.named_scope("update"):
                return (loss_sum + l_k, _jax.tree.map(_jnp.add, grad_sum, gw_k)), gx_k

        init = (_jnp.zeros((), _jnp.float32), _jax.tree.map(_jnp.zeros_like, weights))
        (loss, grad_w), grad_x = _jax.lax.scan(body, init, (per_example, given["loss_target"]))
    with _jax.named_scope("update"):
        delta_w, new_m, new_v = {}, {}, {}
        for n in TWIN_WEIGHTS:
            delta_w[n], new_m[n], new_v[n] = _adamw(weights[n], grad_w[n], given["m_" + n], given["v_" + n])
    return (loss, grad_x, *[grad_w[n] for n in TWIN_WEIGHTS], *[delta_w[n] for n in TWIN_WEIGHTS],
            *[new_m[n] for n in TWIN_WEIGHTS], *[new_v[n] for n in TWIN_WEIGHTS])
```

```python
import jax
import jax.numpy as jnp
import numpy as np
from jax import lax
from jax.experimental import pallas as pl
from jax.experimental.pallas import tpu as pltpu

F32 = jnp.float32
BF16 = jnp.bfloat16
SDS = jax.ShapeDtypeStruct

NORM_EPS = 1e-6
ADAM_LR = 0.001
ADAM_B1 = 0.9
ADAM_B2 = 0.999
ADAM_EPS = 1e-08
ADAM_WD = 0.01
ADAM_STEP = 10

N_DEV = 8
LANES = 128
SUBLANES = 8
HEAD_DIM = 128
CONF_HALO = 32
SHORT_HALO = 8
VMEM_LIMIT = 56 * 1024 * 1024

NT_DIMS = (((1,), (1,)), ((), ()))
TN_DIMS = (((0,), (0,)), ((), ()))
MESH = pl.DeviceIdType.MESH
ANY = pl.BlockSpec(memory_space=pl.ANY)


def _cparams(*sem):
    return pltpu.CompilerParams(dimension_semantics=sem, vmem_limit_bytes=VMEM_LIMIT)


def _divisor_tile(n, cap):
    return max(m for m in range(LANES, min(n, cap) + 1, LANES) if n % m == 0)


def _sigmoid(x):
    return 1.0 / (1.0 + jnp.exp(-x))


def _silu(x):
    return x * _sigmoid(x)


def _dsilu(x):
    s = _sigmoid(x)
    return s * (1.0 + x * (1.0 - s))


def _rows8(v):
    out = v[0:SUBLANES]
    for a in range(1, v.shape[0] // SUBLANES):
        out = out + v[a * SUBLANES:(a + 1) * SUBLANES]
    return out


def _split3(v):
    hi = v.astype(BF16)
    r1 = v - hi.astype(F32)
    mid = r1.astype(BF16)
    lo = (r1 - mid.astype(F32)).astype(BF16)
    return hi, mid, lo


def _tri_matmul(tri, v):
    hi, mid, lo = _split3(v)
    return (jnp.dot(tri, hi, preferred_element_type=F32) + jnp.dot(tri, mid, preferred_element_type=F32)
            + jnp.dot(tri, lo, preferred_element_type=F32))


def _position():
    return lax.axis_index("x"), lax.axis_index("y"), lax.axis_index("c")


def _all_gather(shards, name):
    n = len(shards)

    def body(*refs):
        xs, outs = refs[:n], refs[n:2 * n]
        send_sems, recv_sems, local_sems = refs[2 * n:]
        x, y, c = _position()
        me, sibling = (x, y, c), (x, y, 1 - c)
        chips = [(1 - x, y), (x, 1 - y), (1 - x, 1 - y)]

        def slot(a, px, py, pc):
            return outs[a].at[4 * px + 2 * py + pc]

        def copy(a, k, block, to, src=None):
            return pltpu.make_async_remote_copy(
                src_ref=slot(a, *block) if src is None else src, dst_ref=slot(a, *block),
                send_sem=send_sems.at[a, k], recv_sem=recv_sems.at[a, k], device_id=to, device_id_type=MESH)

        started = []
        mine = []
        for a in range(n):
            cp = pltpu.make_async_copy(xs[a], slot(a, *me), local_sems.at[a])
            cp.start()
            mine.append(cp)
        for a in range(n):
            first = [copy(a, 0, me, sibling, src=xs[a])]
            first += [copy(a, 1 + j, me, (*chip, c), src=xs[a]) for j, chip in enumerate(chips)]
            for cp in first:
                cp.start()
            started += first
        for a in range(n):
            for j, chip in enumerate(chips):
                copy(a, 1 + j, (*chip, c), me).wait_recv()
                fwd = copy(a, 4 + j, (*chip, c), sibling)
                fwd.start()
                started.append(fwd)
        for a in range(n):
            copy(a, 0, sibling, me).wait_recv()
            for j, chip in enumerate(chips):
                copy(a, 4 + j, (*chip, 1 - c), me).wait_recv()
        for cp in started:
            cp.wait_send()
        for cp in mine:
            cp.wait()

    return pl.pallas_call(
        body, name=name,
        out_shape=[SDS((N_DEV,) + s.shape, s.dtype) for s in shards],
        in_specs=[ANY] * n, out_specs=[ANY] * n,
        scratch_shapes=[pltpu.SemaphoreType.DMA((n, 7)), pltpu.SemaphoreType.DMA((n, 7)), pltpu.SemaphoreType.DMA((n,))],
    )(*shards)


def _exchange(slabs, name):
    n = len(slabs)

    def body(*refs):
        ins, outs = refs[:n], refs[n:2 * n]
        send_sems, recv_sems, local_sems = refs[2 * n:]
        x, y, c = _position()
        me = 4 * x + 2 * y + c
        peers = [(x ^ bx, y ^ by, c ^ bc) for bx in (0, 1) for by in (0, 1) for bc in (0, 1)][1:]

        def copy(a, k, peer):
            pid = 4 * peer[0] + 2 * peer[1] + peer[2]
            return pltpu.make_async_remote_copy(
                src_ref=ins[a].at[pid], dst_ref=outs[a].at[me],
                send_sem=send_sems.at[a, k], recv_sem=recv_sems.at[a, k], device_id=peer, device_id_type=MESH)

        def arrival(a, k, peer):
            pid = 4 * peer[0] + 2 * peer[1] + peer[2]
            return pltpu.make_async_remote_copy(
                src_ref=ins[a].at[pid], dst_ref=outs[a].at[pid],
                send_sem=send_sems.at[a, k], recv_sem=recv_sems.at[a, k], device_id=peer, device_id_type=MESH)

        mine = []
        for a in range(n):
            cp = pltpu.make_async_copy(ins[a].at[me], outs[a].at[me], local_sems.at[a])
            cp.start()
            mine.append(cp)
        started = []
        for a in range(n):
            for k, peer in enumerate(peers):
                cp = copy(a, k, peer)
                cp.start()
                started.append(cp)
        for a in range(n):
            for k, peer in enumerate(peers):
                arrival(a, k, peer).wait_recv()
        for cp in started:
            cp.wait_send()
        for cp in mine:
            cp.wait()

    return pl.pallas_call(
        body, name=name,
        out_shape=[SDS(s.shape, s.dtype) for s in slabs],
        in_specs=[ANY] * n, out_specs=[ANY] * n,
        scratch_shapes=[pltpu.SemaphoreType.DMA((n, 7)), pltpu.SemaphoreType.DMA((n, 7)), pltpu.SemaphoreType.DMA((n,))],
    )(*slabs)


def _norm_matmul(x, g, w, name, tm=512, tn=1024):
    t, d = x.shape
    n = w.shape[1]
    tn = min(tn, n)

    def body(x_ref, g_ref, w_ref, hn_ref, o_ref):
        @pl.when(pl.program_id(1) == 0)
        def _():
            xf = x_ref[...]
            r = lax.rsqrt(jnp.mean(xf * xf, axis=-1, keepdims=True) + NORM_EPS)
            hn_ref[...] = ((xf * r) * g_ref[...]).astype(BF16)

        o_ref[...] = jnp.dot(hn_ref[...], w_ref[...], preferred_element_type=F32).astype(o_ref.dtype)

    return pl.pallas_call(
        body, name=name, grid=(t // tm, n // tn),
        in_specs=[pl.BlockSpec((tm, d), lambda i, j: (i, 0)), pl.BlockSpec((1, d), lambda i, j: (0, 0)),
                  pl.BlockSpec((d, tn), lambda i, j: (0, j))],
        out_specs=[pl.BlockSpec((tm, d), lambda i, j: (i, 0)), pl.BlockSpec((tm, tn), lambda i, j: (i, j))],
        out_shape=[SDS((t, d), BF16), SDS((t, n), BF16)],
        compiler_params=_cparams("parallel", "arbitrary"),
    )(x, g, w)


def _matmul_f32out(a, w, name, tm=512):
    t, k = a.shape
    n = w.shape[1]

    def body(a_ref, w_ref, o_ref):
        o_ref[...] = jnp.dot(a_ref[...], w_ref[...], preferred_element_type=F32)

    return pl.pallas_call(
        body, name=name, grid=(t // tm,),
        in_specs=[pl.BlockSpec((tm, k), lambda i: (i, 0)), pl.BlockSpec((k, n), lambda i: (0, 0))],
        out_specs=pl.BlockSpec((tm, n), lambda i: (i, 0)),
        out_shape=SDS((t, n), F32),
        compiler_params=_cparams("parallel"),
    )(a, w)


def _out_matmul_residual(y, w, x, name, tm=512):
    t, e = y.shape
    d = w.shape[1]

    def body(y_ref, w_ref, x_ref, o_ref):
        o_ref[...] = x_ref[...] + jnp.dot(y_ref[...], w_ref[...], preferred_element_type=F32)

    return pl.pallas_call(
        body, name=name, grid=(t // tm,),
        in_specs=[pl.BlockSpec((tm, e), lambda i: (i, 0)), pl.BlockSpec((e, d), lambda i: (0, 0)),
                  pl.BlockSpec((tm, d), lambda i: (i, 0))],
        out_specs=pl.BlockSpec((tm, d), lambda i: (i, 0)),
        out_shape=SDS((t, d), F32),
        compiler_params=_cparams("parallel"),
    )(y, w, x)


def _matmul_nt(a, w, name, tm=512):
    t, d = a.shape
    e = w.shape[0]

    def body(a_ref, w_ref, o_ref):
        o_ref[...] = lax.dot_general(a_ref[...].astype(BF16), w_ref[...], NT_DIMS,
                                     preferred_element_type=F32).astype(o_ref.dtype)

    return pl.pallas_call(
        body, name=name, grid=(t // tm,),
        in_specs=[pl.BlockSpec((tm, d), lambda i: (i, 0)), pl.BlockSpec((e, d), lambda i: (0, 0))],
        out_specs=pl.BlockSpec((tm, e), lambda i: (i, 0)),
        out_shape=SDS((t, e), BF16),
        compiler_params=_cparams("parallel"),
    )(a, w)


def _matmul_tn(a, b, name, out_dtype=BF16, tm=1024, tn=1024, tk=512):
    t, m = a.shape
    n = b.shape[1]
    tm, tn = min(tm, m), _divisor_tile(n, 2 * tn)
    nk = t // tk

    def body(a_ref, b_ref, o_ref, acc_ref):
        k = pl.program_id(2)

        @pl.when(k == 0)
        def _():
            acc_ref[...] = jnp.zeros_like(acc_ref)

        acc_ref[...] += lax.dot_general(a_ref[...].astype(BF16), b_ref[...].astype(BF16), TN_DIMS,
                                        preferred_element_type=F32)

        @pl.when(k == nk - 1)
        def _():
            o_ref[...] = acc_ref[...].astype(o_ref.dtype)

    return pl.pallas_call(
        body, name=name, grid=(m // tm, n // tn, nk),
        in_specs=[pl.BlockSpec((tk, tm), lambda i, j, k: (k, i)), pl.BlockSpec((tk, tn), lambda i, j, k: (k, j))],
        out_specs=pl.BlockSpec((tm, tn), lambda i, j, k: (i, j)),
        out_shape=SDS((m, n), out_dtype),
        scratch_shapes=[pltpu.VMEM((tm, tn), F32)],
        compiler_params=_cparams("parallel", "parallel", "arbitrary"),
    )(a, b)


def _dproj_matmul_normbwd(dproj, w, x, g, dxn, name, tm=512, tk=1024):
    t, n = dproj.shape
    d = w.shape[0]
    tk = _divisor_tile(n, 2 * tk)
    nk = n // tk

    def body(dp_ref, w_ref, x_ref, g_ref, dxn_ref, dx_ref, dg_ref, acc_ref):
        i, k = pl.program_id(0), pl.program_id(1)

        @pl.when(k == 0)
        def _():
            acc_ref[...] = jnp.zeros_like(acc_ref)

        acc_ref[...] += lax.dot_general(dp_ref[...], w_ref[...], NT_DIMS, preferred_element_type=F32)

        @pl.when(k == nk - 1)
        def _():
            dhn = acc_ref[...]
            xf = x_ref[...]
            r = lax.rsqrt(jnp.mean(xf * xf, axis=-1, keepdims=True) + NORM_EPS)
            xh = xf * r
            dy = dhn * g_ref[...]
            dx_ref[...] = dxn_ref[...] + r * (dy - xh * jnp.mean(dy * xh, axis=-1, keepdims=True))
            part = jnp.sum(dhn * xh, axis=0, keepdims=True)

            @pl.when(i == 0)
            def _():
                dg_ref[...] = part

            @pl.when(i > 0)
            def _():
                dg_ref[...] += part

    return pl.pallas_call(
        body, name=name, grid=(t // tm, nk),
        in_specs=[pl.BlockSpec((tm, tk), lambda i, k: (i, k)), pl.BlockSpec((d, tk), lambda i, k: (0, k)),
                  pl.BlockSpec((tm, d), lambda i, k: (i, 0)), pl.BlockSpec((1, d), lambda i, k: (0, 0)),
                  pl.BlockSpec((tm, d), lambda i, k: (i, 0))],
        out_specs=[pl.BlockSpec((tm, d), lambda i, k: (i, 0)), pl.BlockSpec((1, d), lambda i, k: (0, 0))],
        out_shape=[SDS((t, d), F32), SDS((1, d), F32)],
        scratch_shapes=[pltpu.VMEM((tm, d), F32)],
        compiler_params=_cparams("arbitrary", "arbitrary"),
    )(dproj, w, x, g, dxn)


def _loss_grad(y, target, name, tm=512):
    t, d = y.shape
    inv_d = 1.0 / d

    def body(y_ref, t_ref, part_ref, dy_ref):
        i = pl.program_id(0)
        err = y_ref[...] - t_ref[...]
        dy_ref[...] = err * inv_d
        part = jnp.sum(err * err, axis=0, keepdims=True) * (0.5 * inv_d)

        @pl.when(i == 0)
        def _():
            part_ref[...] = part

        @pl.when(i > 0)
        def _():
            part_ref[...] += part

    return pl.pallas_call(
        body, name=name, grid=(t // tm,),
        in_specs=[pl.BlockSpec((tm, d), lambda i: (i, 0)), pl.BlockSpec((tm, d), lambda i: (i, 0))],
        out_specs=[pl.BlockSpec((1, d), lambda i: (0, 0)), pl.BlockSpec((tm, d), lambda i: (i, 0))],
        out_shape=[SDS((1, d), F32), SDS((t, d), F32)],
        compiler_params=_cparams("arbitrary"),
    )(y, target)


CONV_ROWS = 32
CONV_COLS = 512


def _conv_chunk(src_ref, base, w_ref, width, r0, c0, flip):
    acc = None
    for k in range(width):
        off = base + r0 + ((width - 1 - k) if flip else (k - (width - 1)))
        term = src_ref[pl.ds(off, CONV_ROWS), pl.ds(c0, CONV_COLS)] * w_ref[pl.ds(k, 1), pl.ds(c0, CONV_COLS)]
        acc = term if acc is None else acc + term
    return acc


def _conv_weight_grad(dw_ref, d_ref, src_ref, base, width, tt, e):
    for c0 in range(0, e, CONV_COLS):
        for k in range(width):
            acc = None
            for r0 in range(0, tt, CONV_ROWS):
                prod = (d_ref[pl.ds(r0, CONV_ROWS), pl.ds(c0, CONV_COLS)]
                        * src_ref[pl.ds(base + r0 - (width - 1) + k, CONV_ROWS), pl.ds(c0, CONV_COLS)])
                part = _rows8(prod)
                acc = part if acc is None else acc + part
            dw_ref[pl.ds(k, 1), pl.ds(c0, CONV_COLS)] += jnp.sum(acc, axis=0, keepdims=True)


LN_ROWS = 16


def _a_mid_fwd(proj, cw, cb, lg, lb, width, name, tt=256):
    t, e3 = proj.shape
    e = e3 // 3
    halo = CONF_HALO

    def body(p_ref, cw_ref, cb_ref, lg_ref, lb_ref, y_ref, u1_ref, ubuf):
        i = pl.program_id(0)

        @pl.when(i == 0)
        def _():
            ubuf[pl.ds(0, halo), :] = jnp.zeros((halo, e), F32)

        @pl.when(i > 0)
        def _():
            ubuf[pl.ds(0, halo), :] = ubuf[pl.ds(tt, halo), :]

        for r0 in range(0, tt, CONV_ROWS):
            val = p_ref[pl.ds(r0, CONV_ROWS), pl.ds(0, e)].astype(F32)
            gate = p_ref[pl.ds(r0, CONV_ROWS), pl.ds(e, e)].astype(F32)
            ubuf[pl.ds(halo + r0, CONV_ROWS), :] = val * _sigmoid(gate)
        for c0 in range(0, e, CONV_COLS):
            for r0 in range(0, tt, CONV_ROWS):
                acc = _conv_chunk(ubuf, halo, cw_ref, width, r0, c0, False)
                u1_ref[pl.ds(r0, CONV_ROWS), pl.ds(c0, CONV_COLS)] = acc + cb_ref[:, pl.ds(c0, CONV_COLS)]
        for r0 in range(0, tt, LN_ROWS):
            u = u1_ref[pl.ds(r0, LN_ROWS), :]
            mu = jnp.mean(u, axis=-1, keepdims=True)
            dlt = u - mu
            var = jnp.mean(dlt * dlt, axis=-1, keepdims=True)
            u2 = (dlt * lax.rsqrt(var + NORM_EPS)) * lg_ref[...] + lb_ref[...]
            z = p_ref[pl.ds(r0, LN_ROWS), pl.ds(2 * e, e)].astype(F32)
            y_ref[pl.ds(r0, LN_ROWS), :] = (_silu(u2) * _silu(z)).astype(BF16)

    return pl.pallas_call(
        body, name=name, grid=(t // tt,),
        in_specs=[pl.BlockSpec((tt, e3), lambda i: (i, 0)), pl.BlockSpec(cw.shape, lambda i: (0, 0)),
                  pl.BlockSpec((1, e), lambda i: (0, 0)), pl.BlockSpec((1, e), lambda i: (0, 0)),
                  pl.BlockSpec((1, e), lambda i: (0, 0))],
        out_specs=[pl.BlockSpec((tt, e), lambda i: (i, 0)), pl.BlockSpec((tt, e), lambda i: (i, 0))],
        out_shape=[SDS((t, e), BF16), SDS((t, e), F32)],
        scratch_shapes=[pltpu.VMEM((halo + tt, e), F32)],
        compiler_params=_cparams("arbitrary"),
    )(proj, cw, cb, lg, lb)


def _a_mid_bwd(proj, u1, dyz, cw, lg, lb, width, name, tt=256):
    t, e3 = proj.shape
    e = e3 // 3
    halo = CONF_HALO
    nt = t // tt
    hb = tt // halo

    def body(p_ref, pp_ref, u1_ref, dy_ref, cw_ref, lg_ref, lb_ref,
             dp_ref, dcw_ref, dcb_ref, dlg_ref, dlb_ref, ubuf, dbuf, acc_cb, acc_lg, acc_lb):
        i = pl.program_id(0)
        ti = nt - 1 - i

        @pl.when(i == 0)
        def _():
            dbuf[pl.ds(tt, halo), :] = jnp.zeros((halo, e), F32)
            dcw_ref[...] = jnp.zeros_like(dcw_ref)
            acc_cb[...] = jnp.zeros_like(acc_cb)
            acc_lg[...] = jnp.zeros_like(acc_lg)
            acc_lb[...] = jnp.zeros_like(acc_lb)

        @pl.when(i > 0)
        def _():
            dbuf[pl.ds(tt, halo), :] = dbuf[pl.ds(0, halo), :]

        keep = (ti > 0).astype(F32)
        ubuf[pl.ds(0, halo), :] = keep * (pp_ref[:, pl.ds(0, e)].astype(F32) * _sigmoid(pp_ref[:, pl.ds(e, e)].astype(F32)))
        for r0 in range(0, tt, CONV_ROWS):
            val = p_ref[pl.ds(r0, CONV_ROWS), pl.ds(0, e)].astype(F32)
            gate = p_ref[pl.ds(r0, CONV_ROWS), pl.ds(e, e)].astype(F32)
            ubuf[pl.ds(halo + r0, CONV_ROWS), :] = val * _sigmoid(gate)

        for r0 in range(0, tt, LN_ROWS):
            rows = pl.ds(r0, LN_ROWS)
            u = u1_ref[rows, :]
            mu = jnp.mean(u, axis=-1, keepdims=True)
            dlt = u - mu
            var = jnp.mean(dlt * dlt, axis=-1, keepdims=True)
            rstd = lax.rsqrt(var + NORM_EPS)
            xh = dlt * rstd
            u2 = xh * lg_ref[...] + lb_ref[...]
            s2 = _sigmoid(u2)
            u3 = u2 * s2
            z = p_ref[rows, pl.ds(2 * e, e)].astype(F32)
            sz = _sigmoid(z)
            dy = dy_ref[rows, :].astype(F32)
            dp_ref[rows, pl.ds(2 * e, e)] = (dy * u3 * (sz * (1.0 + z * (1.0 - sz)))).astype(BF16)
            du2 = (dy * (z * sz)) * (s2 * (1.0 + u2 * (1.0 - s2)))
            acc_lg[...] += _rows8(du2 * xh)
            acc_lb[...] += _rows8(du2)
            dxh = du2 * lg_ref[...]
            m1 = jnp.mean(dxh, axis=-1, keepdims=True)
            m2 = jnp.mean(dxh * xh, axis=-1, keepdims=True)
            du1 = rstd * (dxh - m1 - xh * m2)
            dbuf[rows, :] = du1
            acc_cb[...] += _rows8(du1)

        for c0 in range(0, e, CONV_COLS):
            for r0 in range(0, tt, CONV_ROWS):
                du0 = _conv_chunk(dbuf, 0, cw_ref, width, r0, c0, True)
                rows, cols = pl.ds(r0, CONV_ROWS), pl.ds(c0, CONV_COLS)
                val = p_ref[rows, cols].astype(F32)
                sg = _sigmoid(p_ref[rows, pl.ds(e + c0, CONV_COLS)].astype(F32))
                dp_ref[rows, cols] = (du0 * sg).astype(BF16)
                dp_ref[rows, pl.ds(e + c0, CONV_COLS)] = (du0 * val * sg * (1.0 - sg)).astype(BF16)

        _conv_weight_grad(dcw_ref, dbuf, ubuf, halo, width, tt, e)

        @pl.when(i == nt - 1)
        def _():
            dcb_ref[...] = jnp.sum(acc_cb[...], axis=0, keepdims=True)
            dlg_ref[...] = jnp.sum(acc_lg[...], axis=0, keepdims=True)
            dlb_ref[...] = jnp.sum(acc_lb[...], axis=0, keepdims=True)

    vec = pl.BlockSpec((1, e), lambda i: (0, 0))
    return pl.pallas_call(
        body, name=name, grid=(nt,),
        in_specs=[pl.BlockSpec((tt, e3), lambda i: (nt - 1 - i, 0)),
                  pl.BlockSpec((halo, e3), lambda i: (jnp.maximum((nt - 1 - i) * hb - 1, 0), 0)),
                  pl.BlockSpec((tt, e), lambda i: (nt - 1 - i, 0)), pl.BlockSpec((tt, e), lambda i: (nt - 1 - i, 0)),
                  pl.BlockSpec(cw.shape, lambda i: (0, 0)), vec, vec],
        out_specs=[pl.BlockSpec((tt, e3), lambda i: (nt - 1 - i, 0)), pl.BlockSpec(cw.shape, lambda i: (0, 0)), vec, vec, vec],
        out_shape=[SDS((t, e3), BF16), SDS(cw.shape, F32), SDS((1, e), F32), SDS((1, e), F32), SDS((1, e), F32)],
        scratch_shapes=[pltpu.VMEM((halo + tt, e), F32), pltpu.VMEM((tt + halo, e), F32),
                        pltpu.VMEM((SUBLANES, e), F32), pltpu.VMEM((SUBLANES, e), F32), pltpu.VMEM((SUBLANES, e), F32)],
        compiler_params=_cparams("arbitrary"),
    )(proj, proj, u1, dyz, cw, lg, lb)


def _c_mid_fwd(proj, cw, width, name, tt=256):
    t, e4 = proj.shape
    e = e4 // 4
    halo = SHORT_HALO

    def body(p_ref, cw_ref, y_ref, wbuf):
        i = pl.program_id(0)

        @pl.when(i == 0)
        def _():
            wbuf[pl.ds(0, halo), :] = jnp.zeros((halo, e), F32)

        @pl.when(i > 0)
        def _():
            wbuf[pl.ds(0, halo), :] = wbuf[pl.ds(tt, halo), :]

        for r0 in range(0, tt, CONV_ROWS):
            rows = pl.ds(r0, CONV_ROWS)
            wbuf[pl.ds(halo + r0, CONV_ROWS), :] = p_ref[rows, pl.ds(2 * e, e)].astype(F32) * p_ref[rows, pl.ds(0, e)].astype(F32)
        for c0 in range(0, e, CONV_COLS):
            for r0 in range(0, tt, CONV_ROWS):
                rows = pl.ds(r0, CONV_ROWS)
                cv = _conv_chunk(wbuf, halo, cw_ref, width, r0, c0, False)
                bg = p_ref[rows, pl.ds(e + c0, CONV_COLS)].astype(F32)
                z = p_ref[rows, pl.ds(3 * e + c0, CONV_COLS)].astype(F32)
                y_ref[rows, pl.ds(c0, CONV_COLS)] = ((bg * cv) * _silu(z)).astype(BF16)

    return pl.pallas_call(
        body, name=name, grid=(t // tt,),
        in_specs=[pl.BlockSpec((tt, e4), lambda i: (i, 0)), pl.BlockSpec(cw.shape, lambda i: (0, 0))],
        out_specs=pl.BlockSpec((tt, e), lambda i: (i, 0)),
        out_shape=SDS((t, e), BF16),
        scratch_shapes=[pltpu.VMEM((halo + tt, e), F32)],
        compiler_params=_cparams("arbitrary"),
    )(proj, cw)


def _c_mid_bwd(proj, dyz, cw, width, name, tt=256):
    t, e4 = proj.shape
    e = e4 // 4
    halo = SHORT_HALO
    nt = t // tt
    hb = tt // halo

    def body(p_ref, pp_ref, dy_ref, cw_ref, dp_ref, dcw_ref, wbuf, dbuf):
        i = pl.program_id(0)
        ti = nt - 1 - i

        @pl.when(i == 0)
        def _():
            dbuf[pl.ds(tt, halo), :] = jnp.zeros((halo, e), F32)
            dcw_ref[...] = jnp.zeros_like(dcw_ref)

        @pl.when(i > 0)
        def _():
            dbuf[pl.ds(tt, halo), :] = dbuf[pl.ds(0, halo), :]

        keep = (ti > 0).astype(F32)
        wbuf[pl.ds(0, halo), :] = keep * (pp_ref[:, pl.ds(2 * e, e)].astype(F32) * pp_ref[:, pl.ds(0, e)].astype(F32))
        for r0 in range(0, tt, CONV_ROWS):
            rows = pl.ds(r0, CONV_ROWS)
            wbuf[pl.ds(halo + r0, CONV_ROWS), :] = p_ref[rows, pl.ds(2 * e, e)].astype(F32) * p_ref[rows, pl.ds(0, e)].astype(F32)
        for c0 in range(0, e, CONV_COLS):
            for r0 in range(0, tt, CONV_ROWS):
                rows, cols = pl.ds(r0, CONV_ROWS), pl.ds(c0, CONV_COLS)
                cv = _conv_chunk(wbuf, halo, cw_ref, width, r0, c0, False)
                bg = p_ref[rows, pl.ds(e + c0, CONV_COLS)].astype(F32)
                z = p_ref[rows, pl.ds(3 * e + c0, CONV_COLS)].astype(F32)
                sz = _sigmoid(z)
                dyz_c = dy_ref[rows, cols].astype(F32)
                dy = dyz_c * (z * sz)
                dp_ref[rows, pl.ds(3 * e + c0, CONV_COLS)] = (dyz_c * (bg * cv) * (sz * (1.0 + z * (1.0 - sz)))).astype(BF16)
                dp_ref[rows, pl.ds(e + c0, CONV_COLS)] = (dy * cv).astype(BF16)
                dbuf[rows, cols] = dy * bg
        for c0 in range(0, e, CONV_COLS):
            for r0 in range(0, tt, CONV_ROWS):
                rows, cols = pl.ds(r0, CONV_ROWS), pl.ds(c0, CONV_COLS)
                dw = _conv_chunk(dbuf, 0, cw_ref, width, r0, c0, True)
                dp_ref[rows, pl.ds(2 * e + c0, CONV_COLS)] = (dw * p_ref[rows, cols].astype(F32)).astype(BF16)
                dp_ref[rows, cols] = (dw * p_ref[rows, pl.ds(2 * e + c0, CONV_COLS)].astype(F32)).astype(BF16)
        _conv_weight_grad(dcw_ref, dbuf, wbuf, halo, width, tt, e)

    return pl.pallas_call(
        body, name=name, grid=(nt,),
        in_specs=[pl.BlockSpec((tt, e4), lambda i: (nt - 1 - i, 0)),
                  pl.BlockSpec((halo, e4), lambda i: (jnp.maximum((nt - 1 - i) * hb - 1, 0), 0)),
                  pl.BlockSpec((tt, e), lambda i: (nt - 1 - i, 0)), pl.BlockSpec(cw.shape, lambda i: (0, 0))],
        out_specs=[pl.BlockSpec((tt, e4), lambda i: (nt - 1 - i, 0)), pl.BlockSpec(cw.shape, lambda i: (0, 0))],
        out_shape=[SDS((t, e4), BF16), SDS(cw.shape, F32)],
        scratch_shapes=[pltpu.VMEM((halo + tt, e), F32), pltpu.VMEM((tt + halo, e), F32)],
        compiler_params=_cparams("arbitrary"),
    )(proj, proj, dyz, cw)


def _b_prep_fwd(proj, flog, fbias, qg, kg, heads, name, tt=256):
    t, e4 = proj.shape
    e = e4 // 4
    scale = HEAD_DIM ** -0.5

    def body(q_ref, k_ref, fl_ref, fb_ref, qg_ref, kg_ref, qs_ref, kn_ref, c_ref, ct_ref, carry):
        i = pl.program_id(0)

        @pl.when(i == 0)
        def _():
            carry[...] = jnp.zeros_like(carry)

        for h in range(heads):
            cols = pl.ds(h * HEAD_DIM, HEAD_DIM)
            qh = q_ref[:, cols].astype(F32)
            r = lax.rsqrt(jnp.mean(qh * qh, axis=-1, keepdims=True) + NORM_EPS)
            qs_ref[:, cols] = (((qh * r) * qg_ref[:, cols]) * scale).astype(BF16)
            kh = k_ref[:, cols].astype(F32)
            r = lax.rsqrt(jnp.mean(kh * kh, axis=-1, keepdims=True) + NORM_EPS)
            kn_ref[:, cols] = ((kh * r) * kg_ref[:, cols]).astype(BF16)

        a = fl_ref[...] + fb_ref[...]
        lf = jnp.minimum(a, 0.0) - jnp.log(1.0 + jnp.exp(-jnp.abs(a)))
        tri = (lax.broadcasted_iota(jnp.int32, (tt, tt), 0) >= lax.broadcasted_iota(jnp.int32, (tt, tt), 1)).astype(BF16)
        c = _tri_matmul(tri, lf) + carry[...]
        c_ref[...] = c
        ct_ref[...] = c.T
        carry[...] = c_ref[pl.ds(tt - 1, 1), :]

    return pl.pallas_call(
        body, name=name, grid=(t // tt,),
        in_specs=[pl.BlockSpec((tt, e), lambda i: (i, 0)), pl.BlockSpec((tt, e), lambda i: (i, 1)),
                  pl.BlockSpec((tt, LANES), lambda i: (i, 0)), pl.BlockSpec((1, LANES), lambda i: (0, 0)),
                  pl.BlockSpec((1, e), lambda i: (0, 0)), pl.BlockSpec((1, e), lambda i: (0, 0))],
        out_specs=[pl.BlockSpec((tt, e), lambda i: (i, 0)), pl.BlockSpec((tt, e), lambda i: (i, 0)),
                   pl.BlockSpec((tt, LANES), lambda i: (i, 0)), pl.BlockSpec((LANES, tt), lambda i: (0, i))],
        out_shape=[SDS((t, e), BF16), SDS((t, e), BF16), SDS((t, LANES), F32), SDS((LANES, t), F32)],
        scratch_shapes=[pltpu.VMEM((1, LANES), F32)],
        compiler_params=_cparams("arbitrary"),
    )(proj, proj, flog, fbias, qg, kg)


ATT_BLOCK = 512
NEG_BIG = -1e30


def _flash_fwd(qs, kn, proj, ck, heads, name):
    t, e = qs.shape
    blk = min(ATT_BLOCK, t)
    nq = t // blk

    def body(q_ref, k_ref, v_ref, ck_ref, z_ref, o_ref, y_ref, lse_ref):
        i = pl.program_id(1)
        q = q_ref[...]

        def step(j, carry, masked):
            m, l, acc = carry
            rows = pl.ds(pl.multiple_of(j * blk, blk), blk)
            s = lax.dot_general(q, k_ref[rows, :], NT_DIMS, preferred_element_type=F32) - ck_ref[j]
            if masked:
                keep = lax.broadcasted_iota(jnp.int32, (blk, blk), 0) >= lax.broadcasted_iota(jnp.int32, (blk, blk), 1)
                s = jnp.where(keep, s, NEG_BIG)
            m_new = jnp.maximum(m, jnp.max(s, axis=-1, keepdims=True))
            alpha = jnp.exp(m - m_new)
            p = jnp.exp(s - m_new)
            l = alpha * l + jnp.sum(p, axis=-1, keepdims=True)
            p_hi = p.astype(BF16)
            p_lo = (p - p_hi.astype(F32)).astype(BF16)
            vb = v_ref[rows, :]
            acc = alpha * acc + (jnp.dot(p_hi, vb, preferred_element_type=F32) + jnp.dot(p_lo, vb, preferred_element_type=F32))
            return m_new, l, acc

        carry = (jnp.full((blk, 1), NEG_BIG, F32), jnp.zeros((blk, 1), F32), jnp.zeros((blk, HEAD_DIM), F32))
        carry = lax.fori_loop(0, i, lambda j, cr: step(j, cr, False), carry)
        m, l, acc = step(i, carry, True)
        o = acc / l
        o_ref[...] = o
        y_ref[...] = (o * _silu(z_ref[...].astype(F32))).astype(BF16)
        lse_ref[...] = jnp.broadcast_to(m + jnp.log(l), (blk, LANES))

    head_all = pl.BlockSpec((t, HEAD_DIM), lambda h, i: (0, h))
    tile = pl.BlockSpec((blk, HEAD_DIM), lambda h, i: (i, h))
    return pl.pallas_call(
        body, name=name, grid=(heads, nq),
        in_specs=[tile, head_all, pl.BlockSpec((t, HEAD_DIM), lambda h, i: (0, 2 * heads + h)),
                  pl.BlockSpec((None, nq, 1, blk), lambda h, i: (h, 0, 0, 0)),
                  pl.BlockSpec((blk, HEAD_DIM), lambda h, i: (i, 3 * heads + h))],
        out_specs=[tile, tile, pl.BlockSpec((None, blk, LANES), lambda h, i: (h, i, 0))],
        out_shape=[SDS((t, e), F32), SDS((t, e), BF16), SDS((heads, t, LANES), F32)],
        compiler_params=_cparams("parallel", "arbitrary"),
    )(qs, kn, proj, ck, proj)


def _b_bwd_pre(dyz, o, proj, heads, name, tt=256):
    t, e = o.shape

    def body(dy_ref, o_ref, z_ref, do_ref, dz_ref, dl_ref):
        for h in range(heads):
            cols = pl.ds(h * HEAD_DIM, HEAD_DIM)
            z = z_ref[:, cols].astype(F32)
            sz = _sigmoid(z)
            dy = dy_ref[:, cols].astype(F32)
            of = o_ref[:, cols].astype(F32)
            do = (dy * (z * sz)).astype(BF16)
            do_ref[:, cols] = do
            dz_ref[:, cols] = (dy * of * (sz * (1.0 + z * (1.0 - sz)))).astype(BF16)
            dl_ref[h] = jnp.broadcast_to(jnp.sum(do.astype(F32) * of, axis=-1, keepdims=True), (tt, LANES))

    return pl.pallas_call(
        body, name=name, grid=(t // tt,),
        in_specs=[pl.BlockSpec((tt, e), lambda i: (i, 0)), pl.BlockSpec((tt, e), lambda i: (i, 0)),
                  pl.BlockSpec((tt, e), lambda i: (i, 3))],
        out_specs=[pl.BlockSpec((tt, e), lambda i: (i, 0)), pl.BlockSpec((tt, e), lambda i: (i, 0)),
                   pl.BlockSpec((heads, tt, LANES), lambda i: (0, i, 0))],
        out_shape=[SDS((t, e), BF16), SDS((t, e), BF16), SDS((heads, t, LANES), F32)],
        compiler_params=_cparams("parallel"),
    )(dyz, o, proj)


def _flash_bwd(qs, kn, proj, ck, do, lse, delta, heads, name):
    t, e = qs.shape
    blk = min(ATT_BLOCK, t)
    nq = t // blk

    def body(q_ref, do_ref, lse_ref, dl_ref, k_ref, v_ref, ck_ref, dq_ref, dk_ref, dv_ref, dc_ref):
        i = pl.program_id(1)

        @pl.when(i == 0)
        def _():
            dk_ref[...] = jnp.zeros_like(dk_ref)
            dv_ref[...] = jnp.zeros_like(dv_ref)
            dc_ref[...] = jnp.zeros_like(dc_ref)

        q = q_ref[...]
        do = do_ref[...]
        lse = lse_ref[:, 0:1]
        dl = dl_ref[:, 0:1]

        def step(j, dq, masked):
            rows = pl.ds(pl.multiple_of(j * blk, blk), blk)
            kb = k_ref[rows, :]
            s = lax.dot_general(q, kb, NT_DIMS, preferred_element_type=F32) - ck_ref[j]
            p = jnp.exp(s - lse)
            if masked:
                keep = lax.broadcasted_iota(jnp.int32, (blk, blk), 0) >= lax.broadcasted_iota(jnp.int32, (blk, blk), 1)
                p = jnp.where(keep, p, 0.0)
            dv_ref[rows, :] += lax.dot_general(p.astype(BF16), do, TN_DIMS, preferred_element_type=F32)
            dp = lax.dot_general(do, v_ref[rows, :], NT_DIMS, preferred_element_type=F32)
            ds = p * (dp - dl)
            dc_ref[j] -= jnp.sum(ds, axis=0, keepdims=True)
            dsb = ds.astype(BF16)
            dk_ref[rows, :] += lax.dot_general(dsb, q, TN_DIMS, preferred_element_type=F32)
            return dq + jnp.dot(dsb, kb, preferred_element_type=F32)

        dq = lax.fori_loop(0, i, lambda j, acc: step(j, acc, False), jnp.zeros((blk, HEAD_DIM), F32))
        dq_ref[...] = step(i, dq, True)

    tile = pl.BlockSpec((blk, HEAD_DIM), lambda h, i: (i, h))
    stat = pl.BlockSpec((None, blk, LANES), lambda h, i: (h, i, 0))
    head_all = pl.BlockSpec((t, HEAD_DIM), lambda h, i: (0, h))
    cspec = pl.BlockSpec((None, nq, 1, blk), lambda h, i: (h, 0, 0, 0))
    return pl.pallas_call(
        body, name=name, grid=(heads, nq),
        in_specs=[tile, tile, stat, stat, head_all, pl.BlockSpec((t, HEAD_DIM), lambda h, i: (0, 2 * heads + h)), cspec],
        out_specs=[tile, head_all, head_all, cspec],
        out_shape=[SDS((t, e), F32), SDS((t, e), F32), SDS((t, e), F32), SDS((heads, nq, 1, blk), F32)],
        compiler_params=_cparams("parallel", "arbitrary"),
    )(qs, do, lse, delta, kn, proj, ck)


def _b_prep_bwd(dqs, dkn, dv, dz, proj, qg, kg, dct, flog, fbias, heads, name, tt=256):
    t, e4 = proj.shape
    e = e4 // 4
    nt = t // tt
    scale = HEAD_DIM ** -0.5

    def body(dq_ref, dk_ref, dv_ref, dz_ref, q_ref, k_ref, qg_ref, kg_ref, dc_ref, fl_ref, fb_ref,
             dp_ref, dqg_ref, dkg_ref, dfb_ref, carry, dlf, acc_q, acc_k, acc_f):
        i = pl.program_id(0)

        @pl.when(i == 0)
        def _():
            carry[...] = jnp.zeros_like(carry)
            acc_q[...] = jnp.zeros_like(acc_q)
            acc_k[...] = jnp.zeros_like(acc_k)
            acc_f[...] = jnp.zeros_like(acc_f)

        for h in range(heads):
            cols = pl.ds(h * HEAD_DIM, HEAD_DIM)
            for src_ref, d_ref, g_ref, acc, mult, off in ((q_ref, dq_ref, qg_ref, acc_q, scale, 0),
                                                          (k_ref, dk_ref, kg_ref, acc_k, 1.0, e)):
                xf = src_ref[:, cols].astype(F32)
                r = lax.rsqrt(jnp.mean(xf * xf, axis=-1, keepdims=True) + NORM_EPS)
                xh = xf * r
                dn = d_ref[:, cols] * mult
                acc[...] += _rows8(dn * xh)
                dxh = dn * g_ref[:, cols]
                dp_ref[:, pl.ds(off + h * HEAD_DIM, HEAD_DIM)] = (
                    r * (dxh - xh * jnp.mean(dxh * xh, axis=-1, keepdims=True))).astype(BF16)
        dp_ref[:, pl.ds(2 * e, e)] = dv_ref[...].astype(BF16)
        dp_ref[:, pl.ds(3 * e, e)] = dz_ref[...]

        tri = (lax.broadcasted_iota(jnp.int32, (tt, tt), 0) <= lax.broadcasted_iota(jnp.int32, (tt, tt), 1)).astype(BF16)
        dlf[...] = _tri_matmul(tri, dc_ref[...]) + carry[...]
        carry[...] = dlf[pl.ds(0, 1), :]
        a = fl_ref[...] + fb_ref[...]
        dfl = dlf[...] * _sigmoid(-a)
        dp_ref[:, pl.ds(4 * e, LANES)] = dfl.astype(BF16)
        acc_f[...] += _rows8(dfl)

        @pl.when(i == nt - 1)
        def _():
            dqg_ref[...] = jnp.sum(acc_q[...], axis=0, keepdims=True)
            dkg_ref[...] = jnp.sum(acc_k[...], axis=0, keepdims=True)
            dfb_ref[...] = jnp.sum(acc_f[...], axis=0, keepdims=True)

    rev = lambda i: (nt - 1 - i, 0)
    vec_e = pl.BlockSpec((1, e), lambda i: (0, 0))
    vec = pl.BlockSpec((1, LANES), lambda i: (0, 0))
    wide = pl.BlockSpec((tt, e), rev)
    lane = pl.BlockSpec((tt, LANES), rev)
    return pl.pallas_call(
        body, name=name, grid=(nt,),
        in_specs=[wide, wide, wide, wide, wide, pl.BlockSpec((tt, e), lambda i: (nt - 1 - i, 1)), vec_e, vec_e, lane, lane, vec],
        out_specs=[pl.BlockSpec((tt, e4 + LANES), rev), vec, vec, vec],
        out_shape=[SDS((t, e4 + LANES), BF16), SDS((1, LANES), F32), SDS((1, LANES), F32), SDS((1, LANES), F32)],
        scratch_shapes=[pltpu.VMEM((1, LANES), F32), pltpu.VMEM((tt, LANES), F32), pltpu.VMEM((SUBLANES, LANES), F32),
                        pltpu.VMEM((SUBLANES, LANES), F32), pltpu.VMEM((SUBLANES, LANES), F32)],
        compiler_params=_cparams("arbitrary"),
    )(dqs, dkn, dv, dz, proj, proj, qg, kg, dct, flog, fbias)


def _b_fwd(h, b_norm, wb_pad, wb_out, qg, kg, fbias, heads, tag):
    t = h.shape[0]
    e = wb_out.shape[0]
    blk = min(ATT_BLOCK, t)
    hn, proj = _norm_matmul(h, b_norm, wb_pad[:, :4 * e], f"b_in_proj_{tag}")
    flog = _matmul_f32out(hn, wb_pad[:, 4 * e:], f"b_forget_proj_{tag}")
    qs, kn, _, ct = _b_prep_fwd(proj, flog, fbias, qg, kg, heads, f"b_prep_fwd_{tag}")
    ck = ct.reshape(LANES, t // blk, 1, blk)
    o, y, lse = _flash_fwd(qs, kn, proj, ck, heads, f"b_attention_fwd_{tag}")
    sv = dict(x=h, hn=hn, proj=proj, flog=flog, qs=qs, kn=kn, ck=ck, o=o, y=y, lse=lse)
    return _out_matmul_residual(y, wb_out, h, f"b_out_proj_{tag}"), sv


def _b_bwd(dh, sv, b_norm, wb_pad, wb_out, qg, kg, fbias, heads, tag):
    t = dh.shape[0]
    e = wb_out.shape[0]
    gb = {}
    dyz = _matmul_nt(dh, wb_out, f"b_out_bwd_{tag}")
    gb["w_out"] = _matmul_tn(sv["y"], dh, f"b_out_wgrad_{tag}")
    do, dz, delta = _b_bwd_pre(dyz, sv["o"], sv["proj"], heads, f"b_gate_bwd_{tag}")
    dqs, dkn, dv, dc = _flash_bwd(sv["qs"], sv["kn"], sv["proj"], sv["ck"], do, sv["lse"], delta, heads,
                                  f"b_attention_bwd_{tag}")
    dct = jnp.pad(dc.reshape(heads, t).T, ((0, 0), (0, LANES - heads)))
    dproj, dqg, dkg, dfb = _b_prep_bwd(dqs, dkn, dv, dz, sv["proj"], qg, kg, dct, sv["flog"], fbias, heads,
                                       f"b_prep_bwd_{tag}")
    gb["w_in"] = _matmul_tn(sv["hn"], dproj, f"b_in_wgrad_{tag}")[:, :4 * e + heads]
    dh, dg = _dproj_matmul_normbwd(dproj, wb_pad, sv["x"], b_norm, dh, f"b_in_bwd_{tag}")
    gb["norm"], gb["q_norm"], gb["k_norm"], gb["f_bias"] = dg, dqg, dkg, dfb[:, :heads]
    return dh, gb


def _sum_adamw(recv, w, m, v, name, tr=256):
    nl, r, c = w.shape
    tr = min(tr, r)

    def body(g_ref, w_ref, m_ref, v_ref, go_ref, d_ref, mo_ref, vo_ref):
        g = g_ref[0].astype(F32)
        for s in range(1, N_DEV):
            g = g + g_ref[s].astype(F32)
        go_ref[...] = g
        mn = ADAM_B1 * m_ref[...] + (1.0 - ADAM_B1) * g
        vn = ADAM_B2 * v_ref[...] + (1.0 - ADAM_B2) * (g * g)
        m_hat = mn / (1.0 - ADAM_B1 ** ADAM_STEP)
        v_hat = vn / (1.0 - ADAM_B2 ** ADAM_STEP)
        d_ref[...] = -ADAM_LR * (m_hat / (jnp.sqrt(v_hat) + ADAM_EPS) + ADAM_WD * w_ref[...])
        mo_ref[...] = mn
        vo_ref[...] = vn

    blk = pl.BlockSpec((None, tr, c), lambda l, i: (l, i, 0))
    return pl.pallas_call(
        body, name=name, grid=(nl, r // tr),
        in_specs=[pl.BlockSpec((N_DEV, None, tr, c), lambda l, i: (0, l, i, 0)), blk, blk, blk],
        out_specs=[blk, blk, blk, blk],
        out_shape=[SDS(w.shape, F32)] * 4,
        compiler_params=_cparams("parallel", "parallel"),
    )(recv, w, m, v)


def _unshard(g, axis):
    g = jnp.moveaxis(g, 0, axis)
    return g.reshape(g.shape[:axis] + (g.shape[axis] * g.shape[axis + 1],) + g.shape[axis + 2:])


def _to_slabs(full, axis):
    n = full.shape[axis]
    s = full.reshape(full.shape[:axis] + (N_DEV, n // N_DEV) + full.shape[axis + 1:])
    return jnp.moveaxis(s, axis, 0)


def _pack_rows(parts, lead):
    flat = [p.reshape(p.shape[:lead] + (-1,)) for p in parts]
    cat = jnp.concatenate(flat, axis=-1)
    n = cat.shape[-1]
    pad = (-n) % (SUBLANES * LANES)
    cat = jnp.pad(cat, [(0, 0)] * lead + [(0, pad)])
    return cat.reshape(cat.shape[:lead] + ((n + pad) // LANES, LANES))


def _unpack_rows(packed, shapes, lead):
    flat = packed.reshape(packed.shape[:lead] + (-1,))
    out, off = [], 0
    for shp in shapes:
        size = int(np.prod(shp))
        out.append(flat[..., off:off + size].reshape(packed.shape[:lead] + tuple(shp)))
        off += size
    return out


def _pad_rows(w, rows):
    return jnp.pad(w, ((0, rows - w.shape[0]), (0, 0)))


def kernel(x, a_norm, a_w_in, a_conv_w, a_conv_b, a_ln_g, a_ln_b, a_w_out, b_norm, b_w_in, b_f_bias, b_q_norm, b_k_norm, b_w_out, c_norm, c_w_in, c_conv_w, c_w_out, loss_target, m_a_norm, m_a_w_in, m_a_conv_w, m_a_conv_b, m_a_ln_g, m_a_ln_b, m_a_w_out, m_b_norm, m_b_w_in, m_b_f_bias, m_b_q_norm, m_b_k_norm, m_b_w_out, m_c_norm, m_c_w_in, m_c_conv_w, m_c_w_out, v_a_norm, v_a_w_in, v_a_conv_w, v_a_conv_b, v_a_ln_g, v_a_ln_b, v_a_w_out, v_b_norm, v_b_w_in, v_b_f_bias, v_b_q_norm, v_b_k_norm, v_b_w_out, v_c_norm, v_c_w_in, v_c_conv_w, v_c_w_out):
    t, d = x.shape[1], x.shape[2]
    e = a_w_out.shape[1] * N_DEV
    heads = b_f_bias.shape[1]
    n_a, n_b, n_c = a_norm.shape[0], b_norm.shape[0], c_norm.shape[0]
    depth = n_a + n_b + n_c
    ka, kc = a_conv_w.shape[1], c_conv_w.shape[1]
    assert e == heads * HEAD_DIM and n_b == 1 and n_c == 1 and x.shape[0] == 1

    small_names = ["a_norm", "a_conv_w", "a_conv_b", "a_ln_g", "a_ln_b", "c_norm", "c_conv_w"]
    small = dict(a_norm=a_norm, a_conv_w=a_conv_w, a_conv_b=a_conv_b, a_ln_g=a_ln_g, a_ln_b=a_ln_b,
                 c_norm=c_norm, c_conv_w=c_conv_w)
    small_pack = _pack_rows([small[k] for k in small_names], 0)
    mats = [a_w_in, a_w_out, b_w_in, b_w_out, c_w_in, c_w_out]
    gathered = _all_gather([w.astype(BF16) for w in mats] + [small_pack], "all_gather_weights")
    wa_in = _unshard(gathered[0], 2)
    wa_out = _unshard(gathered[1], 1)
    wb_in = _unshard(gathered[2], 2)[0]
    wb_out = _unshard(gathered[3], 1)[0]
    wc_in = _unshard(gathered[4], 2)[0]
    wc_out = _unshard(gathered[5], 1)[0]
    wb_pad = jnp.pad(wb_in, ((0, 0), (0, LANES - heads)))
    sm = _unpack_rows(gathered[6], [small[k].shape for k in small_names], 1)
    g_a_norm = _unshard(sm[0], 1)
    g_a_conv_w = _unshard(sm[1], 2)
    g_a_conv_b = _unshard(sm[2], 1)
    g_a_ln_g = _unshard(sm[3], 1)
    g_a_ln_b = _unshard(sm[4], 1)
    g_c_norm = _unshard(sm[5], 1)
    g_c_conv_w = _unshard(sm[6], 2)

    cw_a = [_pad_rows(g_a_conv_w[j], CONF_HALO) for j in range(n_a)]
    cw_c = _pad_rows(g_c_conv_w[0], SHORT_HALO)
    qg = jnp.tile(b_q_norm, (1, heads))
    kg = jnp.tile(b_k_norm, (1, heads))
    fbias = jnp.pad(b_f_bias, ((0, 0), (0, LANES - heads)))

    h = x[0]
    saved = []
    for i in range(depth):
        kind, j = i % 3, i // 3
        tag = f"l{i}"
        if kind == 0:
            hn, proj = _norm_matmul(h, g_a_norm[j:j + 1], wa_in[j], f"a_in_proj_{tag}")
            y, u1 = _a_mid_fwd(proj, cw_a[j], g_a_conv_b[j:j + 1], g_a_ln_g[j:j + 1], g_a_ln_b[j:j + 1], ka, f"a_mid_fwd_{tag}")
            saved.append(dict(x=h, hn=hn, proj=proj, u1=u1, y=y))
            h = _out_matmul_residual(y, wa_out[j], h, f"a_out_proj_{tag}")
        elif kind == 1:
            h, sv = _b_fwd(h, b_norm, wb_pad, wb_out, qg, kg, fbias, heads, tag)
            saved.append(sv)
        else:
            hn, proj = _norm_matmul(h, g_c_norm, wc_in, f"c_in_proj_{tag}")
            y = _c_mid_fwd(proj, cw_c, kc, f"c_mid_fwd_{tag}")
            saved.append(dict(x=h, hn=hn, proj=proj, y=y))
            h = _out_matmul_residual(y, wc_out, h, f"c_out_proj_{tag}")

    loss_part, dh = _loss_grad(h, loss_target[0], "loss_and_grad")
    loss = lax.psum(jnp.sum(loss_part), ("x", "y", "c"))

    ga = dict(norm=[None] * n_a, w_in=[None] * n_a, conv_w=[None] * n_a, conv_b=[None] * n_a, ln_g=[None] * n_a,
              ln_b=[None] * n_a, w_out=[None] * n_a)
    gb, gc = None, {}
    for i in reversed(range(depth)):
        kind, j = i % 3, i // 3
        tag = f"l{i}"
        sv = saved[i]
        if kind == 0:
            dyz = _matmul_nt(dh, wa_out[j], f"a_out_bwd_{tag}")
            ga["w_out"][j] = _matmul_tn(sv["y"], dh, f"a_out_wgrad_{tag}")
            dproj, dcw, dcb, dlg, dlb = _a_mid_bwd(sv["proj"], sv["u1"], dyz, cw_a[j], g_a_ln_g[j:j + 1], g_a_ln_b[j:j + 1],
                                                   ka, f"a_mid_bwd_{tag}")
            ga["w_in"][j] = _matmul_tn(sv["hn"], dproj, f"a_in_wgrad_{tag}")
            dh, dg = _dproj_matmul_normbwd(dproj, wa_in[j], sv["x"], g_a_norm[j:j + 1], dh, f"a_in_bwd_{tag}")
            ga["norm"][j], ga["conv_w"][j], ga["conv_b"][j], ga["ln_g"][j], ga["ln_b"][j] = dg[0], dcw[:ka], dcb[0], dlg[0], dlb[0]
        elif kind == 1:
            dh, gb = _b_bwd(dh, sv, b_norm, wb_pad, wb_out, qg, kg, fbias, heads, tag)
        else:
            dyz = _matmul_nt(dh, wc_out, f"c_out_bwd_{tag}")
            gc["w_out"] = _matmul_tn(sv["y"], dh, f"c_out_wgrad_{tag}")
            dproj, dcw = _c_mid_bwd(sv["proj"], dyz, cw_c, kc, f"c_mid_bwd_{tag}")
            gc["w_in"] = _matmul_tn(sv["hn"], dproj, f"c_in_wgrad_{tag}")
            dh, dg = _dproj_matmul_normbwd(dproj, wc_in, sv["x"], g_c_norm, dh, f"c_in_bwd_{tag}")
            gc["norm"], gc["conv_w"] = dg, dcw[:kc][None]
    grad_x = dh[None]

    mat_slabs = [_to_slabs(jnp.stack(ga["w_in"]), 2), _to_slabs(jnp.stack(ga["w_out"]), 1),
                 _to_slabs(gb["w_in"][None], 2), _to_slabs(gb["w_out"][None], 1),
                 _to_slabs(gc["w_in"][None], 2), _to_slabs(gc["w_out"][None], 1)]
    sharded_small = [(jnp.stack(ga["norm"]), 1), (jnp.stack(ga["conv_w"]), 2), (jnp.stack(ga["conv_b"]), 1),
                     (jnp.stack(ga["ln_g"]), 1), (jnp.stack(ga["ln_b"]), 1), (gc["norm"], 1), (gc["conv_w"], 2)]
    repl_small = [gb["norm"], gb["f_bias"], gb["q_norm"], gb["k_norm"]]
    small_slabs = _pack_rows([_to_slabs(g, ax) for g, ax in sharded_small]
                             + [jnp.broadcast_to(g[None], (N_DEV,) + g.shape) for g in repl_small], 1)
    recv = _exchange(mat_slabs + [small_slabs], "exchange_gradients")

    outs = {}
    mat_names = ["a_w_in", "a_w_out", "b_w_in", "b_w_out", "c_w_in", "c_w_out"]
    mat_w = dict(a_w_in=(a_w_in, m_a_w_in, v_a_w_in), a_w_out=(a_w_out, m_a_w_out, v_a_w_out),
                 b_w_in=(b_w_in, m_b_w_in, v_b_w_in), b_w_out=(b_w_out, m_b_w_out, v_b_w_out),
                 c_w_in=(c_w_in, m_c_w_in, v_c_w_in), c_w_out=(c_w_out, m_c_w_out, v_c_w_out))
    for k, name in enumerate(mat_names):
        outs[name] = _sum_adamw(recv[k], *mat_w[name], f"adamw_{name}")

    small_order = small_names + ["b_norm", "b_f_bias", "b_q_norm", "b_k_norm"]
    small_w = dict(a_norm=(a_norm, m_a_norm, v_a_norm), a_conv_w=(a_conv_w, m_a_conv_w, v_a_conv_w),
                   a_conv_b=(a_conv_b, m_a_conv_b, v_a_conv_b), a_ln_g=(a_ln_g, m_a_ln_g, v_a_ln_g),
                   a_ln_b=(a_ln_b, m_a_ln_b, v_a_ln_b), c_norm=(c_norm, m_c_norm, v_c_norm),
                   c_conv_w=(c_conv_w, m_c_conv_w, v_c_conv_w), b_norm=(b_norm, m_b_norm, v_b_norm),
                   b_f_bias=(b_f_bias, m_b_f_bias, v_b_f_bias), b_q_norm=(b_q_norm, m_b_q_norm, v_b_q_norm),
                   b_k_norm=(b_k_norm, m_b_k_norm, v_b_k_norm))
    packs = [_pack_rows([small_w[k][q] for k in small_order], 0)[None] for q in range(3)]
    small_out = _sum_adamw(recv[6][:, None], *packs, "adamw_vectors")
    shapes = [small_w[k][0].shape for k in small_order]
    unpacked = [_unpack_rows(o[0], shapes, 0) for o in small_out]
    for idx, name in enumerate(small_order):
        outs[name] = tuple(unpacked[q][idx] for q in range(4))

    order = ["a_norm", "a_w_in", "a_conv_w", "a_conv_b", "a_ln_g", "a_ln_b", "a_w_out", "b_norm", "b_w_in", "b_f_bias",
             "b_q_norm", "b_k_norm", "b_w_out", "c_norm", "c_w_in", "c_conv_w", "c_w_out"]
    return (loss, grad_x, *[outs[k][0] for k in order], *[outs[k][1] for k in order],
            *[outs[k][2] for k in order], *[outs[k][3] for k in order])
```

```python
import jax
import jax.numpy as jnp
import numpy as np
from jax import lax
from jax.experimental import pallas as pl
from jax.experimental.pallas import tpu as pltpu

F32 = jnp.float32
BF16 = jnp.bfloat16
SDS = jax.ShapeDtypeStruct

NORM_EPS = 1e-6
ADAM_LR = 0.001
ADAM_B1 = 0.9
ADAM_B2 = 0.999
ADAM_EPS = 1e-08
ADAM_WD = 0.01
ADAM_STEP = 10

N_DEV = 8
LANES = 128
SUBLANES = 8
HEAD_DIM = 128
CONF_HALO = 32
SHORT_HALO = 8
VMEM_LIMIT = 56 * 1024 * 1024

NT_DIMS = (((1,), (1,)), ((), ()))
TN_DIMS = (((0,), (0,)), ((), ()))
MESH = pl.DeviceIdType.MESH
ANY = pl.BlockSpec(memory_space=pl.ANY)


def _cparams(*sem):
    return pltpu.CompilerParams(dimension_semantics=sem, vmem_limit_bytes=VMEM_LIMIT)


def _divisor_tile(n, cap):
    return max(m for m in range(LANES, min(n, cap) + 1, LANES) if n % m == 0)


def _sigmoid(x):
    return 1.0 / (1.0 + jnp.exp(-x))


def _silu(x):
    return x * _sigmoid(x)


def _dsilu(x):
    s = _sigmoid(x)
    return s * (1.0 + x * (1.0 - s))


def _rows8(v):
    out = v[0:SUBLANES]
    for a in range(1, v.shape[0] // SUBLANES):
        out = out + v[a * SUBLANES:(a + 1) * SUBLANES]
    return out


def _split3(v):
    hi = v.astype(BF16)
    r1 = v - hi.astype(F32)
    mid = r1.astype(BF16)
    lo = (r1 - mid.astype(F32)).astype(BF16)
    return hi, mid, lo


def _tri_matmul(tri, v):
    hi, mid, lo = _split3(v)
    return (jnp.dot(tri, hi, preferred_element_type=F32) + jnp.dot(tri, mid, preferred_element_type=F32)
            + jnp.dot(tri, lo, preferred_element_type=F32))


def _position():
    return lax.axis_index("x"), lax.axis_index("y"), lax.axis_index("c")


def _all_gather(shards, name):
    n = len(shards)

    def body(*refs):
        xs, outs = refs[:n], refs[n:2 * n]
        send_sems, recv_sems, local_sems = refs[2 * n:]
        x, y, c = _position()
        me, sibling = (x, y, c), (x, y, 1 - c)
        chips = [(1 - x, y), (x, 1 - y), (1 - x, 1 - y)]

        def slot(a, px, py, pc):
            return outs[a].at[4 * px + 2 * py + pc]

        def copy(a, k, block, to, src=None):
            return pltpu.make_async_remote_copy(
                src_ref=slot(a, *block) if src is None else src, dst_ref=slot(a, *block),
                send_sem=send_sems.at[a, k], recv_sem=recv_sems.at[a, k], device_id=to, device_id_type=MESH)

        started = []
        mine = []
        for a in range(n):
            cp = pltpu.make_async_copy(xs[a], slot(a, *me), local_sems.at[a])
            cp.start()
            mine.append(cp)
        for a in range(n):
            first = [copy(a, 0, me, sibling, src=xs[a])]
            first += [copy(a, 1 + j, me, (*chip, c), src=xs[a]) for j, chip in enumerate(chips)]
            for cp in first:
                cp.start()
            started += first
        for a in range(n):
            for j, chip in enumerate(chips):
                copy(a, 1 + j, (*chip, c), me).wait_recv()
                fwd = copy(a, 4 + j, (*chip, c), sibling)
                fwd.start()
                started.append(fwd)
        for a in range(n):
            copy(a, 0, sibling, me).wait_recv()
            for j, chip in enumerate(chips):
                copy(a, 4 + j, (*chip, 1 - c), me).wait_recv()
        for cp in started:
            cp.wait_send()
        for cp in mine:
            cp.wait()

    return pl.pallas_call(
        body, name=name,
        out_shape=[SDS((N_DEV,) + s.shape, s.dtype) for s in shards],
        in_specs=[ANY] * n, out_specs=[ANY] * n,
        scratch_shapes=[pltpu.SemaphoreType.DMA((n, 7)), pltpu.SemaphoreType.DMA((n, 7)), pltpu.SemaphoreType.DMA((n,))],
    )(*shards)


def _exchange(slabs, name):
    n = len(slabs)

    def body(*refs):
        ins, outs = refs[:n], refs[n:2 * n]
        send_sems, recv_sems, local_sems = refs[2 * n:]
        x, y, c = _position()
        me = 4 * x + 2 * y + c
        peers = [(x ^ bx, y ^ by, c ^ bc) for bx in (0, 1) for by in (0, 1) for bc in (0, 1)][1:]

        def copy(a, k, peer):
            pid = 4 * peer[0] + 2 * peer[1] + peer[2]
            return pltpu.make_async_remote_copy(
                src_ref=ins[a].at[pid], dst_ref=outs[a].at[me],
                send_sem=send_sems.at[a, k], recv_sem=recv_sems.at[a, k], device_id=peer, device_id_type=MESH)

        def arrival(a, k, peer):
            pid = 4 * peer[0] + 2 * peer[1] + peer[2]
            return pltpu.make_async_remote_copy(
                src_ref=ins[a].at[pid], dst_ref=outs[a].at[pid],
                send_sem=send_sems.at[a, k], recv_sem=recv_sems.at[a, k], device_id=peer, device_id_type=MESH)

        mine = []
        for a in range(n):
            cp = pltpu.make_async_copy(ins[a].at[me], outs[a].at[me], local_sems.at[a])
            cp.start()
            mine.append(cp)
        started = []
        for a in range(n):
            for k, peer in enumerate(peers):
                cp = copy(a, k, peer)
                cp.start()
                started.append(cp)
        for a in range(n):
            for k, peer in enumerate(peers):
                arrival(a, k, peer).wait_recv()
        for cp in started:
            cp.wait_send()
        for cp in mine:
            cp.wait()

    return pl.pallas_call(
        body, name=name,
        out_shape=[SDS(s.shape, s.dtype) for s in slabs],
        in_specs=[ANY] * n, out_specs=[ANY] * n,
        scratch_shapes=[pltpu.SemaphoreType.DMA((n, 7)), pltpu.SemaphoreType.DMA((n, 7)), pltpu.SemaphoreType.DMA((n,))],
    )(*slabs)


def _norm_matmul(x, g, w, name, tm=512, tn=1024):
    t, d = x.shape
    n = w.shape[1]
    tn = min(tn, n)

    def body(x_ref, g_ref, w_ref, hn_ref, o_ref):
        @pl.when(pl.program_id(1) == 0)
        def _():
            xf = x_ref[...]
            r = lax.rsqrt(jnp.mean(xf * xf, axis=-1, keepdims=True) + NORM_EPS)
            hn_ref[...] = ((xf * r) * g_ref[...]).astype(BF16)

        o_ref[...] = jnp.dot(hn_ref[...], w_ref[...], preferred_element_type=F32).astype(o_ref.dtype)

    return pl.pallas_call(
        body, name=name, grid=(t // tm, n // tn),
        in_specs=[pl.BlockSpec((tm, d), lambda i, j: (i, 0)), pl.BlockSpec((1, d), lambda i, j: (0, 0)),
                  pl.BlockSpec((d, tn), lambda i, j: (0, j))],
        out_specs=[pl.BlockSpec((tm, d), lambda i, j: (i, 0)), pl.BlockSpec((tm, tn), lambda i, j: (i, j))],
        out_shape=[SDS((t, d), BF16), SDS((t, n), BF16)],
        compiler_params=_cparams("parallel", "arbitrary"),
    )(x, g, w)


def _matmul_f32out(a, w, name, tm=512):
    t, k = a.shape
    n = w.shape[1]

    def body(a_ref, w_ref, o_ref):
        o_ref[...] = jnp.dot(a_ref[...], w_ref[...], preferred_element_type=F32)

    return pl.pallas_call(
        body, name=name, grid=(t // tm,),
        in_specs=[pl.BlockSpec((tm, k), lambda i: (i, 0)), pl.BlockSpec((k, n), lambda i: (0, 0))],
        out_specs=pl.BlockSpec((tm, n), lambda i: (i, 0)),
        out_shape=SDS((t, n), F32),
        compiler_params=_cparams("parallel"),
    )(a, w)


def _out_matmul_residual(y, w, x, name, tm=512):
    t, e = y.shape
    d = w.shape[1]

    def body(y_ref, w_ref, x_ref, o_ref):
        o_ref[...] = x_ref[...] + jnp.dot(y_ref[...], w_ref[...], preferred_element_type=F32)

    return pl.pallas_call(
        body, name=name, grid=(t // tm,),
        in_specs=[pl.BlockSpec((tm, e), lambda i: (i, 0)), pl.BlockSpec((e, d), lambda i: (0, 0)),
                  pl.BlockSpec((tm, d), lambda i: (i, 0))],
        out_specs=pl.BlockSpec((tm, d), lambda i: (i, 0)),
        out_shape=SDS((t, d), F32),
        compiler_params=_cparams("parallel"),
    )(y, w, x)


def _matmul_nt(a, w, name, tm=512):
    t, d = a.shape
    e = w.shape[0]

    def body(a_ref, w_ref, o_ref):
        o_ref[...] = lax.dot_general(a_ref[...].astype(BF16), w_ref[...], NT_DIMS,
                                     preferred_element_type=F32).astype(o_ref.dtype)

    return pl.pallas_call(
        body, name=name, grid=(t // tm,),
        in_specs=[pl.BlockSpec((tm, d), lambda i: (i, 0)), pl.BlockSpec((e, d), lambda i: (0, 0))],
        out_specs=pl.BlockSpec((tm, e), lambda i: (i, 0)),
        out_shape=SDS((t, e), BF16),
        compiler_params=_cparams("parallel"),
    )(a, w)


def _matmul_tn(a, b, name, out_dtype=BF16, tm=1024, tn=1024, tk=512):
    t, m = a.shape
    n = b.shape[1]
    tm, tn = min(tm, m), _divisor_tile(n, 2 * tn)
    nk = t // tk

    def body(a_ref, b_ref, o_ref, acc_ref):
        k = pl.program_id(2)

        @pl.when(k == 0)
        def _():
            acc_ref[...] = jnp.zeros_like(acc_ref)

        acc_ref[...] += lax.dot_general(a_ref[...].astype(BF16), b_ref[...].astype(BF16), TN_DIMS,
                                        preferred_element_type=F32)

        @pl.when(k == nk - 1)
        def _():
            o_ref[...] = acc_ref[...].astype(o_ref.dtype)

    return pl.pallas_call(
        body, name=name, grid=(m // tm, n // tn, nk),
        in_specs=[pl.BlockSpec((tk, tm), lambda i, j, k: (k, i)), pl.BlockSpec((tk, tn), lambda i, j, k: (k, j))],
        out_specs=pl.BlockSpec((tm, tn), lambda i, j, k: (i, j)),
        out_shape=SDS((m, n), out_dtype),
        scratch_shapes=[pltpu.VMEM((tm, tn), F32)],
        compiler_params=_cparams("parallel", "parallel", "arbitrary"),
    )(a, b)


def _dproj_matmul_normbwd(dproj, w, x, g, dxn, name, tm=512, tk=1024):
    t, n = dproj.shape
    d = w.shape[0]
    tk = _divisor_tile(n, 2 * tk)
    nk = n // tk

    def body(dp_ref, w_ref, x_ref, g_ref, dxn_ref, dx_ref, dg_ref, acc_ref):
        i, k = pl.program_id(0), pl.program_id(1)

        @pl.when(k == 0)
        def _():
            acc_ref[...] = jnp.zeros_like(acc_ref)

        acc_ref[...] += lax.dot_general(dp_ref[...], w_ref[...], NT_DIMS, preferred_element_type=F32)

        @pl.when(k == nk - 1)
        def _():
            dhn = acc_ref[...]
            xf = x_ref[...]
            r = lax.rsqrt(jnp.mean(xf * xf, axis=-1, keepdims=True) + NORM_EPS)
            xh = xf * r
            dy = dhn * g_ref[...]
            dx_ref[...] = dxn_ref[...] + r * (dy - xh * jnp.mean(dy * xh, axis=-1, keepdims=True))
            part = jnp.sum(dhn * xh, axis=0, keepdims=True)

            @pl.when(i == 0)
            def _():
                dg_ref[...] = part

            @pl.when(i > 0)
            def _():
                dg_ref[...] += part

    return pl.pallas_call(
        body, name=name, grid=(t // tm, nk),
        in_specs=[pl.BlockSpec((tm, tk), lambda i, k: (i, k)), pl.BlockSpec((d, tk), lambda i, k: (0, k)),
                  pl.BlockSpec((tm, d), lambda i, k: (i, 0)), pl.BlockSpec((1, d), lambda i, k: (0, 0)),
                  pl.BlockSpec((tm, d), lambda i, k: (i, 0))],
        out_specs=[pl.BlockSpec((tm, d), lambda i, k: (i, 0)), pl.BlockSpec((1, d), lambda i, k: (0, 0))],
        out_shape=[SDS((t, d), F32), SDS((1, d), F32)],
        scratch_shapes=[pltpu.VMEM((tm, d), F32)],
        compiler_params=_cparams("arbitrary", "arbitrary"),
    )(dproj, w, x, g, dxn)


def _loss_grad(y, target, name, tm=512):
    t, d = y.shape
    inv_d = 1.0 / d

    def body(y_ref, t_ref, part_ref, dy_ref):
        i = pl.program_id(0)
        err = y_ref[...] - t_ref[...]
        dy_ref[...] = err * inv_d
        part = jnp.sum(err * err, axis=0, keepdims=True) * (0.5 * inv_d)

        @pl.when(i == 0)
        def _():
            part_ref[...] = part

        @pl.when(i > 0)
        def _():
            part_ref[...] += part

    return pl.pallas_call(
        body, name=name, grid=(t // tm,),
        in_specs=[pl.BlockSpec((tm, d), lambda i: (i, 0)), pl.BlockSpec((tm, d), lambda i: (i, 0))],
        out_specs=[pl.BlockSpec((1, d), lambda i: (0, 0)), pl.BlockSpec((tm, d), lambda i: (i, 0))],
        out_shape=[SDS((1, d), F32), SDS((t, d), F32)],
        compiler_params=_cparams("arbitrary"),
    )(y, target)


CONV_ROWS = 32
CONV_COLS = 512


def _conv_chunk(src_ref, base, w_ref, width, r0, c0, flip):
    acc = None
    for k in range(width):
        off = base + r0 + ((width - 1 - k) if flip else (k - (width - 1)))
        term = src_ref[pl.ds(off, CONV_ROWS), pl.ds(c0, CONV_COLS)] * w_ref[pl.ds(k, 1), pl.ds(c0, CONV_COLS)]
        acc = term if acc is None else acc + term
    return acc


def _conv_weight_grad(dw_ref, d_ref, src_ref, base, width, tt, e):
    for c0 in range(0, e, CONV_COLS):
        for k in range(width):
            acc = None
            for r0 in range(0, tt, CONV_ROWS):
                prod = (d_ref[pl.ds(r0, CONV_ROWS), pl.ds(c0, CONV_COLS)]
                        * src_ref[pl.ds(base + r0 - (width - 1) + k, CONV_ROWS), pl.ds(c0, CONV_COLS)])
                part = _rows8(prod)
                acc = part if acc is None else acc + part
            dw_ref[pl.ds(k, 1), pl.ds(c0, CONV_COLS)] += jnp.sum(acc, axis=0, keepdims=True)


def _shifted_copies(dst_ref, src_ref, c0, length, sign):
    lo, hi = (SUBLANES, length) if sign < 0 else (0, length - SUBLANES)
    for b in range(SUBLANES):
        for r0 in range(lo, hi, CONV_ROWS):
            n = min(CONV_ROWS, hi - r0)
            dst_ref[b, pl.ds(r0, n), :] = src_ref[pl.ds(r0 + sign * b, n), pl.ds(c0, CONV_COLS)]


def _conv_aligned(copies_ref, base, w_ref, width, r0, c0, sign):
    acc = None
    for d in range(width):
        a, b = divmod(d, SUBLANES)
        term = (copies_ref[b, pl.ds(base + r0 + sign * SUBLANES * a, CONV_ROWS), :]
                * w_ref[pl.ds(width - 1 - d, 1), pl.ds(c0, CONV_COLS)])
        acc = term if acc is None else acc + term
    return acc


def _conv_weight_grad_aligned(dw_ref, d_ref, copies_ref, base, width, tt, c0):
    for d in range(width):
        a, b = divmod(d, SUBLANES)
        acc = None
        for r0 in range(0, tt, CONV_ROWS):
            prod = (d_ref[pl.ds(r0, CONV_ROWS), pl.ds(c0, CONV_COLS)]
                    * copies_ref[b, pl.ds(base + r0 - SUBLANES * a, CONV_ROWS), :])
            part = _rows8(prod)
            acc = part if acc is None else acc + part
        dw_ref[pl.ds(width - 1 - d, 1), pl.ds(c0, CONV_COLS)] += jnp.sum(acc, axis=0, keepdims=True)


LN_ROWS = 16


def _a_mid_fwd(proj, cw, cb, lg, lb, width, name, tt=256):
    t, e3 = proj.shape
    e = e3 // 3
    halo = CONF_HALO

    def body(p_ref, cw_ref, cb_ref, lg_ref, lb_ref, y_ref, u1_ref, ubuf, shifted):
        i = pl.program_id(0)

        @pl.when(i == 0)
        def _():
            ubuf[pl.ds(0, halo), :] = jnp.zeros((halo, e), F32)

        @pl.when(i > 0)
        def _():
            ubuf[pl.ds(0, halo), :] = ubuf[pl.ds(tt, halo), :]

        for r0 in range(0, tt, CONV_ROWS):
            val = p_ref[pl.ds(r0, CONV_ROWS), pl.ds(0, e)].astype(F32)
            gate = p_ref[pl.ds(r0, CONV_ROWS), pl.ds(e, e)].astype(F32)
            ubuf[pl.ds(halo + r0, CONV_ROWS), :] = val * _sigmoid(gate)
        for c0 in range(0, e, CONV_COLS):
            _shifted_copies(shifted, ubuf, c0, halo + tt, -1)
            for r0 in range(0, tt, CONV_ROWS):
                acc = _conv_aligned(shifted, halo, cw_ref, width, r0, c0, -1)
                u1_ref[pl.ds(r0, CONV_ROWS), pl.ds(c0, CONV_COLS)] = acc + cb_ref[:, pl.ds(c0, CONV_COLS)]
        for r0 in range(0, tt, LN_ROWS):
            u = u1_ref[pl.ds(r0, LN_ROWS), :]
            mu = jnp.mean(u, axis=-1, keepdims=True)
            dlt = u - mu
            var = jnp.mean(dlt * dlt, axis=-1, keepdims=True)
            u2 = (dlt * lax.rsqrt(var + NORM_EPS)) * lg_ref[...] + lb_ref[...]
            z = p_ref[pl.ds(r0, LN_ROWS), pl.ds(2 * e, e)].astype(F32)
            y_ref[pl.ds(r0, LN_ROWS), :] = (_silu(u2) * _silu(z)).astype(BF16)

    return pl.pallas_call(
        body, name=name, grid=(t // tt,),
        in_specs=[pl.BlockSpec((tt, e3), lambda i: (i, 0)), pl.BlockSpec(cw.shape, lambda i: (0, 0)),
                  pl.BlockSpec((1, e), lambda i: (0, 0)), pl.BlockSpec((1, e), lambda i: (0, 0)),
                  pl.BlockSpec((1, e), lambda i: (0, 0))],
        out_specs=[pl.BlockSpec((tt, e), lambda i: (i, 0)), pl.BlockSpec((tt, e), lambda i: (i, 0))],
        out_shape=[SDS((t, e), BF16), SDS((t, e), F32)],
        scratch_shapes=[pltpu.VMEM((halo + tt, e), F32), pltpu.VMEM((SUBLANES, halo + tt, CONV_COLS), F32)],
        compiler_params=_cparams("arbitrary"),
    )(proj, cw, cb, lg, lb)


def _a_mid_bwd(proj, u1, dyz, cw, lg, lb, width, name, tt=256):
    t, e3 = proj.shape
    e = e3 // 3
    halo = CONF_HALO
    nt = t // tt
    hb = tt // halo

    def body(p_ref, pp_ref, u1_ref, dy_ref, cw_ref, lg_ref, lb_ref,
             dp_ref, dcw_ref, dcb_ref, dlg_ref, dlb_ref, ubuf, dbuf, shifted, acc_cb, acc_lg, acc_lb):
        i = pl.program_id(0)
        ti = nt - 1 - i

        @pl.when(i == 0)
        def _():
            dbuf[pl.ds(tt, halo), :] = jnp.zeros((halo, e), F32)
            dcw_ref[...] = jnp.zeros_like(dcw_ref)
            acc_cb[...] = jnp.zeros_like(acc_cb)
            acc_lg[...] = jnp.zeros_like(acc_lg)
            acc_lb[...] = jnp.zeros_like(acc_lb)

        @pl.when(i > 0)
        def _():
            dbuf[pl.ds(tt, halo), :] = dbuf[pl.ds(0, halo), :]

        keep = (ti > 0).astype(F32)
        ubuf[pl.ds(0, halo), :] = keep * (pp_ref[:, pl.ds(0, e)].astype(F32) * _sigmoid(pp_ref[:, pl.ds(e, e)].astype(F32)))
        for r0 in range(0, tt, CONV_ROWS):
            val = p_ref[pl.ds(r0, CONV_ROWS), pl.ds(0, e)].astype(F32)
            gate = p_ref[pl.ds(r0, CONV_ROWS), pl.ds(e, e)].astype(F32)
            ubuf[pl.ds(halo + r0, CONV_ROWS), :] = val * _sigmoid(gate)

        for r0 in range(0, tt, LN_ROWS):
            rows = pl.ds(r0, LN_ROWS)
            u = u1_ref[rows, :]
            mu = jnp.mean(u, axis=-1, keepdims=True)
            dlt = u - mu
            var = jnp.mean(dlt * dlt, axis=-1, keepdims=True)
            rstd = lax.rsqrt(var + NORM_EPS)
            xh = dlt * rstd
            u2 = xh * lg_ref[...] + lb_ref[...]
            s2 = _sigmoid(u2)
            u3 = u2 * s2
            z = p_ref[rows, pl.ds(2 * e, e)].astype(F32)
            sz = _sigmoid(z)
            dy = dy_ref[rows, :].astype(F32)
            dp_ref[rows, pl.ds(2 * e, e)] = (dy * u3 * (sz * (1.0 + z * (1.0 - sz)))).astype(BF16)
            du2 = (dy * (z * sz)) * (s2 * (1.0 + u2 * (1.0 - s2)))
            acc_lg[...] += _rows8(du2 * xh)
            acc_lb[...] += _rows8(du2)
            dxh = du2 * lg_ref[...]
            m1 = jnp.mean(dxh, axis=-1, keepdims=True)
            m2 = jnp.mean(dxh * xh, axis=-1, keepdims=True)
            du1 = rstd * (dxh - m1 - xh * m2)
            dbuf[rows, :] = du1
            acc_cb[...] += _rows8(du1)

        for c0 in range(0, e, CONV_COLS):
            _shifted_copies(shifted, dbuf, c0, tt + halo, 1)
            for r0 in range(0, tt, CONV_ROWS):
                du0 = _conv_aligned(shifted, 0, cw_ref, width, r0, c0, 1)
                rows, cols = pl.ds(r0, CONV_ROWS), pl.ds(c0, CONV_COLS)
                val = p_ref[rows, cols].astype(F32)
                sg = _sigmoid(p_ref[rows, pl.ds(e + c0, CONV_COLS)].astype(F32))
                dp_ref[rows, cols] = (du0 * sg).astype(BF16)
                dp_ref[rows, pl.ds(e + c0, CONV_COLS)] = (du0 * val * sg * (1.0 - sg)).astype(BF16)
            _shifted_copies(shifted, ubuf, c0, halo + tt, -1)
            _conv_weight_grad_aligned(dcw_ref, dbuf, shifted, halo, width, tt, c0)

        @pl.when(i == nt - 1)
        def _():
            dcb_ref[...] = jnp.sum(acc_cb[...], axis=0, keepdims=True)
            dlg_ref[...] = jnp.sum(acc_lg[...], axis=0, keepdims=True)
            dlb_ref[...] = jnp.sum(acc_lb[...], axis=0, keepdims=True)

    vec = pl.BlockSpec((1, e), lambda i: (0, 0))
    return pl.pallas_call(
        body, name=name, grid=(nt,),
        in_specs=[pl.BlockSpec((tt, e3), lambda i: (nt - 1 - i, 0)),
                  pl.BlockSpec((halo, e3), lambda i: (jnp.maximum((nt - 1 - i) * hb - 1, 0), 0)),
                  pl.BlockSpec((tt, e), lambda i: (nt - 1 - i, 0)), pl.BlockSpec((tt, e), lambda i: (nt - 1 - i, 0)),
                  pl.BlockSpec(cw.shape, lambda i: (0, 0)), vec, vec],
        out_specs=[pl.BlockSpec((tt, e3), lambda i: (nt - 1 - i, 0)), pl.BlockSpec(cw.shape, lambda i: (0, 0)), vec, vec, vec],
        out_shape=[SDS((t, e3), BF16), SDS(cw.shape, F32), SDS((1, e), F32), SDS((1, e), F32), SDS((1, e), F32)],
        scratch_shapes=[pltpu.VMEM((halo + tt, e), F32), pltpu.VMEM((tt + halo, e), F32),
                        pltpu.VMEM((SUBLANES, halo + tt, CONV_COLS), F32),
                        pltpu.VMEM((SUBLANES, e), F32), pltpu.VMEM((SUBLANES, e), F32), pltpu.VMEM((SUBLANES, e), F32)],
        compiler_params=_cparams("arbitrary"),
    )(proj, proj, u1, dyz, cw, lg, lb)


def _c_mid_fwd(proj, cw, width, name, tt=256):
    t, e4 = proj.shape
    e = e4 // 4
    halo = SHORT_HALO

    def body(p_ref, cw_ref, y_ref, wbuf):
        i = pl.program_id(0)

        @pl.when(i == 0)
        def _():
            wbuf[pl.ds(0, halo), :] = jnp.zeros((halo, e), F32)

        @pl.when(i > 0)
        def _():
            wbuf[pl.ds(0, halo), :] = wbuf[pl.ds(tt, halo), :]

        for r0 in range(0, tt, CONV_ROWS):
            rows = pl.ds(r0, CONV_ROWS)
            wbuf[pl.ds(halo + r0, CONV_ROWS), :] = p_ref[rows, pl.ds(2 * e, e)].astype(F32) * p_ref[rows, pl.ds(0, e)].astype(F32)
        for c0 in range(0, e, CONV_COLS):
            for r0 in range(0, tt, CONV_ROWS):
                rows = pl.ds(r0, CONV_ROWS)
                cv = _conv_chunk(wbuf, halo, cw_ref, width, r0, c0, False)
                bg = p_ref[rows, pl.ds(e + c0, CONV_COLS)].astype(F32)
                z = p_ref[rows, pl.ds(3 * e + c0, CONV_COLS)].astype(F32)
                y_ref[rows, pl.ds(c0, CONV_COLS)] = ((bg * cv) * _silu(z)).astype(BF16)

    return pl.pallas_call(
        body, name=name, grid=(t // tt,),
        in_specs=[pl.BlockSpec((tt, e4), lambda i: (i, 0)), pl.BlockSpec(cw.shape, lambda i: (0, 0))],
        out_specs=pl.BlockSpec((tt, e), lambda i: (i, 0)),
        out_shape=SDS((t, e), BF16),
        scratch_shapes=[pltpu.VMEM((halo + tt, e), F32)],
        compiler_params=_cparams("arbitrary"),
    )(proj, cw)


def _c_mid_bwd(proj, dyz, cw, width, name, tt=256):
    t, e4 = proj.shape
    e = e4 // 4
    halo = SHORT_HALO
    nt = t // tt
    hb = tt // halo

    def body(p_ref, pp_ref, dy_ref, cw_ref, dp_ref, dcw_ref, wbuf, dbuf):
        i = pl.program_id(0)
        ti = nt - 1 - i

        @pl.when(i == 0)
        def _():
            dbuf[pl.ds(tt, halo), :] = jnp.zeros((halo, e), F32)
            dcw_ref[...] = jnp.zeros_like(dcw_ref)

        @pl.when(i > 0)
        def _():
            dbuf[pl.ds(tt, halo), :] = dbuf[pl.ds(0, halo), :]

        keep = (ti > 0).astype(F32)
        wbuf[pl.ds(0, halo), :] = keep * (pp_ref[:, pl.ds(2 * e, e)].astype(F32) * pp_ref[:, pl.ds(0, e)].astype(F32))
        for r0 in range(0, tt, CONV_ROWS):
            rows = pl.ds(r0, CONV_ROWS)
            wbuf[pl.ds(halo + r0, CONV_ROWS), :] = p_ref[rows, pl.ds(2 * e, e)].astype(F32) * p_ref[rows, pl.ds(0, e)].astype(F32)
        for c0 in range(0, e, CONV_COLS):
            for r0 in range(0, tt, CONV_ROWS):
                rows, cols = pl.ds(r0, CONV_ROWS), pl.ds(c0, CONV_COLS)
                cv = _conv_chunk(wbuf, halo, cw_ref, width, r0, c0, False)
                bg = p_ref[rows, pl.ds(e + c0, CONV_COLS)].astype(F32)
                z = p_ref[rows, pl.ds(3 * e + c0, CONV_COLS)].astype(F32)
                sz = _sigmoid(z)
                dyz_c = dy_ref[rows, cols].astype(F32)
                dy = dyz_c * (z * sz)
                dp_ref[rows, pl.ds(3 * e + c0, CONV_COLS)] = (dyz_c * (bg * cv) * (sz * (1.0 + z * (1.0 - sz)))).astype(BF16)
                dp_ref[rows, pl.ds(e + c0, CONV_COLS)] = (dy * cv).astype(BF16)
                dbuf[rows, cols] = dy * bg
        for c0 in range(0, e, CONV_COLS):
            for r0 in range(0, tt, CONV_ROWS):
                rows, cols = pl.ds(r0, CONV_ROWS), pl.ds(c0, CONV_COLS)
                dw = _conv_chunk(dbuf, 0, cw_ref, width, r0, c0, True)
                dp_ref[rows, pl.ds(2 * e + c0, CONV_COLS)] = (dw * p_ref[rows, cols].astype(F32)).astype(BF16)
                dp_ref[rows, cols] = (dw * p_ref[rows, pl.ds(2 * e + c0, CONV_COLS)].astype(F32)).astype(BF16)
        _conv_weight_grad(dcw_ref, dbuf, wbuf, halo, width, tt, e)

    return pl.pallas_call(
        body, name=name, grid=(nt,),
        in_specs=[pl.BlockSpec((tt, e4), lambda i: (nt - 1 - i, 0)),
                  pl.BlockSpec((halo, e4), lambda i: (jnp.maximum((nt - 1 - i) * hb - 1, 0), 0)),
                  pl.BlockSpec((tt, e), lambda i: (nt - 1 - i, 0)), pl.BlockSpec(cw.shape, lambda i: (0, 0))],
        out_specs=[pl.BlockSpec((tt, e4), lambda i: (nt - 1 - i, 0)), pl.BlockSpec(cw.shape, lambda i: (0, 0))],
        out_shape=[SDS((t, e4), BF16), SDS(cw.shape, F32)],
        scratch_shapes=[pltpu.VMEM((halo + tt, e), F32), pltpu.VMEM((tt + halo, e), F32)],
        compiler_params=_cparams("arbitrary"),
    )(proj, proj, dyz, cw)


def _b_prep_fwd(proj, flog, fbias, qg, kg, heads, name, tt=256):
    t, e4 = proj.shape
    e = e4 // 4
    scale = HEAD_DIM ** -0.5 * LOG2E

    def body(q_ref, k_ref, fl_ref, fb_ref, qg_ref, kg_ref, qs_ref, kn_ref, c_ref, ct_ref, carry):
        i = pl.program_id(0)

        @pl.when(i == 0)
        def _():
            carry[...] = jnp.zeros_like(carry)

        for h in range(heads):
            cols = pl.ds(h * HEAD_DIM, HEAD_DIM)
            qh = q_ref[:, cols].astype(F32)
            r = lax.rsqrt(jnp.mean(qh * qh, axis=-1, keepdims=True) + NORM_EPS)
            qs_ref[:, cols] = (((qh * r) * qg_ref[:, cols]) * scale).astype(BF16)
            kh = k_ref[:, cols].astype(F32)
            r = lax.rsqrt(jnp.mean(kh * kh, axis=-1, keepdims=True) + NORM_EPS)
            kn_ref[:, cols] = ((kh * r) * kg_ref[:, cols]).astype(BF16)

        a = fl_ref[...] + fb_ref[...]
        lf = jnp.minimum(a, 0.0) - jnp.log(1.0 + jnp.exp(-jnp.abs(a)))
        tri = (lax.broadcasted_iota(jnp.int32, (tt, tt), 0) >= lax.broadcasted_iota(jnp.int32, (tt, tt), 1)).astype(BF16)
        c = _tri_matmul(tri, lf) + carry[...]
        c_ref[...] = c
        ct_ref[...] = (c * LOG2E).T
        carry[...] = c_ref[pl.ds(tt - 1, 1), :]

    return pl.pallas_call(
        body, name=name, grid=(t // tt,),
        in_specs=[pl.BlockSpec((tt, e), lambda i: (i, 0)), pl.BlockSpec((tt, e), lambda i: (i, 1)),
                  pl.BlockSpec((tt, LANES), lambda i: (i, 0)), pl.BlockSpec((1, LANES), lambda i: (0, 0)),
                  pl.BlockSpec((1, e), lambda i: (0, 0)), pl.BlockSpec((1, e), lambda i: (0, 0))],
        out_specs=[pl.BlockSpec((tt, e), lambda i: (i, 0)), pl.BlockSpec((tt, e), lambda i: (i, 0)),
                   pl.BlockSpec((tt, LANES), lambda i: (i, 0)), pl.BlockSpec((LANES, tt), lambda i: (0, i))],
        out_shape=[SDS((t, e), BF16), SDS((t, e), BF16), SDS((t, LANES), F32), SDS((LANES, t), F32)],
        scratch_shapes=[pltpu.VMEM((1, LANES), F32)],
        compiler_params=_cparams("arbitrary"),
    )(proj, proj, flog, fbias, qg, kg)


ATT_BLOCK = 512
ATT_CHUNK = 256
NEG_BIG = -1e30
LOG2E = 1.4426950408889634
LN2 = 0.6931471805599453


def _flash_fwd(qs, kn, proj, ck, heads, name):
    t, e = qs.shape
    blk = min(ATT_BLOCK, t)
    cw = min(ATT_CHUNK, blk // 2)
    nq = t // blk
    assert blk == 2 * cw

    def body(q_ref, k_ref, v_ref, ck_ref, z_ref, o_ref, y_ref, m_ref, l_ref, s_a, s_b):
        i = pl.program_id(1)
        q = q_ref[...]
        bufs = (s_a, s_b)

        def key_rows(j, c):
            return pl.ds(pl.multiple_of(j * blk, blk) + c * cw, cw)

        def logits(j, c):
            bufs[c][...] = (lax.dot_general(q, k_ref[key_rows(j, c), :], NT_DIMS, preferred_element_type=F32)
                            - ck_ref[j][:, c * cw:(c + 1) * cw])

        def weights(c, m, l, masked):
            s = bufs[c][...]
            if masked:
                keep = lax.broadcasted_iota(jnp.int32, (blk, cw), 0) >= (lax.broadcasted_iota(jnp.int32, (blk, cw), 1) + c * cw)
                s = jnp.where(keep, s, NEG_BIG)
            m_new = jnp.maximum(m, jnp.ceil(jnp.max(s, axis=-1, keepdims=True)))
            alpha = jnp.exp2(m - m_new)
            p = jnp.exp2(s - m_new).astype(BF16)
            return m_new, alpha * l + jnp.sum(p.astype(F32), axis=-1, keepdims=True), alpha, p

        def block(j, carry, masked):
            m, l, acc = carry
            logits(j, 1)
            m, l, alpha0, p0 = weights(0, m, l, masked)
            if not masked:
                logits(j + 1, 0)
            acc = alpha0 * acc + jnp.dot(p0, v_ref[key_rows(j, 0), :], preferred_element_type=F32)
            m, l, alpha1, p1 = weights(1, m, l, masked)
            acc = alpha1 * acc + jnp.dot(p1, v_ref[key_rows(j, 1), :], preferred_element_type=F32)
            return m, l, acc

        logits(0, 0)
        carry = (jnp.full((blk, 1), NEG_BIG, F32), jnp.zeros((blk, 1), F32), jnp.zeros((blk, HEAD_DIM), F32))
        carry = lax.fori_loop(0, i, lambda j, cr: block(j, cr, False), carry)
        m, l, acc = block(i, carry, True)
        o = acc / l
        o_ref[...] = o
        y_ref[...] = (o * _silu(z_ref[...].astype(F32))).astype(BF16)
        m_ref[...] = jnp.broadcast_to(m, (blk, LANES))
        l_ref[...] = jnp.broadcast_to(l, (blk, LANES))

    head_all = pl.BlockSpec((t, HEAD_DIM), lambda h, i: (0, h))
    tile = pl.BlockSpec((blk, HEAD_DIM), lambda h, i: (i, h))
    stat = pl.BlockSpec((None, blk, LANES), lambda h, i: (h, i, 0))
    return pl.pallas_call(
        body, name=name, grid=(heads, nq),
        in_specs=[tile, head_all, pl.BlockSpec((t, HEAD_DIM), lambda h, i: (0, 2 * heads + h)),
                  pl.BlockSpec((None, nq, 1, blk), lambda h, i: (h, 0, 0, 0)),
                  pl.BlockSpec((blk, HEAD_DIM), lambda h, i: (i, 3 * heads + h))],
        out_specs=[tile, tile, stat, stat],
        out_shape=[SDS((t, e), F32), SDS((t, e), BF16), SDS((heads, t, LANES), F32), SDS((heads, t, LANES), F32)],
        scratch_shapes=[pltpu.VMEM((blk, cw), F32), pltpu.VMEM((blk, cw), F32)],
        compiler_params=_cparams("parallel", "arbitrary"),
    )(qs, kn, proj, ck, proj)


def _b_bwd_pre(dyz, o, proj, lstat, heads, name, tt=256):
    t, e = o.shape

    def body(dy_ref, o_ref, z_ref, l_ref, do_ref, dz_ref, dl_ref):
        for h in range(heads):
            cols = pl.ds(h * HEAD_DIM, HEAD_DIM)
            z = z_ref[:, cols].astype(F32)
            sz = _sigmoid(z)
            dy = dy_ref[:, cols].astype(F32)
            of = o_ref[:, cols]
            dos = ((dy * (z * sz)) / l_ref[h][:, 0:1]).astype(BF16)
            do_ref[:, cols] = dos
            dz_ref[:, cols] = (dy * of * (sz * (1.0 + z * (1.0 - sz)))).astype(BF16)
            dl_ref[h] = jnp.broadcast_to(jnp.sum(dos.astype(F32) * of, axis=-1, keepdims=True), (tt, LANES))

    stat = pl.BlockSpec((heads, tt, LANES), lambda i: (0, i, 0))
    return pl.pallas_call(
        body, name=name, grid=(t // tt,),
        in_specs=[pl.BlockSpec((tt, e), lambda i: (i, 0)), pl.BlockSpec((tt, e), lambda i: (i, 0)),
                  pl.BlockSpec((tt, e), lambda i: (i, 3)), stat],
        out_specs=[pl.BlockSpec((tt, e), lambda i: (i, 0)), pl.BlockSpec((tt, e), lambda i: (i, 0)), stat],
        out_shape=[SDS((t, e), BF16), SDS((t, e), BF16), SDS((heads, t, LANES), F32)],
        compiler_params=_cparams("parallel"),
    )(dyz, o, proj, lstat)


def _flash_bwd(qs, kn, proj, ck, dos, mstat, delta, heads, name):
    t, e = qs.shape
    blk = min(ATT_BLOCK, t)
    cw = min(ATT_CHUNK, blk // 2)
    nq = t // blk

    def body(q_ref, do_ref, m_ref, dl_ref, k_ref, v_ref, ck_ref, dq_ref, dk_ref, dv_ref, dc_ref, s_a, s_b, d_a, d_b):
        i = pl.program_id(1)

        @pl.when(i == 0)
        def _():
            dk_ref[...] = jnp.zeros_like(dk_ref)
            dv_ref[...] = jnp.zeros_like(dv_ref)
            dc_ref[...] = jnp.zeros_like(dc_ref)

        q = q_ref[...]
        do = do_ref[...]
        mrow = m_ref[:, 0:1]
        dl = dl_ref[:, 0:1]
        sbuf, dbuf = (s_a, s_b), (d_a, d_b)

        def key_rows(j, c):
            return pl.ds(pl.multiple_of(j * blk, blk) + c * cw, cw)

        def products(j, c):
            rows = key_rows(j, c)
            sbuf[c][...] = (lax.dot_general(q, k_ref[rows, :], NT_DIMS, preferred_element_type=F32)
                            - ck_ref[j][:, c * cw:(c + 1) * cw])
            dbuf[c][...] = lax.dot_general(do, v_ref[rows, :], NT_DIMS, preferred_element_type=F32)

        def weights(c, masked):
            p = jnp.exp2(sbuf[c][...] - mrow)
            if masked:
                keep = lax.broadcasted_iota(jnp.int32, (blk, cw), 0) >= (lax.broadcasted_iota(jnp.int32, (blk, cw), 1) + c * cw)
                p = jnp.where(keep, p, 0.0)
            p = p.astype(BF16)
            ds = p.astype(F32) * (dbuf[c][...] - dl)
            return p, ds.astype(BF16), jnp.sum(ds, axis=0, keepdims=True)

        def outputs(j, c, p, dsb, colsum, dq):
            rows = key_rows(j, c)
            dv_ref[rows, :] += lax.dot_general(p, do, TN_DIMS, preferred_element_type=F32)
            dk_ref[rows, :] += lax.dot_general(dsb, q, TN_DIMS, preferred_element_type=F32)
            dc_ref[j, :, pl.ds(c * cw, cw)] -= colsum
            return dq + jnp.dot(dsb, k_ref[rows, :], preferred_element_type=F32)

        def block(j, dq, masked):
            products(j, 1)
            p0, ds0, cs0 = weights(0, masked)
            if not masked:
                products(j + 1, 0)
            dq = outputs(j, 0, p0, ds0, cs0, dq)
            p1, ds1, cs1 = weights(1, masked)
            return outputs(j, 1, p1, ds1, cs1, dq)

        products(0, 0)
        dq = lax.fori_loop(0, i, lambda j, acc: block(j, acc, False), jnp.zeros((blk, HEAD_DIM), F32))
        dq_ref[...] = block(i, dq, True)

    tile = pl.BlockSpec((blk, HEAD_DIM), lambda h, i: (i, h))
    stat = pl.BlockSpec((None, blk, LANES), lambda h, i: (h, i, 0))
    head_all = pl.BlockSpec((t, HEAD_DIM), lambda h, i: (0, h))
    cspec = pl.BlockSpec((None, nq, 1, blk), lambda h, i: (h, 0, 0, 0))
    return pl.pallas_call(
        body, name=name, grid=(heads, nq),
        in_specs=[tile, tile, stat, stat, head_all, pl.BlockSpec((t, HEAD_DIM), lambda h, i: (0, 2 * heads + h)), cspec],
        out_specs=[tile, head_all, head_all, cspec],
        out_shape=[SDS((t, e), F32), SDS((t, e), F32), SDS((t, e), F32), SDS((heads, nq, 1, blk), F32)],
        scratch_shapes=[pltpu.VMEM((blk, cw), F32)] * 4,
        compiler_params=_cparams("parallel", "arbitrary"),
    )(qs, dos, mstat, delta, kn, proj, ck)


def _b_prep_bwd(dqs, dkn, dv, dz, proj, qg, kg, dct, flog, fbias, heads, name, tt=256):
    t, e4 = proj.shape
    e = e4 // 4
    nt = t // tt
    scale = HEAD_DIM ** -0.5

    def body(dq_ref, dk_ref, dv_ref, dz_ref, q_ref, k_ref, qg_ref, kg_ref, dc_ref, fl_ref, fb_ref,
             dp_ref, dqg_ref, dkg_ref, dfb_ref, carry, dlf, acc_q, acc_k, acc_f):
        i = pl.program_id(0)

        @pl.when(i == 0)
        def _():
            carry[...] = jnp.zeros_like(carry)
            acc_q[...] = jnp.zeros_like(acc_q)
            acc_k[...] = jnp.zeros_like(acc_k)
            acc_f[...] = jnp.zeros_like(acc_f)

        for h in range(heads):
            cols = pl.ds(h * HEAD_DIM, HEAD_DIM)
            for src_ref, d_ref, g_ref, acc, mult, off in ((q_ref, dq_ref, qg_ref, acc_q, scale, 0),
                                                          (k_ref, dk_ref, kg_ref, acc_k, LN2, e)):
                xf = src_ref[:, cols].astype(F32)
                r = lax.rsqrt(jnp.mean(xf * xf, axis=-1, keepdims=True) + NORM_EPS)
                xh = xf * r
                dn = d_ref[:, cols] * mult
                acc[...] += _rows8(dn * xh)
                dxh = dn * g_ref[:, cols]
                dp_ref[:, pl.ds(off + h * HEAD_DIM, HEAD_DIM)] = (
                    r * (dxh - xh * jnp.mean(dxh * xh, axis=-1, keepdims=True))).astype(BF16)
        dp_ref[:, pl.ds(2 * e, e)] = dv_ref[...].astype(BF16)
        dp_ref[:, pl.ds(3 * e, e)] = dz_ref[...]

        tri = (lax.broadcasted_iota(jnp.int32, (tt, tt), 0) <= lax.broadcasted_iota(jnp.int32, (tt, tt), 1)).astype(BF16)
        dlf[...] = _tri_matmul(tri, dc_ref[...]) + carry[...]
        carry[...] = dlf[pl.ds(0, 1), :]
        a = fl_ref[...] + fb_ref[...]
        dfl = dlf[...] * _sigmoid(-a)
        dp_ref[:, pl.ds(4 * e, LANES)] = dfl.astype(BF16)
        acc_f[...] += _rows8(dfl)

        @pl.when(i == nt - 1)
        def _():
            dqg_ref[...] = jnp.sum(acc_q[...], axis=0, keepdims=True)
            dkg_ref[...] = jnp.sum(acc_k[...], axis=0, keepdims=True)
            dfb_ref[...] = jnp.sum(acc_f[...], axis=0, keepdims=True)

    rev = lambda i: (nt - 1 - i, 0)
    vec_e = pl.BlockSpec((1, e), lambda i: (0, 0))
    vec = pl.BlockSpec((1, LANES), lambda i: (0, 0))
    wide = pl.BlockSpec((tt, e), rev)
    lane = pl.BlockSpec((tt, LANES), rev)
    return pl.pallas_call(
        body, name=name, grid=(nt,),
        in_specs=[wide, wide, wide, wide, wide, pl.BlockSpec((tt, e), lambda i: (nt - 1 - i, 1)), vec_e, vec_e, lane, lane, vec],
        out_specs=[pl.BlockSpec((tt, e4 + LANES), rev), vec, vec, vec],
        out_shape=[SDS((t, e4 + LANES), BF16), SDS((1, LANES), F32), SDS((1, LANES), F32), SDS((1, LANES), F32)],
        scratch_shapes=[pltpu.VMEM((1, LANES), F32), pltpu.VMEM((tt, LANES), F32), pltpu.VMEM((SUBLANES, LANES), F32),
                        pltpu.VMEM((SUBLANES, LANES), F32), pltpu.VMEM((SUBLANES, LANES), F32)],
        compiler_params=_cparams("arbitrary"),
    )(dqs, dkn, dv, dz, proj, proj, qg, kg, dct, flog, fbias)


def _b_fwd(h, b_norm, wb_pad, wb_out, qg, kg, fbias, heads, tag):
    t = h.shape[0]
    e = wb_out.shape[0]
    blk = min(ATT_BLOCK, t)
    hn, proj = _norm_matmul(h, b_norm, wb_pad[:, :4 * e], f"b_in_proj_{tag}")
    flog = _matmul_f32out(hn, wb_pad[:, 4 * e:], f"b_forget_proj_{tag}")
    qs, kn, _, ct = _b_prep_fwd(proj, flog, fbias, qg, kg, heads, f"b_prep_fwd_{tag}")
    ck = ct.reshape(LANES, t // blk, 1, blk)
    o, y, mstat, lstat = _flash_fwd(qs, kn, proj, ck, heads, f"b_attention_fwd_{tag}")
    sv = dict(x=h, hn=hn, proj=proj, flog=flog, qs=qs, kn=kn, ck=ck, o=o, y=y, mstat=mstat, lstat=lstat)
    return _out_matmul_residual(y, wb_out, h, f"b_out_proj_{tag}"), sv


def _b_bwd(dh, sv, b_norm, wb_pad, wb_out, qg, kg, fbias, heads, tag):
    t = dh.shape[0]
    e = wb_out.shape[0]
    gb = {}
    dyz = _matmul_nt(dh, wb_out, f"b_out_bwd_{tag}")
    gb["w_out"] = _matmul_tn(sv["y"], dh, f"b_out_wgrad_{tag}")
    dos, dz, delta = _b_bwd_pre(dyz, sv["o"], sv["proj"], sv["lstat"], heads, f"b_gate_bwd_{tag}")
    dqs, dkn, dv, dc = _flash_bwd(sv["qs"], sv["kn"], sv["proj"], sv["ck"], dos, sv["mstat"], delta, heads,
                                  f"b_attention_bwd_{tag}")
    dct = jnp.pad(dc.reshape(heads, t).T, ((0, 0), (0, LANES - heads)))
    dproj, dqg, dkg, dfb = _b_prep_bwd(dqs, dkn, dv, dz, sv["proj"], qg, kg, dct, sv["flog"], fbias, heads,
                                       f"b_prep_bwd_{tag}")
    gb["w_in"] = _matmul_tn(sv["hn"], dproj, f"b_in_wgrad_{tag}")[:, :4 * e + heads]
    dh, dg = _dproj_matmul_normbwd(dproj, wb_pad, sv["x"], b_norm, dh, f"b_in_bwd_{tag}")
    gb["norm"], gb["q_norm"], gb["k_norm"], gb["f_bias"] = dg, dqg, dkg, dfb[:, :heads]
    return dh, gb


def _sum_adamw(recv, w, m, v, name, tr=256):
    nl, r, c = w.shape
    tr = min(tr, r)

    def body(g_ref, w_ref, m_ref, v_ref, go_ref, d_ref, mo_ref, vo_ref):
        g = g_ref[0].astype(F32)
        for s in range(1, N_DEV):
            g = g + g_ref[s].astype(F32)
        go_ref[...] = g
        mn = ADAM_B1 * m_ref[...] + (1.0 - ADAM_B1) * g
        vn = ADAM_B2 * v_ref[...] + (1.0 - ADAM_B2) * (g * g)
        m_hat = mn / (1.0 - ADAM_B1 ** ADAM_STEP)
        v_hat = vn / (1.0 - ADAM_B2 ** ADAM_STEP)
        d_ref[...] = -ADAM_LR * (m_hat / (jnp.sqrt(v_hat) + ADAM_EPS) + ADAM_WD * w_ref[...])
        mo_ref[...] = mn
        vo_ref[...] = vn

    blk = pl.BlockSpec((None, tr, c), lambda l, i: (l, i, 0))
    return pl.pallas_call(
        body, name=name, grid=(nl, r // tr),
        in_specs=[pl.BlockSpec((N_DEV, None, tr, c), lambda l, i: (0, l, i, 0)), blk, blk, blk],
        out_specs=[blk, blk, blk, blk],
        out_shape=[SDS(w.shape, F32)] * 4,
        compiler_params=_cparams("parallel", "parallel"),
    )(recv, w, m, v)


def _unshard(g, axis):
    g = jnp.moveaxis(g, 0, axis)
    return g.reshape(g.shape[:axis] + (g.shape[axis] * g.shape[axis + 1],) + g.shape[axis + 2:])


def _to_slabs(full, axis):
    n = full.shape[axis]
    s = full.reshape(full.shape[:axis] + (N_DEV, n // N_DEV) + full.shape[axis + 1:])
    return jnp.moveaxis(s, axis, 0)


def _pack_rows(parts, lead):
    flat = [p.reshape(p.shape[:lead] + (-1,)) for p in parts]
    cat = jnp.concatenate(flat, axis=-1)
    n = cat.shape[-1]
    pad = (-n) % (SUBLANES * LANES)
    cat = jnp.pad(cat, [(0, 0)] * lead + [(0, pad)])
    return cat.reshape(cat.shape[:lead] + ((n + pad) // LANES, LANES))


def _unpack_rows(packed, shapes, lead):
    flat = packed.reshape(packed.shape[:lead] + (-1,))
    out, off = [], 0
    for shp in shapes:
        size = int(np.prod(shp))
        out.append(flat[..., off:off + size].reshape(packed.shape[:lead] + tuple(shp)))
        off += size
    return out


def _pad_rows(w, rows):
    return jnp.pad(w, ((0, rows - w.shape[0]), (0, 0)))


def kernel(x, a_norm, a_w_in, a_conv_w, a_conv_b, a_ln_g, a_ln_b, a_w_out, b_norm, b_w_in, b_f_bias, b_q_norm, b_k_norm, b_w_out, c_norm, c_w_in, c_conv_w, c_w_out, loss_target, m_a_norm, m_a_w_in, m_a_conv_w, m_a_conv_b, m_a_ln_g, m_a_ln_b, m_a_w_out, m_b_norm, m_b_w_in, m_b_f_bias, m_b_q_norm, m_b_k_norm, m_b_w_out, m_c_norm, m_c_w_in, m_c_conv_w, m_c_w_out, v_a_norm, v_a_w_in, v_a_conv_w, v_a_conv_b, v_a_ln_g, v_a_ln_b, v_a_w_out, v_b_norm, v_b_w_in, v_b_f_bias, v_b_q_norm, v_b_k_norm, v_b_w_out, v_c_norm, v_c_w_in, v_c_conv_w, v_c_w_out):
    t, d = x.shape[1], x.shape[2]
    e = a_w_out.shape[1] * N_DEV
    heads = b_f_bias.shape[1]
    n_a, n_b, n_c = a_norm.shape[0], b_norm.shape[0], c_norm.shape[0]
    depth = n_a + n_b + n_c
    ka, kc = a_conv_w.shape[1], c_conv_w.shape[1]
    assert e == heads * HEAD_DIM and n_b == 1 and n_c == 1 and x.shape[0] == 1

    small_names = ["a_norm", "a_conv_w", "a_conv_b", "a_ln_g", "a_ln_b", "c_norm", "c_conv_w"]
    small = dict(a_norm=a_norm, a_conv_w=a_conv_w, a_conv_b=a_conv_b, a_ln_g=a_ln_g, a_ln_b=a_ln_b,
                 c_norm=c_norm, c_conv_w=c_conv_w)
    small_pack = _pack_rows([small[k] for k in small_names], 0)
    mats = [a_w_in, a_w_out, b_w_in, b_w_out, c_w_in, c_w_out]
    gathered = _all_gather([w.astype(BF16) for w in mats] + [small_pack], "all_gather_weights")
    wa_in = _unshard(gathered[0], 2)
    wa_out = _unshard(gathered[1], 1)
    wb_in = _unshard(gathered[2], 2)[0]
    wb_out = _unshard(gathered[3], 1)[0]
    wc_in = _unshard(gathered[4], 2)[0]
    wc_out = _unshard(gathered[5], 1)[0]
    wb_pad = jnp.pad(wb_in, ((0, 0), (0, LANES - heads)))
    sm = _unpack_rows(gathered[6], [small[k].shape for k in small_names], 1)
    g_a_norm = _unshard(sm[0], 1)
    g_a_conv_w = _unshard(sm[1], 2)
    g_a_conv_b = _unshard(sm[2], 1)
    g_a_ln_g = _unshard(sm[3], 1)
    g_a_ln_b = _unshard(sm[4], 1)
    g_c_norm = _unshard(sm[5], 1)
    g_c_conv_w = _unshard(sm[6], 2)

    cw_a = [_pad_rows(g_a_conv_w[j], CONF_HALO) for j in range(n_a)]
    cw_c = _pad_rows(g_c_conv_w[0], SHORT_HALO)
    qg = jnp.tile(b_q_norm, (1, heads))
    kg = jnp.tile(b_k_norm, (1, heads))
    fbias = jnp.pad(b_f_bias, ((0, 0), (0, LANES - heads)))

    h = x[0]
    saved = []
    for i in range(depth):
        kind, j = i % 3, i // 3
        tag = f"l{i}"
        if kind == 0:
            hn, proj = _norm_matmul(h, g_a_norm[j:j + 1], wa_in[j], f"a_in_proj_{tag}")
            y, u1 = _a_mid_fwd(proj, cw_a[j], g_a_conv_b[j:j + 1], g_a_ln_g[j:j + 1], g_a_ln_b[j:j + 1], ka, f"a_mid_fwd_{tag}")
            saved.append(dict(x=h, hn=hn, proj=proj, u1=u1, y=y))
            h = _out_matmul_residual(y, wa_out[j], h, f"a_out_proj_{tag}")
        elif kind == 1:
            h, sv = _b_fwd(h, b_norm, wb_pad, wb_out, qg, kg, fbias, heads, tag)
            saved.append(sv)
        else:
            hn, proj = _norm_matmul(h, g_c_norm, wc_in, f"c_in_proj_{tag}")
            y = _c_mid_fwd(proj, cw_c, kc, f"c_mid_fwd_{tag}")
            saved.append(dict(x=h, hn=hn, proj=proj, y=y))
            h = _out_matmul_residual(y, wc_out, h, f"c_out_proj_{tag}")

    loss_part, dh = _loss_grad(h, loss_target[0], "loss_and_grad")
    loss = lax.psum(jnp.sum(loss_part), ("x", "y", "c"))

    ga = dict(norm=[None] * n_a, w_in=[None] * n_a, conv_w=[None] * n_a, conv_b=[None] * n_a, ln_g=[None] * n_a,
              ln_b=[None] * n_a, w_out=[None] * n_a)
    gb, gc = None, {}
    for i in reversed(range(depth)):
        kind, j = i % 3, i // 3
        tag = f"l{i}"
        sv = saved[i]
        if kind == 0:
            dyz = _matmul_nt(dh, wa_out[j], f"a_out_bwd_{tag}")
            ga["w_out"][j] = _matmul_tn(sv["y"], dh, f"a_out_wgrad_{tag}")
            dproj, dcw, dcb, dlg, dlb = _a_mid_bwd(sv["proj"], sv["u1"], dyz, cw_a[j], g_a_ln_g[j:j + 1], g_a_ln_b[j:j + 1],
                                                   ka, f"a_mid_bwd_{tag}")
            ga["w_in"][j] = _matmul_tn(sv["hn"], dproj, f"a_in_wgrad_{tag}")
            dh, dg = _dproj_matmul_normbwd(dproj, wa_in[j], sv["x"], g_a_norm[j:j + 1], dh, f"a_in_bwd_{tag}")
            ga["norm"][j], ga["conv_w"][j], ga["conv_b"][j], ga["ln_g"][j], ga["ln_b"][j] = dg[0], dcw[:ka], dcb[0], dlg[0], dlb[0]
        elif kind == 1:
            dh, gb = _b_bwd(dh, sv, b_norm, wb_pad, wb_out, qg, kg, fbias, heads, tag)
        else:
            dyz = _matmul_nt(dh, wc_out, f"c_out_bwd_{tag}")
            gc["w_out"] = _matmul_tn(sv["y"], dh, f"c_out_wgrad_{tag}")
            dproj, dcw = _c_mid_bwd(sv["proj"], dyz, cw_c, kc, f"c_mid_bwd_{tag}")
            gc["w_in"] = _matmul_tn(sv["hn"], dproj, f"c_in_wgrad_{tag}")
            dh, dg = _dproj_matmul_normbwd(dproj, wc_in, sv["x"], g_c_norm, dh, f"c_in_bwd_{tag}")
            gc["norm"], gc["conv_w"] = dg, dcw[:kc][None]
    grad_x = dh[None]

    mat_slabs = [_to_slabs(jnp.stack(ga["w_in"]), 2), _to_slabs(jnp.stack(ga["w_out"]), 1),
                 _to_slabs(gb["w_in"][None], 2), _to_slabs(gb["w_out"][None], 1),
                 _to_slabs(gc["w_in"][None], 2), _to_slabs(gc["w_out"][None], 1)]
    sharded_small = [(jnp.stack(ga["norm"]), 1), (jnp.stack(ga["conv_w"]), 2), (jnp.stack(ga["conv_b"]), 1),
                     (jnp.stack(ga["ln_g"]), 1), (jnp.stack(ga["ln_b"]), 1), (gc["norm"], 1), (gc["conv_w"], 2)]
    repl_small = [gb["norm"], gb["f_bias"], gb["q_norm"], gb["k_norm"]]
    small_slabs = _pack_rows([_to_slabs(g, ax) for g, ax in sharded_small]
                             + [jnp.broadcast_to(g[None], (N_DEV,) + g.shape) for g in repl_small], 1)
    recv = _exchange(mat_slabs + [small_slabs], "exchange_gradients")

    outs = {}
    mat_names = ["a_w_in", "a_w_out", "b_w_in", "b_w_out", "c_w_in", "c_w_out"]
    mat_w = dict(a_w_in=(a_w_in, m_a_w_in, v_a_w_in), a_w_out=(a_w_out, m_a_w_out, v_a_w_out),
                 b_w_in=(b_w_in, m_b_w_in, v_b_w_in), b_w_out=(b_w_out, m_b_w_out, v_b_w_out),
                 c_w_in=(c_w_in, m_c_w_in, v_c_w_in), c_w_out=(c_w_out, m_c_w_out, v_c_w_out))
    for k, name in enumerate(mat_names):
        outs[name] = _sum_adamw(recv[k], *mat_w[name], f"adamw_{name}")

    small_order = small_names + ["b_norm", "b_f_bias", "b_q_norm", "b_k_norm"]
    small_w = dict(a_norm=(a_norm, m_a_norm, v_a_norm), a_conv_w=(a_conv_w, m_a_conv_w, v_a_conv_w),
                   a_conv_b=(a_conv_b, m_a_conv_b, v_a_conv_b), a_ln_g=(a_ln_g, m_a_ln_g, v_a_ln_g),
                   a_ln_b=(a_ln_b, m_a_ln_b, v_a_ln_b), c_norm=(c_norm, m_c_norm, v_c_norm),
                   c_conv_w=(c_conv_w, m_c_conv_w, v_c_conv_w), b_norm=(b_norm, m_b_norm, v_b_norm),
                   b_f_bias=(b_f_bias, m_b_f_bias, v_b_f_bias), b_q_norm=(b_q_norm, m_b_q_norm, v_b_q_norm),
                   b_k_norm=(b_k_norm, m_b_k_norm, v_b_k_norm))
    packs = [_pack_rows([small_w[k][q] for k in small_order], 0)[None] for q in range(3)]
    small_out = _sum_adamw(recv[6][:, None], *packs, "adamw_vectors")
    shapes = [small_w[k][0].shape for k in small_order]
    unpacked = [_unpack_rows(o[0], shapes, 0) for o in small_out]
    for idx, name in enumerate(small_order):
        outs[name] = tuple(unpacked[q][idx] for q in range(4))

    order = ["a_norm", "a_w_in", "a_conv_w", "a_conv_b", "a_ln_g", "a_ln_b", "a_w_out", "b_norm", "b_w_in", "b_f_bias",
             "b_q_norm", "b_k_norm", "b_w_out", "c_norm", "c_w_in", "c_conv_w", "c_w_out"]
    return (loss, grad_x, *[outs[k][0] for k in order], *[outs[k][1] for k in order],
            *[outs[k][2] for k in order], *[outs[k][3] for k in order])
```

```python
import jax
import jax.numpy as jnp
import numpy as np
from jax import lax
from jax.experimental import pallas as pl
from jax.experimental.pallas import tpu as pltpu

F32 = jnp.float32
BF16 = jnp.bfloat16
SDS = jax.ShapeDtypeStruct

NORM_EPS = 1e-6
ADAM_LR = 0.001
ADAM_B1 = 0.9
ADAM_B2 = 0.999
ADAM_EPS = 1e-08
ADAM_WD = 0.01
ADAM_STEP = 10

N_DEV = 8
LANES = 128
SUBLANES = 8
HEAD_DIM = 128
CONF_HALO = 32
SHORT_HALO = 8
VMEM_LIMIT = 56 * 1024 * 1024

NT_DIMS = (((1,), (1,)), ((), ()))
TN_DIMS = (((0,), (0,)), ((), ()))
MESH = pl.DeviceIdType.MESH
ANY = pl.BlockSpec(memory_space=pl.ANY)


def _cparams(*sem):
    return pltpu.CompilerParams(dimension_semantics=sem, vmem_limit_bytes=VMEM_LIMIT)


def _divisor_tile(n, cap):
    return max(m for m in range(LANES, min(n, cap) + 1, LANES) if n % m == 0)


def _sigmoid(x):
    return 1.0 / (1.0 + jnp.exp(-x))


def _silu(x):
    return x * _sigmoid(x)


def _dsilu(x):
    s = _sigmoid(x)
    return s * (1.0 + x * (1.0 - s))


def _rows8(v):
    out = v[0:SUBLANES]
    for a in range(1, v.shape[0] // SUBLANES):
        out = out + v[a * SUBLANES:(a + 1) * SUBLANES]
    return out


def _split3(v):
    hi = v.astype(BF16)
    r1 = v - hi.astype(F32)
    mid = r1.astype(BF16)
    lo = (r1 - mid.astype(F32)).astype(BF16)
    return hi, mid, lo


def _tri_matmul(tri, v):
    hi, mid, lo = _split3(v)
    return (jnp.dot(tri, hi, preferred_element_type=F32) + jnp.dot(tri, mid, preferred_element_type=F32)
            + jnp.dot(tri, lo, preferred_element_type=F32))


def _position():
    return lax.axis_index("x"), lax.axis_index("y"), lax.axis_index("c")


def _all_gather(shards, name):
    n = len(shards)

    def body(*refs):
        xs, outs = refs[:n], refs[n:2 * n]
        send_sems, recv_sems, local_sems = refs[2 * n:]
        x, y, c = _position()
        me, sibling = (x, y, c), (x, y, 1 - c)
        chips = [(1 - x, y), (x, 1 - y), (1 - x, 1 - y)]

        def slot(a, px, py, pc):
            return outs[a].at[4 * px + 2 * py + pc]

        def copy(a, k, block, to, src=None):
            return pltpu.make_async_remote_copy(
                src_ref=slot(a, *block) if src is None else src, dst_ref=slot(a, *block),
                send_sem=send_sems.at[a, k], recv_sem=recv_sems.at[a, k], device_id=to, device_id_type=MESH)

        started = []
        mine = []
        for a in range(n):
            cp = pltpu.make_async_copy(xs[a], slot(a, *me), local_sems.at[a])
            cp.start()
            mine.append(cp)
        for a in range(n):
            first = [copy(a, 0, me, sibling, src=xs[a])]
            first += [copy(a, 1 + j, me, (*chip, c), src=xs[a]) for j, chip in enumerate(chips)]
            for cp in first:
                cp.start()
            started += first
        for a in range(n):
            for j, chip in enumerate(chips):
                copy(a, 1 + j, (*chip, c), me).wait_recv()
                fwd = copy(a, 4 + j, (*chip, c), sibling)
                fwd.start()
                started.append(fwd)
        for a in range(n):
            copy(a, 0, sibling, me).wait_recv()
            for j, chip in enumerate(chips):
                copy(a, 4 + j, (*chip, 1 - c), me).wait_recv()
        for cp in started:
            cp.wait_send()
        for cp in mine:
            cp.wait()

    return pl.pallas_call(
        body, name=name,
        out_shape=[SDS((N_DEV,) + s.shape, s.dtype) for s in shards],
        in_specs=[ANY] * n, out_specs=[ANY] * n,
        scratch_shapes=[pltpu.SemaphoreType.DMA((n, 7)), pltpu.SemaphoreType.DMA((n, 7)), pltpu.SemaphoreType.DMA((n,))],
    )(*shards)


def _exchange(slabs, name):
    n = len(slabs)

    def body(*refs):
        ins, outs = refs[:n], refs[n:2 * n]
        send_sems, recv_sems, local_sems = refs[2 * n:]
        x, y, c = _position()
        me = 4 * x + 2 * y + c
        peers = [(x ^ bx, y ^ by, c ^ bc) for bx in (0, 1) for by in (0, 1) for bc in (0, 1)][1:]

        def copy(a, k, peer):
            pid = 4 * peer[0] + 2 * peer[1] + peer[2]
            return pltpu.make_async_remote_copy(
                src_ref=ins[a].at[pid], dst_ref=outs[a].at[me],
                send_sem=send_sems.at[a, k], recv_sem=recv_sems.at[a, k], device_id=peer, device_id_type=MESH)

        def arrival(a, k, peer):
            pid = 4 * peer[0] + 2 * peer[1] + peer[2]
            return pltpu.make_async_remote_copy(
                src_ref=ins[a].at[pid], dst_ref=outs[a].at[pid],
                send_sem=send_sems.at[a, k], recv_sem=recv_sems.at[a, k], device_id=peer, device_id_type=MESH)

        mine = []
        for a in range(n):
            cp = pltpu.make_async_copy(ins[a].at[me], outs[a].at[me], local_sems.at[a])
            cp.start()
            mine.append(cp)
        started = []
        for a in range(n):
            for k, peer in enumerate(peers):
                cp = copy(a, k, peer)
                cp.start()
                started.append(cp)
        for a in range(n):
            for k, peer in enumerate(peers):
                arrival(a, k, peer).wait_recv()
        for cp in started:
            cp.wait_send()
        for cp in mine:
            cp.wait()

    return pl.pallas_call(
        body, name=name,
        out_shape=[SDS(s.shape, s.dtype) for s in slabs],
        in_specs=[ANY] * n, out_specs=[ANY] * n,
        scratch_shapes=[pltpu.SemaphoreType.DMA((n, 7)), pltpu.SemaphoreType.DMA((n, 7)), pltpu.SemaphoreType.DMA((n,))],
    )(*slabs)


class _Ride:
    def __init__(self, kind, arrays):
        self.kind, self.arrays, self.n = kind, list(arrays), len(arrays)
        self.in_specs = [ANY] * self.n
        self.out_specs = [ANY] * self.n
        self.out_shape = [SDS(((N_DEV,) + a.shape) if kind == "gather" else a.shape, a.dtype) for a in self.arrays]
        self.scratch = [pltpu.SemaphoreType.DMA((self.n, 7)), pltpu.SemaphoreType.DMA((self.n, 7)),
                        pltpu.SemaphoreType.DMA((self.n,))]

    def _copies(self, ins, outs, sems, arriving):
        send_sems, recv_sems, local_sems = sems
        x, y, c = _position()
        me = 4 * x + 2 * y + c
        peers = [(x ^ bx, y ^ by, c ^ bc) for bx in (0, 1) for by in (0, 1) for bc in (0, 1)][1:]
        local, remote = [], []
        for a in range(self.n):
            own = ins[a] if self.kind == "gather" else ins[a].at[me]
            local.append(pltpu.make_async_copy(own, outs[a].at[me], local_sems.at[a]))
            for k, peer in enumerate(peers):
                pid = 4 * peer[0] + 2 * peer[1] + peer[2]
                remote.append(pltpu.make_async_remote_copy(
                    src_ref=ins[a] if self.kind == "gather" else ins[a].at[pid],
                    dst_ref=outs[a].at[pid if arriving else me],
                    send_sem=send_sems.at[a, k], recv_sem=recv_sems.at[a, k], device_id=peer, device_id_type=MESH))
        return local, remote

    def start(self, ins, outs, sems):
        local, sends = self._copies(ins, outs, sems, False)
        for cp in local + sends:
            cp.start()

    def wait(self, ins, outs, sems):
        local, arrivals = self._copies(ins, outs, sems, True)
        for cp in arrivals:
            cp.wait_recv()
        for cp in arrivals:
            cp.wait_send()
        for cp in local:
            cp.wait()


def _norm_matmul(x, g, w, name, tm=512, tn=1024, ride=None):
    t, d = x.shape
    n = w.shape[1]
    tn = min(tn, n)
    ni, nj = t // tm, n // tn
    nr = ride.n if ride else 0

    def body(*refs):
        x_ref, g_ref, w_ref = refs[:3]
        hn_ref, o_ref = refs[3 + nr:5 + nr]
        rin, rout, sems = refs[3:3 + nr], refs[5 + nr:5 + 2 * nr], refs[5 + 2 * nr:]
        i, j = pl.program_id(0), pl.program_id(1)
        if ride:
            @pl.when((i == 0) & (j == 0))
            def _():
                ride.start(rin, rout, sems)

        @pl.when(j == 0)
        def _():
            xf = x_ref[...]
            r = lax.rsqrt(jnp.mean(xf * xf, axis=-1, keepdims=True) + NORM_EPS)
            hn_ref[...] = ((xf * r) * g_ref[...]).astype(BF16)

        o_ref[...] = jnp.dot(hn_ref[...], w_ref[...], preferred_element_type=F32).astype(o_ref.dtype)
        if ride:
            @pl.when((i == ni - 1) & (j == nj - 1))
            def _():
                ride.wait(rin, rout, sems)

    return pl.pallas_call(
        body, name=name, grid=(ni, nj),
        in_specs=[pl.BlockSpec((tm, d), lambda i, j: (i, 0)), pl.BlockSpec((1, d), lambda i, j: (0, 0)),
                  pl.BlockSpec((d, tn), lambda i, j: (0, j))] + (ride.in_specs if ride else []),
        out_specs=[pl.BlockSpec((tm, d), lambda i, j: (i, 0)), pl.BlockSpec((tm, tn), lambda i, j: (i, j))]
        + (ride.out_specs if ride else []),
        out_shape=[SDS((t, d), BF16), SDS((t, n), BF16)] + (ride.out_shape if ride else []),
        scratch_shapes=ride.scratch if ride else [],
        compiler_params=_cparams("arbitrary", "arbitrary") if ride else _cparams("parallel", "arbitrary"),
    )(x, g, w, *(ride.arrays if ride else []))


def _matmul_f32out(a, w, name, tm=512):
    t, k = a.shape
    n = w.shape[1]

    def body(a_ref, w_ref, o_ref):
        o_ref[...] = jnp.dot(a_ref[...], w_ref[...], preferred_element_type=F32)

    return pl.pallas_call(
        body, name=name, grid=(t // tm,),
        in_specs=[pl.BlockSpec((tm, k), lambda i: (i, 0)), pl.BlockSpec((k, n), lambda i: (0, 0))],
        out_specs=pl.BlockSpec((tm, n), lambda i: (i, 0)),
        out_shape=SDS((t, n), F32),
        compiler_params=_cparams("parallel"),
    )(a, w)


def _out_matmul_residual(y, w, x, name, tm=512):
    t, e = y.shape
    d = w.shape[1]

    def body(y_ref, w_ref, x_ref, o_ref):
        o_ref[...] = x_ref[...] + jnp.dot(y_ref[...], w_ref[...], preferred_element_type=F32)

    return pl.pallas_call(
        body, name=name, grid=(t // tm,),
        in_specs=[pl.BlockSpec((tm, e), lambda i: (i, 0)), pl.BlockSpec((e, d), lambda i: (0, 0)),
                  pl.BlockSpec((tm, d), lambda i: (i, 0))],
        out_specs=pl.BlockSpec((tm, d), lambda i: (i, 0)),
        out_shape=SDS((t, d), F32),
        compiler_params=_cparams("parallel"),
    )(y, w, x)


def _matmul_nt(a, w, name, tm=512):
    t, d = a.shape
    e = w.shape[0]

    def body(a_ref, w_ref, o_ref):
        o_ref[...] = lax.dot_general(a_ref[...].astype(BF16), w_ref[...], NT_DIMS,
                                     preferred_element_type=F32).astype(o_ref.dtype)

    return pl.pallas_call(
        body, name=name, grid=(t // tm,),
        in_specs=[pl.BlockSpec((tm, d), lambda i: (i, 0)), pl.BlockSpec((e, d), lambda i: (0, 0))],
        out_specs=pl.BlockSpec((tm, e), lambda i: (i, 0)),
        out_shape=SDS((t, e), BF16),
        compiler_params=_cparams("parallel"),
    )(a, w)


def _matmul_tn(a, b, name, out_dtype=BF16, tm=1024, tn=1024, tk=512):
    t, m = a.shape
    n = b.shape[1]
    tm, tn = min(tm, m), _divisor_tile(n, 2 * tn)
    nk = t // tk

    def body(a_ref, b_ref, o_ref, acc_ref):
        k = pl.program_id(2)

        @pl.when(k == 0)
        def _():
            acc_ref[...] = jnp.zeros_like(acc_ref)

        acc_ref[...] += lax.dot_general(a_ref[...].astype(BF16), b_ref[...].astype(BF16), TN_DIMS,
                                        preferred_element_type=F32)

        @pl.when(k == nk - 1)
        def _():
            o_ref[...] = acc_ref[...].astype(o_ref.dtype)

    return pl.pallas_call(
        body, name=name, grid=(m // tm, n // tn, nk),
        in_specs=[pl.BlockSpec((tk, tm), lambda i, j, k: (k, i)), pl.BlockSpec((tk, tn), lambda i, j, k: (k, j))],
        out_specs=pl.BlockSpec((tm, tn), lambda i, j, k: (i, j)),
        out_shape=SDS((m, n), out_dtype),
        scratch_shapes=[pltpu.VMEM((tm, tn), F32)],
        compiler_params=_cparams("parallel", "parallel", "arbitrary"),
    )(a, b)


def _dproj_matmul_normbwd(dproj, w, x, g, dxn, name, tm=512, tk=1024, ride=None):
    t, n = dproj.shape
    d = w.shape[0]
    tk = _divisor_tile(n, 2 * tk)
    nk = n // tk
    ni = t // tm
    nr = ride.n if ride else 0

    def body(*refs):
        dp_ref, w_ref, x_ref, g_ref, dxn_ref = refs[:5]
        dx_ref, dg_ref = refs[5 + nr:7 + nr]
        rin, rout = refs[5:5 + nr], refs[7 + nr:7 + 2 * nr]
        sems, acc_ref = refs[7 + 2 * nr:-1], refs[-1]
        i, k = pl.program_id(0), pl.program_id(1)
        if ride:
            @pl.when((i == 0) & (k == 0))
            def _():
                ride.start(rin, rout, sems)

        @pl.when(k == 0)
        def _():
            acc_ref[...] = jnp.zeros_like(acc_ref)

        acc_ref[...] += lax.dot_general(dp_ref[...], w_ref[...], NT_DIMS, preferred_element_type=F32)

        @pl.when(k == nk - 1)
        def _():
            dhn = acc_ref[...]
            xf = x_ref[...]
            r = lax.rsqrt(jnp.mean(xf * xf, axis=-1, keepdims=True) + NORM_EPS)
            xh = xf * r
            dy = dhn * g_ref[...]
            dx_ref[...] = dxn_ref[...] + r * (dy - xh * jnp.mean(dy * xh, axis=-1, keepdims=True))
            part = jnp.sum(dhn * xh, axis=0, keepdims=True)

            @pl.when(i == 0)
            def _():
                dg_ref[...] = part

            @pl.when(i > 0)
            def _():
                dg_ref[...] += part

        if ride:
            @pl.when((i == ni - 1) & (k == nk - 1))
            def _():
                ride.wait(rin, rout, sems)

    return pl.pallas_call(
        body, name=name, grid=(ni, nk),
        in_specs=[pl.BlockSpec((tm, tk), lambda i, k: (i, k)), pl.BlockSpec((d, tk), lambda i, k: (0, k)),
                  pl.BlockSpec((tm, d), lambda i, k: (i, 0)), pl.BlockSpec((1, d), lambda i, k: (0, 0)),
                  pl.BlockSpec((tm, d), lambda i, k: (i, 0))] + (ride.in_specs if ride else []),
        out_specs=[pl.BlockSpec((tm, d), lambda i, k: (i, 0)), pl.BlockSpec((1, d), lambda i, k: (0, 0))]
        + (ride.out_specs if ride else []),
        out_shape=[SDS((t, d), F32), SDS((1, d), F32)] + (ride.out_shape if ride else []),
        scratch_shapes=(ride.scratch if ride else []) + [pltpu.VMEM((tm, d), F32)],
        compiler_params=_cparams("arbitrary", "arbitrary"),
    )(dproj, w, x, g, dxn, *(ride.arrays if ride else []))


def _loss_grad(y, target, name, tm=512):
    t, d = y.shape
    inv_d = 1.0 / d

    def body(y_ref, t_ref, part_ref, dy_ref):
        i = pl.program_id(0)
        err = y_ref[...] - t_ref[...]
        dy_ref[...] = err * inv_d
        part = jnp.sum(err * err, axis=0, keepdims=True) * (0.5 * inv_d)

        @pl.when(i == 0)
        def _():
            part_ref[...] = part

        @pl.when(i > 0)
        def _():
            part_ref[...] += part

    return pl.pallas_call(
        body, name=name, grid=(t // tm,),
        in_specs=[pl.BlockSpec((tm, d), lambda i: (i, 0)), pl.BlockSpec((tm, d), lambda i: (i, 0))],
        out_specs=[pl.BlockSpec((1, d), lambda i: (0, 0)), pl.BlockSpec((tm, d), lambda i: (i, 0))],
        out_shape=[SDS((1, d), F32), SDS((t, d), F32)],
        compiler_params=_cparams("arbitrary"),
    )(y, target)


CONV_ROWS = 32
CONV_COLS = 512


def _conv_chunk(src_ref, base, w_ref, width, r0, c0, flip):
    acc = None
    for k in range(width):
        off = base + r0 + ((width - 1 - k) if flip else (k - (width - 1)))
        term = src_ref[pl.ds(off, CONV_ROWS), pl.ds(c0, CONV_COLS)] * w_ref[pl.ds(k, 1), pl.ds(c0, CONV_COLS)]
        acc = term if acc is None else acc + term
    return acc


def _conv_weight_grad(dw_ref, d_ref, src_ref, base, width, tt, e):
    for c0 in range(0, e, CONV_COLS):
        for k in range(width):
            acc = None
            for r0 in range(0, tt, CONV_ROWS):
                prod = (d_ref[pl.ds(r0, CONV_ROWS), pl.ds(c0, CONV_COLS)]
                        * src_ref[pl.ds(base + r0 - (width - 1) + k, CONV_ROWS), pl.ds(c0, CONV_COLS)])
                part = _rows8(prod)
                acc = part if acc is None else acc + part
            dw_ref[pl.ds(k, 1), pl.ds(c0, CONV_COLS)] += jnp.sum(acc, axis=0, keepdims=True)


def _shifted_copies(dst_ref, src_ref, c0, length, sign):
    lo, hi = (SUBLANES, length) if sign < 0 else (0, length - SUBLANES)
    for b in range(SUBLANES):
        for r0 in range(lo, hi, CONV_ROWS):
            n = min(CONV_ROWS, hi - r0)
            dst_ref[b, pl.ds(r0, n), :] = src_ref[pl.ds(r0 + sign * b, n), pl.ds(c0, CONV_COLS)]


def _conv_aligned(copies_ref, base, w_ref, width, r0, c0, sign):
    acc = None
    for d in range(width):
        a, b = divmod(d, SUBLANES)
        term = (copies_ref[b, pl.ds(base + r0 + sign * SUBLANES * a, CONV_ROWS), :]
                * w_ref[pl.ds(width - 1 - d, 1), pl.ds(c0, CONV_COLS)])
        acc = term if acc is None else acc + term
    return acc


def _conv_weight_grad_aligned(dw_ref, d_ref, copies_ref, base, width, tt, c0):
    for d in range(width):
        a, b = divmod(d, SUBLANES)
        acc = None
        for r0 in range(0, tt, CONV_ROWS):
            prod = (d_ref[pl.ds(r0, CONV_ROWS), pl.ds(c0, CONV_COLS)]
                    * copies_ref[b, pl.ds(base + r0 - SUBLANES * a, CONV_ROWS), :])
            part = _rows8(prod)
            acc = part if acc is None else acc + part
        dw_ref[pl.ds(width - 1 - d, 1), pl.ds(c0, CONV_COLS)] += jnp.sum(acc, axis=0, keepdims=True)


LN_ROWS = 16


def _a_mid_fwd(proj, cw, cb, lg, lb, width, name, tt=256):
    t, e3 = proj.shape
    e = e3 // 3
    halo = CONF_HALO

    def body(p_ref, cw_ref, cb_ref, lg_ref, lb_ref, y_ref, u1_ref, ubuf, shifted):
        i = pl.program_id(0)

        @pl.when(i == 0)
        def _():
            ubuf[pl.ds(0, halo), :] = jnp.zeros((halo, e), F32)

        @pl.when(i > 0)
        def _():
            ubuf[pl.ds(0, halo), :] = ubuf[pl.ds(tt, halo), :]

        for r0 in range(0, tt, CONV_ROWS):
            val = p_ref[pl.ds(r0, CONV_ROWS), pl.ds(0, e)].astype(F32)
            gate = p_ref[pl.ds(r0, CONV_ROWS), pl.ds(e, e)].astype(F32)
            ubuf[pl.ds(halo + r0, CONV_ROWS), :] = val * _sigmoid(gate)
        for c0 in range(0, e, CONV_COLS):
            _shifted_copies(shifted, ubuf, c0, halo + tt, -1)
            for r0 in range(0, tt, CONV_ROWS):
                acc = _conv_aligned(shifted, halo, cw_ref, width, r0, c0, -1)
                u1_ref[pl.ds(r0, CONV_ROWS), pl.ds(c0, CONV_COLS)] = acc + cb_ref[:, pl.ds(c0, CONV_COLS)]
        for r0 in range(0, tt, LN_ROWS):
            u = u1_ref[pl.ds(r0, LN_ROWS), :]
            mu = jnp.mean(u, axis=-1, keepdims=True)
            dlt = u - mu
            var = jnp.mean(dlt * dlt, axis=-1, keepdims=True)
            u2 = (dlt * lax.rsqrt(var + NORM_EPS)) * lg_ref[...] + lb_ref[...]
            z = p_ref[pl.ds(r0, LN_ROWS), pl.ds(2 * e, e)].astype(F32)
            y_ref[pl.ds(r0, LN_ROWS), :] = (_silu(u2) * _silu(z)).astype(BF16)

    return pl.pallas_call(
        body, name=name, grid=(t // tt,),
        in_specs=[pl.BlockSpec((tt, e3), lambda i: (i, 0)), pl.BlockSpec(cw.shape, lambda i: (0, 0)),
                  pl.BlockSpec((1, e), lambda i: (0, 0)), pl.BlockSpec((1, e), lambda i: (0, 0)),
                  pl.BlockSpec((1, e), lambda i: (0, 0))],
        out_specs=[pl.BlockSpec((tt, e), lambda i: (i, 0)), pl.BlockSpec((tt, e), lambda i: (i, 0))],
        out_shape=[SDS((t, e), BF16), SDS((t, e), F32)],
        scratch_shapes=[pltpu.VMEM((halo + tt, e), F32), pltpu.VMEM((SUBLANES, halo + tt, CONV_COLS), F32)],
        compiler_params=_cparams("arbitrary"),
    )(proj, cw, cb, lg, lb)


def _a_mid_bwd(proj, u1, dyz, cw, lg, lb, width, name, tt=256):
    t, e3 = proj.shape
    e = e3 // 3
    halo = CONF_HALO
    nt = t // tt
    hb = tt // halo

    def body(p_ref, pp_ref, u1_ref, dy_ref, cw_ref, lg_ref, lb_ref,
             dp_ref, dcw_ref, dcb_ref, dlg_ref, dlb_ref, ubuf, dbuf, shifted, acc_cb, acc_lg, acc_lb):
        i = pl.program_id(0)
        ti = nt - 1 - i

        @pl.when(i == 0)
        def _():
            dbuf[pl.ds(tt, halo), :] = jnp.zeros((halo, e), F32)
            dcw_ref[...] = jnp.zeros_like(dcw_ref)
            acc_cb[...] = jnp.zeros_like(acc_cb)
            acc_lg[...] = jnp.zeros_like(acc_lg)
            acc_lb[...] = jnp.zeros_like(acc_lb)

        @pl.when(i > 0)
        def _():
            dbuf[pl.ds(tt, halo), :] = dbuf[pl.ds(0, halo), :]

        keep = (ti > 0).astype(F32)
        ubuf[pl.ds(0, halo), :] = keep * (pp_ref[:, pl.ds(0, e)].astype(F32) * _sigmoid(pp_ref[:, pl.ds(e, e)].astype(F32)))
        for r0 in range(0, tt, CONV_ROWS):
            val = p_ref[pl.ds(r0, CONV_ROWS), pl.ds(0, e)].astype(F32)
            gate = p_ref[pl.ds(r0, CONV_ROWS), pl.ds(e, e)].astype(F32)
            ubuf[pl.ds(halo + r0, CONV_ROWS), :] = val * _sigmoid(gate)

        for r0 in range(0, tt, LN_ROWS):
            rows = pl.ds(r0, LN_ROWS)
            u = u1_ref[rows, :]
            mu = jnp.mean(u, axis=-1, keepdims=True)
            dlt = u - mu
            var = jnp.mean(dlt * dlt, axis=-1, keepdims=True)
            rstd = lax.rsqrt(var + NORM_EPS)
            xh = dlt * rstd
            u2 = xh * lg_ref[...] + lb_ref[...]
            s2 = _sigmoid(u2)
            u3 = u2 * s2
            z = p_ref[rows, pl.ds(2 * e, e)].astype(F32)
            sz = _sigmoid(z)
            dy = dy_ref[rows, :].astype(F32)
            dp_ref[rows, pl.ds(2 * e, e)] = (dy * u3 * (sz * (1.0 + z * (1.0 - sz)))).astype(BF16)
            du2 = (dy * (z * sz)) * (s2 * (1.0 + u2 * (1.0 - s2)))
            acc_lg[...] += _rows8(du2 * xh)
            acc_lb[...] += _rows8(du2)
            dxh = du2 * lg_ref[...]
            m1 = jnp.mean(dxh, axis=-1, keepdims=True)
            m2 = jnp.mean(dxh * xh, axis=-1, keepdims=True)
            du1 = rstd * (dxh - m1 - xh * m2)
            dbuf[rows, :] = du1
            acc_cb[...] += _rows8(du1)

        for c0 in range(0, e, CONV_COLS):
            _shifted_copies(shifted, dbuf, c0, tt + halo, 1)
            for r0 in range(0, tt, CONV_ROWS):
                du0 = _conv_aligned(shifted, 0, cw_ref, width, r0, c0, 1)
                rows, cols = pl.ds(r0, CONV_ROWS), pl.ds(c0, CONV_COLS)
                val = p_ref[rows, cols].astype(F32)
                sg = _sigmoid(p_ref[rows, pl.ds(e + c0, CONV_COLS)].astype(F32))
                dp_ref[rows, cols] = (du0 * sg).astype(BF16)
                dp_ref[rows, pl.ds(e + c0, CONV_COLS)] = (du0 * val * sg * (1.0 - sg)).astype(BF16)
            _shifted_copies(shifted, ubuf, c0, halo + tt, -1)
            _conv_weight_grad_aligned(dcw_ref, dbuf, shifted, halo, width, tt, c0)

        @pl.when(i == nt - 1)
        def _():
            dcb_ref[...] = jnp.sum(acc_cb[...], axis=0, keepdims=True)
            dlg_ref[...] = jnp.sum(acc_lg[...], axis=0, keepdims=True)
            dlb_ref[...] = jnp.sum(acc_lb[...], axis=0, keepdims=True)

    vec = pl.BlockSpec((1, e), lambda i: (0, 0))
    return pl.pallas_call(
        body, name=name, grid=(nt,),
        in_specs=[pl.BlockSpec((tt, e3), lambda i: (nt - 1 - i, 0)),
                  pl.BlockSpec((halo, e3), lambda i: (jnp.maximum((nt - 1 - i) * hb - 1, 0), 0)),
                  pl.BlockSpec((tt, e), lambda i: (nt - 1 - i, 0)), pl.BlockSpec((tt, e), lambda i: (nt - 1 - i, 0)),
                  pl.BlockSpec(cw.shape, lambda i: (0, 0)), vec, vec],
        out_specs=[pl.BlockSpec((tt, e3), lambda i: (nt - 1 - i, 0)), pl.BlockSpec(cw.shape, lambda i: (0, 0)), vec, vec, vec],
        out_shape=[SDS((t, e3), BF16), SDS(cw.shape, F32), SDS((1, e), F32), SDS((1, e), F32), SDS((1, e), F32)],
        scratch_shapes=[pltpu.VMEM((halo + tt, e), F32), pltpu.VMEM((tt + halo, e), F32),
                        pltpu.VMEM((SUBLANES, halo + tt, CONV_COLS), F32),
                        pltpu.VMEM((SUBLANES, e), F32), pltpu.VMEM((SUBLANES, e), F32), pltpu.VMEM((SUBLANES, e), F32)],
        compiler_params=_cparams("arbitrary"),
    )(proj, proj, u1, dyz, cw, lg, lb)


def _c_mid_fwd(proj, cw, width, name, tt=256):
    t, e4 = proj.shape
    e = e4 // 4
    halo = SHORT_HALO

    def body(p_ref, cw_ref, y_ref, wbuf):
        i = pl.program_id(0)

        @pl.when(i == 0)
        def _():
            wbuf[pl.ds(0, halo), :] = jnp.zeros((halo, e), F32)

        @pl.when(i > 0)
        def _():
            wbuf[pl.ds(0, halo), :] = wbuf[pl.ds(tt, halo), :]

        for r0 in range(0, tt, CONV_ROWS):
            rows = pl.ds(r0, CONV_ROWS)
            wbuf[pl.ds(halo + r0, CONV_ROWS), :] = p_ref[rows, pl.ds(2 * e, e)].astype(F32) * p_ref[rows, pl.ds(0, e)].astype(F32)
        for c0 in range(0, e, CONV_COLS):
            for r0 in range(0, tt, CONV_ROWS):
                rows = pl.ds(r0, CONV_ROWS)
                cv = _conv_chunk(wbuf, halo, cw_ref, width, r0, c0, False)
                bg = p_ref[rows, pl.ds(e + c0, CONV_COLS)].astype(F32)
                z = p_ref[rows, pl.ds(3 * e + c0, CONV_COLS)].astype(F32)
                y_ref[rows, pl.ds(c0, CONV_COLS)] = ((bg * cv) * _silu(z)).astype(BF16)

    return pl.pallas_call(
        body, name=name, grid=(t // tt,),
        in_specs=[pl.BlockSpec((tt, e4), lambda i: (i, 0)), pl.BlockSpec(cw.shape, lambda i: (0, 0))],
        out_specs=pl.BlockSpec((tt, e), lambda i: (i, 0)),
        out_shape=SDS((t, e), BF16),
        scratch_shapes=[pltpu.VMEM((halo + tt, e), F32)],
        compiler_params=_cparams("arbitrary"),
    )(proj, cw)


def _c_mid_bwd(proj, dyz, cw, width, name, tt=256):
    t, e4 = proj.shape
    e = e4 // 4
    halo = SHORT_HALO
    nt = t // tt
    hb = tt // halo

    def body(p_ref, pp_ref, dy_ref, cw_ref, dp_ref, dcw_ref, wbuf, dbuf):
        i = pl.program_id(0)
        ti = nt - 1 - i

        @pl.when(i == 0)
        def _():
            dbuf[pl.ds(tt, halo), :] = jnp.zeros((halo, e), F32)
            dcw_ref[...] = jnp.zeros_like(dcw_ref)

        @pl.when(i > 0)
        def _():
            dbuf[pl.ds(tt, halo), :] = dbuf[pl.ds(0, halo), :]

        keep = (ti > 0).astype(F32)
        wbuf[pl.ds(0, halo), :] = keep * (pp_ref[:, pl.ds(2 * e, e)].astype(F32) * pp_ref[:, pl.ds(0, e)].astype(F32))
        for r0 in range(0, tt, CONV_ROWS):
            rows = pl.ds(r0, CONV_ROWS)
            wbuf[pl.ds(halo + r0, CONV_ROWS), :] = p_ref[rows, pl.ds(2 * e, e)].astype(F32) * p_ref[rows, pl.ds(0, e)].astype(F32)
        for c0 in range(0, e, CONV_COLS):
            for r0 in range(0, tt, CONV_ROWS):
                rows, cols = pl.ds(r0, CONV_ROWS), pl.ds(c0, CONV_COLS)
                cv = _conv_chunk(wbuf, halo, cw_ref, width, r0, c0, False)
                bg = p_ref[rows, pl.ds(e + c0, CONV_COLS)].astype(F32)
                z = p_ref[rows, pl.ds(3 * e + c0, CONV_COLS)].astype(F32)
                sz = _sigmoid(z)
                dyz_c = dy_ref[rows, cols].astype(F32)
                dy = dyz_c * (z * sz)
                dp_ref[rows, pl.ds(3 * e + c0, CONV_COLS)] = (dyz_c * (bg * cv) * (sz * (1.0 + z * (1.0 - sz)))).astype(BF16)
                dp_ref[rows, pl.ds(e + c0, CONV_COLS)] = (dy * cv).astype(BF16)
                dbuf[rows, cols] = dy * bg
        for c0 in range(0, e, CONV_COLS):
            for r0 in range(0, tt, CONV_ROWS):
                rows, cols = pl.ds(r0, CONV_ROWS), pl.ds(c0, CONV_COLS)
                dw = _conv_chunk(dbuf, 0, cw_ref, width, r0, c0, True)
                dp_ref[rows, pl.ds(2 * e + c0, CONV_COLS)] = (dw * p_ref[rows, cols].astype(F32)).astype(BF16)
                dp_ref[rows, cols] = (dw * p_ref[rows, pl.ds(2 * e + c0, CONV_COLS)].astype(F32)).astype(BF16)
        _conv_weight_grad(dcw_ref, dbuf, wbuf, halo, width, tt, e)

    return pl.pallas_call(
        body, name=name, grid=(nt,),
        in_specs=[pl.BlockSpec((tt, e4), lambda i: (nt - 1 - i, 0)),
                  pl.BlockSpec((halo, e4), lambda i: (jnp.maximum((nt - 1 - i) * hb - 1, 0), 0)),
                  pl.BlockSpec((tt, e), lambda i: (nt - 1 - i, 0)), pl.BlockSpec(cw.shape, lambda i: (0, 0))],
        out_specs=[pl.BlockSpec((tt, e4), lambda i: (nt - 1 - i, 0)), pl.BlockSpec(cw.shape, lambda i: (0, 0))],
        out_shape=[SDS((t, e4), BF16), SDS(cw.shape, F32)],
        scratch_shapes=[pltpu.VMEM((halo + tt, e), F32), pltpu.VMEM((tt + halo, e), F32)],
        compiler_params=_cparams("arbitrary"),
    )(proj, proj, dyz, cw)


def _b_prep_fwd(proj, flog, fbias, qg, kg, heads, name, tt=256):
    t, e4 = proj.shape
    e = e4 // 4
    scale = HEAD_DIM ** -0.5 * LOG2E

    def body(q_ref, k_ref, fl_ref, fb_ref, qg_ref, kg_ref, qs_ref, kn_ref, c_ref, ct_ref, carry):
        i = pl.program_id(0)

        @pl.when(i == 0)
        def _():
            carry[...] = jnp.zeros_like(carry)

        for h in range(heads):
            cols = pl.ds(h * HEAD_DIM, HEAD_DIM)
            qh = q_ref[:, cols].astype(F32)
            r = lax.rsqrt(jnp.mean(qh * qh, axis=-1, keepdims=True) + NORM_EPS)
            qs_ref[:, cols] = (((qh * r) * qg_ref[:, cols]) * scale).astype(BF16)
            kh = k_ref[:, cols].astype(F32)
            r = lax.rsqrt(jnp.mean(kh * kh, axis=-1, keepdims=True) + NORM_EPS)
            kn_ref[:, cols] = ((kh * r) * kg_ref[:, cols]).astype(BF16)

        a = fl_ref[...] + fb_ref[...]
        lf = jnp.minimum(a, 0.0) - jnp.log(1.0 + jnp.exp(-jnp.abs(a)))
        tri = (lax.broadcasted_iota(jnp.int32, (tt, tt), 0) >= lax.broadcasted_iota(jnp.int32, (tt, tt), 1)).astype(BF16)
        c = _tri_matmul(tri, lf) + carry[...]
        c_ref[...] = c
        ct_ref[...] = (c * LOG2E).T
        carry[...] = c_ref[pl.ds(tt - 1, 1), :]

    return pl.pallas_call(
        body, name=name, grid=(t // tt,),
        in_specs=[pl.BlockSpec((tt, e), lambda i: (i, 0)), pl.BlockSpec((tt, e), lambda i: (i, 1)),
                  pl.BlockSpec((tt, LANES), lambda i: (i, 0)), pl.BlockSpec((1, LANES), lambda i: (0, 0)),
                  pl.BlockSpec((1, e), lambda i: (0, 0)), pl.BlockSpec((1, e), lambda i: (0, 0))],
        out_specs=[pl.BlockSpec((tt, e), lambda i: (i, 0)), pl.BlockSpec((tt, e), lambda i: (i, 0)),
                   pl.BlockSpec((tt, LANES), lambda i: (i, 0)), pl.BlockSpec((LANES, tt), lambda i: (0, i))],
        out_shape=[SDS((t, e), BF16), SDS((t, e), BF16), SDS((t, LANES), F32), SDS((LANES, t), F32)],
        scratch_shapes=[pltpu.VMEM((1, LANES), F32)],
        compiler_params=_cparams("arbitrary"),
    )(proj, proj, flog, fbias, qg, kg)


ATT_BLOCK = 512
ATT_CHUNK = 256
NEG_BIG = -1e30
LOG2E = 1.4426950408889634
LN2 = 0.6931471805599453


def _flash_fwd(qs, kn, proj, ck, heads, name):
    t, e = qs.shape
    blk = min(ATT_BLOCK, t)
    cw = min(ATT_CHUNK, blk // 2)
    nq = t // blk
    assert blk == 2 * cw

    def body(q_ref, k_ref, v_ref, ck_ref, z_ref, o_ref, y_ref, m_ref, l_ref, s_a, s_b):
        i = pl.program_id(1)
        q = q_ref[...]
        bufs = (s_a, s_b)

        def key_rows(j, c):
            return pl.ds(pl.multiple_of(j * blk, blk) + c * cw, cw)

        def logits(j, c):
            bufs[c][...] = (lax.dot_general(q, k_ref[key_rows(j, c), :], NT_DIMS, preferred_element_type=F32)
                            - ck_ref[j][:, c * cw:(c + 1) * cw])

        def weights(c, m, l, masked):
            s = bufs[c][...]
            if masked:
                keep = lax.broadcasted_iota(jnp.int32, (blk, cw), 0) >= (lax.broadcasted_iota(jnp.int32, (blk, cw), 1) + c * cw)
                s = jnp.where(keep, s, NEG_BIG)
            m_new = jnp.maximum(m, jnp.ceil(jnp.max(s, axis=-1, keepdims=True)))
            alpha = jnp.exp2(m - m_new)
            p = jnp.exp2(s - m_new).astype(BF16)
            return m_new, alpha * l + jnp.sum(p.astype(F32), axis=-1, keepdims=True), alpha, p

        def block(j, carry, masked):
            m, l, acc = carry
            logits(j, 1)
            m, l, alpha0, p0 = weights(0, m, l, masked)
            if not masked:
                logits(j + 1, 0)
            acc = alpha0 * acc + jnp.dot(p0, v_ref[key_rows(j, 0), :], preferred_element_type=F32)
            m, l, alpha1, p1 = weights(1, m, l, masked)
            acc = alpha1 * acc + jnp.dot(p1, v_ref[key_rows(j, 1), :], preferred_element_type=F32)
            return m, l, acc

        logits(0, 0)
        carry = (jnp.full((blk, 1), NEG_BIG, F32), jnp.zeros((blk, 1), F32), jnp.zeros((blk, HEAD_DIM), F32))
        carry = lax.fori_loop(0, i, lambda j, cr: block(j, cr, False), carry)
        m, l, acc = block(i, carry, True)
        o = acc / l
        o_ref[...] = o
        y_ref[...] = (o * _silu(z_ref[...].astype(F32))).astype(BF16)
        m_ref[...] = jnp.broadcast_to(m, (blk, LANES))
        l_ref[...] = jnp.broadcast_to(l, (blk, LANES))

    head_all = pl.BlockSpec((t, HEAD_DIM), lambda h, i: (0, h))
    tile = pl.BlockSpec((blk, HEAD_DIM), lambda h, i: (i, h))
    stat = pl.BlockSpec((None, blk, LANES), lambda h, i: (h, i, 0))
    return pl.pallas_call(
        body, name=name, grid=(heads, nq),
        in_specs=[tile, head_all, pl.BlockSpec((t, HEAD_DIM), lambda h, i: (0, 2 * heads + h)),
                  pl.BlockSpec((None, nq, 1, blk), lambda h, i: (h, 0, 0, 0)),
                  pl.BlockSpec((blk, HEAD_DIM), lambda h, i: (i, 3 * heads + h))],
        out_specs=[tile, tile, stat, stat],
        out_shape=[SDS((t, e), F32), SDS((t, e), BF16), SDS((heads, t, LANES), F32), SDS((heads, t, LANES), F32)],
        scratch_shapes=[pltpu.VMEM((blk, cw), F32), pltpu.VMEM((blk, cw), F32)],
        compiler_params=_cparams("parallel", "arbitrary"),
    )(qs, kn, proj, ck, proj)


def _b_bwd_pre(dyz, o, proj, lstat, heads, name, tt=256):
    t, e = o.shape

    def body(dy_ref, o_ref, z_ref, l_ref, do_ref, dz_ref, dl_ref):
        for h in range(heads):
            cols = pl.ds(h * HEAD_DIM, HEAD_DIM)
            z = z_ref[:, cols].astype(F32)
            sz = _sigmoid(z)
            dy = dy_ref[:, cols].astype(F32)
            of = o_ref[:, cols]
            dos = ((dy * (z * sz)) / l_ref[h][:, 0:1]).astype(BF16)
            do_ref[:, cols] = dos
            dz_ref[:, cols] = (dy * of * (sz * (1.0 + z * (1.0 - sz)))).astype(BF16)
            dl_ref[h] = jnp.broadcast_to(jnp.sum(dos.astype(F32) * of, axis=-1, keepdims=True), (tt, LANES))

    stat = pl.BlockSpec((heads, tt, LANES), lambda i: (0, i, 0))
    return pl.pallas_call(
        body, name=name, grid=(t // tt,),
        in_specs=[pl.BlockSpec((tt, e), lambda i: (i, 0)), pl.BlockSpec((tt, e), lambda i: (i, 0)),
                  pl.BlockSpec((tt, e), lambda i: (i, 3)), stat],
        out_specs=[pl.BlockSpec((tt, e), lambda i: (i, 0)), pl.BlockSpec((tt, e), lambda i: (i, 0)), stat],
        out_shape=[SDS((t, e), BF16), SDS((t, e), BF16), SDS((heads, t, LANES), F32)],
        compiler_params=_cparams("parallel"),
    )(dyz, o, proj, lstat)


def _flash_bwd(qs, kn, proj, ck, dos, mstat, delta, heads, name):
    t, e = qs.shape
    blk = min(ATT_BLOCK, t)
    cw = min(ATT_CHUNK, blk // 2)
    nq = t // blk

    def body(q_ref, do_ref, m_ref, dl_ref, k_ref, v_ref, ck_ref, dq_ref, dk_ref, dv_ref, dc_ref, s_a, s_b, d_a, d_b):
        i = pl.program_id(1)

        @pl.when(i == 0)
        def _():
            dk_ref[...] = jnp.zeros_like(dk_ref)
            dv_ref[...] = jnp.zeros_like(dv_ref)
            dc_ref[...] = jnp.zeros_like(dc_ref)

        q = q_ref[...]
        do = do_ref[...]
        mrow = m_ref[:, 0:1]
        dl = dl_ref[:, 0:1]
        sbuf, dbuf = (s_a, s_b), (d_a, d_b)

        def key_rows(j, c):
            return pl.ds(pl.multiple_of(j * blk, blk) + c * cw, cw)

        def products(j, c):
            rows = key_rows(j, c)
            sbuf[c][...] = (lax.dot_general(q, k_ref[rows, :], NT_DIMS, preferred_element_type=F32)
                            - ck_ref[j][:, c * cw:(c + 1) * cw])
            dbuf[c][...] = lax.dot_general(do, v_ref[rows, :], NT_DIMS, preferred_element_type=F32)

        def weights(c, masked):
            p = jnp.exp2(sbuf[c][...] - mrow)
            if masked:
                keep = lax.broadcasted_iota(jnp.int32, (blk, cw), 0) >= (lax.broadcasted_iota(jnp.int32, (blk, cw), 1) + c * cw)
                p = jnp.where(keep, p, 0.0)
            p = p.astype(BF16)
            ds = p.astype(F32) * (dbuf[c][...] - dl)
            return p, ds.astype(BF16), jnp.sum(ds, axis=0, keepdims=True)

        def outputs(j, c, p, dsb, colsum, dq):
            rows = key_rows(j, c)
            dv_ref[rows, :] += lax.dot_general(p, do, TN_DIMS, preferred_element_type=F32)
            dk_ref[rows, :] += lax.dot_general(dsb, q, TN_DIMS, preferred_element_type=F32)
            dc_ref[j, :, pl.ds(c * cw, cw)] -= colsum
            return dq + jnp.dot(dsb, k_ref[rows, :], preferred_element_type=F32)

        def block(j, dq, masked):
            products(j, 1)
            p0, ds0, cs0 = weights(0, masked)
            if not masked:
                products(j + 1, 0)
            dq = outputs(j, 0, p0, ds0, cs0, dq)
            p1, ds1, cs1 = weights(1, masked)
            return outputs(j, 1, p1, ds1, cs1, dq)

        products(0, 0)
        dq = lax.fori_loop(0, i, lambda j, acc: block(j, acc, False), jnp.zeros((blk, HEAD_DIM), F32))
        dq_ref[...] = block(i, dq, True)

    tile = pl.BlockSpec((blk, HEAD_DIM), lambda h, i: (i, h))
    stat = pl.BlockSpec((None, blk, LANES), lambda h, i: (h, i, 0))
    head_all = pl.BlockSpec((t, HEAD_DIM), lambda h, i: (0, h))
    cspec = pl.BlockSpec((None, nq, 1, blk), lambda h, i: (h, 0, 0, 0))
    return pl.pallas_call(
        body, name=name, grid=(heads, nq),
        in_specs=[tile, tile, stat, stat, head_all, pl.BlockSpec((t, HEAD_DIM), lambda h, i: (0, 2 * heads + h)), cspec],
        out_specs=[tile, head_all, head_all, cspec],
        out_shape=[SDS((t, e), F32), SDS((t, e), F32), SDS((t, e), F32), SDS((heads, nq, 1, blk), F32)],
        scratch_shapes=[pltpu.VMEM((blk, cw), F32)] * 4,
        compiler_params=_cparams("parallel", "arbitrary"),
    )(qs, dos, mstat, delta, kn, proj, ck)


def _b_prep_bwd(dqs, dkn, dv, dz, proj, qg, kg, dct, flog, fbias, heads, name, tt=256):
    t, e4 = proj.shape
    e = e4 // 4
    nt = t // tt
    scale = HEAD_DIM ** -0.5

    def body(dq_ref, dk_ref, dv_ref, dz_ref, q_ref, k_ref, qg_ref, kg_ref, dc_ref, fl_ref, fb_ref,
             dp_ref, dqg_ref, dkg_ref, dfb_ref, carry, dlf, acc_q, acc_k, acc_f):
        i = pl.program_id(0)

        @pl.when(i == 0)
        def _():
            carry[...] = jnp.zeros_like(carry)
            acc_q[...] = jnp.zeros_like(acc_q)
            acc_k[...] = jnp.zeros_like(acc_k)
            acc_f[...] = jnp.zeros_like(acc_f)

        for h in range(heads):
            cols = pl.ds(h * HEAD_DIM, HEAD_DIM)
            for src_ref, d_ref, g_ref, acc, mult, off in ((q_ref, dq_ref, qg_ref, acc_q, scale, 0),
                                                          (k_ref, dk_ref, kg_ref, acc_k, LN2, e)):
                xf = src_ref[:, cols].astype(F32)
                r = lax.rsqrt(jnp.mean(xf * xf, axis=-1, keepdims=True) + NORM_EPS)
                xh = xf * r
                dn = d_ref[:, cols] * mult
                acc[...] += _rows8(dn * xh)
                dxh = dn * g_ref[:, cols]
                dp_ref[:, pl.ds(off + h * HEAD_DIM, HEAD_DIM)] = (
                    r * (dxh - xh * jnp.mean(dxh * xh, axis=-1, keepdims=True))).astype(BF16)
        dp_ref[:, pl.ds(2 * e, e)] = dv_ref[...].astype(BF16)
        dp_ref[:, pl.ds(3 * e, e)] = dz_ref[...]

        tri = (lax.broadcasted_iota(jnp.int32, (tt, tt), 0) <= lax.broadcasted_iota(jnp.int32, (tt, tt), 1)).astype(BF16)
        dlf[...] = _tri_matmul(tri, dc_ref[...]) + carry[...]
        carry[...] = dlf[pl.ds(0, 1), :]
        a = fl_ref[...] + fb_ref[...]
        dfl = dlf[...] * _sigmoid(-a)
        dp_ref[:, pl.ds(4 * e, LANES)] = dfl.astype(BF16)
        acc_f[...] += _rows8(dfl)

        @pl.when(i == nt - 1)
        def _():
            dqg_ref[...] = jnp.sum(acc_q[...], axis=0, keepdims=True)
            dkg_ref[...] = jnp.sum(acc_k[...], axis=0, keepdims=True)
            dfb_ref[...] = jnp.sum(acc_f[...], axis=0, keepdims=True)

    rev = lambda i: (nt - 1 - i, 0)
    vec_e = pl.BlockSpec((1, e), lambda i: (0, 0))
    vec = pl.BlockSpec((1, LANES), lambda i: (0, 0))
    wide = pl.BlockSpec((tt, e), rev)
    lane = pl.BlockSpec((tt, LANES), rev)
    return pl.pallas_call(
        body, name=name, grid=(nt,),
        in_specs=[wide, wide, wide, wide, wide, pl.BlockSpec((tt, e), lambda i: (nt - 1 - i, 1)), vec_e, vec_e, lane, lane, vec],
        out_specs=[pl.BlockSpec((tt, e4 + LANES), rev), vec, vec, vec],
        out_shape=[SDS((t, e4 + LANES), BF16), SDS((1, LANES), F32), SDS((1, LANES), F32), SDS((1, LANES), F32)],
        scratch_shapes=[pltpu.VMEM((1, LANES), F32), pltpu.VMEM((tt, LANES), F32), pltpu.VMEM((SUBLANES, LANES), F32),
                        pltpu.VMEM((SUBLANES, LANES), F32), pltpu.VMEM((SUBLANES, LANES), F32)],
        compiler_params=_cparams("arbitrary"),
    )(dqs, dkn, dv, dz, proj, proj, qg, kg, dct, flog, fbias)


def _b_fwd(h, b_norm, wb_pad, wb_out, qg, kg, fbias, heads, tag, ride=None):
    t = h.shape[0]
    e = wb_out.shape[0]
    blk = min(ATT_BLOCK, t)
    hn, proj, *rode = _norm_matmul(h, b_norm, wb_pad[:, :4 * e], f"b_in_proj_{tag}", ride=ride)
    flog = _matmul_f32out(hn, wb_pad[:, 4 * e:], f"b_forget_proj_{tag}")
    qs, kn, _, ct = _b_prep_fwd(proj, flog, fbias, qg, kg, heads, f"b_prep_fwd_{tag}")
    ck = ct.reshape(LANES, t // blk, 1, blk)
    o, y, mstat, lstat = _flash_fwd(qs, kn, proj, ck, heads, f"b_attention_fwd_{tag}")
    sv = dict(x=h, hn=hn, proj=proj, flog=flog, qs=qs, kn=kn, ck=ck, o=o, y=y, mstat=mstat, lstat=lstat)
    return _out_matmul_residual(y, wb_out, h, f"b_out_proj_{tag}"), sv, rode


def _b_bwd(dh, sv, b_norm, wb_pad, wb_out, qg, kg, fbias, heads, tag, ride=None):
    t = dh.shape[0]
    e = wb_out.shape[0]
    gb = {}
    dyz = _matmul_nt(dh, wb_out, f"b_out_bwd_{tag}")
    gb["w_out"] = _matmul_tn(sv["y"], dh, f"b_out_wgrad_{tag}")
    dos, dz, delta = _b_bwd_pre(dyz, sv["o"], sv["proj"], sv["lstat"], heads, f"b_gate_bwd_{tag}")
    dqs, dkn, dv, dc = _flash_bwd(sv["qs"], sv["kn"], sv["proj"], sv["ck"], dos, sv["mstat"], delta, heads,
                                  f"b_attention_bwd_{tag}")
    dct = jnp.pad(dc.reshape(heads, t).T, ((0, 0), (0, LANES - heads)))
    dproj, dqg, dkg, dfb = _b_prep_bwd(dqs, dkn, dv, dz, sv["proj"], qg, kg, dct, sv["flog"], fbias, heads,
                                       f"b_prep_bwd_{tag}")
    gb["w_in"] = _matmul_tn(sv["hn"], dproj, f"b_in_wgrad_{tag}")[:, :4 * e + heads]
    dh, dg, *rode = _dproj_matmul_normbwd(dproj, wb_pad, sv["x"], b_norm, dh, f"b_in_bwd_{tag}", ride=ride)
    gb["norm"], gb["q_norm"], gb["k_norm"], gb["f_bias"] = dg, dqg, dkg, dfb[:, :heads]
    return dh, gb, rode


def _sum_adamw(recv, w, m, v, name, tr=256):
    nl, r, c = w.shape
    tr = min(tr, r)

    def body(g_ref, w_ref, m_ref, v_ref, go_ref, d_ref, mo_ref, vo_ref):
        g = g_ref[0].astype(F32)
        for s in range(1, N_DEV):
            g = g + g_ref[s].astype(F32)
        go_ref[...] = g
        mn = ADAM_B1 * m_ref[...] + (1.0 - ADAM_B1) * g
        vn = ADAM_B2 * v_ref[...] + (1.0 - ADAM_B2) * (g * g)
        m_hat = mn / (1.0 - ADAM_B1 ** ADAM_STEP)
        v_hat = vn / (1.0 - ADAM_B2 ** ADAM_STEP)
        d_ref[...] = -ADAM_LR * (m_hat / (jnp.sqrt(v_hat) + ADAM_EPS) + ADAM_WD * w_ref[...])
        mo_ref[...] = mn
        vo_ref[...] = vn

    blk = pl.BlockSpec((None, tr, c), lambda l, i: (l, i, 0))
    return pl.pallas_call(
        body, name=name, grid=(nl, r // tr),
        in_specs=[pl.BlockSpec((N_DEV, None, tr, c), lambda l, i: (0, l, i, 0)), blk, blk, blk],
        out_specs=[blk, blk, blk, blk],
        out_shape=[SDS(w.shape, F32)] * 4,
        compiler_params=_cparams("parallel", "parallel"),
    )(recv, w, m, v)


def _unshard(g, axis):
    g = jnp.moveaxis(g, 0, axis)
    return g.reshape(g.shape[:axis] + (g.shape[axis] * g.shape[axis + 1],) + g.shape[axis + 2:])


def _to_slabs(full, axis):
    n = full.shape[axis]
    s = full.reshape(full.shape[:axis] + (N_DEV, n // N_DEV) + full.shape[axis + 1:])
    return jnp.moveaxis(s, axis, 0)


def _pack_rows(parts, lead):
    flat = [p.reshape(p.shape[:lead] + (-1,)) for p in parts]
    cat = jnp.concatenate(flat, axis=-1)
    n = cat.shape[-1]
    pad = (-n) % (SUBLANES * LANES)
    cat = jnp.pad(cat, [(0, 0)] * lead + [(0, pad)])
    return cat.reshape(cat.shape[:lead] + ((n + pad) // LANES, LANES))


def _unpack_rows(packed, shapes, lead):
    flat = packed.reshape(packed.shape[:lead] + (-1,))
    out, off = [], 0
    for shp in shapes:
        size = int(np.prod(shp))
        out.append(flat[..., off:off + size].reshape(packed.shape[:lead] + tuple(shp)))
        off += size
    return out


def _pad_rows(w, rows):
    return jnp.pad(w, ((0, rows - w.shape[0]), (0, 0)))


def kernel(x, a_norm, a_w_in, a_conv_w, a_conv_b, a_ln_g, a_ln_b, a_w_out, b_norm, b_w_in, b_f_bias, b_q_norm, b_k_norm, b_w_out, c_norm, c_w_in, c_conv_w, c_w_out, loss_target, m_a_norm, m_a_w_in, m_a_conv_w, m_a_conv_b, m_a_ln_g, m_a_ln_b, m_a_w_out, m_b_norm, m_b_w_in, m_b_f_bias, m_b_q_norm, m_b_k_norm, m_b_w_out, m_c_norm, m_c_w_in, m_c_conv_w, m_c_w_out, v_a_norm, v_a_w_in, v_a_conv_w, v_a_conv_b, v_a_ln_g, v_a_ln_b, v_a_w_out, v_b_norm, v_b_w_in, v_b_f_bias, v_b_q_norm, v_b_k_norm, v_b_w_out, v_c_norm, v_c_w_in, v_c_conv_w, v_c_w_out):
    t, d = x.shape[1], x.shape[2]
    e = a_w_out.shape[1] * N_DEV
    heads = b_f_bias.shape[1]
    n_a, n_b, n_c = a_norm.shape[0], b_norm.shape[0], c_norm.shape[0]
    depth = n_a + n_b + n_c
    ka, kc = a_conv_w.shape[1], c_conv_w.shape[1]
    assert e == heads * HEAD_DIM and n_b == 1 and n_c == 1 and x.shape[0] == 1

    layers = [(i % 3, i // 3) for i in range(depth)]

    def mat_shards(kind, j):
        w_in, w_out = ((a_w_in, a_w_out), (b_w_in, b_w_out), (c_w_in, c_w_out))[kind]
        return [w_in[j].astype(BF16), w_out[j].astype(BF16)]

    def full_mats(kind, gathered):
        w_in, w_out = _unshard(gathered[0], 1), _unshard(gathered[1], 0)
        return (jnp.pad(w_in, ((0, 0), (0, LANES - heads))) if kind == 1 else w_in), w_out

    small_names = ["a_norm", "a_conv_w", "a_conv_b", "a_ln_g", "a_ln_b", "c_norm", "c_conv_w"]
    small = dict(a_norm=a_norm, a_conv_w=a_conv_w, a_conv_b=a_conv_b, a_ln_g=a_ln_g, a_ln_b=a_ln_b,
                 c_norm=c_norm, c_conv_w=c_conv_w)
    small_pack = _pack_rows([small[k] for k in small_names], 0)
    first = _all_gather(mat_shards(*layers[0]) + [small_pack], "all_gather_first_layer")
    sm = _unpack_rows(first[2], [small[k].shape for k in small_names], 1)
    g_a_norm = _unshard(sm[0], 1)
    g_a_conv_w = _unshard(sm[1], 2)
    g_a_conv_b = _unshard(sm[2], 1)
    g_a_ln_g = _unshard(sm[3], 1)
    g_a_ln_b = _unshard(sm[4], 1)
    g_c_norm = _unshard(sm[5], 1)
    g_c_conv_w = _unshard(sm[6], 2)

    cw_a = [_pad_rows(g_a_conv_w[j], CONF_HALO) for j in range(n_a)]
    cw_c = _pad_rows(g_c_conv_w[0], SHORT_HALO)
    qg = jnp.tile(b_q_norm, (1, heads))
    kg = jnp.tile(b_k_norm, (1, heads))
    fbias = jnp.pad(b_f_bias, ((0, 0), (0, LANES - heads)))

    h = x[0]
    saved, weights = [], [full_mats(layers[0][0], first[:2])]
    for i, (kind, j) in enumerate(layers):
        tag = f"l{i}"
        w_in, w_out = weights[i]
        ride = _Ride("gather", mat_shards(*layers[i + 1])) if i + 1 < depth else None
        if kind == 0:
            hn, proj, *rode = _norm_matmul(h, g_a_norm[j:j + 1], w_in, f"a_in_proj_{tag}", ride=ride)
            y, u1 = _a_mid_fwd(proj, cw_a[j], g_a_conv_b[j:j + 1], g_a_ln_g[j:j + 1], g_a_ln_b[j:j + 1], ka, f"a_mid_fwd_{tag}")
            saved.append(dict(x=h, hn=hn, proj=proj, u1=u1, y=y))
            h = _out_matmul_residual(y, w_out, h, f"a_out_proj_{tag}")
        elif kind == 1:
            h, sv, rode = _b_fwd(h, b_norm, w_in, w_out, qg, kg, fbias, heads, tag, ride=ride)
            saved.append(sv)
        else:
            hn, proj, *rode = _norm_matmul(h, g_c_norm, w_in, f"c_in_proj_{tag}", ride=ride)
            y = _c_mid_fwd(proj, cw_c, kc, f"c_mid_fwd_{tag}")
            saved.append(dict(x=h, hn=hn, proj=proj, y=y))
            h = _out_matmul_residual(y, w_out, h, f"c_out_proj_{tag}")
        if ride:
            weights.append(full_mats(layers[i + 1][0], rode))

    loss_part, dh = _loss_grad(h, loss_target[0], "loss_and_grad")
    loss = lax.psum(jnp.sum(loss_part), ("x", "y", "c"))

    ga = dict(norm=[None] * n_a, conv_w=[None] * n_a, conv_b=[None] * n_a, ln_g=[None] * n_a, ln_b=[None] * n_a)
    gb, gc = None, {}
    recv_mats = [None] * depth
    pending = None
    for i in reversed(range(depth)):
        kind, j = layers[i]
        tag = f"l{i}"
        sv = saved[i]
        w_in, w_out = weights[i]
        ride = _Ride("exchange", pending) if pending is not None else None
        if kind == 0:
            dyz = _matmul_nt(dh, w_out, f"a_out_bwd_{tag}")
            gw_out = _matmul_tn(sv["y"], dh, f"a_out_wgrad_{tag}")
            dproj, dcw, dcb, dlg, dlb = _a_mid_bwd(sv["proj"], sv["u1"], dyz, cw_a[j], g_a_ln_g[j:j + 1], g_a_ln_b[j:j + 1],
                                                   ka, f"a_mid_bwd_{tag}")
            gw_in = _matmul_tn(sv["hn"], dproj, f"a_in_wgrad_{tag}")
            dh, dg, *rode = _dproj_matmul_normbwd(dproj, w_in, sv["x"], g_a_norm[j:j + 1], dh, f"a_in_bwd_{tag}", ride=ride)
            ga["norm"][j], ga["conv_w"][j], ga["conv_b"][j], ga["ln_g"][j], ga["ln_b"][j] = dg[0], dcw[:ka], dcb[0], dlg[0], dlb[0]
        elif kind == 1:
            dh, gb, rode = _b_bwd(dh, sv, b_norm, w_in, w_out, qg, kg, fbias, heads, tag, ride=ride)
            gw_in, gw_out = gb["w_in"], gb["w_out"]
        else:
            dyz = _matmul_nt(dh, w_out, f"c_out_bwd_{tag}")
            gw_out = _matmul_tn(sv["y"], dh, f"c_out_wgrad_{tag}")
            dproj, dcw = _c_mid_bwd(sv["proj"], dyz, cw_c, kc, f"c_mid_bwd_{tag}")
            gw_in = _matmul_tn(sv["hn"], dproj, f"c_in_wgrad_{tag}")
            dh, dg, *rode = _dproj_matmul_normbwd(dproj, w_in, sv["x"], g_c_norm, dh, f"c_in_bwd_{tag}", ride=ride)
            gc["norm"], gc["conv_w"] = dg, dcw[:kc][None]
        if ride:
            recv_mats[i + 1] = rode
        pending = [_to_slabs(gw_in, 1), _to_slabs(gw_out, 0)]
    grad_x = dh[None]

    sharded_small = [(jnp.stack(ga["norm"]), 1), (jnp.stack(ga["conv_w"]), 2), (jnp.stack(ga["conv_b"]), 1),
                     (jnp.stack(ga["ln_g"]), 1), (jnp.stack(ga["ln_b"]), 1), (gc["norm"], 1), (gc["conv_w"], 2)]
    repl_small = [gb["norm"], gb["f_bias"], gb["q_norm"], gb["k_norm"]]
    small_slabs = _pack_rows([_to_slabs(g, ax) for g, ax in sharded_small]
                             + [jnp.broadcast_to(g[None], (N_DEV,) + g.shape) for g in repl_small], 1)
    last = _exchange(pending + [small_slabs], "exchange_last_gradients")
    recv_mats[0] = last[:2]

    outs = {}
    mat_w = dict(a_w_in=(a_w_in, m_a_w_in, v_a_w_in), a_w_out=(a_w_out, m_a_w_out, v_a_w_out),
                 b_w_in=(b_w_in, m_b_w_in, v_b_w_in), b_w_out=(b_w_out, m_b_w_out, v_b_w_out),
                 c_w_in=(c_w_in, m_c_w_in, v_c_w_in), c_w_out=(c_w_out, m_c_w_out, v_c_w_out))
    for kind, prefix in enumerate("abc"):
        members = [i for i, (k, _) in enumerate(layers) if k == kind]
        for which, name in enumerate((f"{prefix}_w_in", f"{prefix}_w_out")):
            recv = jnp.stack([recv_mats[i][which] for i in members], axis=1)
            outs[name] = _sum_adamw(recv, *mat_w[name], f"adamw_{name}")

    small_order = small_names + ["b_norm", "b_f_bias", "b_q_norm", "b_k_norm"]
    small_w = dict(a_norm=(a_norm, m_a_norm, v_a_norm), a_conv_w=(a_conv_w, m_a_conv_w, v_a_conv_w),
                   a_conv_b=(a_conv_b, m_a_conv_b, v_a_conv_b), a_ln_g=(a_ln_g, m_a_ln_g, v_a_ln_g),
                   a_ln_b=(a_ln_b, m_a_ln_b, v_a_ln_b), c_norm=(c_norm, m_c_norm, v_c_norm),
                   c_conv_w=(c_conv_w, m_c_conv_w, v_c_conv_w), b_norm=(b_norm, m_b_norm, v_b_norm),
                   b_f_bias=(b_f_bias, m_b_f_bias, v_b_f_bias), b_q_norm=(b_q_norm, m_b_q_norm, v_b_q_norm),
                   b_k_norm=(b_k_norm, m_b_k_norm, v_b_k_norm))
    packs = [_pack_rows([small_w[k][q] for k in small_order], 0)[None] for q in range(3)]
    small_out = _sum_adamw(last[2][:, None], *packs, "adamw_vectors")
    shapes = [small_w[k][0].shape for k in small_order]
    unpacked = [_unpack_rows(o[0], shapes, 0) for o in small_out]
    for idx, name in enumerate(small_order):
        outs[name] = tuple(unpacked[q][idx] for q in range(4))

    order = ["a_norm", "a_w_in", "a_conv_w", "a_conv_b", "a_ln_g", "a_ln_b", "a_w_out", "b_norm", "b_w_in", "b_f_bias",
             "b_q_norm", "b_k_norm", "b_w_out", "c_norm", "c_w_in", "c_conv_w", "c_w_out"]
    return (loss, grad_x, *[outs[k][0] for k in order], *[outs[k][1] for k in order],
            *[outs[k][2] for k in order], *[outs[k][3] for k in order])
```

```python
import jax
import jax.numpy as jnp
import numpy as np
from jax import lax
from jax.experimental import pallas as pl
from jax.experimental.pallas import tpu as pltpu

F32 = jnp.float32
BF16 = jnp.bfloat16
SDS = jax.ShapeDtypeStruct

NORM_EPS = 1e-6
ADAM_LR = 0.001
ADAM_B1 = 0.9
ADAM_B2 = 0.999
ADAM_EPS = 1e-08
ADAM_WD = 0.01
ADAM_STEP = 10

N_DEV = 8
LANES = 128
SUBLANES = 8
HEAD_DIM = 128
CONF_HALO = 32
SHORT_HALO = 8
VMEM_LIMIT = 56 * 1024 * 1024

NT_DIMS = (((1,), (1,)), ((), ()))
TN_DIMS = (((0,), (0,)), ((), ()))
MESH = pl.DeviceIdType.MESH
ANY = pl.BlockSpec(memory_space=pl.ANY)


def _cparams(*sem):
    return pltpu.CompilerParams(dimension_semantics=sem, vmem_limit_bytes=VMEM_LIMIT)


def _divisor_tile(n, cap):
    return max(m for m in range(LANES, min(n, cap) + 1, LANES) if n % m == 0)


def _sigmoid(x):
    return 1.0 / (1.0 + jnp.exp(-x))


def _silu(x):
    return x * _sigmoid(x)


def _dsilu(x):
    s = _sigmoid(x)
    return s * (1.0 + x * (1.0 - s))


def _rows8(v):
    out = v[0:SUBLANES]
    for a in range(1, v.shape[0] // SUBLANES):
        out = out + v[a * SUBLANES:(a + 1) * SUBLANES]
    return out


def _split3(v):
    hi = v.astype(BF16)
    r1 = v - hi.astype(F32)
    mid = r1.astype(BF16)
    lo = (r1 - mid.astype(F32)).astype(BF16)
    return hi, mid, lo


def _tri_matmul(tri, v):
    hi, mid, lo = _split3(v)
    return (jnp.dot(tri, hi, preferred_element_type=F32) + jnp.dot(tri, mid, preferred_element_type=F32)
            + jnp.dot(tri, lo, preferred_element_type=F32))


def _position():
    return lax.axis_index("x"), lax.axis_index("y"), lax.axis_index("c")


def _all_gather(shards, name):
    n = len(shards)

    def body(*refs):
        xs, outs = refs[:n], refs[n:2 * n]
        send_sems, recv_sems, local_sems = refs[2 * n:]
        x, y, c = _position()
        me, sibling = (x, y, c), (x, y, 1 - c)
        chips = [(1 - x, y), (x, 1 - y), (1 - x, 1 - y)]

        def slot(a, px, py, pc):
            return outs[a].at[4 * px + 2 * py + pc]

        def copy(a, k, block, to, src=None):
            return pltpu.make_async_remote_copy(
                src_ref=slot(a, *block) if src is None else src, dst_ref=slot(a, *block),
                send_sem=send_sems.at[a, k], recv_sem=recv_sems.at[a, k], device_id=to, device_id_type=MESH)

        started = []
        mine = []
        for a in range(n):
            cp = pltpu.make_async_copy(xs[a], slot(a, *me), local_sems.at[a])
            cp.start()
            mine.append(cp)
        for a in range(n):
            first = [copy(a, 0, me, sibling, src=xs[a])]
            first += [copy(a, 1 + j, me, (*chip, c), src=xs[a]) for j, chip in enumerate(chips)]
            for cp in first:
                cp.start()
            started += first
        for a in range(n):
            for j, chip in enumerate(chips):
                copy(a, 1 + j, (*chip, c), me).wait_recv()
                fwd = copy(a, 4 + j, (*chip, c), sibling)
                fwd.start()
                started.append(fwd)
        for a in range(n):
            copy(a, 0, sibling, me).wait_recv()
            for j, chip in enumerate(chips):
                copy(a, 4 + j, (*chip, 1 - c), me).wait_recv()
        for cp in started:
            cp.wait_send()
        for cp in mine:
            cp.wait()

    return pl.pallas_call(
        body, name=name,
        out_shape=[SDS((N_DEV,) + s.shape, s.dtype) for s in shards],
        in_specs=[ANY] * n, out_specs=[ANY] * n,
        scratch_shapes=[pltpu.SemaphoreType.DMA((n, 7)), pltpu.SemaphoreType.DMA((n, 7)), pltpu.SemaphoreType.DMA((n,))],
    )(*shards)


def _exchange(slabs, name):
    n = len(slabs)

    def body(*refs):
        ins, outs = refs[:n], refs[n:2 * n]
        send_sems, recv_sems, local_sems = refs[2 * n:]
        x, y, c = _position()
        me = 4 * x + 2 * y + c
        peers = [(x ^ bx, y ^ by, c ^ bc) for bx in (0, 1) for by in (0, 1) for bc in (0, 1)][1:]

        def copy(a, k, peer):
            pid = 4 * peer[0] + 2 * peer[1] + peer[2]
            return pltpu.make_async_remote_copy(
                src_ref=ins[a].at[pid], dst_ref=outs[a].at[me],
                send_sem=send_sems.at[a, k], recv_sem=recv_sems.at[a, k], device_id=peer, device_id_type=MESH)

        def arrival(a, k, peer):
            pid = 4 * peer[0] + 2 * peer[1] + peer[2]
            return pltpu.make_async_remote_copy(
                src_ref=ins[a].at[pid], dst_ref=outs[a].at[pid],
                send_sem=send_sems.at[a, k], recv_sem=recv_sems.at[a, k], device_id=peer, device_id_type=MESH)

        mine = []
        for a in range(n):
            cp = pltpu.make_async_copy(ins[a].at[me], outs[a].at[me], local_sems.at[a])
            cp.start()
            mine.append(cp)
        started = []
        for a in range(n):
            for k, peer in enumerate(peers):
                cp = copy(a, k, peer)
                cp.start()
                started.append(cp)
        for a in range(n):
            for k, peer in enumerate(peers):
                arrival(a, k, peer).wait_recv()
        for cp in started:
            cp.wait_send()
        for cp in mine:
            cp.wait()

    return pl.pallas_call(
        body, name=name,
        out_shape=[SDS(s.shape, s.dtype) for s in slabs],
        in_specs=[ANY] * n, out_specs=[ANY] * n,
        scratch_shapes=[pltpu.SemaphoreType.DMA((n, 7)), pltpu.SemaphoreType.DMA((n, 7)), pltpu.SemaphoreType.DMA((n,))],
    )(*slabs)


class _Ride:
    def __init__(self, kind, arrays):
        self.kind, self.arrays, self.n = kind, list(arrays), len(arrays)
        self.in_specs = [ANY] * self.n
        self.out_specs = [ANY] * self.n
        self.out_shape = [SDS(((N_DEV,) + a.shape) if kind == "gather" else a.shape, a.dtype) for a in self.arrays]
        self.scratch = [pltpu.SemaphoreType.DMA((self.n, 7)), pltpu.SemaphoreType.DMA((self.n, 7)),
                        pltpu.SemaphoreType.DMA((self.n,))]

    def _copies(self, ins, outs, sems, arriving):
        send_sems, recv_sems, local_sems = sems
        x, y, c = _position()
        me = 4 * x + 2 * y + c
        peers = [(x ^ bx, y ^ by, c ^ bc) for bx in (0, 1) for by in (0, 1) for bc in (0, 1)][1:]
        local, remote = [], []
        for a in range(self.n):
            own = ins[a] if self.kind == "gather" else ins[a].at[me]
            local.append(pltpu.make_async_copy(own, outs[a].at[me], local_sems.at[a]))
            for k, peer in enumerate(peers):
                pid = 4 * peer[0] + 2 * peer[1] + peer[2]
                remote.append(pltpu.make_async_remote_copy(
                    src_ref=ins[a] if self.kind == "gather" else ins[a].at[pid],
                    dst_ref=outs[a].at[pid if arriving else me],
                    send_sem=send_sems.at[a, k], recv_sem=recv_sems.at[a, k], device_id=peer, device_id_type=MESH))
        return local, remote

    def start(self, ins, outs, sems):
        local, sends = self._copies(ins, outs, sems, False)
        for cp in local + sends:
            cp.start()

    def wait(self, ins, outs, sems):
        local, arrivals = self._copies(ins, outs, sems, True)
        for cp in arrivals:
            cp.wait_recv()
        for cp in arrivals:
            cp.wait_send()
        for cp in local:
            cp.wait()


def _norm_matmul(x, g, w, name, tm=2048, tn=1024, ride=None):
    t, d = x.shape
    n = w.shape[1]
    tm, tn = min(tm, t), min(tn, n)
    ni, nj = t // tm, n // tn
    nr = ride.n if ride else 0

    def body(*refs):
        x_ref, g_ref, w_ref = refs[:3]
        hn_ref, o_ref = refs[3 + nr:5 + nr]
        rin, rout, sems = refs[3:3 + nr], refs[5 + nr:5 + 2 * nr], refs[5 + 2 * nr:]
        i, j = pl.program_id(0), pl.program_id(1)
        if ride:
            @pl.when((i == 0) & (j == 0))
            def _():
                ride.start(rin, rout, sems)

        @pl.when(j == 0)
        def _():
            xf = x_ref[...]
            r = lax.rsqrt(jnp.mean(xf * xf, axis=-1, keepdims=True) + NORM_EPS)
            hn_ref[...] = ((xf * r) * g_ref[...]).astype(BF16)

        o_ref[...] = jnp.dot(hn_ref[...], w_ref[...], preferred_element_type=F32).astype(o_ref.dtype)
        if ride:
            @pl.when((i == ni - 1) & (j == nj - 1))
            def _():
                ride.wait(rin, rout, sems)

    return pl.pallas_call(
        body, name=name, grid=(ni, nj),
        in_specs=[pl.BlockSpec((tm, d), lambda i, j: (i, 0)), pl.BlockSpec((1, d), lambda i, j: (0, 0)),
                  pl.BlockSpec((d, tn), lambda i, j: (0, j))] + (ride.in_specs if ride else []),
        out_specs=[pl.BlockSpec((tm, d), lambda i, j: (i, 0)), pl.BlockSpec((tm, tn), lambda i, j: (i, j))]
        + (ride.out_specs if ride else []),
        out_shape=[SDS((t, d), BF16), SDS((t, n), BF16)] + (ride.out_shape if ride else []),
        scratch_shapes=ride.scratch if ride else [],
        compiler_params=_cparams("arbitrary", "arbitrary") if ride else _cparams("parallel", "arbitrary"),
    )(x, g, w, *(ride.arrays if ride else []))


def _matmul_f32out(a, w, name, tm=512):
    t, k = a.shape
    n = w.shape[1]

    def body(a_ref, w_ref, o_ref):
        o_ref[...] = jnp.dot(a_ref[...], w_ref[...], preferred_element_type=F32)

    return pl.pallas_call(
        body, name=name, grid=(t // tm,),
        in_specs=[pl.BlockSpec((tm, k), lambda i: (i, 0)), pl.BlockSpec((k, n), lambda i: (0, 0))],
        out_specs=pl.BlockSpec((tm, n), lambda i: (i, 0)),
        out_shape=SDS((t, n), F32),
        compiler_params=_cparams("parallel"),
    )(a, w)


def _out_matmul_residual(y, w, x, name, tm=512):
    t, e = y.shape
    d = w.shape[1]

    def body(y_ref, w_ref, x_ref, o_ref):
        o_ref[...] = x_ref[...] + jnp.dot(y_ref[...], w_ref[...], preferred_element_type=F32)

    return pl.pallas_call(
        body, name=name, grid=(t // tm,),
        in_specs=[pl.BlockSpec((tm, e), lambda i: (i, 0)), pl.BlockSpec((e, d), lambda i: (0, 0)),
                  pl.BlockSpec((tm, d), lambda i: (i, 0))],
        out_specs=pl.BlockSpec((tm, d), lambda i: (i, 0)),
        out_shape=SDS((t, d), F32),
        compiler_params=_cparams("parallel"),
    )(y, w, x)


def _matmul_nt(a, w, name, tm=512):
    t, d = a.shape
    e = w.shape[0]

    def body(a_ref, w_ref, o_ref):
        o_ref[...] = lax.dot_general(a_ref[...].astype(BF16), w_ref[...], NT_DIMS,
                                     preferred_element_type=F32).astype(o_ref.dtype)

    return pl.pallas_call(
        body, name=name, grid=(t // tm,),
        in_specs=[pl.BlockSpec((tm, d), lambda i: (i, 0)), pl.BlockSpec((e, d), lambda i: (0, 0))],
        out_specs=pl.BlockSpec((tm, e), lambda i: (i, 0)),
        out_shape=SDS((t, e), BF16),
        compiler_params=_cparams("parallel"),
    )(a, w)


def _matmul_tn(a, b, name, out_dtype=BF16, tm=1024, tn=1024, tk=512):
    t, m = a.shape
    n = b.shape[1]
    tm, tn = min(tm, m), _divisor_tile(n, 2 * tn)
    nk = t // tk

    def body(a_ref, b_ref, o_ref, acc_ref):
        k = pl.program_id(2)

        @pl.when(k == 0)
        def _():
            acc_ref[...] = jnp.zeros_like(acc_ref)

        acc_ref[...] += lax.dot_general(a_ref[...].astype(BF16), b_ref[...].astype(BF16), TN_DIMS,
                                        preferred_element_type=F32)

        @pl.when(k == nk - 1)
        def _():
            o_ref[...] = acc_ref[...].astype(o_ref.dtype)

    return pl.pallas_call(
        body, name=name, grid=(m // tm, n // tn, nk),
        in_specs=[pl.BlockSpec((tk, tm), lambda i, j, k: (k, i)), pl.BlockSpec((tk, tn), lambda i, j, k: (k, j))],
        out_specs=pl.BlockSpec((tm, tn), lambda i, j, k: (i, j)),
        out_shape=SDS((m, n), out_dtype),
        scratch_shapes=[pltpu.VMEM((tm, tn), F32)],
        compiler_params=_cparams("parallel", "parallel", "arbitrary"),
    )(a, b)


def _dproj_matmul_normbwd(dproj, w, x, g, dxn, name, tm=1024, tk=1024, ride=None):
    t, n = dproj.shape
    d = w.shape[0]
    tm, tk = min(tm, t), _divisor_tile(n, tk)
    nk = n // tk
    ni = t // tm
    nr = ride.n if ride else 0

    def body(*refs):
        dp_ref, w_ref, x_ref, g_ref, dxn_ref = refs[:5]
        dx_ref, dg_ref = refs[5 + nr:7 + nr]
        rin, rout = refs[5:5 + nr], refs[7 + nr:7 + 2 * nr]
        sems, acc_ref = refs[7 + 2 * nr:-1], refs[-1]
        i, k = pl.program_id(0), pl.program_id(1)
        if ride:
            @pl.when((i == 0) & (k == 0))
            def _():
                ride.start(rin, rout, sems)

        @pl.when(k == 0)
        def _():
            acc_ref[...] = jnp.zeros_like(acc_ref)

        acc_ref[...] += lax.dot_general(dp_ref[...], w_ref[...], NT_DIMS, preferred_element_type=F32)

        @pl.when(k == nk - 1)
        def _():
            dhn = acc_ref[...]
            xf = x_ref[...]
            r = lax.rsqrt(jnp.mean(xf * xf, axis=-1, keepdims=True) + NORM_EPS)
            xh = xf * r
            dy = dhn * g_ref[...]
            dx_ref[...] = dxn_ref[...] + r * (dy - xh * jnp.mean(dy * xh, axis=-1, keepdims=True))
            part = jnp.sum(dhn * xh, axis=0, keepdims=True)

            @pl.when(i == 0)
            def _():
                dg_ref[...] = part

            @pl.when(i > 0)
            def _():
                dg_ref[...] += part

        if ride:
            @pl.when((i == ni - 1) & (k == nk - 1))
            def _():
                ride.wait(rin, rout, sems)

    return pl.pallas_call(
        body, name=name, grid=(ni, nk),
        in_specs=[pl.BlockSpec((tm, tk), lambda i, k: (i, k)), pl.BlockSpec((d, tk), lambda i, k: (0, k)),
                  pl.BlockSpec((tm, d), lambda i, k: (i, 0)), pl.BlockSpec((1, d), lambda i, k: (0, 0)),
                  pl.BlockSpec((tm, d), lambda i, k: (i, 0))] + (ride.in_specs if ride else []),
        out_specs=[pl.BlockSpec((tm, d), lambda i, k: (i, 0)), pl.BlockSpec((1, d), lambda i, k: (0, 0))]
        + (ride.out_specs if ride else []),
        out_shape=[SDS((t, d), F32), SDS((1, d), F32)] + (ride.out_shape if ride else []),
        scratch_shapes=(ride.scratch if ride else []) + [pltpu.VMEM((tm, d), F32)],
        compiler_params=_cparams("arbitrary", "arbitrary"),
    )(dproj, w, x, g, dxn, *(ride.arrays if ride else []))


def _loss_grad(y, target, name, tm=512):
    t, d = y.shape
    inv_d = 1.0 / d

    def body(y_ref, t_ref, part_ref, dy_ref):
        i = pl.program_id(0)
        err = y_ref[...] - t_ref[...]
        dy_ref[...] = err * inv_d
        part = jnp.sum(err * err, axis=0, keepdims=True) * (0.5 * inv_d)

        @pl.when(i == 0)
        def _():
            part_ref[...] = part

        @pl.when(i > 0)
        def _():
            part_ref[...] += part

    return pl.pallas_call(
        body, name=name, grid=(t // tm,),
        in_specs=[pl.BlockSpec((tm, d), lambda i: (i, 0)), pl.BlockSpec((tm, d), lambda i: (i, 0))],
        out_specs=[pl.BlockSpec((1, d), lambda i: (0, 0)), pl.BlockSpec((tm, d), lambda i: (i, 0))],
        out_shape=[SDS((1, d), F32), SDS((t, d), F32)],
        compiler_params=_cparams("arbitrary"),
    )(y, target)


CONV_ROWS = 32
CONV_COLS = 512


def _conv_chunk(src_ref, base, w_ref, width, r0, c0, flip):
    acc = None
    for k in range(width):
        off = base + r0 + ((width - 1 - k) if flip else (k - (width - 1)))
        term = src_ref[pl.ds(off, CONV_ROWS), pl.ds(c0, CONV_COLS)] * w_ref[pl.ds(k, 1), pl.ds(c0, CONV_COLS)]
        acc = term if acc is None else acc + term
    return acc


def _conv_weight_grad(dw_ref, d_ref, src_ref, base, width, tt, e):
    for c0 in range(0, e, CONV_COLS):
        for k in range(width):
            acc = None
            for r0 in range(0, tt, CONV_ROWS):
                prod = (d_ref[pl.ds(r0, CONV_ROWS), pl.ds(c0, CONV_COLS)]
                        * src_ref[pl.ds(base + r0 - (width - 1) + k, CONV_ROWS), pl.ds(c0, CONV_COLS)])
                part = _rows8(prod)
                acc = part if acc is None else acc + part
            dw_ref[pl.ds(k, 1), pl.ds(c0, CONV_COLS)] += jnp.sum(acc, axis=0, keepdims=True)


def _shifted_copies(dst_ref, src_ref, c0, length, sign):
    lo, hi = (SUBLANES, length) if sign < 0 else (0, length - SUBLANES)
    for b in range(SUBLANES):
        for r0 in range(lo, hi, CONV_ROWS):
            n = min(CONV_ROWS, hi - r0)
            dst_ref[b, pl.ds(r0, n), :] = src_ref[pl.ds(r0 + sign * b, n), pl.ds(c0, CONV_COLS)]


def _conv_aligned(copies_ref, base, w_ref, width, r0, c0, sign):
    acc = None
    for d in range(width):
        a, b = divmod(d, SUBLANES)
        term = (copies_ref[b, pl.ds(base + r0 + sign * SUBLANES * a, CONV_ROWS), :]
                * w_ref[pl.ds(width - 1 - d, 1), pl.ds(c0, CONV_COLS)])
        acc = term if acc is None else acc + term
    return acc


def _conv_weight_grad_aligned(dw_ref, d_ref, copies_ref, base, width, tt, c0):
    for d in range(width):
        a, b = divmod(d, SUBLANES)
        acc = None
        for r0 in range(0, tt, CONV_ROWS):
            prod = (d_ref[pl.ds(r0, CONV_ROWS), pl.ds(c0, CONV_COLS)]
                    * copies_ref[b, pl.ds(base + r0 - SUBLANES * a, CONV_ROWS), :])
            part = _rows8(prod)
            acc = part if acc is None else acc + part
        dw_ref[pl.ds(width - 1 - d, 1), pl.ds(c0, CONV_COLS)] += jnp.sum(acc, axis=0, keepdims=True)


LN_ROWS = 16


def _a_mid_fwd(proj, cw, cb, lg, lb, width, name, tt=256):
    t, e3 = proj.shape
    e = e3 // 3
    halo = CONF_HALO

    def body(p_ref, cw_ref, cb_ref, lg_ref, lb_ref, y_ref, u1_ref, ubuf, shifted):
        i = pl.program_id(0)

        @pl.when(i == 0)
        def _():
            ubuf[pl.ds(0, halo), :] = jnp.zeros((halo, e), F32)

        @pl.when(i > 0)
        def _():
            ubuf[pl.ds(0, halo), :] = ubuf[pl.ds(tt, halo), :]

        for r0 in range(0, tt, CONV_ROWS):
            val = p_ref[pl.ds(r0, CONV_ROWS), pl.ds(0, e)].astype(F32)
            gate = p_ref[pl.ds(r0, CONV_ROWS), pl.ds(e, e)].astype(F32)
            ubuf[pl.ds(halo + r0, CONV_ROWS), :] = val * _sigmoid(gate)
        for c0 in range(0, e, CONV_COLS):
            _shifted_copies(shifted, ubuf, c0, halo + tt, -1)
            for r0 in range(0, tt, CONV_ROWS):
                acc = _conv_aligned(shifted, halo, cw_ref, width, r0, c0, -1)
                u1_ref[pl.ds(r0, CONV_ROWS), pl.ds(c0, CONV_COLS)] = acc + cb_ref[:, pl.ds(c0, CONV_COLS)]
        for r0 in range(0, tt, LN_ROWS):
            u = u1_ref[pl.ds(r0, LN_ROWS), :]
            mu = jnp.mean(u, axis=-1, keepdims=True)
            dlt = u - mu
            var = jnp.mean(dlt * dlt, axis=-1, keepdims=True)
            u2 = (dlt * lax.rsqrt(var + NORM_EPS)) * lg_ref[...] + lb_ref[...]
            z = p_ref[pl.ds(r0, LN_ROWS), pl.ds(2 * e, e)].astype(F32)
            y_ref[pl.ds(r0, LN_ROWS), :] = (_silu(u2) * _silu(z)).astype(BF16)

    return pl.pallas_call(
        body, name=name, grid=(t // tt,),
        in_specs=[pl.BlockSpec((tt, e3), lambda i: (i, 0)), pl.BlockSpec(cw.shape, lambda i: (0, 0)),
                  pl.BlockSpec((1, e), lambda i: (0, 0)), pl.BlockSpec((1, e), lambda i: (0, 0)),
                  pl.BlockSpec((1, e), lambda i: (0, 0))],
        out_specs=[pl.BlockSpec((tt, e), lambda i: (i, 0)), pl.BlockSpec((tt, e), lambda i: (i, 0))],
        out_shape=[SDS((t, e), BF16), SDS((t, e), F32)],
        scratch_shapes=[pltpu.VMEM((halo + tt, e), F32), pltpu.VMEM((SUBLANES, halo + tt, CONV_COLS), F32)],
        compiler_params=_cparams("arbitrary"),
    )(proj, cw, cb, lg, lb)


def _a_mid_bwd(proj, u1, dyz, cw, lg, lb, width, name, tt=256):
    t, e3 = proj.shape
    e = e3 // 3
    halo = CONF_HALO
    nt = t // tt
    hb = tt // halo

    def body(p_ref, pp_ref, u1_ref, dy_ref, cw_ref, lg_ref, lb_ref,
             dp_ref, dcw_ref, dcb_ref, dlg_ref, dlb_ref, ubuf, dbuf, shifted, acc_cb, acc_lg, acc_lb):
        i = pl.program_id(0)
        ti = nt - 1 - i

        @pl.when(i == 0)
        def _():
            dbuf[pl.ds(tt, halo), :] = jnp.zeros((halo, e), F32)
            dcw_ref[...] = jnp.zeros_like(dcw_ref)
            acc_cb[...] = jnp.zeros_like(acc_cb)
            acc_lg[...] = jnp.zeros_like(acc_lg)
            acc_lb[...] = jnp.zeros_like(acc_lb)

        @pl.when(i > 0)
        def _():
            dbuf[pl.ds(tt, halo), :] = dbuf[pl.ds(0, halo), :]

        keep = (ti > 0).astype(F32)
        ubuf[pl.ds(0, halo), :] = keep * (pp_ref[:, pl.ds(0, e)].astype(F32) * _sigmoid(pp_ref[:, pl.ds(e, e)].astype(F32)))
        for r0 in range(0, tt, CONV_ROWS):
            val = p_ref[pl.ds(r0, CONV_ROWS), pl.ds(0, e)].astype(F32)
            gate = p_ref[pl.ds(r0, CONV_ROWS), pl.ds(e, e)].astype(F32)
            ubuf[pl.ds(halo + r0, CONV_ROWS), :] = val * _sigmoid(gate)

        for r0 in range(0, tt, LN_ROWS):
            rows = pl.ds(r0, LN_ROWS)
            u = u1_ref[rows, :]
            mu = jnp.mean(u, axis=-1, keepdims=True)
            dlt = u - mu
            var = jnp.mean(dlt * dlt, axis=-1, keepdims=True)
            rstd = lax.rsqrt(var + NORM_EPS)
            xh = dlt * rstd
            u2 = xh * lg_ref[...] + lb_ref[...]
            s2 = _sigmoid(u2)
            u3 = u2 * s2
            z = p_ref[rows, pl.ds(2 * e, e)].astype(F32)
            sz = _sigmoid(z)
            dy = dy_ref[rows, :].astype(F32)
            dp_ref[rows, pl.ds(2 * e, e)] = (dy * u3 * (sz * (1.0 + z * (1.0 - sz)))).astype(BF16)
            du2 = (dy * (z * sz)) * (s2 * (1.0 + u2 * (1.0 - s2)))
            acc_lg[...] += _rows8(du2 * xh)
            acc_lb[...] += _rows8(du2)
            dxh = du2 * lg_ref[...]
            m1 = jnp.mean(dxh, axis=-1, keepdims=True)
            m2 = jnp.mean(dxh * xh, axis=-1, keepdims=True)
            du1 = rstd * (dxh - m1 - xh * m2)
            dbuf[rows, :] = du1
            acc_cb[...] += _rows8(du1)

        for c0 in range(0, e, CONV_COLS):
            _shifted_copies(shifted, dbuf, c0, tt + halo, 1)
            for r0 in range(0, tt, CONV_ROWS):
                du0 = _conv_aligned(shifted, 0, cw_ref, width, r0, c0, 1)
                rows, cols = pl.ds(r0, CONV_ROWS), pl.ds(c0, CONV_COLS)
                val = p_ref[rows, cols].astype(F32)
                sg = _sigmoid(p_ref[rows, pl.ds(e + c0, CONV_COLS)].astype(F32))
                dp_ref[rows, cols] = (du0 * sg).astype(BF16)
                dp_ref[rows, pl.ds(e + c0, CONV_COLS)] = (du0 * val * sg * (1.0 - sg)).astype(BF16)
            _shifted_copies(shifted, ubuf, c0, halo + tt, -1)
            _conv_weight_grad_aligned(dcw_ref, dbuf, shifted, halo, width, tt, c0)

        @pl.when(i == nt - 1)
        def _():
            dcb_ref[...] = jnp.sum(acc_cb[...], axis=0, keepdims=True)
            dlg_ref[...] = jnp.sum(acc_lg[...], axis=0, keepdims=True)
            dlb_ref[...] = jnp.sum(acc_lb[...], axis=0, keepdims=True)

    vec = pl.BlockSpec((1, e), lambda i: (0, 0))
    return pl.pallas_call(
        body, name=name, grid=(nt,),
        in_specs=[pl.BlockSpec((tt, e3), lambda i: (nt - 1 - i, 0)),
                  pl.BlockSpec((halo, e3), lambda i: (jnp.maximum((nt - 1 - i) * hb - 1, 0), 0)),
                  pl.BlockSpec((tt, e), lambda i: (nt - 1 - i, 0)), pl.BlockSpec((tt, e), lambda i: (nt - 1 - i, 0)),
                  pl.BlockSpec(cw.shape, lambda i: (0, 0)), vec, vec],
        out_specs=[pl.BlockSpec((tt, e3), lambda i: (nt - 1 - i, 0)), pl.BlockSpec(cw.shape, lambda i: (0, 0)), vec, vec, vec],
        out_shape=[SDS((t, e3), BF16), SDS(cw.shape, F32), SDS((1, e), F32), SDS((1, e), F32), SDS((1, e), F32)],
        scratch_shapes=[pltpu.VMEM((halo + tt, e), F32), pltpu.VMEM((tt + halo, e), F32),
                        pltpu.VMEM((SUBLANES, halo + tt, CONV_COLS), F32),
                        pltpu.VMEM((SUBLANES, e), F32), pltpu.VMEM((SUBLANES, e), F32), pltpu.VMEM((SUBLANES, e), F32)],
        compiler_params=_cparams("arbitrary"),
    )(proj, proj, u1, dyz, cw, lg, lb)


def _c_mid_fwd(proj, cw, width, name, tt=256):
    t, e4 = proj.shape
    e = e4 // 4
    halo = SHORT_HALO

    def body(p_ref, cw_ref, y_ref, wbuf):
        i = pl.program_id(0)

        @pl.when(i == 0)
        def _():
            wbuf[pl.ds(0, halo), :] = jnp.zeros((halo, e), F32)

        @pl.when(i > 0)
        def _():
            wbuf[pl.ds(0, halo), :] = wbuf[pl.ds(tt, halo), :]

        for r0 in range(0, tt, CONV_ROWS):
            rows = pl.ds(r0, CONV_ROWS)
            wbuf[pl.ds(halo + r0, CONV_ROWS), :] = p_ref[rows, pl.ds(2 * e, e)].astype(F32) * p_ref[rows, pl.ds(0, e)].astype(F32)
        for c0 in range(0, e, CONV_COLS):
            for r0 in range(0, tt, CONV_ROWS):
                rows = pl.ds(r0, CONV_ROWS)
                cv = _conv_chunk(wbuf, halo, cw_ref, width, r0, c0, False)
                bg = p_ref[rows, pl.ds(e + c0, CONV_COLS)].astype(F32)
                z = p_ref[rows, pl.ds(3 * e + c0, CONV_COLS)].astype(F32)
                y_ref[rows, pl.ds(c0, CONV_COLS)] = ((bg * cv) * _silu(z)).astype(BF16)

    return pl.pallas_call(
        body, name=name, grid=(t // tt,),
        in_specs=[pl.BlockSpec((tt, e4), lambda i: (i, 0)), pl.BlockSpec(cw.shape, lambda i: (0, 0))],
        out_specs=pl.BlockSpec((tt, e), lambda i: (i, 0)),
        out_shape=SDS((t, e), BF16),
        scratch_shapes=[pltpu.VMEM((halo + tt, e), F32)],
        compiler_params=_cparams("arbitrary"),
    )(proj, cw)


def _c_mid_bwd(proj, dyz, cw, width, name, tt=256):
    t, e4 = proj.shape
    e = e4 // 4
    halo = SHORT_HALO
    nt = t // tt
    hb = tt // halo

    def body(p_ref, pp_ref, dy_ref, cw_ref, dp_ref, dcw_ref, wbuf, dbuf):
        i = pl.program_id(0)
        ti = nt - 1 - i

        @pl.when(i == 0)
        def _():
            dbuf[pl.ds(tt, halo), :] = jnp.zeros((halo, e), F32)
            dcw_ref[...] = jnp.zeros_like(dcw_ref)

        @pl.when(i > 0)
        def _():
            dbuf[pl.ds(tt, halo), :] = dbuf[pl.ds(0, halo), :]

        keep = (ti > 0).astype(F32)
        wbuf[pl.ds(0, halo), :] = keep * (pp_ref[:, pl.ds(2 * e, e)].astype(F32) * pp_ref[:, pl.ds(0, e)].astype(F32))
        for r0 in range(0, tt, CONV_ROWS):
            rows = pl.ds(r0, CONV_ROWS)
            wbuf[pl.ds(halo + r0, CONV_ROWS), :] = p_ref[rows, pl.ds(2 * e, e)].astype(F32) * p_ref[rows, pl.ds(0, e)].astype(F32)
        for c0 in range(0, e, CONV_COLS):
            for r0 in range(0, tt, CONV_ROWS):
                rows, cols = pl.ds(r0, CONV_ROWS), pl.ds(c0, CONV_COLS)
                cv = _conv_chunk(wbuf, halo, cw_ref, width, r0, c0, False)
                bg = p_ref[rows, pl.ds(e + c0, CONV_COLS)].astype(F32)
                z = p_ref[rows, pl.ds(3 * e + c0, CONV_COLS)].astype(F32)
                sz = _sigmoid(z)
                dyz_c = dy_ref[rows, cols].astype(F32)
                dy = dyz_c * (z * sz)
                dp_ref[rows, pl.ds(3 * e + c0, CONV_COLS)] = (dyz_c * (bg * cv) * (sz * (1.0 + z * (1.0 - sz)))).astype(BF16)
                dp_ref[rows, pl.ds(e + c0, CONV_COLS)] = (dy * cv).astype(BF16)
                dbuf[rows, cols] = dy * bg
        for c0 in range(0, e, CONV_COLS):
            for r0 in range(0, tt, CONV_ROWS):
                rows, cols = pl.ds(r0, CONV_ROWS), pl.ds(c0, CONV_COLS)
                dw = _conv_chunk(dbuf, 0, cw_ref, width, r0, c0, True)
                dp_ref[rows, pl.ds(2 * e + c0, CONV_COLS)] = (dw * p_ref[rows, cols].astype(F32)).astype(BF16)
                dp_ref[rows, cols] = (dw * p_ref[rows, pl.ds(2 * e + c0, CONV_COLS)].astype(F32)).astype(BF16)
        _conv_weight_grad(dcw_ref, dbuf, wbuf, halo, width, tt, e)

    return pl.pallas_call(
        body, name=name, grid=(nt,),
        in_specs=[pl.BlockSpec((tt, e4), lambda i: (nt - 1 - i, 0)),
                  pl.BlockSpec((halo, e4), lambda i: (jnp.maximum((nt - 1 - i) * hb - 1, 0), 0)),
                  pl.BlockSpec((tt, e), lambda i: (nt - 1 - i, 0)), pl.BlockSpec(cw.shape, lambda i: (0, 0))],
        out_specs=[pl.BlockSpec((tt, e4), lambda i: (nt - 1 - i, 0)), pl.BlockSpec(cw.shape, lambda i: (0, 0))],
        out_shape=[SDS((t, e4), BF16), SDS(cw.shape, F32)],
        scratch_shapes=[pltpu.VMEM((halo + tt, e), F32), pltpu.VMEM((tt + halo, e), F32)],
        compiler_params=_cparams("arbitrary"),
    )(proj, proj, dyz, cw)


def _b_prep_fwd(proj, flog, fbias, qg, kg, heads, name, tt=256):
    t, e4 = proj.shape
    e = e4 // 4
    scale = HEAD_DIM ** -0.5 * LOG2E

    def body(q_ref, k_ref, fl_ref, fb_ref, qg_ref, kg_ref, qs_ref, kn_ref, c_ref, ct_ref, carry):
        i = pl.program_id(0)

        @pl.when(i == 0)
        def _():
            carry[...] = jnp.zeros_like(carry)

        for h in range(heads):
            cols = pl.ds(h * HEAD_DIM, HEAD_DIM)
            qh = q_ref[:, cols].astype(F32)
            r = lax.rsqrt(jnp.mean(qh * qh, axis=-1, keepdims=True) + NORM_EPS)
            qs_ref[:, cols] = (((qh * r) * qg_ref[:, cols]) * scale).astype(BF16)
            kh = k_ref[:, cols].astype(F32)
            r = lax.rsqrt(jnp.mean(kh * kh, axis=-1, keepdims=True) + NORM_EPS)
            kn_ref[:, cols] = ((kh * r) * kg_ref[:, cols]).astype(BF16)

        a = fl_ref[...] + fb_ref[...]
        lf = jnp.minimum(a, 0.0) - jnp.log(1.0 + jnp.exp(-jnp.abs(a)))
        tri = (lax.broadcasted_iota(jnp.int32, (tt, tt), 0) >= lax.broadcasted_iota(jnp.int32, (tt, tt), 1)).astype(BF16)
        c = _tri_matmul(tri, lf) + carry[...]
        c_ref[...] = c
        ct_ref[...] = (c * LOG2E).T
        carry[...] = c_ref[pl.ds(tt - 1, 1), :]

    return pl.pallas_call(
        body, name=name, grid=(t // tt,),
        in_specs=[pl.BlockSpec((tt, e), lambda i: (i, 0)), pl.BlockSpec((tt, e), lambda i: (i, 1)),
                  pl.BlockSpec((tt, LANES), lambda i: (i, 0)), pl.BlockSpec((1, LANES), lambda i: (0, 0)),
                  pl.BlockSpec((1, e), lambda i: (0, 0)), pl.BlockSpec((1, e), lambda i: (0, 0))],
        out_specs=[pl.BlockSpec((tt, e), lambda i: (i, 0)), pl.BlockSpec((tt, e), lambda i: (i, 0)),
                   pl.BlockSpec((tt, LANES), lambda i: (i, 0)), pl.BlockSpec((LANES, tt), lambda i: (0, i))],
        out_shape=[SDS((t, e), BF16), SDS((t, e), BF16), SDS((t, LANES), F32), SDS((LANES, t), F32)],
        scratch_shapes=[pltpu.VMEM((1, LANES), F32)],
        compiler_params=_cparams("arbitrary"),
    )(proj, proj, flog, fbias, qg, kg)


ATT_BLOCK = 1024
ATT_CHUNK = 512
NEG_BIG = -1e30
LOG2E = 1.4426950408889634
LN2 = 0.6931471805599453


def _flash_fwd(qs, kn, proj, ck, heads, name):
    t, e = qs.shape
    blk = min(ATT_BLOCK, t)
    cw = min(ATT_CHUNK, blk // 2)
    nq, nch = t // blk, blk // cw
    assert nch % 2 == 0

    def body(q_ref, k_ref, v_ref, ck_ref, z_ref, o_ref, y_ref, m_ref, l_ref, s_a, s_b):
        i = pl.program_id(1)
        q = q_ref[...]
        bufs = (s_a, s_b)

        def key_rows(j, c):
            return pl.ds(pl.multiple_of(j * blk, blk) + c * cw, cw)

        def logits(j, c):
            bufs[c % 2][...] = (lax.dot_general(q, k_ref[key_rows(j, c), :], NT_DIMS, preferred_element_type=F32)
                            - ck_ref[j][:, c * cw:(c + 1) * cw])

        def weights(c, m, l, masked):
            s = bufs[c % 2][...]
            if masked:
                keep = lax.broadcasted_iota(jnp.int32, (blk, cw), 0) >= (lax.broadcasted_iota(jnp.int32, (blk, cw), 1) + c * cw)
                s = jnp.where(keep, s, NEG_BIG)
            m_new = jnp.maximum(m, jnp.ceil(jnp.max(s, axis=-1, keepdims=True)))
            alpha = jnp.exp2(m - m_new)
            p = jnp.exp2(s - m_new).astype(BF16)
            return m_new, alpha * l + jnp.sum(p.astype(F32), axis=-1, keepdims=True), alpha, p

        def block(j, carry, masked):
            m, l, acc = carry
            for c in range(nch):
                if c + 1 < nch:
                    logits(j, c + 1)
                elif not masked:
                    logits(j + 1, 0)
                m, l, alpha, p = weights(c, m, l, masked)
                acc = alpha * acc + jnp.dot(p, v_ref[key_rows(j, c), :], preferred_element_type=F32)
            return m, l, acc

        logits(0, 0)
        carry = (jnp.full((blk, 1), NEG_BIG, F32), jnp.zeros((blk, 1), F32), jnp.zeros((blk, HEAD_DIM), F32))
        carry = lax.fori_loop(0, i, lambda j, cr: block(j, cr, False), carry)
        m, l, acc = block(i, carry, True)
        o = acc / l
        o_ref[...] = o
        y_ref[...] = (o * _silu(z_ref[...].astype(F32))).astype(BF16)
        m_ref[...] = jnp.broadcast_to(m, (blk, LANES))
        l_ref[...] = jnp.broadcast_to(l, (blk, LANES))

    head_all = pl.BlockSpec((t, HEAD_DIM), lambda h, i: (0, h))
    tile = pl.BlockSpec((blk, HEAD_DIM), lambda h, i: (i, h))
    stat = pl.BlockSpec((None, blk, LANES), lambda h, i: (h, i, 0))
    return pl.pallas_call(
        body, name=name, grid=(heads, nq),
        in_specs=[tile, head_all, pl.BlockSpec((t, HEAD_DIM), lambda h, i: (0, 2 * heads + h)),
                  pl.BlockSpec((None, nq, 1, blk), lambda h, i: (h, 0, 0, 0)),
                  pl.BlockSpec((blk, HEAD_DIM), lambda h, i: (i, 3 * heads + h))],
        out_specs=[tile, tile, stat, stat],
        out_shape=[SDS((t, e), F32), SDS((t, e), BF16), SDS((heads, t, LANES), F32), SDS((heads, t, LANES), F32)],
        scratch_shapes=[pltpu.VMEM((blk, cw), F32), pltpu.VMEM((blk, cw), F32)],
        compiler_params=_cparams("parallel", "arbitrary"),
    )(qs, kn, proj, ck, proj)


def _b_bwd_pre(dyz, o, proj, lstat, heads, name, tt=256):
    t, e = o.shape

    def body(dy_ref, o_ref, z_ref, l_ref, do_ref, dz_ref, dl_ref):
        for h in range(heads):
            cols = pl.ds(h * HEAD_DIM, HEAD_DIM)
            z = z_ref[:, cols].astype(F32)
            sz = _sigmoid(z)
            dy = dy_ref[:, cols].astype(F32)
            of = o_ref[:, cols]
            dos = ((dy * (z * sz)) / l_ref[h][:, 0:1]).astype(BF16)
            do_ref[:, cols] = dos
            dz_ref[:, cols] = (dy * of * (sz * (1.0 + z * (1.0 - sz)))).astype(BF16)
            dl_ref[h] = jnp.broadcast_to(jnp.sum(dos.astype(F32) * of, axis=-1, keepdims=True), (tt, LANES))

    stat = pl.BlockSpec((heads, tt, LANES), lambda i: (0, i, 0))
    return pl.pallas_call(
        body, name=name, grid=(t // tt,),
        in_specs=[pl.BlockSpec((tt, e), lambda i: (i, 0)), pl.BlockSpec((tt, e), lambda i: (i, 0)),
                  pl.BlockSpec((tt, e), lambda i: (i, 3)), stat],
        out_specs=[pl.BlockSpec((tt, e), lambda i: (i, 0)), pl.BlockSpec((tt, e), lambda i: (i, 0)), stat],
        out_shape=[SDS((t, e), BF16), SDS((t, e), BF16), SDS((heads, t, LANES), F32)],
        compiler_params=_cparams("parallel"),
    )(dyz, o, proj, lstat)


def _flash_bwd(qs, kn, proj, ck, dos, mstat, delta, heads, name):
    t, e = qs.shape
    blk = min(ATT_BLOCK, t)
    cw = min(ATT_CHUNK, blk // 2)
    nq, nch = t // blk, blk // cw
    assert nch % 2 == 0

    def body(q_ref, do_ref, m_ref, dl_ref, k_ref, v_ref, ck_ref, dq_ref, dk_ref, dv_ref, dc_ref, s_a, s_b, d_a, d_b):
        i = pl.program_id(1)

        @pl.when(i == 0)
        def _():
            dk_ref[...] = jnp.zeros_like(dk_ref)
            dv_ref[...] = jnp.zeros_like(dv_ref)
            dc_ref[...] = jnp.zeros_like(dc_ref)

        q = q_ref[...]
        do = do_ref[...]
        mrow = m_ref[:, 0:1]
        dl = dl_ref[:, 0:1]
        sbuf, dbuf = (s_a, s_b), (d_a, d_b)

        def key_rows(j, c):
            return pl.ds(pl.multiple_of(j * blk, blk) + c * cw, cw)

        def products(j, c):
            rows = key_rows(j, c)
            sbuf[c % 2][...] = (lax.dot_general(q, k_ref[rows, :], NT_DIMS, preferred_element_type=F32)
                            - ck_ref[j][:, c * cw:(c + 1) * cw])
            dbuf[c % 2][...] = lax.dot_general(do, v_ref[rows, :], NT_DIMS, preferred_element_type=F32)

        def weights(c, masked):
            p = jnp.exp2(sbuf[c % 2][...] - mrow)
            if masked:
                keep = lax.broadcasted_iota(jnp.int32, (blk, cw), 0) >= (lax.broadcasted_iota(jnp.int32, (blk, cw), 1) + c * cw)
                p = jnp.where(keep, p, 0.0)
            p = p.astype(BF16)
            ds = p.astype(F32) * (dbuf[c % 2][...] - dl)
            return p, ds.astype(BF16), jnp.sum(ds, axis=0, keepdims=True)

        def outputs(j, c, p, dsb, colsum, dq):
            rows = key_rows(j, c)
            dv_ref[rows, :] += lax.dot_general(p, do, TN_DIMS, preferred_element_type=F32)
            dk_ref[rows, :] += lax.dot_general(dsb, q, TN_DIMS, preferred_element_type=F32)
            dc_ref[j, :, pl.ds(c * cw, cw)] -= colsum
            return dq + jnp.dot(dsb, k_ref[rows, :], preferred_element_type=F32)

        def block(j, dq, masked):
            products(j, 1)
            for c in range(nch):
                p, dsb, colsum = weights(c, masked)
                if c + 2 < nch:
                    products(j, c + 2)
                elif c + 2 == nch and not masked:
                    products(j + 1, 0)
                dq = outputs(j, c, p, dsb, colsum, dq)
            return dq

        products(0, 0)
        dq = lax.fori_loop(0, i, lambda j, acc: block(j, acc, False), jnp.zeros((blk, HEAD_DIM), F32))
        dq_ref[...] = block(i, dq, True)

    tile = pl.BlockSpec((blk, HEAD_DIM), lambda h, i: (i, h))
    stat = pl.BlockSpec((None, blk, LANES), lambda h, i: (h, i, 0))
    head_all = pl.BlockSpec((t, HEAD_DIM), lambda h, i: (0, h))
    cspec = pl.BlockSpec((None, nq, 1, blk), lambda h, i: (h, 0, 0, 0))
    return pl.pallas_call(
        body, name=name, grid=(heads, nq),
        in_specs=[tile, tile, stat, stat, head_all, pl.BlockSpec((t, HEAD_DIM), lambda h, i: (0, 2 * heads + h)), cspec],
        out_specs=[tile, head_all, head_all, cspec],
        out_shape=[SDS((t, e), F32), SDS((t, e), F32), SDS((t, e), F32), SDS((heads, nq, 1, blk), F32)],
        scratch_shapes=[pltpu.VMEM((blk, cw), F32)] * 4,
        compiler_params=_cparams("parallel", "arbitrary"),
    )(qs, dos, mstat, delta, kn, proj, ck)


def _b_prep_bwd(dqs, dkn, dv, dz, proj, qg, kg, dct, flog, fbias, heads, name, tt=256):
    t, e4 = proj.shape
    e = e4 // 4
    nt = t // tt
    scale = HEAD_DIM ** -0.5

    def body(dq_ref, dk_ref, dv_ref, dz_ref, q_ref, k_ref, qg_ref, kg_ref, dc_ref, fl_ref, fb_ref,
             dp_ref, dqg_ref, dkg_ref, dfb_ref, carry, dlf, acc_q, acc_k, acc_f):
        i = pl.program_id(0)

        @pl.when(i == 0)
        def _():
            carry[...] = jnp.zeros_like(carry)
            acc_q[...] = jnp.zeros_like(acc_q)
            acc_k[...] = jnp.zeros_like(acc_k)
            acc_f[...] = jnp.zeros_like(acc_f)

        for h in range(heads):
            cols = pl.ds(h * HEAD_DIM, HEAD_DIM)
            for src_ref, d_ref, g_ref, acc, mult, off in ((q_ref, dq_ref, qg_ref, acc_q, scale, 0),
                                                          (k_ref, dk_ref, kg_ref, acc_k, LN2, e)):
                xf = src_ref[:, cols].astype(F32)
                r = lax.rsqrt(jnp.mean(xf * xf, axis=-1, keepdims=True) + NORM_EPS)
                xh = xf * r
                dn = d_ref[:, cols] * mult
                acc[...] += _rows8(dn * xh)
                dxh = dn * g_ref[:, cols]
                dp_ref[:, pl.ds(off + h * HEAD_DIM, HEAD_DIM)] = (
                    r * (dxh - xh * jnp.mean(dxh * xh, axis=-1, keepdims=True))).astype(BF16)
        dp_ref[:, pl.ds(2 * e, e)] = dv_ref[...].astype(BF16)
        dp_ref[:, pl.ds(3 * e, e)] = dz_ref[...]

        tri = (lax.broadcasted_iota(jnp.int32, (tt, tt), 0) <= lax.broadcasted_iota(jnp.int32, (tt, tt), 1)).astype(BF16)
        dlf[...] = _tri_matmul(tri, dc_ref[...]) + carry[...]
        carry[...] = dlf[pl.ds(0, 1), :]
        a = fl_ref[...] + fb_ref[...]
        dfl = dlf[...] * _sigmoid(-a)
        dp_ref[:, pl.ds(4 * e, LANES)] = dfl.astype(BF16)
        acc_f[...] += _rows8(dfl)

        @pl.when(i == nt - 1)
        def _():
            dqg_ref[...] = jnp.sum(acc_q[...], axis=0, keepdims=True)
            dkg_ref[...] = jnp.sum(acc_k[...], axis=0, keepdims=True)
            dfb_ref[...] = jnp.sum(acc_f[...], axis=0, keepdims=True)

    rev = lambda i: (nt - 1 - i, 0)
    vec_e = pl.BlockSpec((1, e), lambda i: (0, 0))
    vec = pl.BlockSpec((1, LANES), lambda i: (0, 0))
    wide = pl.BlockSpec((tt, e), rev)
    lane = pl.BlockSpec((tt, LANES), rev)
    return pl.pallas_call(
        body, name=name, grid=(nt,),
        in_specs=[wide, wide, wide, wide, wide, pl.BlockSpec((tt, e), lambda i: (nt - 1 - i, 1)), vec_e, vec_e, lane, lane, vec],
        out_specs=[pl.BlockSpec((tt, e4 + LANES), rev), vec, vec, vec],
        out_shape=[SDS((t, e4 + LANES), BF16), SDS((1, LANES), F32), SDS((1, LANES), F32), SDS((1, LANES), F32)],
        scratch_shapes=[pltpu.VMEM((1, LANES), F32), pltpu.VMEM((tt, LANES), F32), pltpu.VMEM((SUBLANES, LANES), F32),
                        pltpu.VMEM((SUBLANES, LANES), F32), pltpu.VMEM((SUBLANES, LANES), F32)],
        compiler_params=_cparams("arbitrary"),
    )(dqs, dkn, dv, dz, proj, proj, qg, kg, dct, flog, fbias)


def _b_fwd(h, b_norm, wb_pad, wb_out, qg, kg, fbias, heads, tag, ride=None):
    t = h.shape[0]
    e = wb_out.shape[0]
    blk = min(ATT_BLOCK, t)
    hn, proj, *rode = _norm_matmul(h, b_norm, wb_pad[:, :4 * e], f"b_in_proj_{tag}", ride=ride)
    flog = _matmul_f32out(hn, wb_pad[:, 4 * e:], f"b_forget_proj_{tag}")
    qs, kn, _, ct = _b_prep_fwd(proj, flog, fbias, qg, kg, heads, f"b_prep_fwd_{tag}")
    ck = ct.reshape(LANES, t // blk, 1, blk)
    o, y, mstat, lstat = _flash_fwd(qs, kn, proj, ck, heads, f"b_attention_fwd_{tag}")
    sv = dict(x=h, hn=hn, proj=proj, flog=flog, qs=qs, kn=kn, ck=ck, o=o, y=y, mstat=mstat, lstat=lstat)
    return _out_matmul_residual(y, wb_out, h, f"b_out_proj_{tag}"), sv, rode


def _b_bwd(dh, sv, b_norm, wb_pad, wb_out, qg, kg, fbias, heads, tag, ride=None):
    t = dh.shape[0]
    e = wb_out.shape[0]
    gb = {}
    dyz = _matmul_nt(dh, wb_out, f"b_out_bwd_{tag}")
    gb["w_out"] = _matmul_tn(sv["y"], dh, f"b_out_wgrad_{tag}", tm=2048)
    dos, dz, delta = _b_bwd_pre(dyz, sv["o"], sv["proj"], sv["lstat"], heads, f"b_gate_bwd_{tag}")
    dqs, dkn, dv, dc = _flash_bwd(sv["qs"], sv["kn"], sv["proj"], sv["ck"], dos, sv["mstat"], delta, heads,
                                  f"b_attention_bwd_{tag}")
    dct = jnp.pad(dc.reshape(heads, t).T, ((0, 0), (0, LANES - heads)))
    dproj, dqg, dkg, dfb = _b_prep_bwd(dqs, dkn, dv, dz, sv["proj"], qg, kg, dct, sv["flog"], fbias, heads,
                                       f"b_prep_bwd_{tag}")
    gb["w_in"] = _matmul_tn(sv["hn"], dproj, f"b_in_wgrad_{tag}")[:, :4 * e + heads]
    dh, dg, *rode = _dproj_matmul_normbwd(dproj, wb_pad, sv["x"], b_norm, dh, f"b_in_bwd_{tag}", ride=ride)
    gb["norm"], gb["q_norm"], gb["k_norm"], gb["f_bias"] = dg, dqg, dkg, dfb[:, :heads]
    return dh, gb, rode


def _sum_adamw(recv, w, m, v, name, tr=256):
    nl, r, c = w.shape
    tr = min(tr, r)

    def body(g_ref, w_ref, m_ref, v_ref, go_ref, d_ref, mo_ref, vo_ref):
        g = g_ref[0].astype(F32)
        for s in range(1, N_DEV):
            g = g + g_ref[s].astype(F32)
        go_ref[...] = g
        mn = ADAM_B1 * m_ref[...] + (1.0 - ADAM_B1) * g
        vn = ADAM_B2 * v_ref[...] + (1.0 - ADAM_B2) * (g * g)
        m_hat = mn / (1.0 - ADAM_B1 ** ADAM_STEP)
        v_hat = vn / (1.0 - ADAM_B2 ** ADAM_STEP)
        d_ref[...] = -ADAM_LR * (m_hat / (jnp.sqrt(v_hat) + ADAM_EPS) + ADAM_WD * w_ref[...])
        mo_ref[...] = mn
        vo_ref[...] = vn

    blk = pl.BlockSpec((None, tr, c), lambda l, i: (l, i, 0))
    return pl.pallas_call(
        body, name=name, grid=(nl, r // tr),
        in_specs=[pl.BlockSpec((N_DEV, None, tr, c), lambda l, i: (0, l, i, 0)), blk, blk, blk],
        out_specs=[blk, blk, blk, blk],
        out_shape=[SDS(w.shape, F32)] * 4,
        compiler_params=_cparams("parallel", "parallel"),
    )(recv, w, m, v)


def _unshard(g, axis):
    g = jnp.moveaxis(g, 0, axis)
    return g.reshape(g.shape[:axis] + (g.shape[axis] * g.shape[axis + 1],) + g.shape[axis + 2:])


def _to_slabs(full, axis):
    n = full.shape[axis]
    s = full.reshape(full.shape[:axis] + (N_DEV, n // N_DEV) + full.shape[axis + 1:])
    return jnp.moveaxis(s, axis, 0)


def _pack_rows(parts, lead):
    flat = [p.reshape(p.shape[:lead] + (-1,)) for p in parts]
    cat = jnp.concatenate(flat, axis=-1)
    n = cat.shape[-1]
    pad = (-n) % (SUBLANES * LANES)
    cat = jnp.pad(cat, [(0, 0)] * lead + [(0, pad)])
    return cat.reshape(cat.shape[:lead] + ((n + pad) // LANES, LANES))


def _unpack_rows(packed, shapes, lead):
    flat = packed.reshape(packed.shape[:lead] + (-1,))
    out, off = [], 0
    for shp in shapes:
        size = int(np.prod(shp))
        out.append(flat[..., off:off + size].reshape(packed.shape[:lead] + tuple(shp)))
        off += size
    return out


def _pad_rows(w, rows):
    return jnp.pad(w, ((0, rows - w.shape[0]), (0, 0)))


def kernel(x, a_norm, a_w_in, a_conv_w, a_conv_b, a_ln_g, a_ln_b, a_w_out, b_norm, b_w_in, b_f_bias, b_q_norm, b_k_norm, b_w_out, c_norm, c_w_in, c_conv_w, c_w_out, loss_target, m_a_norm, m_a_w_in, m_a_conv_w, m_a_conv_b, m_a_ln_g, m_a_ln_b, m_a_w_out, m_b_norm, m_b_w_in, m_b_f_bias, m_b_q_norm, m_b_k_norm, m_b_w_out, m_c_norm, m_c_w_in, m_c_conv_w, m_c_w_out, v_a_norm, v_a_w_in, v_a_conv_w, v_a_conv_b, v_a_ln_g, v_a_ln_b, v_a_w_out, v_b_norm, v_b_w_in, v_b_f_bias, v_b_q_norm, v_b_k_norm, v_b_w_out, v_c_norm, v_c_w_in, v_c_conv_w, v_c_w_out):
    t, d = x.shape[1], x.shape[2]
    e = a_w_out.shape[1] * N_DEV
    heads = b_f_bias.shape[1]
    n_a, n_b, n_c = a_norm.shape[0], b_norm.shape[0], c_norm.shape[0]
    depth = n_a + n_b + n_c
    ka, kc = a_conv_w.shape[1], c_conv_w.shape[1]
    assert e == heads * HEAD_DIM and n_b == 1 and n_c == 1 and x.shape[0] == 1

    layers = [(i % 3, i // 3) for i in range(depth)]

    def mat_shards(kind, j):
        w_in, w_out = ((a_w_in, a_w_out), (b_w_in, b_w_out), (c_w_in, c_w_out))[kind]
        return [w_in[j].astype(BF16), w_out[j].astype(BF16)]

    def full_mats(kind, gathered):
        w_in, w_out = _unshard(gathered[0], 1), _unshard(gathered[1], 0)
        return (jnp.pad(w_in, ((0, 0), (0, LANES - heads))) if kind == 1 else w_in), w_out

    small_names = ["a_norm", "a_conv_w", "a_conv_b", "a_ln_g", "a_ln_b", "c_norm", "c_conv_w"]
    small = dict(a_norm=a_norm, a_conv_w=a_conv_w, a_conv_b=a_conv_b, a_ln_g=a_ln_g, a_ln_b=a_ln_b,
                 c_norm=c_norm, c_conv_w=c_conv_w)
    small_pack = _pack_rows([small[k] for k in small_names], 0)
    first = _all_gather(mat_shards(*layers[0]) + [small_pack], "all_gather_first_layer")
    sm = _unpack_rows(first[2], [small[k].shape for k in small_names], 1)
    g_a_norm = _unshard(sm[0], 1)
    g_a_conv_w = _unshard(sm[1], 2)
    g_a_conv_b = _unshard(sm[2], 1)
    g_a_ln_g = _unshard(sm[3], 1)
    g_a_ln_b = _unshard(sm[4], 1)
    g_c_norm = _unshard(sm[5], 1)
    g_c_conv_w = _unshard(sm[6], 2)

    cw_a = [_pad_rows(g_a_conv_w[j], CONF_HALO) for j in range(n_a)]
    cw_c = _pad_rows(g_c_conv_w[0], SHORT_HALO)
    qg = jnp.tile(b_q_norm, (1, heads))
    kg = jnp.tile(b_k_norm, (1, heads))
    fbias = jnp.pad(b_f_bias, ((0, 0), (0, LANES - heads)))

    h = x[0]
    saved, weights = [], [full_mats(layers[0][0], first[:2])]
    for i, (kind, j) in enumerate(layers):
        tag = f"l{i}"
        w_in, w_out = weights[i]
        ride = _Ride("gather", mat_shards(*layers[i + 1])) if i + 1 < depth else None
        if kind == 0:
            hn, proj, *rode = _norm_matmul(h, g_a_norm[j:j + 1], w_in, f"a_in_proj_{tag}", ride=ride)
            y, u1 = _a_mid_fwd(proj, cw_a[j], g_a_conv_b[j:j + 1], g_a_ln_g[j:j + 1], g_a_ln_b[j:j + 1], ka, f"a_mid_fwd_{tag}")
            saved.append(dict(x=h, hn=hn, proj=proj, u1=u1, y=y))
            h = _out_matmul_residual(y, w_out, h, f"a_out_proj_{tag}")
        elif kind == 1:
            h, sv, rode = _b_fwd(h, b_norm, w_in, w_out, qg, kg, fbias, heads, tag, ride=ride)
            saved.append(sv)
        else:
            hn, proj, *rode = _norm_matmul(h, g_c_norm, w_in, f"c_in_proj_{tag}", ride=ride)
            y = _c_mid_fwd(proj, cw_c, kc, f"c_mid_fwd_{tag}")
            saved.append(dict(x=h, hn=hn, proj=proj, y=y))
            h = _out_matmul_residual(y, w_out, h, f"c_out_proj_{tag}")
        if ride:
            weights.append(full_mats(layers[i + 1][0], rode))

    loss_part, dh = _loss_grad(h, loss_target[0], "loss_and_grad")
    loss_local = jnp.sum(loss_part).reshape(1, 1)

    ga = dict(norm=[None] * n_a, conv_w=[None] * n_a, conv_b=[None] * n_a, ln_g=[None] * n_a, ln_b=[None] * n_a)
    gb, gc = None, {}
    recv_mats = [None] * depth
    pending = None
    for i in reversed(range(depth)):
        kind, j = layers[i]
        tag = f"l{i}"
        sv = saved[i]
        w_in, w_out = weights[i]
        ride = _Ride("exchange", pending) if pending is not None else None
        if kind == 0:
            dyz = _matmul_nt(dh, w_out, f"a_out_bwd_{tag}")
            gw_out = _matmul_tn(sv["y"], dh, f"a_out_wgrad_{tag}", tm=2048)
            dproj, dcw, dcb, dlg, dlb = _a_mid_bwd(sv["proj"], sv["u1"], dyz, cw_a[j], g_a_ln_g[j:j + 1], g_a_ln_b[j:j + 1],
                                                   ka, f"a_mid_bwd_{tag}")
            gw_in = _matmul_tn(sv["hn"], dproj, f"a_in_wgrad_{tag}")
            dh, dg, *rode = _dproj_matmul_normbwd(dproj, w_in, sv["x"], g_a_norm[j:j + 1], dh, f"a_in_bwd_{tag}", ride=ride)
            ga["norm"][j], ga["conv_w"][j], ga["conv_b"][j], ga["ln_g"][j], ga["ln_b"][j] = dg[0], dcw[:ka], dcb[0], dlg[0], dlb[0]
        elif kind == 1:
            dh, gb, rode = _b_bwd(dh, sv, b_norm, w_in, w_out, qg, kg, fbias, heads, tag, ride=ride)
            gw_in, gw_out = gb["w_in"], gb["w_out"]
        else:
            dyz = _matmul_nt(dh, w_out, f"c_out_bwd_{tag}")
            gw_out = _matmul_tn(sv["y"], dh, f"c_out_wgrad_{tag}", tm=2048)
            dproj, dcw = _c_mid_bwd(sv["proj"], dyz, cw_c, kc, f"c_mid_bwd_{tag}")
            gw_in = _matmul_tn(sv["hn"], dproj, f"c_in_wgrad_{tag}")
            dh, dg, *rode = _dproj_matmul_normbwd(dproj, w_in, sv["x"], g_c_norm, dh, f"c_in_bwd_{tag}", ride=ride)
            gc["norm"], gc["conv_w"] = dg, dcw[:kc][None]
        if ride:
            recv_mats[i + 1] = rode
        pending = [_to_slabs(gw_in, 1), _to_slabs(gw_out, 0)]
    grad_x = dh[None]

    sharded_small = [(jnp.stack(ga["norm"]), 1), (jnp.stack(ga["conv_w"]), 2), (jnp.stack(ga["conv_b"]), 1),
                     (jnp.stack(ga["ln_g"]), 1), (jnp.stack(ga["ln_b"]), 1), (gc["norm"], 1), (gc["conv_w"], 2)]
    repl_small = [gb["norm"], gb["f_bias"], gb["q_norm"], gb["k_norm"], loss_local]
    small_slabs = _pack_rows([_to_slabs(g, ax) for g, ax in sharded_small]
                             + [jnp.broadcast_to(g[None], (N_DEV,) + g.shape) for g in repl_small], 1)
    last = _exchange(pending + [small_slabs], "exchange_last_gradients")
    recv_mats[0] = last[:2]

    outs = {}
    mat_w = dict(a_w_in=(a_w_in, m_a_w_in, v_a_w_in), a_w_out=(a_w_out, m_a_w_out, v_a_w_out),
                 b_w_in=(b_w_in, m_b_w_in, v_b_w_in), b_w_out=(b_w_out, m_b_w_out, v_b_w_out),
                 c_w_in=(c_w_in, m_c_w_in, v_c_w_in), c_w_out=(c_w_out, m_c_w_out, v_c_w_out))
    for kind, prefix in enumerate("abc"):
        members = [i for i, (k, _) in enumerate(layers) if k == kind]
        for which, name in enumerate((f"{prefix}_w_in", f"{prefix}_w_out")):
            recv = jnp.stack([recv_mats[i][which] for i in members], axis=1)
            outs[name] = _sum_adamw(recv, *mat_w[name], f"adamw_{name}")

    small_order = small_names + ["b_norm", "b_f_bias", "b_q_norm", "b_k_norm", "loss"]
    no_state = jnp.zeros((1, 1), F32)
    small_w = dict(a_norm=(a_norm, m_a_norm, v_a_norm), a_conv_w=(a_conv_w, m_a_conv_w, v_a_conv_w),
                   a_conv_b=(a_conv_b, m_a_conv_b, v_a_conv_b), a_ln_g=(a_ln_g, m_a_ln_g, v_a_ln_g),
                   a_ln_b=(a_ln_b, m_a_ln_b, v_a_ln_b), c_norm=(c_norm, m_c_norm, v_c_norm),
                   c_conv_w=(c_conv_w, m_c_conv_w, v_c_conv_w), b_norm=(b_norm, m_b_norm, v_b_norm),
                   b_f_bias=(b_f_bias, m_b_f_bias, v_b_f_bias), b_q_norm=(b_q_norm, m_b_q_norm, v_b_q_norm),
                   b_k_norm=(b_k_norm, m_b_k_norm, v_b_k_norm), loss=(no_state, no_state, no_state))
    packs = [_pack_rows([small_w[k][q] for k in small_order], 0)[None] for q in range(3)]
    small_out = _sum_adamw(last[2][:, None], *packs, "adamw_vectors")
    shapes = [small_w[k][0].shape for k in small_order]
    unpacked = [_unpack_rows(o[0], shapes, 0) for o in small_out]
    for idx, name in enumerate(small_order):
        outs[name] = tuple(unpacked[q][idx] for q in range(4))

    loss = outs["loss"][0].reshape(())
    order = ["a_norm", "a_w_in", "a_conv_w", "a_conv_b", "a_ln_g", "a_ln_b", "a_w_out", "b_norm", "b_w_in", "b_f_bias",
             "b_q_norm", "b_k_norm", "b_w_out", "c_norm", "c_w_in", "c_conv_w", "c_w_out"]
    return (loss, grad_x, *[outs[k][0] for k in order], *[outs[k][1] for k in order],
            *[outs[k][2] for k in order], *[outs[k][3] for k in order])
```

```python
import jax
import jax.numpy as jnp
import numpy as np
from jax import lax
from jax.experimental import pallas as pl
from jax.experimental.pallas import tpu as pltpu

F32 = jnp.float32
BF16 = jnp.bfloat16
SDS = jax.ShapeDtypeStruct

NORM_EPS = 1e-6
ADAM_LR = 0.001
ADAM_B1 = 0.9
ADAM_B2 = 0.999
ADAM_EPS = 1e-08
ADAM_WD = 0.01
ADAM_STEP = 10

N_DEV = 8
LANES = 128
SUBLANES = 8
HEAD_DIM = 128
CONF_HALO = 32
SHORT_HALO = 8
VMEM_LIMIT = 56 * 1024 * 1024

NT_DIMS = (((1,), (1,)), ((), ()))
TN_DIMS = (((0,), (0,)), ((), ()))
MESH = pl.DeviceIdType.MESH
ANY = pl.BlockSpec(memory_space=pl.ANY)


def _cparams(*sem):
    return pltpu.CompilerParams(dimension_semantics=sem, vmem_limit_bytes=VMEM_LIMIT)


def _divisor_tile(n, cap):
    return max(m for m in range(LANES, min(n, cap) + 1, LANES) if n % m == 0)


def _sigmoid(x):
    return 1.0 / (1.0 + jnp.exp(-x))


def _silu(x):
    return x * _sigmoid(x)


def _dsilu(x):
    s = _sigmoid(x)
    return s * (1.0 + x * (1.0 - s))


def _rows8(v):
    out = v[0:SUBLANES]
    for a in range(1, v.shape[0] // SUBLANES):
        out = out + v[a * SUBLANES:(a + 1) * SUBLANES]
    return out


def _split3(v):
    hi = v.astype(BF16)
    r1 = v - hi.astype(F32)
    mid = r1.astype(BF16)
    lo = (r1 - mid.astype(F32)).astype(BF16)
    return hi, mid, lo


def _tri_matmul(tri, v):
    hi, mid, lo = _split3(v)
    return (jnp.dot(tri, hi, preferred_element_type=F32) + jnp.dot(tri, mid, preferred_element_type=F32)
            + jnp.dot(tri, lo, preferred_element_type=F32))


def _position():
    return lax.axis_index("x"), lax.axis_index("y"), lax.axis_index("c")


def _all_gather(shards, name):
    n = len(shards)

    def body(*refs):
        xs, outs = refs[:n], refs[n:2 * n]
        send_sems, recv_sems, local_sems = refs[2 * n:]
        x, y, c = _position()
        me, sibling = (x, y, c), (x, y, 1 - c)
        chips = [(1 - x, y), (x, 1 - y), (1 - x, 1 - y)]

        def slot(a, px, py, pc):
            return outs[a].at[4 * px + 2 * py + pc]

        def copy(a, k, block, to, src=None):
            return pltpu.make_async_remote_copy(
                src_ref=slot(a, *block) if src is None else src, dst_ref=slot(a, *block),
                send_sem=send_sems.at[a, k], recv_sem=recv_sems.at[a, k], device_id=to, device_id_type=MESH)

        started = []
        mine = []
        for a in range(n):
            cp = pltpu.make_async_copy(xs[a], slot(a, *me), local_sems.at[a])
            cp.start()
            mine.append(cp)
        for a in range(n):
            first = [copy(a, 0, me, sibling, src=xs[a])]
            first += [copy(a, 1 + j, me, (*chip, c), src=xs[a]) for j, chip in enumerate(chips)]
            for cp in first:
                cp.start()
            started += first
        for a in range(n):
            for j, chip in enumerate(chips):
                copy(a, 1 + j, (*chip, c), me).wait_recv()
                fwd = copy(a, 4 + j, (*chip, c), sibling)
                fwd.start()
                started.append(fwd)
        for a in range(n):
            copy(a, 0, sibling, me).wait_recv()
            for j, chip in enumerate(chips):
                copy(a, 4 + j, (*chip, 1 - c), me).wait_recv()
        for cp in started:
            cp.wait_send()
        for cp in mine:
            cp.wait()

    return pl.pallas_call(
        body, name=name,
        out_shape=[SDS((N_DEV,) + s.shape, s.dtype) for s in shards],
        in_specs=[ANY] * n, out_specs=[ANY] * n,
        scratch_shapes=[pltpu.SemaphoreType.DMA((n, 7)), pltpu.SemaphoreType.DMA((n, 7)), pltpu.SemaphoreType.DMA((n,))],
    )(*shards)


def _exchange(slabs, name):
    n = len(slabs)

    def body(*refs):
        ins, outs = refs[:n], refs[n:2 * n]
        send_sems, recv_sems, local_sems = refs[2 * n:]
        x, y, c = _position()
        me = 4 * x + 2 * y + c
        peers = [(x ^ bx, y ^ by, c ^ bc) for bx in (0, 1) for by in (0, 1) for bc in (0, 1)][1:]

        def copy(a, k, peer):
            pid = 4 * peer[0] + 2 * peer[1] + peer[2]
            return pltpu.make_async_remote_copy(
                src_ref=ins[a].at[pid], dst_ref=outs[a].at[me],
                send_sem=send_sems.at[a, k], recv_sem=recv_sems.at[a, k], device_id=peer, device_id_type=MESH)

        def arrival(a, k, peer):
            pid = 4 * peer[0] + 2 * peer[1] + peer[2]
            return pltpu.make_async_remote_copy(
                src_ref=ins[a].at[pid], dst_ref=outs[a].at[pid],
                send_sem=send_sems.at[a, k], recv_sem=recv_sems.at[a, k], device_id=peer, device_id_type=MESH)

        mine = []
        for a in range(n):
            cp = pltpu.make_async_copy(ins[a].at[me], outs[a].at[me], local_sems.at[a])
            cp.start()
            mine.append(cp)
        started = []
        for a in range(n):
            for k, peer in enumerate(peers):
                cp = copy(a, k, peer)
                cp.start()
                started.append(cp)
        for a in range(n):
            for k, peer in enumerate(peers):
                arrival(a, k, peer).wait_recv()
        for cp in started:
            cp.wait_send()
        for cp in mine:
            cp.wait()

    return pl.pallas_call(
        body, name=name,
        out_shape=[SDS(s.shape, s.dtype) for s in slabs],
        in_specs=[ANY] * n, out_specs=[ANY] * n,
        scratch_shapes=[pltpu.SemaphoreType.DMA((n, 7)), pltpu.SemaphoreType.DMA((n, 7)), pltpu.SemaphoreType.DMA((n,))],
    )(*slabs)


class _Ride:
    def __init__(self, kind, arrays):
        self.kind, self.arrays, self.n = kind, list(arrays), len(arrays)
        self.in_specs = [ANY] * self.n
        self.out_specs = [ANY] * self.n
        self.out_shape = [SDS(((N_DEV,) + a.shape) if kind == "gather" else a.shape, a.dtype) for a in self.arrays]
        self.scratch = [pltpu.SemaphoreType.DMA((self.n, 7)), pltpu.SemaphoreType.DMA((self.n, 7)),
                        pltpu.SemaphoreType.DMA((self.n,))]

    def _copies(self, ins, outs, sems, arriving):
        send_sems, recv_sems, local_sems = sems
        x, y, c = _position()
        me = 4 * x + 2 * y + c
        peers = [(x ^ bx, y ^ by, c ^ bc) for bx in (0, 1) for by in (0, 1) for bc in (0, 1)][1:]
        local, remote = [], []
        for a in range(self.n):
            own = ins[a] if self.kind == "gather" else ins[a].at[me]
            local.append(pltpu.make_async_copy(own, outs[a].at[me], local_sems.at[a]))
            for k, peer in enumerate(peers):
                pid = 4 * peer[0] + 2 * peer[1] + peer[2]
                remote.append(pltpu.make_async_remote_copy(
                    src_ref=ins[a] if self.kind == "gather" else ins[a].at[pid],
                    dst_ref=outs[a].at[pid if arriving else me],
                    send_sem=send_sems.at[a, k], recv_sem=recv_sems.at[a, k], device_id=peer, device_id_type=MESH))
        return local, remote

    def start(self, ins, outs, sems):
        local, sends = self._copies(ins, outs, sems, False)
        for cp in local + sends:
            cp.start()

    def wait(self, ins, outs, sems):
        local, arrivals = self._copies(ins, outs, sems, True)
        for cp in arrivals:
            cp.wait_recv()
        for cp in arrivals:
            cp.wait_send()
        for cp in local:
            cp.wait()


def _norm_matmul(x, g, w, name, tm=2048, tn=1024, ride=None):
    t, d = x.shape
    n = w.shape[1]
    tm, tn = min(tm, t), min(tn, n)
    ni, nj = t // tm, n // tn
    nr = ride.n if ride else 0

    def body(*refs):
        x_ref, g_ref, w_ref = refs[:3]
        hn_ref, o_ref = refs[3 + nr:5 + nr]
        rin, rout, sems = refs[3:3 + nr], refs[5 + nr:5 + 2 * nr], refs[5 + 2 * nr:]
        i, j = pl.program_id(0), pl.program_id(1)
        if ride:
            @pl.when((i == 0) & (j == 0))
            def _():
                ride.start(rin, rout, sems)

        @pl.when(j == 0)
        def _():
            xf = x_ref[...]
            r = lax.rsqrt(jnp.mean(xf * xf, axis=-1, keepdims=True) + NORM_EPS)
            hn_ref[...] = ((xf * r) * g_ref[...]).astype(BF16)

        o_ref[...] = jnp.dot(hn_ref[...], w_ref[...], preferred_element_type=F32).astype(o_ref.dtype)
        if ride:
            @pl.when((i == ni - 1) & (j == nj - 1))
            def _():
                ride.wait(rin, rout, sems)

    return pl.pallas_call(
        body, name=name, grid=(ni, nj),
        in_specs=[pl.BlockSpec((tm, d), lambda i, j: (i, 0)), pl.BlockSpec((1, d), lambda i, j: (0, 0)),
                  pl.BlockSpec((d, tn), lambda i, j: (0, j))] + (ride.in_specs if ride else []),
        out_specs=[pl.BlockSpec((tm, d), lambda i, j: (i, 0)), pl.BlockSpec((tm, tn), lambda i, j: (i, j))]
        + (ride.out_specs if ride else []),
        out_shape=[SDS((t, d), BF16), SDS((t, n), BF16)] + (ride.out_shape if ride else []),
        scratch_shapes=ride.scratch if ride else [],
        compiler_params=_cparams("arbitrary", "arbitrary") if ride else _cparams("parallel", "arbitrary"),
    )(x, g, w, *(ride.arrays if ride else []))


def _matmul_f32out(a, w, name, tm=512):
    t, k = a.shape
    n = w.shape[1]

    def body(a_ref, w_ref, o_ref):
        o_ref[...] = jnp.dot(a_ref[...], w_ref[...], preferred_element_type=F32)

    return pl.pallas_call(
        body, name=name, grid=(t // tm,),
        in_specs=[pl.BlockSpec((tm, k), lambda i: (i, 0)), pl.BlockSpec((k, n), lambda i: (0, 0))],
        out_specs=pl.BlockSpec((tm, n), lambda i: (i, 0)),
        out_shape=SDS((t, n), F32),
        compiler_params=_cparams("parallel"),
    )(a, w)


def _out_matmul_residual(y, w, x, name, tm=512):
    t, e = y.shape
    d = w.shape[1]

    def body(y_ref, w_ref, x_ref, o_ref):
        o_ref[...] = x_ref[...] + jnp.dot(y_ref[...], w_ref[...], preferred_element_type=F32)

    return pl.pallas_call(
        body, name=name, grid=(t // tm,),
        in_specs=[pl.BlockSpec((tm, e), lambda i: (i, 0)), pl.BlockSpec((e, d), lambda i: (0, 0)),
                  pl.BlockSpec((tm, d), lambda i: (i, 0))],
        out_specs=pl.BlockSpec((tm, d), lambda i: (i, 0)),
        out_shape=SDS((t, d), F32),
        compiler_params=_cparams("parallel"),
    )(y, w, x)


def _matmul_nt(a, w, name, tm=512):
    t, d = a.shape
    e = w.shape[0]

    def body(a_ref, w_ref, o_ref):
        o_ref[...] = lax.dot_general(a_ref[...].astype(BF16), w_ref[...], NT_DIMS,
                                     preferred_element_type=F32).astype(o_ref.dtype)

    return pl.pallas_call(
        body, name=name, grid=(t // tm,),
        in_specs=[pl.BlockSpec((tm, d), lambda i: (i, 0)), pl.BlockSpec((e, d), lambda i: (0, 0))],
        out_specs=pl.BlockSpec((tm, e), lambda i: (i, 0)),
        out_shape=SDS((t, e), BF16),
        compiler_params=_cparams("parallel"),
    )(a, w)


def _matmul_tn(a, b, name, out_dtype=BF16, tm=1024, tn=1024, tk=512):
    t, m = a.shape
    n = b.shape[1]
    tm, tn = min(tm, m), _divisor_tile(n, 2 * tn)
    nk = t // tk

    def body(a_ref, b_ref, o_ref, acc_ref):
        k = pl.program_id(2)

        @pl.when(k == 0)
        def _():
            acc_ref[...] = jnp.zeros_like(acc_ref)

        acc_ref[...] += lax.dot_general(a_ref[...].astype(BF16), b_ref[...].astype(BF16), TN_DIMS,
                                        preferred_element_type=F32)

        @pl.when(k == nk - 1)
        def _():
            o_ref[...] = acc_ref[...].astype(o_ref.dtype)

    return pl.pallas_call(
        body, name=name, grid=(m // tm, n // tn, nk),
        in_specs=[pl.BlockSpec((tk, tm), lambda i, j, k: (k, i)), pl.BlockSpec((tk, tn), lambda i, j, k: (k, j))],
        out_specs=pl.BlockSpec((tm, tn), lambda i, j, k: (i, j)),
        out_shape=SDS((m, n), out_dtype),
        scratch_shapes=[pltpu.VMEM((tm, tn), F32)],
        compiler_params=_cparams("parallel", "parallel", "arbitrary"),
    )(a, b)


def _dproj_matmul_normbwd(dproj, w, x, g, dxn, name, tm=1024, tk=1024, ride=None):
    t, n = dproj.shape
    d = w.shape[0]
    tm, tk = min(tm, t), _divisor_tile(n, tk)
    nk = n // tk
    ni = t // tm
    nr = ride.n if ride else 0

    def body(*refs):
        dp_ref, w_ref, x_ref, g_ref, dxn_ref = refs[:5]
        dx_ref, dg_ref = refs[5 + nr:7 + nr]
        rin, rout = refs[5:5 + nr], refs[7 + nr:7 + 2 * nr]
        sems, acc_ref = refs[7 + 2 * nr:-1], refs[-1]
        i, k = pl.program_id(0), pl.program_id(1)
        if ride:
            @pl.when((i == 0) & (k == 0))
            def _():
                ride.start(rin, rout, sems)

        @pl.when(k == 0)
        def _():
            acc_ref[...] = jnp.zeros_like(acc_ref)

        acc_ref[...] += lax.dot_general(dp_ref[...], w_ref[...], NT_DIMS, preferred_element_type=F32)

        @pl.when(k == nk - 1)
        def _():
            dhn = acc_ref[...]
            xf = x_ref[...]
            r = lax.rsqrt(jnp.mean(xf * xf, axis=-1, keepdims=True) + NORM_EPS)
            xh = xf * r
            dy = dhn * g_ref[...]
            dx_ref[...] = dxn_ref[...] + r * (dy - xh * jnp.mean(dy * xh, axis=-1, keepdims=True))
            part = jnp.sum(dhn * xh, axis=0, keepdims=True)

            @pl.when(i == 0)
            def _():
                dg_ref[...] = part

            @pl.when(i > 0)
            def _():
                dg_ref[...] += part

        if ride:
            @pl.when((i == ni - 1) & (k == nk - 1))
            def _():
                ride.wait(rin, rout, sems)

    return pl.pallas_call(
        body, name=name, grid=(ni, nk),
        in_specs=[pl.BlockSpec((tm, tk), lambda i, k: (i, k)), pl.BlockSpec((d, tk), lambda i, k: (0, k)),
                  pl.BlockSpec((tm, d), lambda i, k: (i, 0)), pl.BlockSpec((1, d), lambda i, k: (0, 0)),
                  pl.BlockSpec((tm, d), lambda i, k: (i, 0))] + (ride.in_specs if ride else []),
        out_specs=[pl.BlockSpec((tm, d), lambda i, k: (i, 0)), pl.BlockSpec((1, d), lambda i, k: (0, 0))]
        + (ride.out_specs if ride else []),
        out_shape=[SDS((t, d), F32), SDS((1, d), F32)] + (ride.out_shape if ride else []),
        scratch_shapes=(ride.scratch if ride else []) + [pltpu.VMEM((tm, d), F32)],
        compiler_params=_cparams("arbitrary", "arbitrary"),
    )(dproj, w, x, g, dxn, *(ride.arrays if ride else []))


def _out_matmul_loss(y, w, x, target, name, tm=512):
    t, e = y.shape
    d = w.shape[1]
    inv_d = 1.0 / d

    def body(y_ref, w_ref, x_ref, t_ref, part_ref, dy_ref):
        i = pl.program_id(0)
        err = (x_ref[...] + jnp.dot(y_ref[...], w_ref[...], preferred_element_type=F32)) - t_ref[...]
        dy_ref[...] = err * inv_d
        part = jnp.sum(err * err, axis=0, keepdims=True) * (0.5 * inv_d)

        @pl.when(i == 0)
        def _():
            part_ref[...] = part

        @pl.when(i > 0)
        def _():
            part_ref[...] += part

    return pl.pallas_call(
        body, name=name, grid=(t // tm,),
        in_specs=[pl.BlockSpec((tm, e), lambda i: (i, 0)), pl.BlockSpec((e, d), lambda i: (0, 0)),
                  pl.BlockSpec((tm, d), lambda i: (i, 0)), pl.BlockSpec((tm, d), lambda i: (i, 0))],
        out_specs=[pl.BlockSpec((1, d), lambda i: (0, 0)), pl.BlockSpec((tm, d), lambda i: (i, 0))],
        out_shape=[SDS((1, d), F32), SDS((t, d), F32)],
        compiler_params=_cparams("arbitrary"),
    )(y, w, x, target)


CONV_ROWS = 32
CONV_COLS = 512


def _conv_chunk(src_ref, base, w_ref, width, r0, c0, flip):
    acc = None
    for k in range(width):
        off = base + r0 + ((width - 1 - k) if flip else (k - (width - 1)))
        term = src_ref[pl.ds(off, CONV_ROWS), pl.ds(c0, CONV_COLS)] * w_ref[pl.ds(k, 1), pl.ds(c0, CONV_COLS)]
        acc = term if acc is None else acc + term
    return acc


def _conv_weight_grad(dw_ref, d_ref, src_ref, base, width, tt, e):
    for c0 in range(0, e, CONV_COLS):
        for k in range(width):
            acc = None
            for r0 in range(0, tt, CONV_ROWS):
                prod = (d_ref[pl.ds(r0, CONV_ROWS), pl.ds(c0, CONV_COLS)]
                        * src_ref[pl.ds(base + r0 - (width - 1) + k, CONV_ROWS), pl.ds(c0, CONV_COLS)])
                part = _rows8(prod)
                acc = part if acc is None else acc + part
            dw_ref[pl.ds(k, 1), pl.ds(c0, CONV_COLS)] += jnp.sum(acc, axis=0, keepdims=True)


def _shifted_copies(dst_ref, src_ref, c0, length, sign):
    lo, hi = (SUBLANES, length) if sign < 0 else (0, length - SUBLANES)
    for b in range(SUBLANES):
        for r0 in range(lo, hi, CONV_ROWS):
            n = min(CONV_ROWS, hi - r0)
            dst_ref[b, pl.ds(r0, n), :] = src_ref[pl.ds(r0 + sign * b, n), pl.ds(c0, CONV_COLS)]


def _conv_aligned(copies_ref, base, w_ref, width, r0, c0, sign):
    acc = None
    for d in range(width):
        a, b = divmod(d, SUBLANES)
        term = (copies_ref[b, pl.ds(base + r0 + sign * SUBLANES * a, CONV_ROWS), :]
                * w_ref[pl.ds(width - 1 - d, 1), pl.ds(c0, CONV_COLS)])
        acc = term if acc is None else acc + term
    return acc


def _conv_weight_grad_aligned(dw_ref, d_ref, copies_ref, base, width, tt, c0):
    for d in range(width):
        a, b = divmod(d, SUBLANES)
        acc = None
        for r0 in range(0, tt, CONV_ROWS):
            prod = (d_ref[pl.ds(r0, CONV_ROWS), pl.ds(c0, CONV_COLS)]
                    * copies_ref[b, pl.ds(base + r0 - SUBLANES * a, CONV_ROWS), :])
            part = _rows8(prod)
            acc = part if acc is None else acc + part
        dw_ref[pl.ds(width - 1 - d, 1), pl.ds(c0, CONV_COLS)] += jnp.sum(acc, axis=0, keepdims=True)


LN_ROWS = 16


def _a_mid_fwd(proj, cw, cb, lg, lb, width, name, tt=256):
    t, e3 = proj.shape
    e = e3 // 3
    halo = CONF_HALO

    def body(p_ref, cw_ref, cb_ref, lg_ref, lb_ref, y_ref, u1_ref, ubuf, shifted):
        i = pl.program_id(0)

        @pl.when(i == 0)
        def _():
            ubuf[pl.ds(0, halo), :] = jnp.zeros((halo, e), F32)

        @pl.when(i > 0)
        def _():
            ubuf[pl.ds(0, halo), :] = ubuf[pl.ds(tt, halo), :]

        for r0 in range(0, tt, CONV_ROWS):
            val = p_ref[pl.ds(r0, CONV_ROWS), pl.ds(0, e)].astype(F32)
            gate = p_ref[pl.ds(r0, CONV_ROWS), pl.ds(e, e)].astype(F32)
            ubuf[pl.ds(halo + r0, CONV_ROWS), :] = val * _sigmoid(gate)
        for c0 in range(0, e, CONV_COLS):
            _shifted_copies(shifted, ubuf, c0, halo + tt, -1)
            for r0 in range(0, tt, CONV_ROWS):
                acc = _conv_aligned(shifted, halo, cw_ref, width, r0, c0, -1)
                u1_ref[pl.ds(r0, CONV_ROWS), pl.ds(c0, CONV_COLS)] = acc + cb_ref[:, pl.ds(c0, CONV_COLS)]
        for r0 in range(0, tt, LN_ROWS):
            u = u1_ref[pl.ds(r0, LN_ROWS), :]
            mu = jnp.mean(u, axis=-1, keepdims=True)
            dlt = u - mu
            var = jnp.mean(dlt * dlt, axis=-1, keepdims=True)
            u2 = (dlt * lax.rsqrt(var + NORM_EPS)) * lg_ref[...] + lb_ref[...]
            z = p_ref[pl.ds(r0, LN_ROWS), pl.ds(2 * e, e)].astype(F32)
            y_ref[pl.ds(r0, LN_ROWS), :] = (_silu(u2) * _silu(z)).astype(BF16)

    return pl.pallas_call(
        body, name=name, grid=(t // tt,),
        in_specs=[pl.BlockSpec((tt, e3), lambda i: (i, 0)), pl.BlockSpec(cw.shape, lambda i: (0, 0)),
                  pl.BlockSpec((1, e), lambda i: (0, 0)), pl.BlockSpec((1, e), lambda i: (0, 0)),
                  pl.BlockSpec((1, e), lambda i: (0, 0))],
        out_specs=[pl.BlockSpec((tt, e), lambda i: (i, 0)), pl.BlockSpec((tt, e), lambda i: (i, 0))],
        out_shape=[SDS((t, e), BF16), SDS((t, e), F32)],
        scratch_shapes=[pltpu.VMEM((halo + tt, e), F32), pltpu.VMEM((SUBLANES, halo + tt, CONV_COLS), F32)],
        compiler_params=_cparams("arbitrary"),
    )(proj, cw, cb, lg, lb)


def _a_mid_bwd(proj, u1, dyz, cw, lg, lb, width, name, tt=256, ride=None):
    t, e3 = proj.shape
    e = e3 // 3
    halo = CONF_HALO
    nt = t // tt
    hb = tt // halo
    nr = ride.n if ride else 0

    def body(*refs):
        rin, rout = refs[7:7 + nr], refs[12 + nr:12 + 2 * nr]
        scratch = refs[12 + 2 * nr:]
        sems, own = (scratch[:3], scratch[3:]) if ride else ((), scratch)
        if ride:
            @pl.when(pl.program_id(0) == 0)
            def _():
                ride.start(rin, rout, sems)

        tile(*refs[:7], *refs[7 + nr:12 + nr], *own)
        if ride:
            @pl.when(pl.program_id(0) == nt - 1)
            def _():
                ride.wait(rin, rout, sems)

    def tile(p_ref, pp_ref, u1_ref, dy_ref, cw_ref, lg_ref, lb_ref,
             dp_ref, dcw_ref, dcb_ref, dlg_ref, dlb_ref, ubuf, dbuf, shifted, acc_cb, acc_lg, acc_lb):
        i = pl.program_id(0)
        ti = nt - 1 - i

        @pl.when(i == 0)
        def _():
            dbuf[pl.ds(tt, halo), :] = jnp.zeros((halo, e), F32)
            dcw_ref[...] = jnp.zeros_like(dcw_ref)
            acc_cb[...] = jnp.zeros_like(acc_cb)
            acc_lg[...] = jnp.zeros_like(acc_lg)
            acc_lb[...] = jnp.zeros_like(acc_lb)

        @pl.when(i > 0)
        def _():
            dbuf[pl.ds(tt, halo), :] = dbuf[pl.ds(0, halo), :]

        keep = (ti > 0).astype(F32)
        ubuf[pl.ds(0, halo), :] = keep * (pp_ref[:, pl.ds(0, e)].astype(F32) * _sigmoid(pp_ref[:, pl.ds(e, e)].astype(F32)))
        for r0 in range(0, tt, CONV_ROWS):
            val = p_ref[pl.ds(r0, CONV_ROWS), pl.ds(0, e)].astype(F32)
            gate = p_ref[pl.ds(r0, CONV_ROWS), pl.ds(e, e)].astype(F32)
            ubuf[pl.ds(halo + r0, CONV_ROWS), :] = val * _sigmoid(gate)

        for r0 in range(0, tt, LN_ROWS):
            rows = pl.ds(r0, LN_ROWS)
            u = u1_ref[rows, :]
            mu = jnp.mean(u, axis=-1, keepdims=True)
            dlt = u - mu
            var = jnp.mean(dlt * dlt, axis=-1, keepdims=True)
            rstd = lax.rsqrt(var + NORM_EPS)
            xh = dlt * rstd
            u2 = xh * lg_ref[...] + lb_ref[...]
            s2 = _sigmoid(u2)
            u3 = u2 * s2
            z = p_ref[rows, pl.ds(2 * e, e)].astype(F32)
            sz = _sigmoid(z)
            dy = dy_ref[rows, :].astype(F32)
            dp_ref[rows, pl.ds(2 * e, e)] = (dy * u3 * (sz * (1.0 + z * (1.0 - sz)))).astype(BF16)
            du2 = (dy * (z * sz)) * (s2 * (1.0 + u2 * (1.0 - s2)))
            acc_lg[...] += _rows8(du2 * xh)
            acc_lb[...] += _rows8(du2)
            dxh = du2 * lg_ref[...]
            m1 = jnp.mean(dxh, axis=-1, keepdims=True)
            m2 = jnp.mean(dxh * xh, axis=-1, keepdims=True)
            du1 = rstd * (dxh - m1 - xh * m2)
            dbuf[rows, :] = du1
            acc_cb[...] += _rows8(du1)

        for c0 in range(0, e, CONV_COLS):
            _shifted_copies(shifted, dbuf, c0, tt + halo, 1)
            for r0 in range(0, tt, CONV_ROWS):
                du0 = _conv_aligned(shifted, 0, cw_ref, width, r0, c0, 1)
                rows, cols = pl.ds(r0, CONV_ROWS), pl.ds(c0, CONV_COLS)
                val = p_ref[rows, cols].astype(F32)
                sg = _sigmoid(p_ref[rows, pl.ds(e + c0, CONV_COLS)].astype(F32))
                dp_ref[rows, cols] = (du0 * sg).astype(BF16)
                dp_ref[rows, pl.ds(e + c0, CONV_COLS)] = (du0 * val * sg * (1.0 - sg)).astype(BF16)
            _shifted_copies(shifted, ubuf, c0, halo + tt, -1)
            _conv_weight_grad_aligned(dcw_ref, dbuf, shifted, halo, width, tt, c0)

        @pl.when(i == nt - 1)
        def _():
            dcb_ref[...] = jnp.sum(acc_cb[...], axis=0, keepdims=True)
            dlg_ref[...] = jnp.sum(acc_lg[...], axis=0, keepdims=True)
            dlb_ref[...] = jnp.sum(acc_lb[...], axis=0, keepdims=True)

    vec = pl.BlockSpec((1, e), lambda i: (0, 0))
    return pl.pallas_call(
        body, name=name, grid=(nt,),
        in_specs=[pl.BlockSpec((tt, e3), lambda i: (nt - 1 - i, 0)),
                  pl.BlockSpec((halo, e3), lambda i: (jnp.maximum((nt - 1 - i) * hb - 1, 0), 0)),
                  pl.BlockSpec((tt, e), lambda i: (nt - 1 - i, 0)), pl.BlockSpec((tt, e), lambda i: (nt - 1 - i, 0)),
                  pl.BlockSpec(cw.shape, lambda i: (0, 0)), vec, vec] + (ride.in_specs if ride else []),
        out_specs=[pl.BlockSpec((tt, e3), lambda i: (nt - 1 - i, 0)), pl.BlockSpec(cw.shape, lambda i: (0, 0)), vec, vec, vec]
        + (ride.out_specs if ride else []),
        out_shape=[SDS((t, e3), BF16), SDS(cw.shape, F32), SDS((1, e), F32), SDS((1, e), F32), SDS((1, e), F32)]
        + (ride.out_shape if ride else []),
        scratch_shapes=(ride.scratch if ride else [])
        + [pltpu.VMEM((halo + tt, e), F32), pltpu.VMEM((tt + halo, e), F32),
           pltpu.VMEM((SUBLANES, halo + tt, CONV_COLS), F32),
           pltpu.VMEM((SUBLANES, e), F32), pltpu.VMEM((SUBLANES, e), F32), pltpu.VMEM((SUBLANES, e), F32)],
        compiler_params=_cparams("arbitrary"),
    )(proj, proj, u1, dyz, cw, lg, lb, *(ride.arrays if ride else []))


def _c_mid_fwd(proj, cw, width, name, tt=256):
    t, e4 = proj.shape
    e = e4 // 4
    halo = SHORT_HALO

    def body(p_ref, cw_ref, y_ref, wbuf):
        i = pl.program_id(0)

        @pl.when(i == 0)
        def _():
            wbuf[pl.ds(0, halo), :] = jnp.zeros((halo, e), F32)

        @pl.when(i > 0)
        def _():
            wbuf[pl.ds(0, halo), :] = wbuf[pl.ds(tt, halo), :]

        for r0 in range(0, tt, CONV_ROWS):
            rows = pl.ds(r0, CONV_ROWS)
            wbuf[pl.ds(halo + r0, CONV_ROWS), :] = p_ref[rows, pl.ds(2 * e, e)].astype(F32) * p_ref[rows, pl.ds(0, e)].astype(F32)
        for c0 in range(0, e, CONV_COLS):
            for r0 in range(0, tt, CONV_ROWS):
                rows = pl.ds(r0, CONV_ROWS)
                cv = _conv_chunk(wbuf, halo, cw_ref, width, r0, c0, False)
                bg = p_ref[rows, pl.ds(e + c0, CONV_COLS)].astype(F32)
                z = p_ref[rows, pl.ds(3 * e + c0, CONV_COLS)].astype(F32)
                y_ref[rows, pl.ds(c0, CONV_COLS)] = ((bg * cv) * _silu(z)).astype(BF16)

    return pl.pallas_call(
        body, name=name, grid=(t // tt,),
        in_specs=[pl.BlockSpec((tt, e4), lambda i: (i, 0)), pl.BlockSpec(cw.shape, lambda i: (0, 0))],
        out_specs=pl.BlockSpec((tt, e), lambda i: (i, 0)),
        out_shape=SDS((t, e), BF16),
        scratch_shapes=[pltpu.VMEM((halo + tt, e), F32)],
        compiler_params=_cparams("arbitrary"),
    )(proj, cw)


def _c_mid_bwd(proj, dyz, cw, width, name, tt=256):
    t, e4 = proj.shape
    e = e4 // 4
    halo = SHORT_HALO
    nt = t // tt
    hb = tt // halo

    def body(p_ref, pp_ref, dy_ref, cw_ref, dp_ref, dcw_ref, wbuf, dbuf):
        i = pl.program_id(0)
        ti = nt - 1 - i

        @pl.when(i == 0)
        def _():
            dbuf[pl.ds(tt, halo), :] = jnp.zeros((halo, e), F32)
            dcw_ref[...] = jnp.zeros_like(dcw_ref)

        @pl.when(i > 0)
        def _():
            dbuf[pl.ds(tt, halo), :] = dbuf[pl.ds(0, halo), :]

        keep = (ti > 0).astype(F32)
        wbuf[pl.ds(0, halo), :] = keep * (pp_ref[:, pl.ds(2 * e, e)].astype(F32) * pp_ref[:, pl.ds(0, e)].astype(F32))
        for r0 in range(0, tt, CONV_ROWS):
            rows = pl.ds(r0, CONV_ROWS)
            wbuf[pl.ds(halo + r0, CONV_ROWS), :] = p_ref[rows, pl.ds(2 * e, e)].astype(F32) * p_ref[rows, pl.ds(0, e)].astype(F32)
        for c0 in range(0, e, CONV_COLS):
            for r0 in range(0, tt, CONV_ROWS):
                rows, cols = pl.ds(r0, CONV_ROWS), pl.ds(c0, CONV_COLS)
                cv = _conv_chunk(wbuf, halo, cw_ref, width, r0, c0, False)
                bg = p_ref[rows, pl.ds(e + c0, CONV_COLS)].astype(F32)
                z = p_ref[rows, pl.ds(3 * e + c0, CONV_COLS)].astype(F32)
                sz = _sigmoid(z)
                dyz_c = dy_ref[rows, cols].astype(F32)
                dy = dyz_c * (z * sz)
                dp_ref[rows, pl.ds(3 * e + c0, CONV_COLS)] = (dyz_c * (bg * cv) * (sz * (1.0 + z * (1.0 - sz)))).astype(BF16)
                dp_ref[rows, pl.ds(e + c0, CONV_COLS)] = (dy * cv).astype(BF16)
                dbuf[rows, cols] = dy * bg
        for c0 in range(0, e, CONV_COLS):
            for r0 in range(0, tt, CONV_ROWS):
                rows, cols = pl.ds(r0, CONV_ROWS), pl.ds(c0, CONV_COLS)
                dw = _conv_chunk(dbuf, 0, cw_ref, width, r0, c0, True)
                dp_ref[rows, pl.ds(2 * e + c0, CONV_COLS)] = (dw * p_ref[rows, cols].astype(F32)).astype(BF16)
                dp_ref[rows, cols] = (dw * p_ref[rows, pl.ds(2 * e + c0, CONV_COLS)].astype(F32)).astype(BF16)
        _conv_weight_grad(dcw_ref, dbuf, wbuf, halo, width, tt, e)

    return pl.pallas_call(
        body, name=name, grid=(nt,),
        in_specs=[pl.BlockSpec((tt, e4), lambda i: (nt - 1 - i, 0)),
                  pl.BlockSpec((halo, e4), lambda i: (jnp.maximum((nt - 1 - i) * hb - 1, 0), 0)),
                  pl.BlockSpec((tt, e), lambda i: (nt - 1 - i, 0)), pl.BlockSpec(cw.shape, lambda i: (0, 0))],
        out_specs=[pl.BlockSpec((tt, e4), lambda i: (nt - 1 - i, 0)), pl.BlockSpec(cw.shape, lambda i: (0, 0))],
        out_shape=[SDS((t, e4), BF16), SDS(cw.shape, F32)],
        scratch_shapes=[pltpu.VMEM((halo + tt, e), F32), pltpu.VMEM((tt + halo, e), F32)],
        compiler_params=_cparams("arbitrary"),
    )(proj, proj, dyz, cw)


def _b_prep_fwd(proj, flog, fbias, qg, kg, heads, name, tt=256):
    t, e4 = proj.shape
    e = e4 // 4
    scale = HEAD_DIM ** -0.5 * LOG2E

    def body(q_ref, k_ref, fl_ref, fb_ref, qg_ref, kg_ref, qs_ref, kn_ref, c_ref, ct_ref, carry):
        i = pl.program_id(0)

        @pl.when(i == 0)
        def _():
            carry[...] = jnp.zeros_like(carry)

        for h in range(heads):
            cols = pl.ds(h * HEAD_DIM, HEAD_DIM)
            qh = q_ref[:, cols].astype(F32)
            r = lax.rsqrt(jnp.mean(qh * qh, axis=-1, keepdims=True) + NORM_EPS)
            qs_ref[:, cols] = (((qh * r) * qg_ref[:, cols]) * scale).astype(BF16)
            kh = k_ref[:, cols].astype(F32)
            r = lax.rsqrt(jnp.mean(kh * kh, axis=-1, keepdims=True) + NORM_EPS)
            kn_ref[:, cols] = ((kh * r) * kg_ref[:, cols]).astype(BF16)

        a = fl_ref[...] + fb_ref[...]
        lf = jnp.minimum(a, 0.0) - jnp.log(1.0 + jnp.exp(-jnp.abs(a)))
        tri = (lax.broadcasted_iota(jnp.int32, (tt, tt), 0) >= lax.broadcasted_iota(jnp.int32, (tt, tt), 1)).astype(BF16)
        c = _tri_matmul(tri, lf) + carry[...]
        c_ref[...] = c
        ct_ref[...] = (c * LOG2E).T
        carry[...] = c_ref[pl.ds(tt - 1, 1), :]

    return pl.pallas_call(
        body, name=name, grid=(t // tt,),
        in_specs=[pl.BlockSpec((tt, e), lambda i: (i, 0)), pl.BlockSpec((tt, e), lambda i: (i, 1)),
                  pl.BlockSpec((tt, LANES), lambda i: (i, 0)), pl.BlockSpec((1, LANES), lambda i: (0, 0)),
                  pl.BlockSpec((1, e), lambda i: (0, 0)), pl.BlockSpec((1, e), lambda i: (0, 0))],
        out_specs=[pl.BlockSpec((tt, e), lambda i: (i, 0)), pl.BlockSpec((tt, e), lambda i: (i, 0)),
                   pl.BlockSpec((tt, LANES), lambda i: (i, 0)), pl.BlockSpec((LANES, tt), lambda i: (0, i))],
        out_shape=[SDS((t, e), BF16), SDS((t, e), BF16), SDS((t, LANES), F32), SDS((LANES, t), F32)],
        scratch_shapes=[pltpu.VMEM((1, LANES), F32)],
        compiler_params=_cparams("arbitrary"),
    )(proj, proj, flog, fbias, qg, kg)


ATT_BLOCK = 1024
ATT_CHUNK = 512
NEG_BIG = -1e30
LOG2E = 1.4426950408889634
LN2 = 0.6931471805599453


def _flash_fwd(qs, kn, proj, ck, heads, name):
    t, e = qs.shape
    blk = min(ATT_BLOCK, t)
    cw = min(ATT_CHUNK, blk // 2)
    nq, nch = t // blk, blk // cw
    assert nch % 2 == 0

    def body(q_ref, k_ref, v_ref, ck_ref, z_ref, o_ref, y_ref, m_ref, l_ref, s_a, s_b):
        i = pl.program_id(1)
        q = q_ref[...]
        bufs = (s_a, s_b)

        def key_rows(j, c):
            return pl.ds(pl.multiple_of(j * blk, blk) + c * cw, cw)

        def logits(j, c):
            bufs[c % 2][...] = (lax.dot_general(q, k_ref[key_rows(j, c), :], NT_DIMS, preferred_element_type=F32)
                            - ck_ref[j][:, c * cw:(c + 1) * cw])

        def weights(c, m, l, masked):
            s = bufs[c % 2][...]
            if masked:
                keep = lax.broadcasted_iota(jnp.int32, (blk, cw), 0) >= (lax.broadcasted_iota(jnp.int32, (blk, cw), 1) + c * cw)
                s = jnp.where(keep, s, NEG_BIG)
            m_new = jnp.maximum(m, jnp.ceil(jnp.max(s, axis=-1, keepdims=True)))
            alpha = jnp.exp2(m - m_new)
            p = jnp.exp2(s - m_new).astype(BF16)
            return m_new, alpha * l + jnp.sum(p.astype(F32), axis=-1, keepdims=True), alpha, p

        def block(j, carry, masked):
            m, l, acc = carry
            for c in range(nch):
                if c + 1 < nch:
                    logits(j, c + 1)
                elif not masked:
                    logits(j + 1, 0)
                m, l, alpha, p = weights(c, m, l, masked)
                acc = alpha * acc + jnp.dot(p, v_ref[key_rows(j, c), :], preferred_element_type=F32)
            return m, l, acc

        logits(0, 0)
        carry = (jnp.full((blk, 1), NEG_BIG, F32), jnp.zeros((blk, 1), F32), jnp.zeros((blk, HEAD_DIM), F32))
        carry = lax.fori_loop(0, i, lambda j, cr: block(j, cr, False), carry)
        m, l, acc = block(i, carry, True)
        o = acc / l
        o_ref[...] = o
        y_ref[...] = (o * _silu(z_ref[...].astype(F32))).astype(BF16)
        m_ref[...] = jnp.broadcast_to(m, (blk, LANES))
        l_ref[...] = jnp.broadcast_to(l, (blk, LANES))

    head_all = pl.BlockSpec((t, HEAD_DIM), lambda h, i: (0, h))
    tile = pl.BlockSpec((blk, HEAD_DIM), lambda h, i: (i, h))
    stat = pl.BlockSpec((None, blk, LANES), lambda h, i: (h, i, 0))
    return pl.pallas_call(
        body, name=name, grid=(heads, nq),
        in_specs=[tile, head_all, pl.BlockSpec((t, HEAD_DIM), lambda h, i: (0, 2 * heads + h)),
                  pl.BlockSpec((None, nq, 1, blk), lambda h, i: (h, 0, 0, 0)),
                  pl.BlockSpec((blk, HEAD_DIM), lambda h, i: (i, 3 * heads + h))],
        out_specs=[tile, tile, stat, stat],
        out_shape=[SDS((t, e), F32), SDS((t, e), BF16), SDS((heads, t, LANES), F32), SDS((heads, t, LANES), F32)],
        scratch_shapes=[pltpu.VMEM((blk, cw), F32), pltpu.VMEM((blk, cw), F32)],
        compiler_params=_cparams("parallel", "arbitrary"),
    )(qs, kn, proj, ck, proj)


def _flash_bwd(qs, kn, proj, ck, dyz, o, mstat, lstat, heads, name):
    t, e = qs.shape
    blk = min(ATT_BLOCK, t)
    cw = min(ATT_CHUNK, blk // 2)
    nq, nch = t // blk, blk // cw
    assert nch % 2 == 0

    def body(q_ref, dy_ref, o_ref, z_ref, m_ref, l_ref, k_ref, v_ref, ck_ref,
             dq_ref, dk_ref, dv_ref, dc_ref, dz_ref, s_a, s_b, d_a, d_b):
        i = pl.program_id(1)

        @pl.when(i == 0)
        def _():
            dk_ref[...] = jnp.zeros_like(dk_ref)
            dv_ref[...] = jnp.zeros_like(dv_ref)
            dc_ref[...] = jnp.zeros_like(dc_ref)

        z = z_ref[...].astype(F32)
        sz = _sigmoid(z)
        dy = dy_ref[...].astype(F32)
        of = o_ref[...]
        do = ((dy * (z * sz)) / l_ref[:, 0:1]).astype(BF16)
        dz_ref[...] = (dy * of * (sz * (1.0 + z * (1.0 - sz)))).astype(BF16)
        dl = jnp.sum(do.astype(F32) * of, axis=-1, keepdims=True)
        q = q_ref[...]
        mrow = m_ref[:, 0:1]
        sbuf, dbuf = (s_a, s_b), (d_a, d_b)

        def key_rows(j, c):
            return pl.ds(pl.multiple_of(j * blk, blk) + c * cw, cw)

        def products(j, c):
            rows = key_rows(j, c)
            sbuf[c % 2][...] = (lax.dot_general(q, k_ref[rows, :], NT_DIMS, preferred_element_type=F32)
                            - ck_ref[j][:, c * cw:(c + 1) * cw])
            dbuf[c % 2][...] = lax.dot_general(do, v_ref[rows, :], NT_DIMS, preferred_element_type=F32)

        def weights(c, masked):
            p = jnp.exp2(sbuf[c % 2][...] - mrow)
            if masked:
                keep = lax.broadcasted_iota(jnp.int32, (blk, cw), 0) >= (lax.broadcasted_iota(jnp.int32, (blk, cw), 1) + c * cw)
                p = jnp.where(keep, p, 0.0)
            p = p.astype(BF16)
            ds = p.astype(F32) * (dbuf[c % 2][...] - dl)
            return p, ds.astype(BF16), jnp.sum(ds, axis=0, keepdims=True)

        def outputs(j, c, p, dsb, colsum, dq):
            rows = key_rows(j, c)
            dv_ref[rows, :] += lax.dot_general(p, do, TN_DIMS, preferred_element_type=F32)
            dk_ref[rows, :] += lax.dot_general(dsb, q, TN_DIMS, preferred_element_type=F32)
            dc_ref[j, :, pl.ds(c * cw, cw)] -= colsum
            return dq + jnp.dot(dsb, k_ref[rows, :], preferred_element_type=F32)

        def block(j, dq, masked):
            products(j, 1)
            for c in range(nch):
                p, dsb, colsum = weights(c, masked)
                if c + 2 < nch:
                    products(j, c + 2)
                elif c + 2 == nch and not masked:
                    products(j + 1, 0)
                dq = outputs(j, c, p, dsb, colsum, dq)
            return dq

        products(0, 0)
        dq = lax.fori_loop(0, i, lambda j, acc: block(j, acc, False), jnp.zeros((blk, HEAD_DIM), F32))
        dq_ref[...] = block(i, dq, True)

    tile = pl.BlockSpec((blk, HEAD_DIM), lambda h, i: (i, h))
    stat = pl.BlockSpec((None, blk, LANES), lambda h, i: (h, i, 0))
    head_all = pl.BlockSpec((t, HEAD_DIM), lambda h, i: (0, h))
    cspec = pl.BlockSpec((None, nq, 1, blk), lambda h, i: (h, 0, 0, 0))
    return pl.pallas_call(
        body, name=name, grid=(heads, nq),
        in_specs=[tile, tile, tile, pl.BlockSpec((blk, HEAD_DIM), lambda h, i: (i, 3 * heads + h)), stat, stat, head_all,
                  pl.BlockSpec((t, HEAD_DIM), lambda h, i: (0, 2 * heads + h)), cspec],
        out_specs=[tile, head_all, head_all, cspec, tile],
        out_shape=[SDS((t, e), F32), SDS((t, e), F32), SDS((t, e), F32), SDS((heads, nq, 1, blk), F32), SDS((t, e), BF16)],
        scratch_shapes=[pltpu.VMEM((blk, cw), F32)] * 4,
        compiler_params=_cparams("parallel", "arbitrary"),
    )(qs, dyz, o, proj, mstat, lstat, kn, proj, ck)


def _b_prep_bwd(dqs, dkn, dv, dz, proj, qg, kg, dct, flog, fbias, heads, name, tt=256):
    t, e4 = proj.shape
    e = e4 // 4
    nt = t // tt
    scale = HEAD_DIM ** -0.5

    def body(dq_ref, dk_ref, dv_ref, dz_ref, q_ref, k_ref, qg_ref, kg_ref, dc_ref, fl_ref, fb_ref,
             dp_ref, dqg_ref, dkg_ref, dfb_ref, carry, dlf, acc_q, acc_k, acc_f):
        i = pl.program_id(0)

        @pl.when(i == 0)
        def _():
            carry[...] = jnp.zeros_like(carry)
            acc_q[...] = jnp.zeros_like(acc_q)
            acc_k[...] = jnp.zeros_like(acc_k)
            acc_f[...] = jnp.zeros_like(acc_f)

        for h in range(heads):
            cols = pl.ds(h * HEAD_DIM, HEAD_DIM)
            for src_ref, d_ref, g_ref, acc, mult, off in ((q_ref, dq_ref, qg_ref, acc_q, scale, 0),
                                                          (k_ref, dk_ref, kg_ref, acc_k, LN2, e)):
                xf = src_ref[:, cols].astype(F32)
                r = lax.rsqrt(jnp.mean(xf * xf, axis=-1, keepdims=True) + NORM_EPS)
                xh = xf * r
                dn = d_ref[:, cols] * mult
                acc[...] += _rows8(dn * xh)
                dxh = dn * g_ref[:, cols]
                dp_ref[:, pl.ds(off + h * HEAD_DIM, HEAD_DIM)] = (
                    r * (dxh - xh * jnp.mean(dxh * xh, axis=-1, keepdims=True))).astype(BF16)
        dp_ref[:, pl.ds(2 * e, e)] = dv_ref[...].astype(BF16)
        dp_ref[:, pl.ds(3 * e, e)] = dz_ref[...]

        tri = (lax.broadcasted_iota(jnp.int32, (tt, tt), 0) <= lax.broadcasted_iota(jnp.int32, (tt, tt), 1)).astype(BF16)
        dlf[...] = _tri_matmul(tri, dc_ref[...]) + carry[...]
        carry[...] = dlf[pl.ds(0, 1), :]
        a = fl_ref[...] + fb_ref[...]
        dfl = dlf[...] * _sigmoid(-a)
        dp_ref[:, pl.ds(4 * e, LANES)] = dfl.astype(BF16)
        acc_f[...] += _rows8(dfl)

        @pl.when(i == nt - 1)
        def _():
            dqg_ref[...] = jnp.sum(acc_q[...], axis=0, keepdims=True)
            dkg_ref[...] = jnp.sum(acc_k[...], axis=0, keepdims=True)
            dfb_ref[...] = jnp.sum(acc_f[...], axis=0, keepdims=True)

    rev = lambda i: (nt - 1 - i, 0)
    vec_e = pl.BlockSpec((1, e), lambda i: (0, 0))
    vec = pl.BlockSpec((1, LANES), lambda i: (0, 0))
    wide = pl.BlockSpec((tt, e), rev)
    lane = pl.BlockSpec((tt, LANES), rev)
    return pl.pallas_call(
        body, name=name, grid=(nt,),
        in_specs=[wide, wide, wide, wide, wide, pl.BlockSpec((tt, e), lambda i: (nt - 1 - i, 1)), vec_e, vec_e, lane, lane, vec],
        out_specs=[pl.BlockSpec((tt, e4 + LANES), rev), vec, vec, vec],
        out_shape=[SDS((t, e4 + LANES), BF16), SDS((1, LANES), F32), SDS((1, LANES), F32), SDS((1, LANES), F32)],
        scratch_shapes=[pltpu.VMEM((1, LANES), F32), pltpu.VMEM((tt, LANES), F32), pltpu.VMEM((SUBLANES, LANES), F32),
                        pltpu.VMEM((SUBLANES, LANES), F32), pltpu.VMEM((SUBLANES, LANES), F32)],
        compiler_params=_cparams("arbitrary"),
    )(dqs, dkn, dv, dz, proj, proj, qg, kg, dct, flog, fbias)


def _b_fwd(h, b_norm, wb_pad, wb_out, qg, kg, fbias, heads, tag, ride=None):
    t = h.shape[0]
    e = wb_out.shape[0]
    blk = min(ATT_BLOCK, t)
    hn, proj, *rode = _norm_matmul(h, b_norm, wb_pad[:, :4 * e], f"b_in_proj_{tag}", ride=ride)
    flog = _matmul_f32out(hn, wb_pad[:, 4 * e:], f"b_forget_proj_{tag}")
    qs, kn, _, ct = _b_prep_fwd(proj, flog, fbias, qg, kg, heads, f"b_prep_fwd_{tag}")
    ck = ct.reshape(LANES, t // blk, 1, blk)
    o, y, mstat, lstat = _flash_fwd(qs, kn, proj, ck, heads, f"b_attention_fwd_{tag}")
    sv = dict(x=h, hn=hn, proj=proj, flog=flog, qs=qs, kn=kn, ck=ck, o=o, y=y, mstat=mstat, lstat=lstat)
    return y, sv, rode


def _b_bwd(dh, sv, b_norm, wb_pad, wb_out, qg, kg, fbias, heads, tag, ride=None):
    t = dh.shape[0]
    e = wb_out.shape[0]
    gb = {}
    dyz = _matmul_nt(dh, wb_out, f"b_out_bwd_{tag}")
    gb["w_out"] = _matmul_tn(sv["y"], dh, f"b_out_wgrad_{tag}", tm=2048)
    dqs, dkn, dv, dc, dz = _flash_bwd(sv["qs"], sv["kn"], sv["proj"], sv["ck"], dyz, sv["o"], sv["mstat"], sv["lstat"],
                                      heads, f"b_attention_bwd_{tag}")
    dct = jnp.pad(dc.reshape(heads, t).T, ((0, 0), (0, LANES - heads)))
    dproj, dqg, dkg, dfb = _b_prep_bwd(dqs, dkn, dv, dz, sv["proj"], qg, kg, dct, sv["flog"], fbias, heads,
                                       f"b_prep_bwd_{tag}")
    gb["w_in"] = _matmul_tn(sv["hn"], dproj, f"b_in_wgrad_{tag}")[:, :4 * e + heads]
    dh, dg, *rode = _dproj_matmul_normbwd(dproj, wb_pad, sv["x"], b_norm, dh, f"b_in_bwd_{tag}", ride=ride)
    gb["norm"], gb["q_norm"], gb["k_norm"], gb["f_bias"] = dg, dqg, dkg, dfb[:, :heads]
    return dh, gb, rode


def _sum_adamw(recv, w, m, v, name, tr=256):
    nl, r, c = w.shape
    tr = min(tr, r)

    def body(g_ref, w_ref, m_ref, v_ref, go_ref, d_ref, mo_ref, vo_ref):
        g = g_ref[0].astype(F32)
        for s in range(1, N_DEV):
            g = g + g_ref[s].astype(F32)
        go_ref[...] = g
        mn = ADAM_B1 * m_ref[...] + (1.0 - ADAM_B1) * g
        vn = ADAM_B2 * v_ref[...] + (1.0 - ADAM_B2) * (g * g)
        m_hat = mn / (1.0 - ADAM_B1 ** ADAM_STEP)
        v_hat = vn / (1.0 - ADAM_B2 ** ADAM_STEP)
        d_ref[...] = -ADAM_LR * (m_hat / (jnp.sqrt(v_hat) + ADAM_EPS) + ADAM_WD * w_ref[...])
        mo_ref[...] = mn
        vo_ref[...] = vn

    blk = pl.BlockSpec((None, tr, c), lambda l, i: (l, i, 0))
    return pl.pallas_call(
        body, name=name, grid=(nl, r // tr),
        in_specs=[pl.BlockSpec((N_DEV, None, tr, c), lambda l, i: (0, l, i, 0)), blk, blk, blk],
        out_specs=[blk, blk, blk, blk],
        out_shape=[SDS(w.shape, F32)] * 4,
        compiler_params=_cparams("parallel", "parallel"),
    )(recv, w, m, v)


def _unshard(g, axis):
    g = jnp.moveaxis(g, 0, axis)
    return g.reshape(g.shape[:axis] + (g.shape[axis] * g.shape[axis + 1],) + g.shape[axis + 2:])


def _to_slabs(full, axis):
    n = full.shape[axis]
    s = full.reshape(full.shape[:axis] + (N_DEV, n // N_DEV) + full.shape[axis + 1:])
    return jnp.moveaxis(s, axis, 0)


def _pack_rows(parts, lead):
    flat = [p.reshape(p.shape[:lead] + (-1,)) for p in parts]
    cat = jnp.concatenate(flat, axis=-1)
    n = cat.shape[-1]
    pad = (-n) % (SUBLANES * LANES)
    cat = jnp.pad(cat, [(0, 0)] * lead + [(0, pad)])
    return cat.reshape(cat.shape[:lead] + ((n + pad) // LANES, LANES))


def _unpack_rows(packed, shapes, lead):
    flat = packed.reshape(packed.shape[:lead] + (-1,))
    out, off = [], 0
    for shp in shapes:
        size = int(np.prod(shp))
        out.append(flat[..., off:off + size].reshape(packed.shape[:lead] + tuple(shp)))
        off += size
    return out


def _pad_rows(w, rows):
    return jnp.pad(w, ((0, rows - w.shape[0]), (0, 0)))


def kernel(x, a_norm, a_w_in, a_conv_w, a_conv_b, a_ln_g, a_ln_b, a_w_out, b_norm, b_w_in, b_f_bias, b_q_norm, b_k_norm, b_w_out, c_norm, c_w_in, c_conv_w, c_w_out, loss_target, m_a_norm, m_a_w_in, m_a_conv_w, m_a_conv_b, m_a_ln_g, m_a_ln_b, m_a_w_out, m_b_norm, m_b_w_in, m_b_f_bias, m_b_q_norm, m_b_k_norm, m_b_w_out, m_c_norm, m_c_w_in, m_c_conv_w, m_c_w_out, v_a_norm, v_a_w_in, v_a_conv_w, v_a_conv_b, v_a_ln_g, v_a_ln_b, v_a_w_out, v_b_norm, v_b_w_in, v_b_f_bias, v_b_q_norm, v_b_k_norm, v_b_w_out, v_c_norm, v_c_w_in, v_c_conv_w, v_c_w_out):
    t, d = x.shape[1], x.shape[2]
    e = a_w_out.shape[1] * N_DEV
    heads = b_f_bias.shape[1]
    n_a, n_b, n_c = a_norm.shape[0], b_norm.shape[0], c_norm.shape[0]
    depth = n_a + n_b + n_c
    ka, kc = a_conv_w.shape[1], c_conv_w.shape[1]
    assert e == heads * HEAD_DIM and n_b == 1 and n_c == 1 and x.shape[0] == 1

    layers = [(i % 3, i // 3) for i in range(depth)]

    def mat_shards(kind, j):
        w_in, w_out = ((a_w_in, a_w_out), (b_w_in, b_w_out), (c_w_in, c_w_out))[kind]
        return [w_in[j].astype(BF16), w_out[j].astype(BF16)]

    def full_mats(kind, gathered):
        w_in, w_out = _unshard(gathered[0], 1), _unshard(gathered[1], 0)
        return (jnp.pad(w_in, ((0, 0), (0, LANES - heads))) if kind == 1 else w_in), w_out

    small_names = ["a_norm", "a_conv_w", "a_conv_b", "a_ln_g", "a_ln_b", "c_norm", "c_conv_w"]
    small = dict(a_norm=a_norm, a_conv_w=a_conv_w, a_conv_b=a_conv_b, a_ln_g=a_ln_g, a_ln_b=a_ln_b,
                 c_norm=c_norm, c_conv_w=c_conv_w)
    small_pack = _pack_rows([small[k] for k in small_names], 0)
    first = _all_gather(mat_shards(*layers[0]) + [small_pack], "all_gather_first_layer")
    sm = _unpack_rows(first[2], [small[k].shape for k in small_names], 1)
    g_a_norm = _unshard(sm[0], 1)
    g_a_conv_w = _unshard(sm[1], 2)
    g_a_conv_b = _unshard(sm[2], 1)
    g_a_ln_g = _unshard(sm[3], 1)
    g_a_ln_b = _unshard(sm[4], 1)
    g_c_norm = _unshard(sm[5], 1)
    g_c_conv_w = _unshard(sm[6], 2)

    cw_a = [_pad_rows(g_a_conv_w[j], CONF_HALO) for j in range(n_a)]
    cw_c = _pad_rows(g_c_conv_w[0], SHORT_HALO)
    qg = jnp.tile(b_q_norm, (1, heads))
    kg = jnp.tile(b_k_norm, (1, heads))
    fbias = jnp.pad(b_f_bias, ((0, 0), (0, LANES - heads)))

    h = x[0]
    saved, weights = [], [full_mats(layers[0][0], first[:2])]
    for i, (kind, j) in enumerate(layers):
        tag = f"l{i}"
        w_in, w_out = weights[i]
        ride = _Ride("gather", mat_shards(*layers[i + 1])) if i + 1 < depth else None
        if kind == 0:
            hn, proj, *rode = _norm_matmul(h, g_a_norm[j:j + 1], w_in, f"a_in_proj_{tag}", ride=ride)
            y, u1 = _a_mid_fwd(proj, cw_a[j], g_a_conv_b[j:j + 1], g_a_ln_g[j:j + 1], g_a_ln_b[j:j + 1], ka, f"a_mid_fwd_{tag}")
            saved.append(dict(x=h, hn=hn, proj=proj, u1=u1, y=y))
        elif kind == 1:
            y, sv, rode = _b_fwd(h, b_norm, w_in, w_out, qg, kg, fbias, heads, tag, ride=ride)
            saved.append(sv)
        else:
            hn, proj, *rode = _norm_matmul(h, g_c_norm, w_in, f"c_in_proj_{tag}", ride=ride)
            y = _c_mid_fwd(proj, cw_c, kc, f"c_mid_fwd_{tag}")
            saved.append(dict(x=h, hn=hn, proj=proj, y=y))
        if ride:
            weights.append(full_mats(layers[i + 1][0], rode))
            h = _out_matmul_residual(y, w_out, h, f"{'abc'[kind]}_out_proj_{tag}")
        else:
            loss_part, dh = _out_matmul_loss(y, w_out, h, loss_target[0], f"{'abc'[kind]}_out_proj_loss_{tag}")
    loss_local = jnp.sum(loss_part).reshape(1, 1)

    ga = dict(norm=[None] * n_a, conv_w=[None] * n_a, conv_b=[None] * n_a, ln_g=[None] * n_a, ln_b=[None] * n_a)
    gb, gc = None, {}
    recv_mats = [None] * depth
    pending = None
    for i in reversed(range(depth)):
        kind, j = layers[i]
        tag = f"l{i}"
        sv = saved[i]
        w_in, w_out = weights[i]
        ride = _Ride("exchange", pending) if pending is not None else None
        if kind == 0:
            dyz = _matmul_nt(dh, w_out, f"a_out_bwd_{tag}")
            gw_out = _matmul_tn(sv["y"], dh, f"a_out_wgrad_{tag}", tm=2048)
            mid_ride, ride = (ride, None) if i == 0 else (None, ride)
            dproj, dcw, dcb, dlg, dlb, *mid_rode = _a_mid_bwd(sv["proj"], sv["u1"], dyz, cw_a[j], g_a_ln_g[j:j + 1],
                                                              g_a_ln_b[j:j + 1], ka, f"a_mid_bwd_{tag}", ride=mid_ride)
            gw_in = _matmul_tn(sv["hn"], dproj, f"a_in_wgrad_{tag}")
            if i == 0:
                ride = _Ride("exchange", [_to_slabs(gw_in, 1), _to_slabs(gw_out, 0)])
            dh, dg, *rode = _dproj_matmul_normbwd(dproj, w_in, sv["x"], g_a_norm[j:j + 1], dh, f"a_in_bwd_{tag}", ride=ride)
            if i == 0:
                if mid_ride:
                    recv_mats[1] = mid_rode
                recv_mats[0], ride = rode, None
            ga["norm"][j], ga["conv_w"][j], ga["conv_b"][j], ga["ln_g"][j], ga["ln_b"][j] = dg[0], dcw[:ka], dcb[0], dlg[0], dlb[0]
        elif kind == 1:
            dh, gb, rode = _b_bwd(dh, sv, b_norm, w_in, w_out, qg, kg, fbias, heads, tag, ride=ride)
            gw_in, gw_out = gb["w_in"], gb["w_out"]
        else:
            dyz = _matmul_nt(dh, w_out, f"c_out_bwd_{tag}")
            gw_out = _matmul_tn(sv["y"], dh, f"c_out_wgrad_{tag}", tm=2048)
            dproj, dcw = _c_mid_bwd(sv["proj"], dyz, cw_c, kc, f"c_mid_bwd_{tag}")
            gw_in = _matmul_tn(sv["hn"], dproj, f"c_in_wgrad_{tag}")
            dh, dg, *rode = _dproj_matmul_normbwd(dproj, w_in, sv["x"], g_c_norm, dh, f"c_in_bwd_{tag}", ride=ride)
            gc["norm"], gc["conv_w"] = dg, dcw[:kc][None]
        if ride:
            recv_mats[i + 1] = rode
        pending = [_to_slabs(gw_in, 1), _to_slabs(gw_out, 0)]
    grad_x = dh[None]

    sharded_small = [(jnp.stack(ga["norm"]), 1), (jnp.stack(ga["conv_w"]), 2), (jnp.stack(ga["conv_b"]), 1),
                     (jnp.stack(ga["ln_g"]), 1), (jnp.stack(ga["ln_b"]), 1), (gc["norm"], 1), (gc["conv_w"], 2)]
    repl_small = [gb["norm"], gb["f_bias"], gb["q_norm"], gb["k_norm"], loss_local]
    small_slabs = _pack_rows([_to_slabs(g, ax) for g, ax in sharded_small]
                             + [jnp.broadcast_to(g[None], (N_DEV,) + g.shape) for g in repl_small], 1)
    recv_small, = _exchange([small_slabs], "exchange_vector_gradients")

    outs = {}
    mat_w = dict(a_w_in=(a_w_in, m_a_w_in, v_a_w_in), a_w_out=(a_w_out, m_a_w_out, v_a_w_out),
                 b_w_in=(b_w_in, m_b_w_in, v_b_w_in), b_w_out=(b_w_out, m_b_w_out, v_b_w_out),
                 c_w_in=(c_w_in, m_c_w_in, v_c_w_in), c_w_out=(c_w_out, m_c_w_out, v_c_w_out))
    for kind, prefix in enumerate("abc"):
        members = [i for i, (k, _) in enumerate(layers) if k == kind]
        for which, name in enumerate((f"{prefix}_w_in", f"{prefix}_w_out")):
            recv = jnp.stack([recv_mats[i][which] for i in members], axis=1)
            outs[name] = _sum_adamw(recv, *mat_w[name], f"adamw_{name}")

    small_order = small_names + ["b_norm", "b_f_bias", "b_q_norm", "b_k_norm", "loss"]
    no_state = jnp.zeros((1, 1), F32)
    small_w = dict(a_norm=(a_norm, m_a_norm, v_a_norm), a_conv_w=(a_conv_w, m_a_conv_w, v_a_conv_w),
                   a_conv_b=(a_conv_b, m_a_conv_b, v_a_conv_b), a_ln_g=(a_ln_g, m_a_ln_g, v_a_ln_g),
                   a_ln_b=(a_ln_b, m_a_ln_b, v_a_ln_b), c_norm=(c_norm, m_c_norm, v_c_norm),
                   c_conv_w=(c_conv_w, m_c_conv_w, v_c_conv_w), b_norm=(b_norm, m_b_norm, v_b_norm),
                   b_f_bias=(b_f_bias, m_b_f_bias, v_b_f_bias), b_q_norm=(b_q_norm, m_b_q_norm, v_b_q_norm),
                   b_k_norm=(b_k_norm, m_b_k_norm, v_b_k_norm), loss=(no_state, no_state, no_state))
    packs = [_pack_rows([small_w[k][q] for k in small_order], 0)[None] for q in range(3)]
    small_out = _sum_adamw(recv_small[:, None], *packs, "adamw_vectors")
    shapes = [small_w[k][0].shape for k in small_order]
    unpacked = [_unpack_rows(o[0], shapes, 0) for o in small_out]
    for idx, name in enumerate(small_order):
        outs[name] = tuple(unpacked[q][idx] for q in range(4))

    loss = outs["loss"][0].reshape(())
    order = ["a_norm", "a_w_in", "a_conv_w", "a_conv_b", "a_ln_g", "a_ln_b", "a_w_out", "b_norm", "b_w_in", "b_f_bias",
             "b_q_norm", "b_k_norm", "b_w_out", "c_norm", "c_w_in", "c_conv_w", "c_w_out"]
    return (loss, grad_x, *[outs[k][0] for k in order], *[outs[k][1] for k in order],
            *[outs[k][2] for k in order], *[outs[k][3] for k in order])
```

```python
import jax
import jax.numpy as jnp
import numpy as np
from jax import lax
from jax.experimental import pallas as pl
from jax.experimental.pallas import tpu as pltpu

F32 = jnp.float32
BF16 = jnp.bfloat16
SDS = jax.ShapeDtypeStruct

NORM_EPS = 1e-6
ADAM_LR = 0.001
ADAM_B1 = 0.9
ADAM_B2 = 0.999
ADAM_EPS = 1e-08
ADAM_WD = 0.01
ADAM_STEP = 10

N_DEV = 8
LANES = 128
SUBLANES = 8
HEAD_DIM = 128
CONF_HALO = 32
SHORT_HALO = 8
VMEM_LIMIT = 56 * 1024 * 1024

NT_DIMS = (((1,), (1,)), ((), ()))
TN_DIMS = (((0,), (0,)), ((), ()))
MESH = pl.DeviceIdType.MESH
ANY = pl.BlockSpec(memory_space=pl.ANY)


def _cparams(*sem):
    return pltpu.CompilerParams(dimension_semantics=sem, vmem_limit_bytes=VMEM_LIMIT)


def _divisor_tile(n, cap):
    return max(m for m in range(LANES, min(n, cap) + 1, LANES) if n % m == 0)


def _sigmoid(x):
    return 1.0 / (1.0 + jnp.exp(-x))


def _silu(x):
    return x * _sigmoid(x)


def _dsilu(x):
    s = _sigmoid(x)
    return s * (1.0 + x * (1.0 - s))


def _rows8(v):
    out = v[0:SUBLANES]
    for a in range(1, v.shape[0] // SUBLANES):
        out = out + v[a * SUBLANES:(a + 1) * SUBLANES]
    return out


def _split3(v):
    hi = v.astype(BF16)
    r1 = v - hi.astype(F32)
    mid = r1.astype(BF16)
    lo = (r1 - mid.astype(F32)).astype(BF16)
    return hi, mid, lo


def _tri_matmul(tri, v):
    hi, mid, lo = _split3(v)
    return (jnp.dot(tri, hi, preferred_element_type=F32) + jnp.dot(tri, mid, preferred_element_type=F32)
            + jnp.dot(tri, lo, preferred_element_type=F32))


def _position():
    return lax.axis_index("x"), lax.axis_index("y"), lax.axis_index("c")


def _all_gather(shards, name):
    n = len(shards)

    def body(*refs):
        xs, outs = refs[:n], refs[n:2 * n]
        send_sems, recv_sems, local_sems = refs[2 * n:]
        x, y, c = _position()
        me, sibling = (x, y, c), (x, y, 1 - c)
        chips = [(1 - x, y), (x, 1 - y), (1 - x, 1 - y)]

        def slot(a, px, py, pc):
            return outs[a].at[4 * px + 2 * py + pc]

        def copy(a, k, block, to, src=None):
            return pltpu.make_async_remote_copy(
                src_ref=slot(a, *block) if src is None else src, dst_ref=slot(a, *block),
                send_sem=send_sems.at[a, k], recv_sem=recv_sems.at[a, k], device_id=to, device_id_type=MESH)

        started = []
        mine = []
        for a in range(n):
            cp = pltpu.make_async_copy(xs[a], slot(a, *me), local_sems.at[a])
            cp.start()
            mine.append(cp)
        for a in range(n):
            first = [copy(a, 0, me, sibling, src=xs[a])]
            first += [copy(a, 1 + j, me, (*chip, c), src=xs[a]) for j, chip in enumerate(chips)]
            for cp in first:
                cp.start()
            started += first
        for a in range(n):
            for j, chip in enumerate(chips):
                copy(a, 1 + j, (*chip, c), me).wait_recv()
                fwd = copy(a, 4 + j, (*chip, c), sibling)
                fwd.start()
                started.append(fwd)
        for a in range(n):
            copy(a, 0, sibling, me).wait_recv()
            for j, chip in enumerate(chips):
                copy(a, 4 + j, (*chip, 1 - c), me).wait_recv()
        for cp in started:
            cp.wait_send()
        for cp in mine:
            cp.wait()

    return pl.pallas_call(
        body, name=name,
        out_shape=[SDS((N_DEV,) + s.shape, s.dtype) for s in shards],
        in_specs=[ANY] * n, out_specs=[ANY] * n,
        scratch_shapes=[pltpu.SemaphoreType.DMA((n, 7)), pltpu.SemaphoreType.DMA((n, 7)), pltpu.SemaphoreType.DMA((n,))],
    )(*shards)


def _exchange(slabs, name):
    n = len(slabs)

    def body(*refs):
        ins, outs = refs[:n], refs[n:2 * n]
        send_sems, recv_sems, local_sems = refs[2 * n:]
        x, y, c = _position()
        me = 4 * x + 2 * y + c
        peers = [(x ^ bx, y ^ by, c ^ bc) for bx in (0, 1) for by in (0, 1) for bc in (0, 1)][1:]

        def copy(a, k, peer):
            pid = 4 * peer[0] + 2 * peer[1] + peer[2]
            return pltpu.make_async_remote_copy(
                src_ref=ins[a].at[pid], dst_ref=outs[a].at[me],
                send_sem=send_sems.at[a, k], recv_sem=recv_sems.at[a, k], device_id=peer, device_id_type=MESH)

        def arrival(a, k, peer):
            pid = 4 * peer[0] + 2 * peer[1] + peer[2]
            return pltpu.make_async_remote_copy(
                src_ref=ins[a].at[pid], dst_ref=outs[a].at[pid],
                send_sem=send_sems.at[a, k], recv_sem=recv_sems.at[a, k], device_id=peer, device_id_type=MESH)

        mine = []
        for a in range(n):
            cp = pltpu.make_async_copy(ins[a].at[me], outs[a].at[me], local_sems.at[a])
            cp.start()
            mine.append(cp)
        started = []
        for a in range(n):
            for k, peer in enumerate(peers):
                cp = copy(a, k, peer)
                cp.start()
                started.append(cp)
        for a in range(n):
            for k, peer in enumerate(peers):
                arrival(a, k, peer).wait_recv()
        for cp in started:
            cp.wait_send()
        for cp in mine:
            cp.wait()

    return pl.pallas_call(
        body, name=name,
        out_shape=[SDS(s.shape, s.dtype) for s in slabs],
        in_specs=[ANY] * n, out_specs=[ANY] * n,
        scratch_shapes=[pltpu.SemaphoreType.DMA((n, 7)), pltpu.SemaphoreType.DMA((n, 7)), pltpu.SemaphoreType.DMA((n,))],
    )(*slabs)


class _Ride:
    def __init__(self, kind, arrays):
        self.kind, self.arrays, self.n = kind, list(arrays), len(arrays)
        self.in_specs = [ANY] * self.n
        self.out_specs = [ANY] * self.n
        self.out_shape = [SDS(((N_DEV,) + a.shape) if kind == "gather" else a.shape, a.dtype) for a in self.arrays]
        self.scratch = [pltpu.SemaphoreType.DMA((self.n, 7)), pltpu.SemaphoreType.DMA((self.n, 7)),
                        pltpu.SemaphoreType.DMA((self.n,))]

    def _copies(self, ins, outs, sems, arriving):
        send_sems, recv_sems, local_sems = sems
        x, y, c = _position()
        me = 4 * x + 2 * y + c
        peers = [(x ^ bx, y ^ by, c ^ bc) for bx in (0, 1) for by in (0, 1) for bc in (0, 1)][1:]
        local, remote = [], []
        for a in range(self.n):
            own = ins[a] if self.kind == "gather" else ins[a].at[me]
            local.append(pltpu.make_async_copy(own, outs[a].at[me], local_sems.at[a]))
            for k, peer in enumerate(peers):
                pid = 4 * peer[0] + 2 * peer[1] + peer[2]
                remote.append(pltpu.make_async_remote_copy(
                    src_ref=ins[a] if self.kind == "gather" else ins[a].at[pid],
                    dst_ref=outs[a].at[pid if arriving else me],
                    send_sem=send_sems.at[a, k], recv_sem=recv_sems.at[a, k], device_id=peer, device_id_type=MESH))
        return local, remote

    def start(self, ins, outs, sems):
        local, sends = self._copies(ins, outs, sems, False)
        for cp in local + sends:
            cp.start()

    def wait(self, ins, outs, sems):
        local, arrivals = self._copies(ins, outs, sems, True)
        for cp in arrivals:
            cp.wait_recv()
        for cp in arrivals:
            cp.wait_send()
        for cp in local:
            cp.wait()


def _norm_matmul(x, g, w, name, tm=2048, tn=1024, ride=None, n_cols=None):
    t, d = x.shape
    n = n_cols or w.shape[1]
    tm, tn = min(tm, t), min(tn, n)
    ni, nj = t // tm, n // tn
    nr = ride.n if ride else 0

    def body(*refs):
        x_ref, g_ref, w_ref = refs[:3]
        hn_ref, o_ref = refs[3 + nr:5 + nr]
        rin, rout, sems = refs[3:3 + nr], refs[5 + nr:5 + 2 * nr], refs[5 + 2 * nr:]
        i, j = pl.program_id(0), pl.program_id(1)
        if ride:
            @pl.when((i == 0) & (j == 0))
            def _():
                ride.start(rin, rout, sems)

        @pl.when(j == 0)
        def _():
            xf = x_ref[...]
            r = lax.rsqrt(jnp.mean(xf * xf, axis=-1, keepdims=True) + NORM_EPS)
            hn_ref[...] = ((xf * r) * g_ref[...]).astype(BF16)

        o_ref[...] = jnp.dot(hn_ref[...], w_ref[...], preferred_element_type=F32).astype(o_ref.dtype)
        if ride:
            @pl.when((i == ni - 1) & (j == nj - 1))
            def _():
                ride.wait(rin, rout, sems)

    return pl.pallas_call(
        body, name=name, grid=(ni, nj),
        in_specs=[pl.BlockSpec((tm, d), lambda i, j: (i, 0)), pl.BlockSpec((1, d), lambda i, j: (0, 0)),
                  pl.BlockSpec((d, tn), lambda i, j: (0, j))] + (ride.in_specs if ride else []),
        out_specs=[pl.BlockSpec((tm, d), lambda i, j: (i, 0)), pl.BlockSpec((tm, tn), lambda i, j: (i, j))]
        + (ride.out_specs if ride else []),
        out_shape=[SDS((t, d), BF16), SDS((t, n), BF16)] + (ride.out_shape if ride else []),
        scratch_shapes=ride.scratch if ride else [],
        compiler_params=_cparams("arbitrary", "arbitrary") if ride else _cparams("parallel", "arbitrary"),
    )(x, g, w, *(ride.arrays if ride else []))


def _matmul_f32out(a, w, name, tm=512):
    t, k = a.shape
    n = w.shape[1]

    def body(a_ref, w_ref, o_ref):
        o_ref[...] = jnp.dot(a_ref[...], w_ref[...], preferred_element_type=F32)

    return pl.pallas_call(
        body, name=name, grid=(t // tm,),
        in_specs=[pl.BlockSpec((tm, k), lambda i: (i, 0)), pl.BlockSpec((k, n), lambda i: (0, 0))],
        out_specs=pl.BlockSpec((tm, n), lambda i: (i, 0)),
        out_shape=SDS((t, n), F32),
        compiler_params=_cparams("parallel"),
    )(a, w)


def _out_matmul_residual(y, w, x, name, tm=512):
    t, e = y.shape
    d = w.shape[1]

    def body(y_ref, w_ref, x_ref, o_ref):
        o_ref[...] = x_ref[...] + jnp.dot(y_ref[...], w_ref[...], preferred_element_type=F32)

    return pl.pallas_call(
        body, name=name, grid=(t // tm,),
        in_specs=[pl.BlockSpec((tm, e), lambda i: (i, 0)), pl.BlockSpec((e, d), lambda i: (0, 0)),
                  pl.BlockSpec((tm, d), lambda i: (i, 0))],
        out_specs=pl.BlockSpec((tm, d), lambda i: (i, 0)),
        out_shape=SDS((t, d), F32),
        compiler_params=_cparams("parallel"),
    )(y, w, x)


def _matmul_nt(a, w, name, tm=512):
    t, d = a.shape
    e = w.shape[0]

    def body(a_ref, w_ref, o_ref):
        o_ref[...] = lax.dot_general(a_ref[...].astype(BF16), w_ref[...], NT_DIMS,
                                     preferred_element_type=F32).astype(o_ref.dtype)

    return pl.pallas_call(
        body, name=name, grid=(t // tm,),
        in_specs=[pl.BlockSpec((tm, d), lambda i: (i, 0)), pl.BlockSpec((e, d), lambda i: (0, 0))],
        out_specs=pl.BlockSpec((tm, e), lambda i: (i, 0)),
        out_shape=SDS((t, e), BF16),
        compiler_params=_cparams("parallel"),
    )(a, w)


def _matmul_tn(a, b, name, out_dtype=BF16, tm=1024, tn=1024, tk=512):
    t, m = a.shape
    n = b.shape[1]
    tm, tn = min(tm, m), _divisor_tile(n, 2 * tn)
    nk = t // tk

    def body(a_ref, b_ref, o_ref, acc_ref):
        k = pl.program_id(2)

        @pl.when(k == 0)
        def _():
            acc_ref[...] = jnp.zeros_like(acc_ref)

        acc_ref[...] += lax.dot_general(a_ref[...].astype(BF16), b_ref[...].astype(BF16), TN_DIMS,
                                        preferred_element_type=F32)

        @pl.when(k == nk - 1)
        def _():
            o_ref[...] = acc_ref[...].astype(o_ref.dtype)

    return pl.pallas_call(
        body, name=name, grid=(m // tm, n // tn, nk),
        in_specs=[pl.BlockSpec((tk, tm), lambda i, j, k: (k, i)), pl.BlockSpec((tk, tn), lambda i, j, k: (k, j))],
        out_specs=pl.BlockSpec((tm, tn), lambda i, j, k: (i, j)),
        out_shape=SDS((m, n), out_dtype),
        scratch_shapes=[pltpu.VMEM((tm, tn), F32)],
        compiler_params=_cparams("parallel", "parallel", "arbitrary"),
    )(a, b)


def _dproj_matmul_normbwd(dproj, w, x, g, dxn, name, tm=1024, tk=1024, ride=None):
    t, n = dproj.shape
    d = w.shape[0]
    tm, tk = min(tm, t), _divisor_tile(n, tk)
    nk = n // tk
    ni = t // tm
    nr = ride.n if ride else 0

    def body(*refs):
        dp_ref, w_ref, x_ref, g_ref, dxn_ref = refs[:5]
        dx_ref, dg_ref = refs[5 + nr:7 + nr]
        rin, rout = refs[5:5 + nr], refs[7 + nr:7 + 2 * nr]
        sems, acc_ref = refs[7 + 2 * nr:-1], refs[-1]
        i, k = pl.program_id(0), pl.program_id(1)
        if ride:
            @pl.when((i == 0) & (k == 0))
            def _():
                ride.start(rin, rout, sems)

        @pl.when(k == 0)
        def _():
            acc_ref[...] = jnp.zeros_like(acc_ref)

        acc_ref[...] += lax.dot_general(dp_ref[...], w_ref[...], NT_DIMS, preferred_element_type=F32)

        @pl.when(k == nk - 1)
        def _():
            dhn = acc_ref[...]
            xf = x_ref[...]
            r = lax.rsqrt(jnp.mean(xf * xf, axis=-1, keepdims=True) + NORM_EPS)
            xh = xf * r
            dy = dhn * g_ref[...]
            dx_ref[...] = dxn_ref[...] + r * (dy - xh * jnp.mean(dy * xh, axis=-1, keepdims=True))
            part = jnp.sum(dhn * xh, axis=0, keepdims=True)

            @pl.when(i == 0)
            def _():
                dg_ref[...] = part

            @pl.when(i > 0)
            def _():
                dg_ref[...] += part

        if ride:
            @pl.when((i == ni - 1) & (k == nk - 1))
            def _():
                ride.wait(rin, rout, sems)

    return pl.pallas_call(
        body, name=name, grid=(ni, nk),
        in_specs=[pl.BlockSpec((tm, tk), lambda i, k: (i, k)), pl.BlockSpec((d, tk), lambda i, k: (0, k)),
                  pl.BlockSpec((tm, d), lambda i, k: (i, 0)), pl.BlockSpec((1, d), lambda i, k: (0, 0)),
                  pl.BlockSpec((tm, d), lambda i, k: (i, 0))] + (ride.in_specs if ride else []),
        out_specs=[pl.BlockSpec((tm, d), lambda i, k: (i, 0)), pl.BlockSpec((1, d), lambda i, k: (0, 0))]
        + (ride.out_specs if ride else []),
        out_shape=[SDS((t, d), F32), SDS((1, d), F32)] + (ride.out_shape if ride else []),
        scratch_shapes=(ride.scratch if ride else []) + [pltpu.VMEM((tm, d), F32)],
        compiler_params=_cparams("arbitrary", "arbitrary"),
    )(dproj, w, x, g, dxn, *(ride.arrays if ride else []))


def _out_matmul_loss(y, w, x, target, name, tm=512):
    t, e = y.shape
    d = w.shape[1]
    inv_d = 1.0 / d

    def body(y_ref, w_ref, x_ref, t_ref, part_ref, dy_ref):
        i = pl.program_id(0)
        err = (x_ref[...] + jnp.dot(y_ref[...], w_ref[...], preferred_element_type=F32)) - t_ref[...]
        dy_ref[...] = err * inv_d
        part = jnp.sum(err * err, axis=0, keepdims=True) * (0.5 * inv_d)

        @pl.when(i == 0)
        def _():
            part_ref[...] = part

        @pl.when(i > 0)
        def _():
            part_ref[...] += part

    return pl.pallas_call(
        body, name=name, grid=(t // tm,),
        in_specs=[pl.BlockSpec((tm, e), lambda i: (i, 0)), pl.BlockSpec((e, d), lambda i: (0, 0)),
                  pl.BlockSpec((tm, d), lambda i: (i, 0)), pl.BlockSpec((tm, d), lambda i: (i, 0))],
        out_specs=[pl.BlockSpec((1, d), lambda i: (0, 0)), pl.BlockSpec((tm, d), lambda i: (i, 0))],
        out_shape=[SDS((1, d), F32), SDS((t, d), F32)],
        compiler_params=_cparams("arbitrary"),
    )(y, w, x, target)


CONV_ROWS = 32
CONV_COLS = 512


def _conv_chunk(src_ref, base, w_ref, width, r0, c0, flip):
    acc = None
    for k in range(width):
        off = base + r0 + ((width - 1 - k) if flip else (k - (width - 1)))
        term = src_ref[pl.ds(off, CONV_ROWS), pl.ds(c0, CONV_COLS)] * w_ref[pl.ds(k, 1), pl.ds(c0, CONV_COLS)]
        acc = term if acc is None else acc + term
    return acc


def _conv_weight_grad(dw_ref, d_ref, src_ref, base, width, tt, e):
    for c0 in range(0, e, CONV_COLS):
        for k in range(width):
            acc = None
            for r0 in range(0, tt, CONV_ROWS):
                prod = (d_ref[pl.ds(r0, CONV_ROWS), pl.ds(c0, CONV_COLS)]
                        * src_ref[pl.ds(base + r0 - (width - 1) + k, CONV_ROWS), pl.ds(c0, CONV_COLS)])
                part = _rows8(prod)
                acc = part if acc is None else acc + part
            dw_ref[pl.ds(k, 1), pl.ds(c0, CONV_COLS)] += jnp.sum(acc, axis=0, keepdims=True)


def _shifted_copies(dst_ref, src_ref, c0, length, sign):
    lo, hi = (SUBLANES, length) if sign < 0 else (0, length - SUBLANES)
    for b in range(SUBLANES):
        for r0 in range(lo, hi, CONV_ROWS):
            n = min(CONV_ROWS, hi - r0)
            dst_ref[b, pl.ds(r0, n), :] = src_ref[pl.ds(r0 + sign * b, n), pl.ds(c0, CONV_COLS)]


def _conv_aligned(copies_ref, base, w_ref, width, r0, c0, sign):
    acc = None
    for d in range(width):
        a, b = divmod(d, SUBLANES)
        term = (copies_ref[b, pl.ds(base + r0 + sign * SUBLANES * a, CONV_ROWS), :]
                * w_ref[pl.ds(width - 1 - d, 1), pl.ds(c0, CONV_COLS)])
        acc = term if acc is None else acc + term
    return acc


def _conv_weight_grad_aligned(dw_ref, d_ref, copies_ref, base, width, tt, c0):
    for d in range(width):
        a, b = divmod(d, SUBLANES)
        acc = None
        for r0 in range(0, tt, CONV_ROWS):
            prod = (d_ref[pl.ds(r0, CONV_ROWS), pl.ds(c0, CONV_COLS)]
                    * copies_ref[b, pl.ds(base + r0 - SUBLANES * a, CONV_ROWS), :])
            part = _rows8(prod)
            acc = part if acc is None else acc + part
        dw_ref[pl.ds(width - 1 - d, 1), pl.ds(c0, CONV_COLS)] += jnp.sum(acc, axis=0, keepdims=True)


LN_ROWS = 16


def _a_mid_fwd(proj, cw, cb, lg, lb, width, name, tt=256, ride=None):
    t, e3 = proj.shape
    e = e3 // 3
    halo = CONF_HALO
    nt = t // tt
    nr = ride.n if ride else 0

    def body(*refs):
        rin, rout = refs[5:5 + nr], refs[7 + nr:7 + 2 * nr]
        scratch = refs[7 + 2 * nr:]
        sems, own = (scratch[:3], scratch[3:]) if ride else ((), scratch)
        if ride:
            @pl.when(pl.program_id(0) == 0)
            def _():
                ride.start(rin, rout, sems)

        tile(*refs[:5], *refs[5 + nr:7 + nr], *own)
        if ride:
            @pl.when(pl.program_id(0) == nt - 1)
            def _():
                ride.wait(rin, rout, sems)

    def tile(p_ref, cw_ref, cb_ref, lg_ref, lb_ref, y_ref, u1_ref, ubuf, shifted):
        i = pl.program_id(0)

        @pl.when(i == 0)
        def _():
            ubuf[pl.ds(0, halo), :] = jnp.zeros((halo, e), F32)

        @pl.when(i > 0)
        def _():
            ubuf[pl.ds(0, halo), :] = ubuf[pl.ds(tt, halo), :]

        for r0 in range(0, tt, CONV_ROWS):
            val = p_ref[pl.ds(r0, CONV_ROWS), pl.ds(0, e)].astype(F32)
            gate = p_ref[pl.ds(r0, CONV_ROWS), pl.ds(e, e)].astype(F32)
            ubuf[pl.ds(halo + r0, CONV_ROWS), :] = val * _sigmoid(gate)
        for c0 in range(0, e, CONV_COLS):
            _shifted_copies(shifted, ubuf, c0, halo + tt, -1)
            for r0 in range(0, tt, CONV_ROWS):
                acc = _conv_aligned(shifted, halo, cw_ref, width, r0, c0, -1)
                u1_ref[pl.ds(r0, CONV_ROWS), pl.ds(c0, CONV_COLS)] = acc + cb_ref[:, pl.ds(c0, CONV_COLS)]
        for r0 in range(0, tt, LN_ROWS):
            u = u1_ref[pl.ds(r0, LN_ROWS), :]
            mu = jnp.mean(u, axis=-1, keepdims=True)
            dlt = u - mu
            var = jnp.mean(dlt * dlt, axis=-1, keepdims=True)
            u2 = (dlt * lax.rsqrt(var + NORM_EPS)) * lg_ref[...] + lb_ref[...]
            z = p_ref[pl.ds(r0, LN_ROWS), pl.ds(2 * e, e)].astype(F32)
            y_ref[pl.ds(r0, LN_ROWS), :] = (_silu(u2) * _silu(z)).astype(BF16)

    return pl.pallas_call(
        body, name=name, grid=(nt,),
        in_specs=[pl.BlockSpec((tt, e3), lambda i: (i, 0)), pl.BlockSpec(cw.shape, lambda i: (0, 0)),
                  pl.BlockSpec((1, e), lambda i: (0, 0)), pl.BlockSpec((1, e), lambda i: (0, 0)),
                  pl.BlockSpec((1, e), lambda i: (0, 0))] + (ride.in_specs if ride else []),
        out_specs=[pl.BlockSpec((tt, e), lambda i: (i, 0)), pl.BlockSpec((tt, e), lambda i: (i, 0))]
        + (ride.out_specs if ride else []),
        out_shape=[SDS((t, e), BF16), SDS((t, e), F32)] + (ride.out_shape if ride else []),
        scratch_shapes=(ride.scratch if ride else [])
        + [pltpu.VMEM((halo + tt, e), F32), pltpu.VMEM((SUBLANES, halo + tt, CONV_COLS), F32)],
        compiler_params=_cparams("arbitrary"),
    )(proj, cw, cb, lg, lb, *(ride.arrays if ride else []))


def _a_mid_bwd(proj, u1, dyz, cw, lg, lb, width, name, tt=256, ride=None):
    t, e3 = proj.shape
    e = e3 // 3
    halo = CONF_HALO
    nt = t // tt
    hb = tt // halo
    nr = ride.n if ride else 0

    def body(*refs):
        rin, rout = refs[7:7 + nr], refs[12 + nr:12 + 2 * nr]
        scratch = refs[12 + 2 * nr:]
        sems, own = (scratch[:3], scratch[3:]) if ride else ((), scratch)
        if ride:
            @pl.when(pl.program_id(0) == 0)
            def _():
                ride.start(rin, rout, sems)

        tile(*refs[:7], *refs[7 + nr:12 + nr], *own)
        if ride:
            @pl.when(pl.program_id(0) == nt - 1)
            def _():
                ride.wait(rin, rout, sems)

    def tile(p_ref, pp_ref, u1_ref, dy_ref, cw_ref, lg_ref, lb_ref,
             dp_ref, dcw_ref, dcb_ref, dlg_ref, dlb_ref, ubuf, dbuf, shifted, acc_cb, acc_lg, acc_lb):
        i = pl.program_id(0)
        ti = nt - 1 - i

        @pl.when(i == 0)
        def _():
            dbuf[pl.ds(tt, halo), :] = jnp.zeros((halo, e), F32)
            dcw_ref[...] = jnp.zeros_like(dcw_ref)
            acc_cb[...] = jnp.zeros_like(acc_cb)
            acc_lg[...] = jnp.zeros_like(acc_lg)
            acc_lb[...] = jnp.zeros_like(acc_lb)

        @pl.when(i > 0)
        def _():
            dbuf[pl.ds(tt, halo), :] = dbuf[pl.ds(0, halo), :]

        keep = (ti > 0).astype(F32)
        ubuf[pl.ds(0, halo), :] = keep * (pp_ref[:, pl.ds(0, e)].astype(F32) * _sigmoid(pp_ref[:, pl.ds(e, e)].astype(F32)))
        for r0 in range(0, tt, CONV_ROWS):
            val = p_ref[pl.ds(r0, CONV_ROWS), pl.ds(0, e)].astype(F32)
            gate = p_ref[pl.ds(r0, CONV_ROWS), pl.ds(e, e)].astype(F32)
            ubuf[pl.ds(halo + r0, CONV_ROWS), :] = val * _sigmoid(gate)

        for r0 in range(0, tt, LN_ROWS):
            rows = pl.ds(r0, LN_ROWS)
            u = u1_ref[rows, :]
            mu = jnp.mean(u, axis=-1, keepdims=True)
            dlt = u - mu
            var = jnp.mean(dlt * dlt, axis=-1, keepdims=True)
            rstd = lax.rsqrt(var + NORM_EPS)
            xh = dlt * rstd
            u2 = xh * lg_ref[...] + lb_ref[...]
            s2 = _sigmoid(u2)
            u3 = u2 * s2
            z = p_ref[rows, pl.ds(2 * e, e)].astype(F32)
            sz = _sigmoid(z)
            dy = dy_ref[rows, :].astype(F32)
            dp_ref[rows, pl.ds(2 * e, e)] = (dy * u3 * (sz * (1.0 + z * (1.0 - sz)))).astype(BF16)
            du2 = (dy * (z * sz)) * (s2 * (1.0 + u2 * (1.0 - s2)))
            acc_lg[...] += _rows8(du2 * xh)
            acc_lb[...] += _rows8(du2)
            dxh = du2 * lg_ref[...]
            m1 = jnp.mean(dxh, axis=-1, keepdims=True)
            m2 = jnp.mean(dxh * xh, axis=-1, keepdims=True)
            du1 = rstd * (dxh - m1 - xh * m2)
            dbuf[rows, :] = du1
            acc_cb[...] += _rows8(du1)

        for c0 in range(0, e, CONV_COLS):
            _shifted_copies(shifted, dbuf, c0, tt + halo, 1)
            for r0 in range(0, tt, CONV_ROWS):
                du0 = _conv_aligned(shifted, 0, cw_ref, width, r0, c0, 1)
                rows, cols = pl.ds(r0, CONV_ROWS), pl.ds(c0, CONV_COLS)
                val = p_ref[rows, cols].astype(F32)
                sg = _sigmoid(p_ref[rows, pl.ds(e + c0, CONV_COLS)].astype(F32))
                dp_ref[rows, cols] = (du0 * sg).astype(BF16)
                dp_ref[rows, pl.ds(e + c0, CONV_COLS)] = (du0 * val * sg * (1.0 - sg)).astype(BF16)
            _shifted_copies(shifted, ubuf, c0, halo + tt, -1)
            _conv_weight_grad_aligned(dcw_ref, dbuf, shifted, halo, width, tt, c0)

        @pl.when(i == nt - 1)
        def _():
            dcb_ref[...] = jnp.sum(acc_cb[...], axis=0, keepdims=True)
            dlg_ref[...] = jnp.sum(acc_lg[...], axis=0, keepdims=True)
            dlb_ref[...] = jnp.sum(acc_lb[...], axis=0, keepdims=True)

    vec = pl.BlockSpec((1, e), lambda i: (0, 0))
    return pl.pallas_call(
        body, name=name, grid=(nt,),
        in_specs=[pl.BlockSpec((tt, e3), lambda i: (nt - 1 - i, 0)),
                  pl.BlockSpec((halo, e3), lambda i: (jnp.maximum((nt - 1 - i) * hb - 1, 0), 0)),
                  pl.BlockSpec((tt, e), lambda i: (nt - 1 - i, 0)), pl.BlockSpec((tt, e), lambda i: (nt - 1 - i, 0)),
                  pl.BlockSpec(cw.shape, lambda i: (0, 0)), vec, vec] + (ride.in_specs if ride else []),
        out_specs=[pl.BlockSpec((tt, e3), lambda i: (nt - 1 - i, 0)), pl.BlockSpec(cw.shape, lambda i: (0, 0)), vec, vec, vec]
        + (ride.out_specs if ride else []),
        out_shape=[SDS((t, e3), BF16), SDS(cw.shape, F32), SDS((1, e), F32), SDS((1, e), F32), SDS((1, e), F32)]
        + (ride.out_shape if ride else []),
        scratch_shapes=(ride.scratch if ride else [])
        + [pltpu.VMEM((halo + tt, e), F32), pltpu.VMEM((tt + halo, e), F32),
           pltpu.VMEM((SUBLANES, halo + tt, CONV_COLS), F32),
           pltpu.VMEM((SUBLANES, e), F32), pltpu.VMEM((SUBLANES, e), F32), pltpu.VMEM((SUBLANES, e), F32)],
        compiler_params=_cparams("arbitrary"),
    )(proj, proj, u1, dyz, cw, lg, lb, *(ride.arrays if ride else []))


def _c_mid_fwd(proj, cw, width, name, tt=256):
    t, e4 = proj.shape
    e = e4 // 4
    halo = SHORT_HALO

    def body(p_ref, cw_ref, y_ref, wbuf):
        i = pl.program_id(0)

        @pl.when(i == 0)
        def _():
            wbuf[pl.ds(0, halo), :] = jnp.zeros((halo, e), F32)

        @pl.when(i > 0)
        def _():
            wbuf[pl.ds(0, halo), :] = wbuf[pl.ds(tt, halo), :]

        for r0 in range(0, tt, CONV_ROWS):
            rows = pl.ds(r0, CONV_ROWS)
            wbuf[pl.ds(halo + r0, CONV_ROWS), :] = p_ref[rows, pl.ds(2 * e, e)].astype(F32) * p_ref[rows, pl.ds(0, e)].astype(F32)
        for c0 in range(0, e, CONV_COLS):
            for r0 in range(0, tt, CONV_ROWS):
                rows = pl.ds(r0, CONV_ROWS)
                cv = _conv_chunk(wbuf, halo, cw_ref, width, r0, c0, False)
                bg = p_ref[rows, pl.ds(e + c0, CONV_COLS)].astype(F32)
                z = p_ref[rows, pl.ds(3 * e + c0, CONV_COLS)].astype(F32)
                y_ref[rows, pl.ds(c0, CONV_COLS)] = ((bg * cv) * _silu(z)).astype(BF16)

    return pl.pallas_call(
        body, name=name, grid=(t // tt,),
        in_specs=[pl.BlockSpec((tt, e4), lambda i: (i, 0)), pl.BlockSpec(cw.shape, lambda i: (0, 0))],
        out_specs=pl.BlockSpec((tt, e), lambda i: (i, 0)),
        out_shape=SDS((t, e), BF16),
        scratch_shapes=[pltpu.VMEM((halo + tt, e), F32)],
        compiler_params=_cparams("arbitrary"),
    )(proj, cw)


def _c_mid_bwd(proj, dyz, cw, width, name, tt=256):
    t, e4 = proj.shape
    e = e4 // 4
    halo = SHORT_HALO
    nt = t // tt
    hb = tt // halo

    def body(p_ref, pp_ref, dy_ref, cw_ref, dp_ref, dcw_ref, wbuf, dbuf):
        i = pl.program_id(0)
        ti = nt - 1 - i

        @pl.when(i == 0)
        def _():
            dbuf[pl.ds(tt, halo), :] = jnp.zeros((halo, e), F32)
            dcw_ref[...] = jnp.zeros_like(dcw_ref)

        @pl.when(i > 0)
        def _():
            dbuf[pl.ds(tt, halo), :] = dbuf[pl.ds(0, halo), :]

        keep = (ti > 0).astype(F32)
        wbuf[pl.ds(0, halo), :] = keep * (pp_ref[:, pl.ds(2 * e, e)].astype(F32) * pp_ref[:, pl.ds(0, e)].astype(F32))
        for r0 in range(0, tt, CONV_ROWS):
            rows = pl.ds(r0, CONV_ROWS)
            wbuf[pl.ds(halo + r0, CONV_ROWS), :] = p_ref[rows, pl.ds(2 * e, e)].astype(F32) * p_ref[rows, pl.ds(0, e)].astype(F32)
        for c0 in range(0, e, CONV_COLS):
            for r0 in range(0, tt, CONV_ROWS):
                rows, cols = pl.ds(r0, CONV_ROWS), pl.ds(c0, CONV_COLS)
                cv = _conv_chunk(wbuf, halo, cw_ref, width, r0, c0, False)
                bg = p_ref[rows, pl.ds(e + c0, CONV_COLS)].astype(F32)
                z = p_ref[rows, pl.ds(3 * e + c0, CONV_COLS)].astype(F32)
                sz = _sigmoid(z)
                dyz_c = dy_ref[rows, cols].astype(F32)
                dy = dyz_c * (z * sz)
                dp_ref[rows, pl.ds(3 * e + c0, CONV_COLS)] = (dyz_c * (bg * cv) * (sz * (1.0 + z * (1.0 - sz)))).astype(BF16)
                dp_ref[rows, pl.ds(e + c0, CONV_COLS)] = (dy * cv).astype(BF16)
                dbuf[rows, cols] = dy * bg
        for c0 in range(0, e, CONV_COLS):
            for r0 in range(0, tt, CONV_ROWS):
                rows, cols = pl.ds(r0, CONV_ROWS), pl.ds(c0, CONV_COLS)
                dw = _conv_chunk(dbuf, 0, cw_ref, width, r0, c0, True)
                dp_ref[rows, pl.ds(2 * e + c0, CONV_COLS)] = (dw * p_ref[rows, cols].astype(F32)).astype(BF16)
                dp_ref[rows, cols] = (dw * p_ref[rows, pl.ds(2 * e + c0, CONV_COLS)].astype(F32)).astype(BF16)
        _conv_weight_grad(dcw_ref, dbuf, wbuf, halo, width, tt, e)

    return pl.pallas_call(
        body, name=name, grid=(nt,),
        in_specs=[pl.BlockSpec((tt, e4), lambda i: (nt - 1 - i, 0)),
                  pl.BlockSpec((halo, e4), lambda i: (jnp.maximum((nt - 1 - i) * hb - 1, 0), 0)),
                  pl.BlockSpec((tt, e), lambda i: (nt - 1 - i, 0)), pl.BlockSpec(cw.shape, lambda i: (0, 0))],
        out_specs=[pl.BlockSpec((tt, e4), lambda i: (nt - 1 - i, 0)), pl.BlockSpec(cw.shape, lambda i: (0, 0))],
        out_shape=[SDS((t, e4), BF16), SDS(cw.shape, F32)],
        scratch_shapes=[pltpu.VMEM((halo + tt, e), F32), pltpu.VMEM((tt + halo, e), F32)],
        compiler_params=_cparams("arbitrary"),
    )(proj, proj, dyz, cw)


def _b_prep_fwd(proj, flog, fbias, qg, kg, heads, name, tt=256):
    t, e4 = proj.shape
    e = e4 // 4
    scale = HEAD_DIM ** -0.5 * LOG2E

    def body(q_ref, k_ref, fl_ref, fb_ref, qg_ref, kg_ref, qs_ref, kn_ref, c_ref, ct_ref, carry):
        i = pl.program_id(0)

        @pl.when(i == 0)
        def _():
            carry[...] = jnp.zeros_like(carry)

        for h in range(heads):
            cols = pl.ds(h * HEAD_DIM, HEAD_DIM)
            qh = q_ref[:, cols].astype(F32)
            r = lax.rsqrt(jnp.mean(qh * qh, axis=-1, keepdims=True) + NORM_EPS)
            qs_ref[:, cols] = (((qh * r) * qg_ref[:, cols]) * scale).astype(BF16)
            kh = k_ref[:, cols].astype(F32)
            r = lax.rsqrt(jnp.mean(kh * kh, axis=-1, keepdims=True) + NORM_EPS)
            kn_ref[:, cols] = ((kh * r) * kg_ref[:, cols]).astype(BF16)

        a = fl_ref[...] + fb_ref[...]
        lf = jnp.minimum(a, 0.0) - jnp.log(1.0 + jnp.exp(-jnp.abs(a)))
        tri = (lax.broadcasted_iota(jnp.int32, (tt, tt), 0) >= lax.broadcasted_iota(jnp.int32, (tt, tt), 1)).astype(BF16)
        c = _tri_matmul(tri, lf) + carry[...]
        c_ref[...] = c
        ct_ref[...] = (c * LOG2E).T
        carry[...] = c_ref[pl.ds(tt - 1, 1), :]

    return pl.pallas_call(
        body, name=name, grid=(t // tt,),
        in_specs=[pl.BlockSpec((tt, e), lambda i: (i, 0)), pl.BlockSpec((tt, e), lambda i: (i, 1)),
                  pl.BlockSpec((tt, LANES), lambda i: (i, 0)), pl.BlockSpec((1, LANES), lambda i: (0, 0)),
                  pl.BlockSpec((1, e), lambda i: (0, 0)), pl.BlockSpec((1, e), lambda i: (0, 0))],
        out_specs=[pl.BlockSpec((tt, e), lambda i: (i, 0)), pl.BlockSpec((tt, e), lambda i: (i, 0)),
                   pl.BlockSpec((tt, LANES), lambda i: (i, 0)), pl.BlockSpec((LANES, tt), lambda i: (0, i))],
        out_shape=[SDS((t, e), BF16), SDS((t, e), BF16), SDS((t, LANES), F32), SDS((LANES, t), F32)],
        scratch_shapes=[pltpu.VMEM((1, LANES), F32)],
        compiler_params=_cparams("arbitrary"),
    )(proj, proj, flog, fbias, qg, kg)


ATT_BLOCK = 1024
ATT_CHUNK = 512
NEG_BIG = -1e30
LOG2E = 1.4426950408889634
LN2 = 0.6931471805599453


def _flash_fwd(qs, kn, proj, ck, heads, name):
    t, e = qs.shape
    blk = min(ATT_BLOCK, t)
    cw = min(ATT_CHUNK, blk // 2)
    nq, nch = t // blk, blk // cw
    assert nch % 2 == 0

    def body(q_ref, k_ref, v_ref, ck_ref, z_ref, o_ref, y_ref, m_ref, l_ref, s_a, s_b):
        i = pl.program_id(1)
        q = q_ref[...]
        bufs = (s_a, s_b)

        def key_rows(j, c):
            return pl.ds(pl.multiple_of(j * blk, blk) + c * cw, cw)

        def logits(j, c):
            bufs[c % 2][...] = (lax.dot_general(q, k_ref[key_rows(j, c), :], NT_DIMS, preferred_element_type=F32)
                            - ck_ref[j][:, c * cw:(c + 1) * cw])

        def weights(c, m, l, masked):
            s = bufs[c % 2][...]
            if masked:
                keep = lax.broadcasted_iota(jnp.int32, (blk, cw), 0) >= (lax.broadcasted_iota(jnp.int32, (blk, cw), 1) + c * cw)
                s = jnp.where(keep, s, NEG_BIG)
            m_new = jnp.maximum(m, jnp.ceil(jnp.max(s, axis=-1, keepdims=True)))
            alpha = jnp.exp2(m - m_new)
            p = jnp.exp2(s - m_new).astype(BF16)
            return m_new, alpha * l + jnp.sum(p.astype(F32), axis=-1, keepdims=True), alpha, p

        def block(j, carry, masked):
            m, l, acc = carry
            for c in range(nch):
                if c + 1 < nch:
                    logits(j, c + 1)
                elif not masked:
                    logits(j + 1, 0)
                m, l, alpha, p = weights(c, m, l, masked)
                acc = alpha * acc + jnp.dot(p, v_ref[key_rows(j, c), :], preferred_element_type=F32)
            return m, l, acc

        logits(0, 0)
        carry = (jnp.full((blk, 1), NEG_BIG, F32), jnp.zeros((blk, 1), F32), jnp.zeros((blk, HEAD_DIM), F32))
        carry = lax.fori_loop(0, i, lambda j, cr: block(j, cr, False), carry)
        m, l, acc = block(i, carry, True)
        o = acc / l
        o_ref[...] = o
        y_ref[...] = (o * _silu(z_ref[...].astype(F32))).astype(BF16)
        m_ref[...] = jnp.broadcast_to(m, (blk, LANES))
        l_ref[...] = jnp.broadcast_to(l, (blk, LANES))

    head_all = pl.BlockSpec((t, HEAD_DIM), lambda h, i: (0, h))
    tile = pl.BlockSpec((blk, HEAD_DIM), lambda h, i: (i, h))
    stat = pl.BlockSpec((None, blk, LANES), lambda h, i: (h, i, 0))
    return pl.pallas_call(
        body, name=name, grid=(heads, nq),
        in_specs=[tile, head_all, pl.BlockSpec((t, HEAD_DIM), lambda h, i: (0, 2 * heads + h)),
                  pl.BlockSpec((None, nq, 1, blk), lambda h, i: (h, 0, 0, 0)),
                  pl.BlockSpec((blk, HEAD_DIM), lambda h, i: (i, 3 * heads + h))],
        out_specs=[tile, tile, stat, stat],
        out_shape=[SDS((t, e), F32), SDS((t, e), BF16), SDS((heads, t, LANES), F32), SDS((heads, t, LANES), F32)],
        scratch_shapes=[pltpu.VMEM((blk, cw), F32), pltpu.VMEM((blk, cw), F32)],
        compiler_params=_cparams("parallel", "arbitrary"),
    )(qs, kn, proj, ck, proj)


def _flash_bwd(qs, kn, proj, ck, dyz, o, mstat, lstat, heads, name):
    t, e = qs.shape
    blk = min(ATT_BLOCK, t)
    cw = min(ATT_CHUNK, blk // 2)
    nq, nch = t // blk, blk // cw
    assert nch % 2 == 0

    def body(q_ref, dy_ref, o_ref, z_ref, m_ref, l_ref, k_ref, v_ref, ck_ref,
             dq_ref, dk_ref, dv_ref, dc_ref, dz_ref, s_a, s_b, d_a, d_b):
        i = pl.program_id(1)

        @pl.when(i == 0)
        def _():
            dk_ref[...] = jnp.zeros_like(dk_ref)
            dv_ref[...] = jnp.zeros_like(dv_ref)
            dc_ref[...] = jnp.zeros_like(dc_ref)

        z = z_ref[...].astype(F32)
        sz = _sigmoid(z)
        dy = dy_ref[...].astype(F32)
        of = o_ref[...]
        do = ((dy * (z * sz)) / l_ref[:, 0:1]).astype(BF16)
        dz_ref[...] = (dy * of * (sz * (1.0 + z * (1.0 - sz)))).astype(BF16)
        dl = jnp.sum(do.astype(F32) * of, axis=-1, keepdims=True)
        q = q_ref[...]
        mrow = m_ref[:, 0:1]
        sbuf, dbuf = (s_a, s_b), (d_a, d_b)

        def key_rows(j, c):
            return pl.ds(pl.multiple_of(j * blk, blk) + c * cw, cw)

        def products(j, c):
            rows = key_rows(j, c)
            sbuf[c % 2][...] = (lax.dot_general(q, k_ref[rows, :], NT_DIMS, preferred_element_type=F32)
                            - ck_ref[j][:, c * cw:(c + 1) * cw])
            dbuf[c % 2][...] = lax.dot_general(do, v_ref[rows, :], NT_DIMS, preferred_element_type=F32)

        def weights(c, masked):
            p = jnp.exp2(sbuf[c % 2][...] - mrow)
            if masked:
                keep = lax.broadcasted_iota(jnp.int32, (blk, cw), 0) >= (lax.broadcasted_iota(jnp.int32, (blk, cw), 1) + c * cw)
                p = jnp.where(keep, p, 0.0)
            p = p.astype(BF16)
            ds = p.astype(F32) * (dbuf[c % 2][...] - dl)
            return p, ds.astype(BF16), jnp.sum(ds, axis=0, keepdims=True)

        def outputs(j, c, p, dsb, colsum, dq):
            rows = key_rows(j, c)
            dv_ref[rows, :] += lax.dot_general(p, do, TN_DIMS, preferred_element_type=F32)
            dk_ref[rows, :] += lax.dot_general(dsb, q, TN_DIMS, preferred_element_type=F32)
            dc_ref[j, :, pl.ds(c * cw, cw)] -= colsum
            return dq + jnp.dot(dsb, k_ref[rows, :], preferred_element_type=F32)

        def block(j, dq, masked):
            products(j, 1)
            for c in range(nch):
                p, dsb, colsum = weights(c, masked)
                if c + 2 < nch:
                    products(j, c + 2)
                elif c + 2 == nch and not masked:
                    products(j + 1, 0)
                dq = outputs(j, c, p, dsb, colsum, dq)
            return dq

        products(0, 0)
        dq = lax.fori_loop(0, i, lambda j, acc: block(j, acc, False), jnp.zeros((blk, HEAD_DIM), F32))
        dq_ref[...] = block(i, dq, True)

    tile = pl.BlockSpec((blk, HEAD_DIM), lambda h, i: (i, h))
    stat = pl.BlockSpec((None, blk, LANES), lambda h, i: (h, i, 0))
    head_all = pl.BlockSpec((t, HEAD_DIM), lambda h, i: (0, h))
    cspec = pl.BlockSpec((None, nq, 1, blk), lambda h, i: (h, 0, 0, 0))
    return pl.pallas_call(
        body, name=name, grid=(heads, nq),
        in_specs=[tile, tile, tile, pl.BlockSpec((blk, HEAD_DIM), lambda h, i: (i, 3 * heads + h)), stat, stat, head_all,
                  pl.BlockSpec((t, HEAD_DIM), lambda h, i: (0, 2 * heads + h)), cspec],
        out_specs=[tile, head_all, head_all, cspec, tile],
        out_shape=[SDS((t, e), F32), SDS((t, e), F32), SDS((t, e), F32), SDS((heads, nq, 1, blk), F32), SDS((t, e), BF16)],
        scratch_shapes=[pltpu.VMEM((blk, cw), F32)] * 4,
        compiler_params=_cparams("parallel", "arbitrary"),
    )(qs, dyz, o, proj, mstat, lstat, kn, proj, ck)


def _b_prep_bwd(dqs, dkn, dv, dz, proj, qg, kg, dct, flog, fbias, heads, name, tt=256):
    t, e4 = proj.shape
    e = e4 // 4
    nt = t // tt
    scale = HEAD_DIM ** -0.5

    def body(dq_ref, dk_ref, dv_ref, dz_ref, q_ref, k_ref, qg_ref, kg_ref, dc_ref, fl_ref, fb_ref,
             dp_ref, dqg_ref, dkg_ref, dfb_ref, carry, dlf, acc_q, acc_k, acc_f):
        i = pl.program_id(0)

        @pl.when(i == 0)
        def _():
            carry[...] = jnp.zeros_like(carry)
            acc_q[...] = jnp.zeros_like(acc_q)
            acc_k[...] = jnp.zeros_like(acc_k)
            acc_f[...] = jnp.zeros_like(acc_f)

        for h in range(heads):
            cols = pl.ds(h * HEAD_DIM, HEAD_DIM)
            for src_ref, d_ref, g_ref, acc, mult, off in ((q_ref, dq_ref, qg_ref, acc_q, scale, 0),
                                                          (k_ref, dk_ref, kg_ref, acc_k, LN2, e)):
                xf = src_ref[:, cols].astype(F32)
                r = lax.rsqrt(jnp.mean(xf * xf, axis=-1, keepdims=True) + NORM_EPS)
                xh = xf * r
                dn = d_ref[:, cols] * mult
                acc[...] += _rows8(dn * xh)
                dxh = dn * g_ref[:, cols]
                dp_ref[:, pl.ds(off + h * HEAD_DIM, HEAD_DIM)] = (
                    r * (dxh - xh * jnp.mean(dxh * xh, axis=-1, keepdims=True))).astype(BF16)
        dp_ref[:, pl.ds(2 * e, e)] = dv_ref[...].astype(BF16)
        dp_ref[:, pl.ds(3 * e, e)] = dz_ref[...]

        tri = (lax.broadcasted_iota(jnp.int32, (tt, tt), 0) <= lax.broadcasted_iota(jnp.int32, (tt, tt), 1)).astype(BF16)
        dlf[...] = _tri_matmul(tri, dc_ref[...]) + carry[...]
        carry[...] = dlf[pl.ds(0, 1), :]
        a = fl_ref[...] + fb_ref[...]
        dfl = dlf[...] * _sigmoid(-a)
        dp_ref[:, pl.ds(4 * e, LANES)] = dfl.astype(BF16)
        acc_f[...] += _rows8(dfl)

        @pl.when(i == nt - 1)
        def _():
            dqg_ref[...] = jnp.sum(acc_q[...], axis=0, keepdims=True)
            dkg_ref[...] = jnp.sum(acc_k[...], axis=0, keepdims=True)
            dfb_ref[...] = jnp.sum(acc_f[...], axis=0, keepdims=True)

    rev = lambda i: (nt - 1 - i, 0)
    vec_e = pl.BlockSpec((1, e), lambda i: (0, 0))
    vec = pl.BlockSpec((1, LANES), lambda i: (0, 0))
    wide = pl.BlockSpec((tt, e), rev)
    lane = pl.BlockSpec((tt, LANES), rev)
    return pl.pallas_call(
        body, name=name, grid=(nt,),
        in_specs=[wide, wide, wide, wide, wide, pl.BlockSpec((tt, e), lambda i: (nt - 1 - i, 1)), vec_e, vec_e, lane, lane, vec],
        out_specs=[pl.BlockSpec((tt, e4 + LANES), rev), vec, vec, vec],
        out_shape=[SDS((t, e4 + LANES), BF16), SDS((1, LANES), F32), SDS((1, LANES), F32), SDS((1, LANES), F32)],
        scratch_shapes=[pltpu.VMEM((1, LANES), F32), pltpu.VMEM((tt, LANES), F32), pltpu.VMEM((SUBLANES, LANES), F32),
                        pltpu.VMEM((SUBLANES, LANES), F32), pltpu.VMEM((SUBLANES, LANES), F32)],
        compiler_params=_cparams("arbitrary"),
    )(dqs, dkn, dv, dz, proj, proj, qg, kg, dct, flog, fbias)


def _b_fwd(h, b_norm, wb_pad, wb_out, qg, kg, fbias, heads, tag, ride=None):
    t = h.shape[0]
    e = wb_out.shape[0]
    blk = min(ATT_BLOCK, t)
    hn, proj, *rode = _norm_matmul(h, b_norm, wb_pad, f"b_in_proj_{tag}", ride=ride, n_cols=4 * e)
    flog = _matmul_f32out(hn, wb_pad[:, 4 * e:], f"b_forget_proj_{tag}")
    qs, kn, _, ct = _b_prep_fwd(proj, flog, fbias, qg, kg, heads, f"b_prep_fwd_{tag}")
    ck = ct.reshape(LANES, t // blk, 1, blk)
    o, y, mstat, lstat = _flash_fwd(qs, kn, proj, ck, heads, f"b_attention_fwd_{tag}")
    sv = dict(x=h, hn=hn, proj=proj, flog=flog, qs=qs, kn=kn, ck=ck, o=o, y=y, mstat=mstat, lstat=lstat)
    return y, sv, rode


def _b_bwd(dh, sv, b_norm, wb_pad, wb_out, qg, kg, fbias, heads, tag, ride=None):
    t = dh.shape[0]
    e = wb_out.shape[0]
    gb = {}
    dyz = _matmul_nt(dh, wb_out, f"b_out_bwd_{tag}")
    gb["w_out"] = _matmul_tn(sv["y"], dh, f"b_out_wgrad_{tag}", tm=2048)
    dqs, dkn, dv, dc, dz = _flash_bwd(sv["qs"], sv["kn"], sv["proj"], sv["ck"], dyz, sv["o"], sv["mstat"], sv["lstat"],
                                      heads, f"b_attention_bwd_{tag}")
    dct = jnp.pad(dc.reshape(heads, t).T, ((0, 0), (0, LANES - heads)))
    dproj, dqg, dkg, dfb = _b_prep_bwd(dqs, dkn, dv, dz, sv["proj"], qg, kg, dct, sv["flog"], fbias, heads,
                                       f"b_prep_bwd_{tag}")
    gb["w_in"] = _matmul_tn(sv["hn"], dproj, f"b_in_wgrad_{tag}")[:, :4 * e + heads]
    dh, dg, *rode = _dproj_matmul_normbwd(dproj, wb_pad, sv["x"], b_norm, dh, f"b_in_bwd_{tag}", ride=ride)
    gb["norm"], gb["q_norm"], gb["k_norm"], gb["f_bias"] = dg, dqg, dkg, dfb[:, :heads]
    return dh, gb, rode


def _sum_adamw(recv, w, m, v, name, tr=256):
    nl, r, c = w.shape
    tr = min(tr, r)

    def body(g_ref, w_ref, m_ref, v_ref, go_ref, d_ref, mo_ref, vo_ref):
        g = g_ref[0].astype(F32)
        for s in range(1, N_DEV):
            g = g + g_ref[s].astype(F32)
        go_ref[...] = g
        mn = ADAM_B1 * m_ref[...] + (1.0 - ADAM_B1) * g
        vn = ADAM_B2 * v_ref[...] + (1.0 - ADAM_B2) * (g * g)
        m_hat = mn / (1.0 - ADAM_B1 ** ADAM_STEP)
        v_hat = vn / (1.0 - ADAM_B2 ** ADAM_STEP)
        d_ref[...] = -ADAM_LR * (m_hat / (jnp.sqrt(v_hat) + ADAM_EPS) + ADAM_WD * w_ref[...])
        mo_ref[...] = mn
        vo_ref[...] = vn

    blk = pl.BlockSpec((None, tr, c), lambda l, i: (l, i, 0))
    return pl.pallas_call(
        body, name=name, grid=(nl, r // tr),
        in_specs=[pl.BlockSpec((N_DEV, None, tr, c), lambda l, i: (0, l, i, 0)), blk, blk, blk],
        out_specs=[blk, blk, blk, blk],
        out_shape=[SDS(w.shape, F32)] * 4,
        compiler_params=_cparams("parallel", "parallel"),
    )(recv, w, m, v)


def _unshard(g, axis):
    g = jnp.moveaxis(g, 0, axis)
    return g.reshape(g.shape[:axis] + (g.shape[axis] * g.shape[axis + 1],) + g.shape[axis + 2:])


def _to_slabs(full, axis):
    n = full.shape[axis]
    s = full.reshape(full.shape[:axis] + (N_DEV, n // N_DEV) + full.shape[axis + 1:])
    return jnp.moveaxis(s, axis, 0)


def _pack_rows(parts, lead):
    flat = [p.reshape(p.shape[:lead] + (-1,)) for p in parts]
    cat = jnp.concatenate(flat, axis=-1)
    n = cat.shape[-1]
    pad = (-n) % (SUBLANES * LANES)
    cat = jnp.pad(cat, [(0, 0)] * lead + [(0, pad)])
    return cat.reshape(cat.shape[:lead] + ((n + pad) // LANES, LANES))


def _unpack_rows(packed, shapes, lead):
    flat = packed.reshape(packed.shape[:lead] + (-1,))
    out, off = [], 0
    for shp in shapes:
        size = int(np.prod(shp))
        out.append(flat[..., off:off + size].reshape(packed.shape[:lead] + tuple(shp)))
        off += size
    return out


def _pad_rows(w, rows):
    return jnp.pad(w, ((0, rows - w.shape[0]), (0, 0)))


def kernel(x, a_norm, a_w_in, a_conv_w, a_conv_b, a_ln_g, a_ln_b, a_w_out, b_norm, b_w_in, b_f_bias, b_q_norm, b_k_norm, b_w_out, c_norm, c_w_in, c_conv_w, c_w_out, loss_target, m_a_norm, m_a_w_in, m_a_conv_w, m_a_conv_b, m_a_ln_g, m_a_ln_b, m_a_w_out, m_b_norm, m_b_w_in, m_b_f_bias, m_b_q_norm, m_b_k_norm, m_b_w_out, m_c_norm, m_c_w_in, m_c_conv_w, m_c_w_out, v_a_norm, v_a_w_in, v_a_conv_w, v_a_conv_b, v_a_ln_g, v_a_ln_b, v_a_w_out, v_b_norm, v_b_w_in, v_b_f_bias, v_b_q_norm, v_b_k_norm, v_b_w_out, v_c_norm, v_c_w_in, v_c_conv_w, v_c_w_out):
    t, d = x.shape[1], x.shape[2]
    e = a_w_out.shape[1] * N_DEV
    heads = b_f_bias.shape[1]
    n_a, n_b, n_c = a_norm.shape[0], b_norm.shape[0], c_norm.shape[0]
    depth = n_a + n_b + n_c
    ka, kc = a_conv_w.shape[1], c_conv_w.shape[1]
    assert e == heads * HEAD_DIM and n_b == 1 and n_c == 1 and x.shape[0] == 1

    layers = [(i % 3, i // 3) for i in range(depth)]

    def mat_shards(kind, j):
        w_in, w_out = ((a_w_in, a_w_out), (b_w_in, b_w_out), (c_w_in, c_w_out))[kind]
        return [w_in[j].astype(BF16), w_out[j].astype(BF16)]

    def full_mats(kind, gathered):
        w_in, w_out = _unshard(gathered[0], 1), _unshard(gathered[1], 0)
        return (jnp.pad(w_in, ((0, 0), (0, LANES - heads))) if kind == 1 else w_in), w_out

    small_names = ["a_norm", "a_conv_w", "a_conv_b", "a_ln_g", "a_ln_b", "c_norm", "c_conv_w"]
    small = dict(a_norm=a_norm, a_conv_w=a_conv_w, a_conv_b=a_conv_b, a_ln_g=a_ln_g, a_ln_b=a_ln_b,
                 c_norm=c_norm, c_conv_w=c_conv_w)
    small_pack = _pack_rows([small[k] for k in small_names], 0)
    first = _all_gather(mat_shards(*layers[0]) + [small_pack], "all_gather_first_layer")
    sm = _unpack_rows(first[2], [small[k].shape for k in small_names], 1)
    g_a_norm = _unshard(sm[0], 1)
    g_a_conv_w = _unshard(sm[1], 2)
    g_a_conv_b = _unshard(sm[2], 1)
    g_a_ln_g = _unshard(sm[3], 1)
    g_a_ln_b = _unshard(sm[4], 1)
    g_c_norm = _unshard(sm[5], 1)
    g_c_conv_w = _unshard(sm[6], 2)

    cw_a = [_pad_rows(g_a_conv_w[j], CONF_HALO) for j in range(n_a)]
    cw_c = _pad_rows(g_c_conv_w[0], SHORT_HALO)
    qg = jnp.tile(b_q_norm, (1, heads))
    kg = jnp.tile(b_k_norm, (1, heads))
    fbias = jnp.pad(b_f_bias, ((0, 0), (0, LANES - heads)))

    h = x[0]
    saved, weights = [], [full_mats(layers[0][0], first[:2])]
    for i, (kind, j) in enumerate(layers):
        tag = f"l{i}"
        w_in, w_out = weights[i]
        ride = _Ride("gather", mat_shards(*layers[i + 1])) if i + 1 < depth else None
        if kind == 0:
            hn, proj = _norm_matmul(h, g_a_norm[j:j + 1], w_in, f"a_in_proj_{tag}")
            y, u1, *rode = _a_mid_fwd(proj, cw_a[j], g_a_conv_b[j:j + 1], g_a_ln_g[j:j + 1], g_a_ln_b[j:j + 1], ka,
                                      f"a_mid_fwd_{tag}", ride=ride)
            saved.append(dict(x=h, hn=hn, proj=proj, u1=u1, y=y))
        elif kind == 1:
            y, sv, rode = _b_fwd(h, b_norm, w_in, w_out, qg, kg, fbias, heads, tag, ride=ride)
            saved.append(sv)
        else:
            hn, proj, *rode = _norm_matmul(h, g_c_norm, w_in, f"c_in_proj_{tag}", ride=ride)
            y = _c_mid_fwd(proj, cw_c, kc, f"c_mid_fwd_{tag}")
            saved.append(dict(x=h, hn=hn, proj=proj, y=y))
        if ride:
            weights.append(full_mats(layers[i + 1][0], rode))
            h = _out_matmul_residual(y, w_out, h, f"{'abc'[kind]}_out_proj_{tag}")
        else:
            loss_part, dh = _out_matmul_loss(y, w_out, h, loss_target[0], f"{'abc'[kind]}_out_proj_loss_{tag}")
    loss_local = jnp.sum(loss_part).reshape(1, 1)

    ga = dict(norm=[None] * n_a, conv_w=[None] * n_a, conv_b=[None] * n_a, ln_g=[None] * n_a, ln_b=[None] * n_a)
    gb, gc = None, {}
    recv_mats = [None] * depth
    pending = None
    for i in reversed(range(depth)):
        kind, j = layers[i]
        tag = f"l{i}"
        sv = saved[i]
        w_in, w_out = weights[i]
        ride = _Ride("exchange", pending) if pending is not None else None
        if kind == 0:
            dyz = _matmul_nt(dh, w_out, f"a_out_bwd_{tag}")
            gw_out = _matmul_tn(sv["y"], dh, f"a_out_wgrad_{tag}", tm=2048)
            mid_ride, ride = (ride, None) if i == 0 else (None, ride)
            dproj, dcw, dcb, dlg, dlb, *mid_rode = _a_mid_bwd(sv["proj"], sv["u1"], dyz, cw_a[j], g_a_ln_g[j:j + 1],
                                                              g_a_ln_b[j:j + 1], ka, f"a_mid_bwd_{tag}", ride=mid_ride)
            gw_in = _matmul_tn(sv["hn"], dproj, f"a_in_wgrad_{tag}")
            if i == 0:
                ride = _Ride("exchange", [_to_slabs(gw_in, 1), _to_slabs(gw_out, 0)])
            dh, dg, *rode = _dproj_matmul_normbwd(dproj, w_in, sv["x"], g_a_norm[j:j + 1], dh, f"a_in_bwd_{tag}", ride=ride)
            if i == 0:
                if mid_ride:
                    recv_mats[1] = mid_rode
                recv_mats[0], ride = rode, None
            ga["norm"][j], ga["conv_w"][j], ga["conv_b"][j], ga["ln_g"][j], ga["ln_b"][j] = dg[0], dcw[:ka], dcb[0], dlg[0], dlb[0]
        elif kind == 1:
            dh, gb, rode = _b_bwd(dh, sv, b_norm, w_in, w_out, qg, kg, fbias, heads, tag, ride=ride)
            gw_in, gw_out = gb["w_in"], gb["w_out"]
        else:
            dyz = _matmul_nt(dh, w_out, f"c_out_bwd_{tag}")
            gw_out = _matmul_tn(sv["y"], dh, f"c_out_wgrad_{tag}", tm=2048)
            dproj, dcw = _c_mid_bwd(sv["proj"], dyz, cw_c, kc, f"c_mid_bwd_{tag}")
            gw_in = _matmul_tn(sv["hn"], dproj, f"c_in_wgrad_{tag}")
            dh, dg, *rode = _dproj_matmul_normbwd(dproj, w_in, sv["x"], g_c_norm, dh, f"c_in_bwd_{tag}", ride=ride)
            gc["norm"], gc["conv_w"] = dg, dcw[:kc][None]
        if ride:
            recv_mats[i + 1] = rode
        pending = [_to_slabs(gw_in, 1), _to_slabs(gw_out, 0)]
    grad_x = dh[None]

    sharded_small = [(jnp.stack(ga["norm"]), 1), (jnp.stack(ga["conv_w"]), 2), (jnp.stack(ga["conv_b"]), 1),
                     (jnp.stack(ga["ln_g"]), 1), (jnp.stack(ga["ln_b"]), 1), (gc["norm"], 1), (gc["conv_w"], 2)]
    repl_small = [gb["norm"], gb["f_bias"], gb["q_norm"], gb["k_norm"], loss_local]
    small_slabs = _pack_rows([_to_slabs(g, ax) for g, ax in sharded_small]
                             + [jnp.broadcast_to(g[None], (N_DEV,) + g.shape) for g in repl_small], 1)
    recv_small, = _exchange([small_slabs], "exchange_vector_gradients")

    outs = {}
    mat_w = dict(a_w_in=(a_w_in, m_a_w_in, v_a_w_in), a_w_out=(a_w_out, m_a_w_out, v_a_w_out),
                 b_w_in=(b_w_in, m_b_w_in, v_b_w_in), b_w_out=(b_w_out, m_b_w_out, v_b_w_out),
                 c_w_in=(c_w_in, m_c_w_in, v_c_w_in), c_w_out=(c_w_out, m_c_w_out, v_c_w_out))
    for kind, prefix in enumerate("abc"):
        members = [i for i, (k, _) in enumerate(layers) if k == kind]
        for which, name in enumerate((f"{prefix}_w_in", f"{prefix}_w_out")):
            recv = jnp.stack([recv_mats[i][which] for i in members], axis=1)
            outs[name] = _sum_adamw(recv, *mat_w[name], f"adamw_{name}")

    small_order = small_names + ["b_norm", "b_f_bias", "b_q_norm", "b_k_norm", "loss"]
    no_state = jnp.zeros((1, 1), F32)
    small_w = dict(a_norm=(a_norm, m_a_norm, v_a_norm), a_conv_w=(a_conv_w, m_a_conv_w, v_a_conv_w),
                   a_conv_b=(a_conv_b, m_a_conv_b, v_a_conv_b), a_ln_g=(a_ln_g, m_a_ln_g, v_a_ln_g),
                   a_ln_b=(a_ln_b, m_a_ln_b, v_a_ln_b), c_norm=(c_norm, m_c_norm, v_c_norm),
                   c_conv_w=(c_conv_w, m_c_conv_w, v_c_conv_w), b_norm=(b_norm, m_b_norm, v_b_norm),
                   b_f_bias=(b_f_bias, m_b_f_bias, v_b_f_bias), b_q_norm=(b_q_norm, m_b_q_norm, v_b_q_norm),
                   b_k_norm=(b_k_norm, m_b_k_norm, v_b_k_norm), loss=(no_state, no_state, no_state))
    packs = [_pack_rows([small_w[k][q] for k in small_order], 0)[None] for q in range(3)]
    small_out = _sum_adamw(recv_small[:, None], *packs, "adamw_vectors")
    shapes = [small_w[k][0].shape for k in small_order]
    unpacked = [_unpack_rows(o[0], shapes, 0) for o in small_out]
    for idx, name in enumerate(small_order):
        outs[name] = tuple(unpacked[q][idx] for q in range(4))

    loss = outs["loss"][0].reshape(())
    order = ["a_norm", "a_w_in", "a_conv_w", "a_conv_b", "a_ln_g", "a_ln_b", "a_w_out", "b_norm", "b_w_in", "b_f_bias",
             "b_q_norm", "b_k_norm", "b_w_out", "c_norm", "c_w_in", "c_conv_w", "c_w_out"]
    return (loss, grad_x, *[outs[k][0] for k in order], *[outs[k][1] for k in order],
            *[outs[k][2] for k in order], *[outs[k][3] for k in order])
```

```python
import jax
import jax.numpy as jnp
import numpy as np
from jax import lax
from jax.experimental import pallas as pl
from jax.experimental.pallas import tpu as pltpu

F32 = jnp.float32
BF16 = jnp.bfloat16
SDS = jax.ShapeDtypeStruct

NORM_EPS = 1e-6
ADAM_LR = 0.001
ADAM_B1 = 0.9
ADAM_B2 = 0.999
ADAM_EPS = 1e-08
ADAM_WD = 0.01
ADAM_STEP = 10

N_DEV = 8
LANES = 128
SUBLANES = 8
HEAD_DIM = 128
CONF_HALO = 32
SHORT_HALO = 8
VMEM_LIMIT = 56 * 1024 * 1024

NT_DIMS = (((1,), (1,)), ((), ()))
TN_DIMS = (((0,), (0,)), ((), ()))
MESH = pl.DeviceIdType.MESH
ANY = pl.BlockSpec(memory_space=pl.ANY)


def _cparams(*sem):
    return pltpu.CompilerParams(dimension_semantics=sem, vmem_limit_bytes=VMEM_LIMIT)


def _divisor_tile(n, cap):
    return max(m for m in range(LANES, min(n, cap) + 1, LANES) if n % m == 0)


def _sigmoid(x):
    return 1.0 / (1.0 + jnp.exp(-x))


def _silu(x):
    return x * _sigmoid(x)


def _dsilu(x):
    s = _sigmoid(x)
    return s * (1.0 + x * (1.0 - s))


def _rows8(v):
    out = v[0:SUBLANES]
    for a in range(1, v.shape[0] // SUBLANES):
        out = out + v[a * SUBLANES:(a + 1) * SUBLANES]
    return out


def _split3(v):
    hi = v.astype(BF16)
    r1 = v - hi.astype(F32)
    mid = r1.astype(BF16)
    lo = (r1 - mid.astype(F32)).astype(BF16)
    return hi, mid, lo


def _tri_matmul(tri, v):
    hi, mid, lo = _split3(v)
    return (jnp.dot(tri, hi, preferred_element_type=F32) + jnp.dot(tri, mid, preferred_element_type=F32)
            + jnp.dot(tri, lo, preferred_element_type=F32))


def _position():
    return lax.axis_index("x"), lax.axis_index("y"), lax.axis_index("c")


def _all_gather(shards, name):
    n = len(shards)

    def body(*refs):
        xs, outs = refs[:n], refs[n:2 * n]
        send_sems, recv_sems, local_sems = refs[2 * n:]
        x, y, c = _position()
        me, sibling = (x, y, c), (x, y, 1 - c)
        chips = [(1 - x, y), (x, 1 - y), (1 - x, 1 - y)]

        def slot(a, px, py, pc):
            return outs[a].at[4 * px + 2 * py + pc]

        def copy(a, k, block, to, src=None):
            return pltpu.make_async_remote_copy(
                src_ref=slot(a, *block) if src is None else src, dst_ref=slot(a, *block),
                send_sem=send_sems.at[a, k], recv_sem=recv_sems.at[a, k], device_id=to, device_id_type=MESH)

        started = []
        mine = []
        for a in range(n):
            cp = pltpu.make_async_copy(xs[a], slot(a, *me), local_sems.at[a])
            cp.start()
            mine.append(cp)
        for a in range(n):
            first = [copy(a, 0, me, sibling, src=xs[a])]
            first += [copy(a, 1 + j, me, (*chip, c), src=xs[a]) for j, chip in enumerate(chips)]
            for cp in first:
                cp.start()
            started += first
        for a in range(n):
            for j, chip in enumerate(chips):
                copy(a, 1 + j, (*chip, c), me).wait_recv()
                fwd = copy(a, 4 + j, (*chip, c), sibling)
                fwd.start()
                started.append(fwd)
        for a in range(n):
            copy(a, 0, sibling, me).wait_recv()
            for j, chip in enumerate(chips):
                copy(a, 4 + j, (*chip, 1 - c), me).wait_recv()
        for cp in started:
            cp.wait_send()
        for cp in mine:
            cp.wait()

    return pl.pallas_call(
        body, name=name,
        out_shape=[SDS((N_DEV,) + s.shape, s.dtype) for s in shards],
        in_specs=[ANY] * n, out_specs=[ANY] * n,
        scratch_shapes=[pltpu.SemaphoreType.DMA((n, 7)), pltpu.SemaphoreType.DMA((n, 7)), pltpu.SemaphoreType.DMA((n,))],
    )(*shards)


def _exchange(slabs, name):
    n = len(slabs)

    def body(*refs):
        ins, outs = refs[:n], refs[n:2 * n]
        send_sems, recv_sems, local_sems = refs[2 * n:]
        x, y, c = _position()
        me = 4 * x + 2 * y + c
        peers = [(x ^ bx, y ^ by, c ^ bc) for bx in (0, 1) for by in (0, 1) for bc in (0, 1)][1:]

        def copy(a, k, peer):
            pid = 4 * peer[0] + 2 * peer[1] + peer[2]
            return pltpu.make_async_remote_copy(
                src_ref=ins[a].at[pid], dst_ref=outs[a].at[me],
                send_sem=send_sems.at[a, k], recv_sem=recv_sems.at[a, k], device_id=peer, device_id_type=MESH)

        def arrival(a, k, peer):
            pid = 4 * peer[0] + 2 * peer[1] + peer[2]
            return pltpu.make_async_remote_copy(
                src_ref=ins[a].at[pid], dst_ref=outs[a].at[pid],
                send_sem=send_sems.at[a, k], recv_sem=recv_sems.at[a, k], device_id=peer, device_id_type=MESH)

        mine = []
        for a in range(n):
            cp = pltpu.make_async_copy(ins[a].at[me], outs[a].at[me], local_sems.at[a])
            cp.start()
            mine.append(cp)
        started = []
        for a in range(n):
            for k, peer in enumerate(peers):
                cp = copy(a, k, peer)
                cp.start()
                started.append(cp)
        for a in range(n):
            for k, peer in enumerate(peers):
                arrival(a, k, peer).wait_recv()
        for cp in started:
            cp.wait_send()
        for cp in mine:
            cp.wait()

    return pl.pallas_call(
        body, name=name,
        out_shape=[SDS(s.shape, s.dtype) for s in slabs],
        in_specs=[ANY] * n, out_specs=[ANY] * n,
        scratch_shapes=[pltpu.SemaphoreType.DMA((n, 7)), pltpu.SemaphoreType.DMA((n, 7)), pltpu.SemaphoreType.DMA((n,))],
    )(*slabs)


class _Ride:
    def __init__(self, kind, arrays):
        self.kind, self.arrays, self.n = kind, list(arrays), len(arrays)
        self.in_specs = [ANY] * self.n
        self.out_specs = [ANY] * self.n
        self.out_shape = [SDS(((N_DEV,) + a.shape) if kind == "gather" else a.shape, a.dtype) for a in self.arrays]
        self.scratch = [pltpu.SemaphoreType.DMA((self.n, 7)), pltpu.SemaphoreType.DMA((self.n, 7)),
                        pltpu.SemaphoreType.DMA((self.n,))]

    def _copies(self, ins, outs, sems, arriving):
        send_sems, recv_sems, local_sems = sems
        x, y, c = _position()
        me = 4 * x + 2 * y + c
        peers = [(x ^ bx, y ^ by, c ^ bc) for bx in (0, 1) for by in (0, 1) for bc in (0, 1)][1:]
        local, remote = [], []
        for a in range(self.n):
            own = ins[a] if self.kind == "gather" else ins[a].at[me]
            local.append(pltpu.make_async_copy(own, outs[a].at[me], local_sems.at[a]))
            for k, peer in enumerate(peers):
                pid = 4 * peer[0] + 2 * peer[1] + peer[2]
                remote.append(pltpu.make_async_remote_copy(
                    src_ref=ins[a] if self.kind == "gather" else ins[a].at[pid],
                    dst_ref=outs[a].at[pid if arriving else me],
                    send_sem=send_sems.at[a, k], recv_sem=recv_sems.at[a, k], device_id=peer, device_id_type=MESH))
        return local, remote

    def start(self, ins, outs, sems):
        local, sends = self._copies(ins, outs, sems, False)
        for cp in local + sends:
            cp.start()

    def wait(self, ins, outs, sems):
        local, arrivals = self._copies(ins, outs, sems, True)
        for cp in arrivals:
            cp.wait_recv()
        for cp in arrivals:
            cp.wait_send()
        for cp in local:
            cp.wait()


def _norm_matmul(x, g, w, name, tm=2048, tn=1024, ride=None, n_cols=None):
    t, d = x.shape
    n = n_cols or w.shape[1]
    tm, tn = min(tm, t), min(tn, n)
    ni, nj = t // tm, n // tn
    nr = ride.n if ride else 0

    def body(*refs):
        x_ref, g_ref, w_ref = refs[:3]
        hn_ref, o_ref = refs[3 + nr:5 + nr]
        rin, rout, sems = refs[3:3 + nr], refs[5 + nr:5 + 2 * nr], refs[5 + 2 * nr:]
        i, j = pl.program_id(0), pl.program_id(1)
        if ride:
            @pl.when((i == 0) & (j == 0))
            def _():
                ride.start(rin, rout, sems)

        @pl.when(j == 0)
        def _():
            xf = x_ref[...]
            r = lax.rsqrt(jnp.mean(xf * xf, axis=-1, keepdims=True) + NORM_EPS)
            hn_ref[...] = ((xf * r) * g_ref[...]).astype(BF16)

        o_ref[...] = jnp.dot(hn_ref[...], w_ref[...], preferred_element_type=F32).astype(o_ref.dtype)
        if ride:
            @pl.when((i == ni - 1) & (j == nj - 1))
            def _():
                ride.wait(rin, rout, sems)

    return pl.pallas_call(
        body, name=name, grid=(ni, nj),
        in_specs=[pl.BlockSpec((tm, d), lambda i, j: (i, 0)), pl.BlockSpec((1, d), lambda i, j: (0, 0)),
                  pl.BlockSpec((d, tn), lambda i, j: (0, j))] + (ride.in_specs if ride else []),
        out_specs=[pl.BlockSpec((tm, d), lambda i, j: (i, 0)), pl.BlockSpec((tm, tn), lambda i, j: (i, j))]
        + (ride.out_specs if ride else []),
        out_shape=[SDS((t, d), BF16), SDS((t, n), BF16)] + (ride.out_shape if ride else []),
        scratch_shapes=ride.scratch if ride else [],
        compiler_params=_cparams("arbitrary", "arbitrary") if ride else _cparams("parallel", "arbitrary"),
    )(x, g, w, *(ride.arrays if ride else []))


def _matmul_f32out(a, w, name, tm=512):
    t, k = a.shape
    n = w.shape[1]

    def body(a_ref, w_ref, o_ref):
        o_ref[...] = jnp.dot(a_ref[...], w_ref[...], preferred_element_type=F32)

    return pl.pallas_call(
        body, name=name, grid=(t // tm,),
        in_specs=[pl.BlockSpec((tm, k), lambda i: (i, 0)), pl.BlockSpec((k, n), lambda i: (0, 0))],
        out_specs=pl.BlockSpec((tm, n), lambda i: (i, 0)),
        out_shape=SDS((t, n), F32),
        compiler_params=_cparams("parallel"),
    )(a, w)


def _out_matmul_residual(y, w, x, name, tm=512):
    t, e = y.shape
    d = w.shape[1]

    def body(y_ref, w_ref, x_ref, o_ref):
        o_ref[...] = x_ref[...] + jnp.dot(y_ref[...], w_ref[...], preferred_element_type=F32)

    return pl.pallas_call(
        body, name=name, grid=(t // tm,),
        in_specs=[pl.BlockSpec((tm, e), lambda i: (i, 0)), pl.BlockSpec((e, d), lambda i: (0, 0)),
                  pl.BlockSpec((tm, d), lambda i: (i, 0))],
        out_specs=pl.BlockSpec((tm, d), lambda i: (i, 0)),
        out_shape=SDS((t, d), F32),
        compiler_params=_cparams("parallel"),
    )(y, w, x)


def _matmul_nt(a, w, name, tm=512):
    t, d = a.shape
    e = w.shape[0]

    def body(a_ref, w_ref, o_ref):
        o_ref[...] = lax.dot_general(a_ref[...].astype(BF16), w_ref[...], NT_DIMS,
                                     preferred_element_type=F32).astype(o_ref.dtype)

    return pl.pallas_call(
        body, name=name, grid=(t // tm,),
        in_specs=[pl.BlockSpec((tm, d), lambda i: (i, 0)), pl.BlockSpec((e, d), lambda i: (0, 0))],
        out_specs=pl.BlockSpec((tm, e), lambda i: (i, 0)),
        out_shape=SDS((t, e), BF16),
        compiler_params=_cparams("parallel"),
    )(a, w)


def _matmul_tn(a, b, name, out_dtype=BF16, tm=1024, tn=1024, tk=512):
    t, m = a.shape
    n = b.shape[1]
    tm, tn = min(tm, m), _divisor_tile(n, 2 * tn)
    nk = t // tk

    def body(a_ref, b_ref, o_ref, acc_ref):
        k = pl.program_id(2)

        @pl.when(k == 0)
        def _():
            acc_ref[...] = jnp.zeros_like(acc_ref)

        acc_ref[...] += lax.dot_general(a_ref[...].astype(BF16), b_ref[...].astype(BF16), TN_DIMS,
                                        preferred_element_type=F32)

        @pl.when(k == nk - 1)
        def _():
            o_ref[...] = acc_ref[...].astype(o_ref.dtype)

    return pl.pallas_call(
        body, name=name, grid=(m // tm, n // tn, nk),
        in_specs=[pl.BlockSpec((tk, tm), lambda i, j, k: (k, i)), pl.BlockSpec((tk, tn), lambda i, j, k: (k, j))],
        out_specs=pl.BlockSpec((tm, tn), lambda i, j, k: (i, j)),
        out_shape=SDS((m, n), out_dtype),
        scratch_shapes=[pltpu.VMEM((tm, tn), F32)],
        compiler_params=_cparams("parallel", "parallel", "arbitrary"),
    )(a, b)


def _dproj_matmul_normbwd(dproj, w, x, g, dxn, name, tm=1024, tk=1024, ride=None):
    t, n = dproj.shape
    d = w.shape[0]
    tm, tk = min(tm, t), _divisor_tile(n, tk)
    nk = n // tk
    ni = t // tm
    nr = ride.n if ride else 0

    def body(*refs):
        dp_ref, w_ref, x_ref, g_ref, dxn_ref = refs[:5]
        dx_ref, dg_ref = refs[5 + nr:7 + nr]
        rin, rout = refs[5:5 + nr], refs[7 + nr:7 + 2 * nr]
        sems, acc_ref = refs[7 + 2 * nr:-1], refs[-1]
        i, k = pl.program_id(0), pl.program_id(1)
        if ride:
            @pl.when((i == 0) & (k == 0))
            def _():
                ride.start(rin, rout, sems)

        @pl.when(k == 0)
        def _():
            acc_ref[...] = jnp.zeros_like(acc_ref)

        acc_ref[...] += lax.dot_general(dp_ref[...], w_ref[...], NT_DIMS, preferred_element_type=F32)

        @pl.when(k == nk - 1)
        def _():
            dhn = acc_ref[...]
            xf = x_ref[...]
            r = lax.rsqrt(jnp.mean(xf * xf, axis=-1, keepdims=True) + NORM_EPS)
            xh = xf * r
            dy = dhn * g_ref[...]
            dx_ref[...] = dxn_ref[...] + r * (dy - xh * jnp.mean(dy * xh, axis=-1, keepdims=True))
            part = jnp.sum(dhn * xh, axis=0, keepdims=True)

            @pl.when(i == 0)
            def _():
                dg_ref[...] = part

            @pl.when(i > 0)
            def _():
                dg_ref[...] += part

        if ride:
            @pl.when((i == ni - 1) & (k == nk - 1))
            def _():
                ride.wait(rin, rout, sems)

    return pl.pallas_call(
        body, name=name, grid=(ni, nk),
        in_specs=[pl.BlockSpec((tm, tk), lambda i, k: (i, k)), pl.BlockSpec((d, tk), lambda i, k: (0, k)),
                  pl.BlockSpec((tm, d), lambda i, k: (i, 0)), pl.BlockSpec((1, d), lambda i, k: (0, 0)),
                  pl.BlockSpec((tm, d), lambda i, k: (i, 0))] + (ride.in_specs if ride else []),
        out_specs=[pl.BlockSpec((tm, d), lambda i, k: (i, 0)), pl.BlockSpec((1, d), lambda i, k: (0, 0))]
        + (ride.out_specs if ride else []),
        out_shape=[SDS((t, d), F32), SDS((1, d), F32)] + (ride.out_shape if ride else []),
        scratch_shapes=(ride.scratch if ride else []) + [pltpu.VMEM((tm, d), F32)],
        compiler_params=_cparams("arbitrary", "arbitrary"),
    )(dproj, w, x, g, dxn, *(ride.arrays if ride else []))


def _out_matmul_loss(y, w, x, target, name, tm=512):
    t, e = y.shape
    d = w.shape[1]
    inv_d = 1.0 / d

    def body(y_ref, w_ref, x_ref, t_ref, part_ref, dy_ref):
        i = pl.program_id(0)
        err = (x_ref[...] + jnp.dot(y_ref[...], w_ref[...], preferred_element_type=F32)) - t_ref[...]
        dy_ref[...] = err * inv_d
        part = jnp.sum(err * err, axis=0, keepdims=True) * (0.5 * inv_d)

        @pl.when(i == 0)
        def _():
            part_ref[...] = part

        @pl.when(i > 0)
        def _():
            part_ref[...] += part

    return pl.pallas_call(
        body, name=name, grid=(t // tm,),
        in_specs=[pl.BlockSpec((tm, e), lambda i: (i, 0)), pl.BlockSpec((e, d), lambda i: (0, 0)),
                  pl.BlockSpec((tm, d), lambda i: (i, 0)), pl.BlockSpec((tm, d), lambda i: (i, 0))],
        out_specs=[pl.BlockSpec((1, d), lambda i: (0, 0)), pl.BlockSpec((tm, d), lambda i: (i, 0))],
        out_shape=[SDS((1, d), F32), SDS((t, d), F32)],
        compiler_params=_cparams("arbitrary"),
    )(y, w, x, target)


CONV_ROWS = 32
CONV_COLS = 512


def _conv_chunk(src_ref, base, w_ref, width, r0, c0, flip):
    acc = None
    for k in range(width):
        off = base + r0 + ((width - 1 - k) if flip else (k - (width - 1)))
        term = src_ref[pl.ds(off, CONV_ROWS), pl.ds(c0, CONV_COLS)] * w_ref[pl.ds(k, 1), pl.ds(c0, CONV_COLS)]
        acc = term if acc is None else acc + term
    return acc


def _conv_weight_grad(dw_ref, d_ref, src_ref, base, width, tt, e):
    for c0 in range(0, e, CONV_COLS):
        for k in range(width):
            acc = None
            for r0 in range(0, tt, CONV_ROWS):
                prod = (d_ref[pl.ds(r0, CONV_ROWS), pl.ds(c0, CONV_COLS)]
                        * src_ref[pl.ds(base + r0 - (width - 1) + k, CONV_ROWS), pl.ds(c0, CONV_COLS)])
                part = _rows8(prod)
                acc = part if acc is None else acc + part
            dw_ref[pl.ds(k, 1), pl.ds(c0, CONV_COLS)] += jnp.sum(acc, axis=0, keepdims=True)


def _shifted_copies(dst_ref, src_ref, c0, length, sign):
    lo, hi = (SUBLANES, length) if sign < 0 else (0, length - SUBLANES)
    for b in range(SUBLANES):
        for r0 in range(lo, hi, CONV_ROWS):
            n = min(CONV_ROWS, hi - r0)
            dst_ref[b, pl.ds(r0, n), :] = src_ref[pl.ds(r0 + sign * b, n), pl.ds(c0, CONV_COLS)]


def _conv_aligned(copies_ref, base, w_ref, width, r0, c0, sign):
    acc = None
    for d in range(width):
        a, b = divmod(d, SUBLANES)
        term = (copies_ref[b, pl.ds(base + r0 + sign * SUBLANES * a, CONV_ROWS), :]
                * w_ref[pl.ds(width - 1 - d, 1), pl.ds(c0, CONV_COLS)])
        acc = term if acc is None else acc + term
    return acc


def _conv_weight_grad_aligned(dw_ref, dcopies_ref, src_ref, base, width, tt, c0):
    for d in range(width):
        a, b = divmod(d, SUBLANES)
        acc = None
        for r0 in range(0, tt, CONV_ROWS):
            prod = (dcopies_ref[b, pl.ds(r0, CONV_ROWS), :]
                    * src_ref[pl.ds(base + r0 - SUBLANES * a, CONV_ROWS), pl.ds(c0, CONV_COLS)])
            part = _rows8(prod)
            acc = part if acc is None else acc + part
        dw_ref[pl.ds(width - 1 - d, 1), pl.ds(c0, CONV_COLS)] += jnp.sum(acc, axis=0, keepdims=True)


LN_ROWS = 16


def _a_mid_fwd(proj, cw, cb, lg, lb, width, name, tt=256, ride=None):
    t, e3 = proj.shape
    e = e3 // 3
    halo = CONF_HALO
    nt = t // tt
    nr = ride.n if ride else 0

    def body(*refs):
        rin, rout = refs[5:5 + nr], refs[7 + nr:7 + 2 * nr]
        scratch = refs[7 + 2 * nr:]
        sems, own = (scratch[:3], scratch[3:]) if ride else ((), scratch)
        if ride:
            @pl.when(pl.program_id(0) == 0)
            def _():
                ride.start(rin, rout, sems)

        tile(*refs[:5], *refs[5 + nr:7 + nr], *own)
        if ride:
            @pl.when(pl.program_id(0) == nt - 1)
            def _():
                ride.wait(rin, rout, sems)

    def tile(p_ref, cw_ref, cb_ref, lg_ref, lb_ref, y_ref, u1_ref, ubuf, shifted):
        i = pl.program_id(0)

        @pl.when(i == 0)
        def _():
            ubuf[pl.ds(0, halo), :] = jnp.zeros((halo, e), F32)

        @pl.when(i > 0)
        def _():
            ubuf[pl.ds(0, halo), :] = ubuf[pl.ds(tt, halo), :]

        for r0 in range(0, tt, CONV_ROWS):
            val = p_ref[pl.ds(r0, CONV_ROWS), pl.ds(0, e)].astype(F32)
            gate = p_ref[pl.ds(r0, CONV_ROWS), pl.ds(e, e)].astype(F32)
            ubuf[pl.ds(halo + r0, CONV_ROWS), :] = val * _sigmoid(gate)
        for c0 in range(0, e, CONV_COLS):
            _shifted_copies(shifted, ubuf, c0, halo + tt, -1)
            for r0 in range(0, tt, CONV_ROWS):
                acc = _conv_aligned(shifted, halo, cw_ref, width, r0, c0, -1)
                u1_ref[pl.ds(r0, CONV_ROWS), pl.ds(c0, CONV_COLS)] = acc + cb_ref[:, pl.ds(c0, CONV_COLS)]
        for r0 in range(0, tt, LN_ROWS):
            u = u1_ref[pl.ds(r0, LN_ROWS), :]
            mu = jnp.mean(u, axis=-1, keepdims=True)
            dlt = u - mu
            var = jnp.mean(dlt * dlt, axis=-1, keepdims=True)
            u2 = (dlt * lax.rsqrt(var + NORM_EPS)) * lg_ref[...] + lb_ref[...]
            z = p_ref[pl.ds(r0, LN_ROWS), pl.ds(2 * e, e)].astype(F32)
            y_ref[pl.ds(r0, LN_ROWS), :] = (_silu(u2) * _silu(z)).astype(BF16)

    return pl.pallas_call(
        body, name=name, grid=(nt,),
        in_specs=[pl.BlockSpec((tt, e3), lambda i: (i, 0)), pl.BlockSpec(cw.shape, lambda i: (0, 0)),
                  pl.BlockSpec((1, e), lambda i: (0, 0)), pl.BlockSpec((1, e), lambda i: (0, 0)),
                  pl.BlockSpec((1, e), lambda i: (0, 0))] + (ride.in_specs if ride else []),
        out_specs=[pl.BlockSpec((tt, e), lambda i: (i, 0)), pl.BlockSpec((tt, e), lambda i: (i, 0))]
        + (ride.out_specs if ride else []),
        out_shape=[SDS((t, e), BF16), SDS((t, e), F32)] + (ride.out_shape if ride else []),
        scratch_shapes=(ride.scratch if ride else [])
        + [pltpu.VMEM((halo + tt, e), F32), pltpu.VMEM((SUBLANES, halo + tt, CONV_COLS), F32)],
        compiler_params=_cparams("arbitrary"),
    )(proj, cw, cb, lg, lb, *(ride.arrays if ride else []))


def _a_mid_bwd(proj, u1, dyz, cw, lg, lb, width, name, tt=256, ride=None):
    t, e3 = proj.shape
    e = e3 // 3
    halo = CONF_HALO
    nt = t // tt
    hb = tt // halo
    nr = ride.n if ride else 0

    def body(*refs):
        rin, rout = refs[7:7 + nr], refs[12 + nr:12 + 2 * nr]
        scratch = refs[12 + 2 * nr:]
        sems, own = (scratch[:3], scratch[3:]) if ride else ((), scratch)
        if ride:
            @pl.when(pl.program_id(0) == 0)
            def _():
                ride.start(rin, rout, sems)

        tile(*refs[:7], *refs[7 + nr:12 + nr], *own)
        if ride:
            @pl.when(pl.program_id(0) == nt - 1)
            def _():
                ride.wait(rin, rout, sems)

    def tile(p_ref, pp_ref, u1_ref, dy_ref, cw_ref, lg_ref, lb_ref,
             dp_ref, dcw_ref, dcb_ref, dlg_ref, dlb_ref, ubuf, dbuf, shifted, acc_cb, acc_lg, acc_lb):
        i = pl.program_id(0)
        ti = nt - 1 - i

        @pl.when(i == 0)
        def _():
            dbuf[pl.ds(tt, halo), :] = jnp.zeros((halo, e), F32)
            dcw_ref[...] = jnp.zeros_like(dcw_ref)
            acc_cb[...] = jnp.zeros_like(acc_cb)
            acc_lg[...] = jnp.zeros_like(acc_lg)
            acc_lb[...] = jnp.zeros_like(acc_lb)

        @pl.when(i > 0)
        def _():
            dbuf[pl.ds(tt, halo), :] = dbuf[pl.ds(0, halo), :]

        keep = (ti > 0).astype(F32)
        ubuf[pl.ds(0, halo), :] = keep * (pp_ref[:, pl.ds(0, e)].astype(F32) * _sigmoid(pp_ref[:, pl.ds(e, e)].astype(F32)))
        for r0 in range(0, tt, CONV_ROWS):
            val = p_ref[pl.ds(r0, CONV_ROWS), pl.ds(0, e)].astype(F32)
            gate = p_ref[pl.ds(r0, CONV_ROWS), pl.ds(e, e)].astype(F32)
            ubuf[pl.ds(halo + r0, CONV_ROWS), :] = val * _sigmoid(gate)

        for r0 in range(0, tt, LN_ROWS):
            rows = pl.ds(r0, LN_ROWS)
            u = u1_ref[rows, :]
            mu = jnp.mean(u, axis=-1, keepdims=True)
            dlt = u - mu
            var = jnp.mean(dlt * dlt, axis=-1, keepdims=True)
            rstd = lax.rsqrt(var + NORM_EPS)
            xh = dlt * rstd
            u2 = xh * lg_ref[...] + lb_ref[...]
            s2 = _sigmoid(u2)
            u3 = u2 * s2
            z = p_ref[rows, pl.ds(2 * e, e)].astype(F32)
            sz = _sigmoid(z)
            dy = dy_ref[rows, :].astype(F32)
            dp_ref[rows, pl.ds(2 * e, e)] = (dy * u3 * (sz * (1.0 + z * (1.0 - sz)))).astype(BF16)
            du2 = (dy * (z * sz)) * (s2 * (1.0 + u2 * (1.0 - s2)))
            acc_lg[...] += _rows8(du2 * xh)
            acc_lb[...] += _rows8(du2)
            dxh = du2 * lg_ref[...]
            m1 = jnp.mean(dxh, axis=-1, keepdims=True)
            m2 = jnp.mean(dxh * xh, axis=-1, keepdims=True)
            du1 = rstd * (dxh - m1 - xh * m2)
            dbuf[rows, :] = du1
            acc_cb[...] += _rows8(du1)

        for c0 in range(0, e, CONV_COLS):
            _shifted_copies(shifted, dbuf, c0, tt + halo, 1)
            for r0 in range(0, tt, CONV_ROWS):
                du0 = _conv_aligned(shifted, 0, cw_ref, width, r0, c0, 1)
                rows, cols = pl.ds(r0, CONV_ROWS), pl.ds(c0, CONV_COLS)
                val = p_ref[rows, cols].astype(F32)
                sg = _sigmoid(p_ref[rows, pl.ds(e + c0, CONV_COLS)].astype(F32))
                dp_ref[rows, cols] = (du0 * sg).astype(BF16)
                dp_ref[rows, pl.ds(e + c0, CONV_COLS)] = (du0 * val * sg * (1.0 - sg)).astype(BF16)
            _conv_weight_grad_aligned(dcw_ref, shifted, ubuf, halo, width, tt, c0)

        @pl.when(i == nt - 1)
        def _():
            dcb_ref[...] = jnp.sum(acc_cb[...], axis=0, keepdims=True)
            dlg_ref[...] = jnp.sum(acc_lg[...], axis=0, keepdims=True)
            dlb_ref[...] = jnp.sum(acc_lb[...], axis=0, keepdims=True)

    vec = pl.BlockSpec((1, e), lambda i: (0, 0))
    return pl.pallas_call(
        body, name=name, grid=(nt,),
        in_specs=[pl.BlockSpec((tt, e3), lambda i: (nt - 1 - i, 0)),
                  pl.BlockSpec((halo, e3), lambda i: (jnp.maximum((nt - 1 - i) * hb - 1, 0), 0)),
                  pl.BlockSpec((tt, e), lambda i: (nt - 1 - i, 0)), pl.BlockSpec((tt, e), lambda i: (nt - 1 - i, 0)),
                  pl.BlockSpec(cw.shape, lambda i: (0, 0)), vec, vec] + (ride.in_specs if ride else []),
        out_specs=[pl.BlockSpec((tt, e3), lambda i: (nt - 1 - i, 0)), pl.BlockSpec(cw.shape, lambda i: (0, 0)), vec, vec, vec]
        + (ride.out_specs if ride else []),
        out_shape=[SDS((t, e3), BF16), SDS(cw.shape, F32), SDS((1, e), F32), SDS((1, e), F32), SDS((1, e), F32)]
        + (ride.out_shape if ride else []),
        scratch_shapes=(ride.scratch if ride else [])
        + [pltpu.VMEM((halo + tt, e), F32), pltpu.VMEM((tt + halo, e), F32),
           pltpu.VMEM((SUBLANES, halo + tt, CONV_COLS), F32),
           pltpu.VMEM((SUBLANES, e), F32), pltpu.VMEM((SUBLANES, e), F32), pltpu.VMEM((SUBLANES, e), F32)],
        compiler_params=_cparams("arbitrary"),
    )(proj, proj, u1, dyz, cw, lg, lb, *(ride.arrays if ride else []))


def _c_mid_fwd(proj, cw, width, name, tt=256):
    t, e4 = proj.shape
    e = e4 // 4
    halo = SHORT_HALO

    def body(p_ref, cw_ref, y_ref, wbuf):
        i = pl.program_id(0)

        @pl.when(i == 0)
        def _():
            wbuf[pl.ds(0, halo), :] = jnp.zeros((halo, e), F32)

        @pl.when(i > 0)
        def _():
            wbuf[pl.ds(0, halo), :] = wbuf[pl.ds(tt, halo), :]

        for r0 in range(0, tt, CONV_ROWS):
            rows = pl.ds(r0, CONV_ROWS)
            wbuf[pl.ds(halo + r0, CONV_ROWS), :] = p_ref[rows, pl.ds(2 * e, e)].astype(F32) * p_ref[rows, pl.ds(0, e)].astype(F32)
        for c0 in range(0, e, CONV_COLS):
            for r0 in range(0, tt, CONV_ROWS):
                rows = pl.ds(r0, CONV_ROWS)
                cv = _conv_chunk(wbuf, halo, cw_ref, width, r0, c0, False)
                bg = p_ref[rows, pl.ds(e + c0, CONV_COLS)].astype(F32)
                z = p_ref[rows, pl.ds(3 * e + c0, CONV_COLS)].astype(F32)
                y_ref[rows, pl.ds(c0, CONV_COLS)] = ((bg * cv) * _silu(z)).astype(BF16)

    return pl.pallas_call(
        body, name=name, grid=(t // tt,),
        in_specs=[pl.BlockSpec((tt, e4), lambda i: (i, 0)), pl.BlockSpec(cw.shape, lambda i: (0, 0))],
        out_specs=pl.BlockSpec((tt, e), lambda i: (i, 0)),
        out_shape=SDS((t, e), BF16),
        scratch_shapes=[pltpu.VMEM((halo + tt, e), F32)],
        compiler_params=_cparams("arbitrary"),
    )(proj, cw)


def _c_mid_bwd(proj, dyz, cw, width, name, tt=256):
    t, e4 = proj.shape
    e = e4 // 4
    halo = SHORT_HALO
    nt = t // tt
    hb = tt // halo

    def body(p_ref, pp_ref, dy_ref, cw_ref, dp_ref, dcw_ref, wbuf, dbuf):
        i = pl.program_id(0)
        ti = nt - 1 - i

        @pl.when(i == 0)
        def _():
            dbuf[pl.ds(tt, halo), :] = jnp.zeros((halo, e), F32)
            dcw_ref[...] = jnp.zeros_like(dcw_ref)

        @pl.when(i > 0)
        def _():
            dbuf[pl.ds(tt, halo), :] = dbuf[pl.ds(0, halo), :]

        keep = (ti > 0).astype(F32)
        wbuf[pl.ds(0, halo), :] = keep * (pp_ref[:, pl.ds(2 * e, e)].astype(F32) * pp_ref[:, pl.ds(0, e)].astype(F32))
        for r0 in range(0, tt, CONV_ROWS):
            rows = pl.ds(r0, CONV_ROWS)
            wbuf[pl.ds(halo + r0, CONV_ROWS), :] = p_ref[rows, pl.ds(2 * e, e)].astype(F32) * p_ref[rows, pl.ds(0, e)].astype(F32)
        for c0 in range(0, e, CONV_COLS):
            for r0 in range(0, tt, CONV_ROWS):
                rows, cols = pl.ds(r0, CONV_ROWS), pl.ds(c0, CONV_COLS)
                cv = _conv_chunk(wbuf, halo, cw_ref, width, r0, c0, False)
                bg = p_ref[rows, pl.ds(e + c0, CONV_COLS)].astype(F32)
                z = p_ref[rows, pl.ds(3 * e + c0, CONV_COLS)].astype(F32)
                sz = _sigmoid(z)
                dyz_c = dy_ref[rows, cols].astype(F32)
                dy = dyz_c * (z * sz)
                dp_ref[rows, pl.ds(3 * e + c0, CONV_COLS)] = (dyz_c * (bg * cv) * (sz * (1.0 + z * (1.0 - sz)))).astype(BF16)
                dp_ref[rows, pl.ds(e + c0, CONV_COLS)] = (dy * cv).astype(BF16)
                dbuf[rows, cols] = dy * bg
        for c0 in range(0, e, CONV_COLS):
            for r0 in range(0, tt, CONV_ROWS):
                rows, cols = pl.ds(r0, CONV_ROWS), pl.ds(c0, CONV_COLS)
                dw = _conv_chunk(dbuf, 0, cw_ref, width, r0, c0, True)
                dp_ref[rows, pl.ds(2 * e + c0, CONV_COLS)] = (dw * p_ref[rows, cols].astype(F32)).astype(BF16)
                dp_ref[rows, cols] = (dw * p_ref[rows, pl.ds(2 * e + c0, CONV_COLS)].astype(F32)).astype(BF16)
        _conv_weight_grad(dcw_ref, dbuf, wbuf, halo, width, tt, e)

    return pl.pallas_call(
        body, name=name, grid=(nt,),
        in_specs=[pl.BlockSpec((tt, e4), lambda i: (nt - 1 - i, 0)),
                  pl.BlockSpec((halo, e4), lambda i: (jnp.maximum((nt - 1 - i) * hb - 1, 0), 0)),
                  pl.BlockSpec((tt, e), lambda i: (nt - 1 - i, 0)), pl.BlockSpec(cw.shape, lambda i: (0, 0))],
        out_specs=[pl.BlockSpec((tt, e4), lambda i: (nt - 1 - i, 0)), pl.BlockSpec(cw.shape, lambda i: (0, 0))],
        out_shape=[SDS((t, e4), BF16), SDS(cw.shape, F32)],
        scratch_shapes=[pltpu.VMEM((halo + tt, e), F32), pltpu.VMEM((tt + halo, e), F32)],
        compiler_params=_cparams("arbitrary"),
    )(proj, proj, dyz, cw)


def _b_prep_fwd(proj, flog, fbias, qg, kg, heads, name, tt=256):
    t, e4 = proj.shape
    e = e4 // 4
    scale = HEAD_DIM ** -0.5 * LOG2E

    def body(q_ref, k_ref, fl_ref, fb_ref, qg_ref, kg_ref, qs_ref, kn_ref, c_ref, ct_ref, carry):
        i = pl.program_id(0)

        @pl.when(i == 0)
        def _():
            carry[...] = jnp.zeros_like(carry)

        for h in range(heads):
            cols = pl.ds(h * HEAD_DIM, HEAD_DIM)
            qh = q_ref[:, cols].astype(F32)
            r = lax.rsqrt(jnp.mean(qh * qh, axis=-1, keepdims=True) + NORM_EPS)
            qs_ref[:, cols] = (((qh * r) * qg_ref[:, cols]) * scale).astype(BF16)
            kh = k_ref[:, cols].astype(F32)
            r = lax.rsqrt(jnp.mean(kh * kh, axis=-1, keepdims=True) + NORM_EPS)
            kn_ref[:, cols] = ((kh * r) * kg_ref[:, cols]).astype(BF16)

        a = fl_ref[...] + fb_ref[...]
        lf = jnp.minimum(a, 0.0) - jnp.log(1.0 + jnp.exp(-jnp.abs(a)))
        tri = (lax.broadcasted_iota(jnp.int32, (tt, tt), 0) >= lax.broadcasted_iota(jnp.int32, (tt, tt), 1)).astype(BF16)
        c = _tri_matmul(tri, lf) + carry[...]
        c_ref[...] = c
        ct_ref[...] = (c * LOG2E).T
        carry[...] = c_ref[pl.ds(tt - 1, 1), :]

    return pl.pallas_call(
        body, name=name, grid=(t // tt,),
        in_specs=[pl.BlockSpec((tt, e), lambda i: (i, 0)), pl.BlockSpec((tt, e), lambda i: (i, 1)),
                  pl.BlockSpec((tt, LANES), lambda i: (i, 0)), pl.BlockSpec((1, LANES), lambda i: (0, 0)),
                  pl.BlockSpec((1, e), lambda i: (0, 0)), pl.BlockSpec((1, e), lambda i: (0, 0))],
        out_specs=[pl.BlockSpec((tt, e), lambda i: (i, 0)), pl.BlockSpec((tt, e), lambda i: (i, 0)),
                   pl.BlockSpec((tt, LANES), lambda i: (i, 0)), pl.BlockSpec((LANES, tt), lambda i: (0, i))],
        out_shape=[SDS((t, e), BF16), SDS((t, e), BF16), SDS((t, LANES), F32), SDS((LANES, t), F32)],
        scratch_shapes=[pltpu.VMEM((1, LANES), F32)],
        compiler_params=_cparams("arbitrary"),
    )(proj, proj, flog, fbias, qg, kg)


ATT_BLOCK = 1024
ATT_CHUNK = 512
NEG_BIG = -1e30
LOG2E = 1.4426950408889634
LN2 = 0.6931471805599453


def _flash_fwd(qs, kn, proj, ck, heads, name):
    t, e = qs.shape
    blk = min(ATT_BLOCK, t)
    cw = min(ATT_CHUNK, blk // 2)
    nq, nch = t // blk, blk // cw
    assert nch % 2 == 0

    def body(q_ref, k_ref, v_ref, ck_ref, z_ref, o_ref, y_ref, m_ref, l_ref, s_a, s_b):
        i = pl.program_id(1)
        q = q_ref[...]
        bufs = (s_a, s_b)

        def key_rows(j, c):
            return pl.ds(pl.multiple_of(j * blk, blk) + c * cw, cw)

        def logits(j, c):
            bufs[c % 2][...] = (lax.dot_general(q, k_ref[key_rows(j, c), :], NT_DIMS, preferred_element_type=F32)
                            - ck_ref[j][:, c * cw:(c + 1) * cw])

        def weights(c, m, masked):
            s = bufs[c % 2][...]
            if masked:
                keep = lax.broadcasted_iota(jnp.int32, (blk, cw), 0) >= (lax.broadcasted_iota(jnp.int32, (blk, cw), 1) + c * cw)
                s = jnp.where(keep, s, NEG_BIG)
            m_new = jnp.maximum(m, jnp.ceil(jnp.max(s, axis=-1, keepdims=True)))
            return m_new, jnp.exp2(m - m_new), jnp.exp2(s - m_new).astype(BF16)

        ones = jnp.ones((cw, HEAD_DIM), BF16)

        def block(j, carry, masked):
            m, acc = carry
            for c in range(nch):
                if c + 1 < nch:
                    logits(j, c + 1)
                elif not masked:
                    logits(j + 1, 0)
                m, alpha, p = weights(c, m, masked)
                v1 = jnp.concatenate([v_ref[key_rows(j, c), :], ones], axis=1)
                acc = alpha * acc + jnp.dot(p, v1, preferred_element_type=F32)
            return m, acc

        logits(0, 0)
        carry = (jnp.full((blk, 1), NEG_BIG, F32), jnp.zeros((blk, 2 * HEAD_DIM), F32))
        carry = lax.fori_loop(0, i, lambda j, cr: block(j, cr, False), carry)
        m, acc = block(i, carry, True)
        l = acc[:, HEAD_DIM:HEAD_DIM + 1]
        o = acc[:, :HEAD_DIM] / l
        o_ref[...] = o
        y_ref[...] = (o * _silu(z_ref[...].astype(F32))).astype(BF16)
        m_ref[...] = jnp.broadcast_to(m, (blk, LANES))
        l_ref[...] = jnp.broadcast_to(l, (blk, LANES))

    head_all = pl.BlockSpec((t, HEAD_DIM), lambda h, i: (0, h))
    tile = pl.BlockSpec((blk, HEAD_DIM), lambda h, i: (i, h))
    stat = pl.BlockSpec((None, blk, LANES), lambda h, i: (h, i, 0))
    return pl.pallas_call(
        body, name=name, grid=(heads, nq),
        in_specs=[tile, head_all, pl.BlockSpec((t, HEAD_DIM), lambda h, i: (0, 2 * heads + h)),
                  pl.BlockSpec((None, nq, 1, blk), lambda h, i: (h, 0, 0, 0)),
                  pl.BlockSpec((blk, HEAD_DIM), lambda h, i: (i, 3 * heads + h))],
        out_specs=[tile, tile, stat, stat],
        out_shape=[SDS((t, e), F32), SDS((t, e), BF16), SDS((heads, t, LANES), F32), SDS((heads, t, LANES), F32)],
        scratch_shapes=[pltpu.VMEM((blk, cw), F32), pltpu.VMEM((blk, cw), F32)],
        compiler_params=_cparams("parallel", "arbitrary"),
    )(qs, kn, proj, ck, proj)


def _flash_bwd(qs, kn, proj, ck, dyz, o, mstat, lstat, heads, name):
    t, e = qs.shape
    blk = min(ATT_BLOCK, t)
    cw = min(ATT_CHUNK, blk // 2)
    nq, nch = t // blk, blk // cw
    assert nch % 2 == 0

    def body(q_ref, dy_ref, o_ref, z_ref, m_ref, l_ref, k_ref, v_ref, ck_ref,
             dq_ref, dk_ref, dv_ref, dc_ref, dz_ref, s_a, s_b, d_a, d_b):
        i = pl.program_id(1)

        @pl.when(i == 0)
        def _():
            dk_ref[...] = jnp.zeros_like(dk_ref)
            dv_ref[...] = jnp.zeros_like(dv_ref)
            dc_ref[...] = jnp.zeros_like(dc_ref)

        z = z_ref[...].astype(F32)
        sz = _sigmoid(z)
        dy = dy_ref[...].astype(F32)
        of = o_ref[...]
        do = ((dy * (z * sz)) / l_ref[:, 0:1]).astype(BF16)
        dz_ref[...] = (dy * of * (sz * (1.0 + z * (1.0 - sz)))).astype(BF16)
        dl = jnp.sum(do.astype(F32) * of, axis=-1, keepdims=True)
        q = q_ref[...]
        mrow = m_ref[:, 0:1]
        sbuf, dbuf = (s_a, s_b), (d_a, d_b)

        def key_rows(j, c):
            return pl.ds(pl.multiple_of(j * blk, blk) + c * cw, cw)

        def products(j, c):
            rows = key_rows(j, c)
            sbuf[c % 2][...] = (lax.dot_general(q, k_ref[rows, :], NT_DIMS, preferred_element_type=F32)
                            - ck_ref[j][:, c * cw:(c + 1) * cw])
            dbuf[c % 2][...] = lax.dot_general(do, v_ref[rows, :], NT_DIMS, preferred_element_type=F32)

        def weights(c, masked):
            p = jnp.exp2(sbuf[c % 2][...] - mrow)
            if masked:
                keep = lax.broadcasted_iota(jnp.int32, (blk, cw), 0) >= (lax.broadcasted_iota(jnp.int32, (blk, cw), 1) + c * cw)
                p = jnp.where(keep, p, 0.0)
            p = p.astype(BF16)
            ds = p.astype(F32) * (dbuf[c % 2][...] - dl)
            return p, ds.astype(BF16), jnp.sum(ds, axis=0, keepdims=True)

        def outputs(j, c, p, dsb, colsum, dq):
            rows = key_rows(j, c)
            dv_ref[rows, :] += lax.dot_general(p, do, TN_DIMS, preferred_element_type=F32)
            dk_ref[rows, :] += lax.dot_general(dsb, q, TN_DIMS, preferred_element_type=F32)
            dc_ref[j, :, pl.ds(c * cw, cw)] -= colsum
            return dq + jnp.dot(dsb, k_ref[rows, :], preferred_element_type=F32)

        def block(j, dq, masked):
            products(j, 1)
            for c in range(nch):
                p, dsb, colsum = weights(c, masked)
                if c + 2 < nch:
                    products(j, c + 2)
                elif c + 2 == nch and not masked:
                    products(j + 1, 0)
                dq = outputs(j, c, p, dsb, colsum, dq)
            return dq

        products(0, 0)
        dq = lax.fori_loop(0, i, lambda j, acc: block(j, acc, False), jnp.zeros((blk, HEAD_DIM), F32))
        dq_ref[...] = block(i, dq, True)

    tile = pl.BlockSpec((blk, HEAD_DIM), lambda h, i: (i, h))
    stat = pl.BlockSpec((None, blk, LANES), lambda h, i: (h, i, 0))
    head_all = pl.BlockSpec((t, HEAD_DIM), lambda h, i: (0, h))
    cspec = pl.BlockSpec((None, nq, 1, blk), lambda h, i: (h, 0, 0, 0))
    return pl.pallas_call(
        body, name=name, grid=(heads, nq),
        in_specs=[tile, tile, tile, pl.BlockSpec((blk, HEAD_DIM), lambda h, i: (i, 3 * heads + h)), stat, stat, head_all,
                  pl.BlockSpec((t, HEAD_DIM), lambda h, i: (0, 2 * heads + h)), cspec],
        out_specs=[tile, head_all, head_all, cspec, tile],
        out_shape=[SDS((t, e), F32), SDS((t, e), F32), SDS((t, e), F32), SDS((heads, nq, 1, blk), F32), SDS((t, e), BF16)],
        scratch_shapes=[pltpu.VMEM((blk, cw), F32)] * 4,
        compiler_params=_cparams("parallel", "arbitrary"),
    )(qs, dyz, o, proj, mstat, lstat, kn, proj, ck)


def _b_prep_bwd(dqs, dkn, dv, dz, proj, qg, kg, dct, flog, fbias, heads, name, tt=256):
    t, e4 = proj.shape
    e = e4 // 4
    nt = t // tt
    scale = HEAD_DIM ** -0.5

    def body(dq_ref, dk_ref, dv_ref, dz_ref, q_ref, k_ref, qg_ref, kg_ref, dc_ref, fl_ref, fb_ref,
             dp_ref, dqg_ref, dkg_ref, dfb_ref, carry, dlf, acc_q, acc_k, acc_f):
        i = pl.program_id(0)

        @pl.when(i == 0)
        def _():
            carry[...] = jnp.zeros_like(carry)
            acc_q[...] = jnp.zeros_like(acc_q)
            acc_k[...] = jnp.zeros_like(acc_k)
            acc_f[...] = jnp.zeros_like(acc_f)

        for h in range(heads):
            cols = pl.ds(h * HEAD_DIM, HEAD_DIM)
            for src_ref, d_ref, g_ref, acc, mult, off in ((q_ref, dq_ref, qg_ref, acc_q, scale, 0),
                                                          (k_ref, dk_ref, kg_ref, acc_k, LN2, e)):
                xf = src_ref[:, cols].astype(F32)
                r = lax.rsqrt(jnp.mean(xf * xf, axis=-1, keepdims=True) + NORM_EPS)
                xh = xf * r
                dn = d_ref[:, cols] * mult
                acc[...] += _rows8(dn * xh)
                dxh = dn * g_ref[:, cols]
                dp_ref[:, pl.ds(off + h * HEAD_DIM, HEAD_DIM)] = (
                    r * (dxh - xh * jnp.mean(dxh * xh, axis=-1, keepdims=True))).astype(BF16)
        dp_ref[:, pl.ds(2 * e, e)] = dv_ref[...].astype(BF16)
        dp_ref[:, pl.ds(3 * e, e)] = dz_ref[...]

        tri = (lax.broadcasted_iota(jnp.int32, (tt, tt), 0) <= lax.broadcasted_iota(jnp.int32, (tt, tt), 1)).astype(BF16)
        dlf[...] = _tri_matmul(tri, dc_ref[...]) + carry[...]
        carry[...] = dlf[pl.ds(0, 1), :]
        a = fl_ref[...] + fb_ref[...]
        dfl = dlf[...] * _sigmoid(-a)
        dp_ref[:, pl.ds(4 * e, LANES)] = dfl.astype(BF16)
        acc_f[...] += _rows8(dfl)

        @pl.when(i == nt - 1)
        def _():
            dqg_ref[...] = jnp.sum(acc_q[...], axis=0, keepdims=True)
            dkg_ref[...] = jnp.sum(acc_k[...], axis=0, keepdims=True)
            dfb_ref[...] = jnp.sum(acc_f[...], axis=0, keepdims=True)

    rev = lambda i: (nt - 1 - i, 0)
    vec_e = pl.BlockSpec((1, e), lambda i: (0, 0))
    vec = pl.BlockSpec((1, LANES), lambda i: (0, 0))
    wide = pl.BlockSpec((tt, e), rev)
    lane = pl.BlockSpec((tt, LANES), rev)
    return pl.pallas_call(
        body, name=name, grid=(nt,),
        in_specs=[wide, wide, wide, wide, wide, pl.BlockSpec((tt, e), lambda i: (nt - 1 - i, 1)), vec_e, vec_e, lane, lane, vec],
        out_specs=[pl.BlockSpec((tt, e4 + LANES), rev), vec, vec, vec],
        out_shape=[SDS((t, e4 + LANES), BF16), SDS((1, LANES), F32), SDS((1, LANES), F32), SDS((1, LANES), F32)],
        scratch_shapes=[pltpu.VMEM((1, LANES), F32), pltpu.VMEM((tt, LANES), F32), pltpu.VMEM((SUBLANES, LANES), F32),
                        pltpu.VMEM((SUBLANES, LANES), F32), pltpu.VMEM((SUBLANES, LANES), F32)],
        compiler_params=_cparams("arbitrary"),
    )(dqs, dkn, dv, dz, proj, proj, qg, kg, dct, flog, fbias)


def _b_fwd(h, b_norm, wb_pad, wb_out, qg, kg, fbias, heads, tag, ride=None):
    t = h.shape[0]
    e = wb_out.shape[0]
    blk = min(ATT_BLOCK, t)
    hn, proj, *rode = _norm_matmul(h, b_norm, wb_pad, f"b_in_proj_{tag}", ride=ride, n_cols=4 * e)
    flog = _matmul_f32out(hn, wb_pad[:, 4 * e:], f"b_forget_proj_{tag}")
    qs, kn, _, ct = _b_prep_fwd(proj, flog, fbias, qg, kg, heads, f"b_prep_fwd_{tag}")
    ck = ct.reshape(LANES, t // blk, 1, blk)
    o, y, mstat, lstat = _flash_fwd(qs, kn, proj, ck, heads, f"b_attention_fwd_{tag}")
    sv = dict(x=h, hn=hn, proj=proj, flog=flog, qs=qs, kn=kn, ck=ck, o=o, y=y, mstat=mstat, lstat=lstat)
    return y, sv, rode


def _b_bwd(dh, sv, b_norm, wb_pad, wb_out, qg, kg, fbias, heads, tag, ride=None):
    t = dh.shape[0]
    e = wb_out.shape[0]
    gb = {}
    dyz = _matmul_nt(dh, wb_out, f"b_out_bwd_{tag}")
    gb["w_out"] = _matmul_tn(sv["y"], dh, f"b_out_wgrad_{tag}", tm=2048)
    dqs, dkn, dv, dc, dz = _flash_bwd(sv["qs"], sv["kn"], sv["proj"], sv["ck"], dyz, sv["o"], sv["mstat"], sv["lstat"],
                                      heads, f"b_attention_bwd_{tag}")
    dct = jnp.pad(dc.reshape(heads, t).T, ((0, 0), (0, LANES - heads)))
    dproj, dqg, dkg, dfb = _b_prep_bwd(dqs, dkn, dv, dz, sv["proj"], qg, kg, dct, sv["flog"], fbias, heads,
                                       f"b_prep_bwd_{tag}")
    gb["w_in"] = _matmul_tn(sv["hn"], dproj, f"b_in_wgrad_{tag}")[:, :4 * e + heads]
    dh, dg, *rode = _dproj_matmul_normbwd(dproj, wb_pad, sv["x"], b_norm, dh, f"b_in_bwd_{tag}", ride=ride)
    gb["norm"], gb["q_norm"], gb["k_norm"], gb["f_bias"] = dg, dqg, dkg, dfb[:, :heads]
    return dh, gb, rode


def _sum_adamw(recv, w, m, v, name, tr=256):
    nl, r, c = w.shape
    tr = min(tr, r)

    def body(g_ref, w_ref, m_ref, v_ref, go_ref, d_ref, mo_ref, vo_ref):
        g = g_ref[0].astype(F32)
        for s in range(1, N_DEV):
            g = g + g_ref[s].astype(F32)
        go_ref[...] = g
        mn = ADAM_B1 * m_ref[...] + (1.0 - ADAM_B1) * g
        vn = ADAM_B2 * v_ref[...] + (1.0 - ADAM_B2) * (g * g)
        m_hat = mn / (1.0 - ADAM_B1 ** ADAM_STEP)
        v_hat = vn / (1.0 - ADAM_B2 ** ADAM_STEP)
        d_ref[...] = -ADAM_LR * (m_hat / (jnp.sqrt(v_hat) + ADAM_EPS) + ADAM_WD * w_ref[...])
        mo_ref[...] = mn
        vo_ref[...] = vn

    blk = pl.BlockSpec((None, tr, c), lambda l, i: (l, i, 0))
    return pl.pallas_call(
        body, name=name, grid=(nl, r // tr),
        in_specs=[pl.BlockSpec((N_DEV, None, tr, c), lambda l, i: (0, l, i, 0)), blk, blk, blk],
        out_specs=[blk, blk, blk, blk],
        out_shape=[SDS(w.shape, F32)] * 4,
        compiler_params=_cparams("parallel", "parallel"),
    )(recv, w, m, v)


def _unshard(g, axis):
    g = jnp.moveaxis(g, 0, axis)
    return g.reshape(g.shape[:axis] + (g.shape[axis] * g.shape[axis + 1],) + g.shape[axis + 2:])


def _to_slabs(full, axis):
    n = full.shape[axis]
    s = full.reshape(full.shape[:axis] + (N_DEV, n // N_DEV) + full.shape[axis + 1:])
    return jnp.moveaxis(s, axis, 0)


def _pack_rows(parts, lead):
    flat = [p.reshape(p.shape[:lead] + (-1,)) for p in parts]
    cat = jnp.concatenate(flat, axis=-1)
    n = cat.shape[-1]
    pad = (-n) % (SUBLANES * LANES)
    cat = jnp.pad(cat, [(0, 0)] * lead + [(0, pad)])
    return cat.reshape(cat.shape[:lead] + ((n + pad) // LANES, LANES))


def _unpack_rows(packed, shapes, lead):
    flat = packed.reshape(packed.shape[:lead] + (-1,))
    out, off = [], 0
    for shp in shapes:
        size = int(np.prod(shp))
        out.append(flat[..., off:off + size].reshape(packed.shape[:lead] + tuple(shp)))
        off += size
    return out


def _pad_rows(w, rows):
    return jnp.pad(w, ((0, rows - w.shape[0]), (0, 0)))


def kernel(x, a_norm, a_w_in, a_conv_w, a_conv_b, a_ln_g, a_ln_b, a_w_out, b_norm, b_w_in, b_f_bias, b_q_norm, b_k_norm, b_w_out, c_norm, c_w_in, c_conv_w, c_w_out, loss_target, m_a_norm, m_a_w_in, m_a_conv_w, m_a_conv_b, m_a_ln_g, m_a_ln_b, m_a_w_out, m_b_norm, m_b_w_in, m_b_f_bias, m_b_q_norm, m_b_k_norm, m_b_w_out, m_c_norm, m_c_w_in, m_c_conv_w, m_c_w_out, v_a_norm, v_a_w_in, v_a_conv_w, v_a_conv_b, v_a_ln_g, v_a_ln_b, v_a_w_out, v_b_norm, v_b_w_in, v_b_f_bias, v_b_q_norm, v_b_k_norm, v_b_w_out, v_c_norm, v_c_w_in, v_c_conv_w, v_c_w_out):
    t, d = x.shape[1], x.shape[2]
    e = a_w_out.shape[1] * N_DEV
    heads = b_f_bias.shape[1]
    n_a, n_b, n_c = a_norm.shape[0], b_norm.shape[0], c_norm.shape[0]
    depth = n_a + n_b + n_c
    ka, kc = a_conv_w.shape[1], c_conv_w.shape[1]
    assert e == heads * HEAD_DIM and n_b == 1 and n_c == 1 and x.shape[0] == 1

    layers = [(i % 3, i // 3) for i in range(depth)]

    def mat_shards(kind, j):
        w_in, w_out = ((a_w_in, a_w_out), (b_w_in, b_w_out), (c_w_in, c_w_out))[kind]
        return [w_in[j].astype(BF16), w_out[j].astype(BF16)]

    def full_mats(kind, gathered):
        w_in, w_out = _unshard(gathered[0], 1), _unshard(gathered[1], 0)
        return (jnp.pad(w_in, ((0, 0), (0, LANES - heads))) if kind == 1 else w_in), w_out

    small_names = ["a_norm", "a_conv_w", "a_conv_b", "a_ln_g", "a_ln_b", "c_norm", "c_conv_w"]
    small = dict(a_norm=a_norm, a_conv_w=a_conv_w, a_conv_b=a_conv_b, a_ln_g=a_ln_g, a_ln_b=a_ln_b,
                 c_norm=c_norm, c_conv_w=c_conv_w)
    small_pack = _pack_rows([small[k] for k in small_names], 0)
    first = _all_gather(mat_shards(*layers[0]) + [small_pack], "all_gather_first_layer")
    sm = _unpack_rows(first[2], [small[k].shape for k in small_names], 1)
    g_a_norm = _unshard(sm[0], 1)
    g_a_conv_w = _unshard(sm[1], 2)
    g_a_conv_b = _unshard(sm[2], 1)
    g_a_ln_g = _unshard(sm[3], 1)
    g_a_ln_b = _unshard(sm[4], 1)
    g_c_norm = _unshard(sm[5], 1)
    g_c_conv_w = _unshard(sm[6], 2)

    cw_a = [_pad_rows(g_a_conv_w[j], CONF_HALO) for j in range(n_a)]
    cw_c = _pad_rows(g_c_conv_w[0], SHORT_HALO)
    qg = jnp.tile(b_q_norm, (1, heads))
    kg = jnp.tile(b_k_norm, (1, heads))
    fbias = jnp.pad(b_f_bias, ((0, 0), (0, LANES - heads)))

    h = x[0]
    saved, weights = [], [full_mats(layers[0][0], first[:2])]
    for i, (kind, j) in enumerate(layers):
        tag = f"l{i}"
        w_in, w_out = weights[i]
        ride = _Ride("gather", mat_shards(*layers[i + 1])) if i + 1 < depth else None
        if kind == 0:
            hn, proj = _norm_matmul(h, g_a_norm[j:j + 1], w_in, f"a_in_proj_{tag}")
            y, u1, *rode = _a_mid_fwd(proj, cw_a[j], g_a_conv_b[j:j + 1], g_a_ln_g[j:j + 1], g_a_ln_b[j:j + 1], ka,
                                      f"a_mid_fwd_{tag}", ride=ride)
            saved.append(dict(x=h, hn=hn, proj=proj, u1=u1, y=y))
        elif kind == 1:
            y, sv, rode = _b_fwd(h, b_norm, w_in, w_out, qg, kg, fbias, heads, tag, ride=ride)
            saved.append(sv)
        else:
            hn, proj, *rode = _norm_matmul(h, g_c_norm, w_in, f"c_in_proj_{tag}", ride=ride)
            y = _c_mid_fwd(proj, cw_c, kc, f"c_mid_fwd_{tag}")
            saved.append(dict(x=h, hn=hn, proj=proj, y=y))
        if ride:
            weights.append(full_mats(layers[i + 1][0], rode))
            h = _out_matmul_residual(y, w_out, h, f"{'abc'[kind]}_out_proj_{tag}")
        else:
            loss_part, dh = _out_matmul_loss(y, w_out, h, loss_target[0], f"{'abc'[kind]}_out_proj_loss_{tag}")
    loss_local = jnp.sum(loss_part).reshape(1, 1)

    ga = dict(norm=[None] * n_a, conv_w=[None] * n_a, conv_b=[None] * n_a, ln_g=[None] * n_a, ln_b=[None] * n_a)
    gb, gc = None, {}
    recv_mats = [None] * depth
    pending = None
    for i in reversed(range(depth)):
        kind, j = layers[i]
        tag = f"l{i}"
        sv = saved[i]
        w_in, w_out = weights[i]
        ride = _Ride("exchange", pending) if pending is not None else None
        if kind == 0:
            dyz = _matmul_nt(dh, w_out, f"a_out_bwd_{tag}")
            gw_out = _matmul_tn(sv["y"], dh, f"a_out_wgrad_{tag}", tm=2048)
            mid_ride, ride = (ride, None) if i == 0 else (None, ride)
            dproj, dcw, dcb, dlg, dlb, *mid_rode = _a_mid_bwd(sv["proj"], sv["u1"], dyz, cw_a[j], g_a_ln_g[j:j + 1],
                                                              g_a_ln_b[j:j + 1], ka, f"a_mid_bwd_{tag}", ride=mid_ride)
            gw_in = _matmul_tn(sv["hn"], dproj, f"a_in_wgrad_{tag}")
            if i == 0:
                ride = _Ride("exchange", [_to_slabs(gw_in, 1), _to_slabs(gw_out, 0)])
            dh, dg, *rode = _dproj_matmul_normbwd(dproj, w_in, sv["x"], g_a_norm[j:j + 1], dh, f"a_in_bwd_{tag}", ride=ride)
            if i == 0:
                if mid_ride:
                    recv_mats[1] = mid_rode
                recv_mats[0], ride = rode, None
            ga["norm"][j], ga["conv_w"][j], ga["conv_b"][j], ga["ln_g"][j], ga["ln_b"][j] = dg[0], dcw[:ka], dcb[0], dlg[0], dlb[0]
        elif kind == 1:
            dh, gb, rode = _b_bwd(dh, sv, b_norm, w_in, w_out, qg, kg, fbias, heads, tag, ride=ride)
            gw_in, gw_out = gb["w_in"], gb["w_out"]
        else:
            dyz = _matmul_nt(dh, w_out, f"c_out_bwd_{tag}")
            gw_out = _matmul_tn(sv["y"], dh, f"c_out_wgrad_{tag}", tm=2048)
            dproj, dcw = _c_mid_bwd(sv["proj"], dyz, cw_c, kc, f"c_mid_bwd_{tag}")
            gw_in = _matmul_tn(sv["hn"], dproj, f"c_in_wgrad_{tag}")
            dh, dg, *rode = _dproj_matmul_normbwd(dproj, w_in, sv["x"], g_c_norm, dh, f"c_in_bwd_{tag}", ride=ride)
            gc["norm"], gc["conv_w"] = dg, dcw[:kc][None]
        if ride:
            recv_mats[i + 1] = rode
        pending = [_to_slabs(gw_in, 1), _to_slabs(gw_out, 0)]
    grad_x = dh[None]

    sharded_small = [(jnp.stack(ga["norm"]), 1), (jnp.stack(ga["conv_w"]), 2), (jnp.stack(ga["conv_b"]), 1),
                     (jnp.stack(ga["ln_g"]), 1), (jnp.stack(ga["ln_b"]), 1), (gc["norm"], 1), (gc["conv_w"], 2)]
    repl_small = [gb["norm"], gb["f_bias"], gb["q_norm"], gb["k_norm"], loss_local]
    small_slabs = _pack_rows([_to_slabs(g, ax) for g, ax in sharded_small]
                             + [jnp.broadcast_to(g[None], (N_DEV,) + g.shape) for g in repl_small], 1)
    recv_small, = _exchange([small_slabs], "exchange_vector_gradients")

    outs = {}
    mat_w = dict(a_w_in=(a_w_in, m_a_w_in, v_a_w_in), a_w_out=(a_w_out, m_a_w_out, v_a_w_out),
                 b_w_in=(b_w_in, m_b_w_in, v_b_w_in), b_w_out=(b_w_out, m_b_w_out, v_b_w_out),
                 c_w_in=(c_w_in, m_c_w_in, v_c_w_in), c_w_out=(c_w_out, m_c_w_out, v_c_w_out))
    for kind, prefix in enumerate("abc"):
        members = [i for i, (k, _) in enumerate(layers) if k == kind]
        for which, name in enumerate((f"{prefix}_w_in", f"{prefix}_w_out")):
            recv = jnp.stack([recv_mats[i][which] for i in members], axis=1)
            outs[name] = _sum_adamw(recv, *mat_w[name], f"adamw_{name}")

    small_order = small_names + ["b_norm", "b_f_bias", "b_q_norm", "b_k_norm", "loss"]
    no_state = jnp.zeros((1, 1), F32)
    small_w = dict(a_norm=(a_norm, m_a_norm, v_a_norm), a_conv_w=(a_conv_w, m_a_conv_w, v_a_conv_w),
                   a_conv_b=(a_conv_b, m_a_conv_b, v_a_conv_b), a_ln_g=(a_ln_g, m_a_ln_g, v_a_ln_g),
                   a_ln_b=(a_ln_b, m_a_ln_b, v_a_ln_b), c_norm=(c_norm, m_c_norm, v_c_norm),
                   c_conv_w=(c_conv_w, m_c_conv_w, v_c_conv_w), b_norm=(b_norm, m_b_norm, v_b_norm),
                   b_f_bias=(b_f_bias, m_b_f_bias, v_b_f_bias), b_q_norm=(b_q_norm, m_b_q_norm, v_b_q_norm),
                   b_k_norm=(b_k_norm, m_b_k_norm, v_b_k_norm), loss=(no_state, no_state, no_state))
    packs = [_pack_rows([small_w[k][q] for k in small_order], 0)[None] for q in range(3)]
    small_out = _sum_adamw(recv_small[:, None], *packs, "adamw_vectors")
    shapes = [small_w[k][0].shape for k in small_order]
    unpacked = [_unpack_rows(o[0], shapes, 0) for o in small_out]
    for idx, name in enumerate(small_order):
        outs[name] = tuple(unpacked[q][idx] for q in range(4))

    loss = outs["loss"][0].reshape(())
    order = ["a_norm", "a_w_in", "a_conv_w", "a_conv_b", "a_ln_g", "a_ln_b", "a_w_out", "b_norm", "b_w_in", "b_f_bias",
             "b_q_norm", "b_k_norm", "b_w_out", "c_norm", "c_w_in", "c_conv_w", "c_w_out"]
    return (loss, grad_x, *[outs[k][0] for k in order], *[outs[k][1] for k in order],
            *[outs[k][2] for k in order], *[outs[k][3] for k in order])
```

```python
import jax
import jax.numpy as jnp
import numpy as np
from jax import lax
from jax.experimental import pallas as pl
from jax.experimental.pallas import tpu as pltpu

F32 = jnp.float32
BF16 = jnp.bfloat16
SDS = jax.ShapeDtypeStruct

NORM_EPS = 1e-6
ADAM_LR = 0.001
ADAM_B1 = 0.9
ADAM_B2 = 0.999
ADAM_EPS = 1e-08
ADAM_WD = 0.01
ADAM_STEP = 10

N_DEV = 8
LANES = 128
SUBLANES = 8
HEAD_DIM = 128
CONF_HALO = 32
SHORT_HALO = 8
VMEM_LIMIT = 56 * 1024 * 1024

NT_DIMS = (((1,), (1,)), ((), ()))
TN_DIMS = (((0,), (0,)), ((), ()))
MESH = pl.DeviceIdType.MESH
ANY = pl.BlockSpec(memory_space=pl.ANY)


def _cparams(*sem):
    return pltpu.CompilerParams(dimension_semantics=sem, vmem_limit_bytes=VMEM_LIMIT)


def _divisor_tile(n, cap):
    return max(m for m in range(LANES, min(n, cap) + 1, LANES) if n % m == 0)


def _sigmoid(x):
    return 1.0 / (1.0 + jnp.exp(-x))


def _silu(x):
    return x * _sigmoid(x)


def _dsilu(x):
    s = _sigmoid(x)
    return s * (1.0 + x * (1.0 - s))


def _rows8(v):
    out = v[0:SUBLANES]
    for a in range(1, v.shape[0] // SUBLANES):
        out = out + v[a * SUBLANES:(a + 1) * SUBLANES]
    return out


def _split3(v):
    hi = v.astype(BF16)
    r1 = v - hi.astype(F32)
    mid = r1.astype(BF16)
    lo = (r1 - mid.astype(F32)).astype(BF16)
    return hi, mid, lo


def _tri_matmul(tri, v):
    hi, mid, lo = _split3(v)
    return (jnp.dot(tri, hi, preferred_element_type=F32) + jnp.dot(tri, mid, preferred_element_type=F32)
            + jnp.dot(tri, lo, preferred_element_type=F32))


def _position():
    return lax.axis_index("x"), lax.axis_index("y"), lax.axis_index("c")


def _all_gather(shards, name):
    n = len(shards)

    def body(*refs):
        xs, outs = refs[:n], refs[n:2 * n]
        send_sems, recv_sems, local_sems = refs[2 * n:]
        x, y, c = _position()
        me, sibling = (x, y, c), (x, y, 1 - c)
        chips = [(1 - x, y), (x, 1 - y), (1 - x, 1 - y)]

        def slot(a, px, py, pc):
            return outs[a].at[4 * px + 2 * py + pc]

        def copy(a, k, block, to, src=None):
            return pltpu.make_async_remote_copy(
                src_ref=slot(a, *block) if src is None else src, dst_ref=slot(a, *block),
                send_sem=send_sems.at[a, k], recv_sem=recv_sems.at[a, k], device_id=to, device_id_type=MESH)

        started = []
        mine = []
        for a in range(n):
            cp = pltpu.make_async_copy(xs[a], slot(a, *me), local_sems.at[a])
            cp.start()
            mine.append(cp)
        for a in range(n):
            first = [copy(a, 0, me, sibling, src=xs[a])]
            first += [copy(a, 1 + j, me, (*chip, c), src=xs[a]) for j, chip in enumerate(chips)]
            for cp in first:
                cp.start()
            started += first
        for a in range(n):
            for j, chip in enumerate(chips):
                copy(a, 1 + j, (*chip, c), me).wait_recv()
                fwd = copy(a, 4 + j, (*chip, c), sibling)
                fwd.start()
                started.append(fwd)
        for a in range(n):
            copy(a, 0, sibling, me).wait_recv()
            for j, chip in enumerate(chips):
                copy(a, 4 + j, (*chip, 1 - c), me).wait_recv()
        for cp in started:
            cp.wait_send()
        for cp in mine:
            cp.wait()

    return pl.pallas_call(
        body, name=name,
        out_shape=[SDS((N_DEV,) + s.shape, s.dtype) for s in shards],
        in_specs=[ANY] * n, out_specs=[ANY] * n,
        scratch_shapes=[pltpu.SemaphoreType.DMA((n, 7)), pltpu.SemaphoreType.DMA((n, 7)), pltpu.SemaphoreType.DMA((n,))],
    )(*shards)


def _exchange(slabs, name):
    n = len(slabs)

    def body(*refs):
        ins, outs = refs[:n], refs[n:2 * n]
        send_sems, recv_sems, local_sems = refs[2 * n:]
        x, y, c = _position()
        me = 4 * x + 2 * y + c
        peers = [(x ^ bx, y ^ by, c ^ bc) for bx in (0, 1) for by in (0, 1) for bc in (0, 1)][1:]

        def copy(a, k, peer):
            pid = 4 * peer[0] + 2 * peer[1] + peer[2]
            return pltpu.make_async_remote_copy(
                src_ref=ins[a].at[pid], dst_ref=outs[a].at[me],
                send_sem=send_sems.at[a, k], recv_sem=recv_sems.at[a, k], device_id=peer, device_id_type=MESH)

        def arrival(a, k, peer):
            pid = 4 * peer[0] + 2 * peer[1] + peer[2]
            return pltpu.make_async_remote_copy(
                src_ref=ins[a].at[pid], dst_ref=outs[a].at[pid],
                send_sem=send_sems.at[a, k], recv_sem=recv_sems.at[a, k], device_id=peer, device_id_type=MESH)

        mine = []
        for a in range(n):
            cp = pltpu.make_async_copy(ins[a].at[me], outs[a].at[me], local_sems.at[a])
            cp.start()
            mine.append(cp)
        started = []
        for a in range(n):
            for k, peer in enumerate(peers):
                cp = copy(a, k, peer)
                cp.start()
                started.append(cp)
        for a in range(n):
            for k, peer in enumerate(peers):
                arrival(a, k, peer).wait_recv()
        for cp in started:
            cp.wait_send()
        for cp in mine:
            cp.wait()

    return pl.pallas_call(
        body, name=name,
        out_shape=[SDS(s.shape, s.dtype) for s in slabs],
        in_specs=[ANY] * n, out_specs=[ANY] * n,
        scratch_shapes=[pltpu.SemaphoreType.DMA((n, 7)), pltpu.SemaphoreType.DMA((n, 7)), pltpu.SemaphoreType.DMA((n,))],
    )(*slabs)


class _Ride:
    def __init__(self, kind, arrays):
        self.kind, self.arrays, self.n = kind, list(arrays), len(arrays)
        self.in_specs = [ANY] * self.n
        self.out_specs = [ANY] * self.n
        self.out_shape = [SDS(((N_DEV,) + a.shape) if kind == "gather" else a.shape, a.dtype) for a in self.arrays]
        self.scratch = [pltpu.SemaphoreType.DMA((self.n, 7)), pltpu.SemaphoreType.DMA((self.n, 7)),
                        pltpu.SemaphoreType.DMA((self.n,))]

    def _copies(self, ins, outs, sems, arriving):
        send_sems, recv_sems, local_sems = sems
        x, y, c = _position()
        me = 4 * x + 2 * y + c
        peers = [(x ^ bx, y ^ by, c ^ bc) for bx in (0, 1) for by in (0, 1) for bc in (0, 1)][1:]
        local, remote = [], []
        for a in range(self.n):
            own = ins[a] if self.kind == "gather" else ins[a].at[me]
            local.append(pltpu.make_async_copy(own, outs[a].at[me], local_sems.at[a]))
            for k, peer in enumerate(peers):
                pid = 4 * peer[0] + 2 * peer[1] + peer[2]
                remote.append(pltpu.make_async_remote_copy(
                    src_ref=ins[a] if self.kind == "gather" else ins[a].at[pid],
                    dst_ref=outs[a].at[pid if arriving else me],
                    send_sem=send_sems.at[a, k], recv_sem=recv_sems.at[a, k], device_id=peer, device_id_type=MESH))
        return local, remote

    def start(self, ins, outs, sems):
        local, sends = self._copies(ins, outs, sems, False)
        for cp in local + sends:
            cp.start()

    def wait(self, ins, outs, sems):
        local, arrivals = self._copies(ins, outs, sems, True)
        for cp in arrivals:
            cp.wait_recv()
        for cp in arrivals:
            cp.wait_send()
        for cp in local:
            cp.wait()


def _norm_matmul(x, g, w, name, tm=2048, tn=1024, ride=None, n_cols=None):
    t, d = x.shape
    n = n_cols or w.shape[1]
    tm, tn = min(tm, t), min(tn, n)
    ni, nj = t // tm, n // tn
    nr = ride.n if ride else 0

    def body(*refs):
        x_ref, g_ref, w_ref = refs[:3]
        hn_ref, o_ref = refs[3 + nr:5 + nr]
        rin, rout, sems = refs[3:3 + nr], refs[5 + nr:5 + 2 * nr], refs[5 + 2 * nr:]
        i, j = pl.program_id(0), pl.program_id(1)
        if ride:
            @pl.when((i == 0) & (j == 0))
            def _():
                ride.start(rin, rout, sems)

        @pl.when(j == 0)
        def _():
            xf = x_ref[...]
            r = lax.rsqrt(jnp.mean(xf * xf, axis=-1, keepdims=True) + NORM_EPS)
            hn_ref[...] = ((xf * r) * g_ref[...]).astype(BF16)

        o_ref[...] = jnp.dot(hn_ref[...], w_ref[...], preferred_element_type=F32).astype(o_ref.dtype)
        if ride:
            @pl.when((i == ni - 1) & (j == nj - 1))
            def _():
                ride.wait(rin, rout, sems)

    return pl.pallas_call(
        body, name=name, grid=(ni, nj),
        in_specs=[pl.BlockSpec((tm, d), lambda i, j: (i, 0)), pl.BlockSpec((1, d), lambda i, j: (0, 0)),
                  pl.BlockSpec((d, tn), lambda i, j: (0, j))] + (ride.in_specs if ride else []),
        out_specs=[pl.BlockSpec((tm, d), lambda i, j: (i, 0)), pl.BlockSpec((tm, tn), lambda i, j: (i, j))]
        + (ride.out_specs if ride else []),
        out_shape=[SDS((t, d), BF16), SDS((t, n), BF16)] + (ride.out_shape if ride else []),
        scratch_shapes=ride.scratch if ride else [],
        compiler_params=_cparams("arbitrary", "arbitrary") if ride else _cparams("parallel", "arbitrary"),
    )(x, g, w, *(ride.arrays if ride else []))


def _matmul_f32out(a, w, name, tm=512):
    t, k = a.shape
    n = w.shape[1]

    def body(a_ref, w_ref, o_ref):
        o_ref[...] = jnp.dot(a_ref[...], w_ref[...], preferred_element_type=F32)

    return pl.pallas_call(
        body, name=name, grid=(t // tm,),
        in_specs=[pl.BlockSpec((tm, k), lambda i: (i, 0)), pl.BlockSpec((k, n), lambda i: (0, 0))],
        out_specs=pl.BlockSpec((tm, n), lambda i: (i, 0)),
        out_shape=SDS((t, n), F32),
        compiler_params=_cparams("parallel"),
    )(a, w)


def _out_matmul_residual(y, w, x, name, tm=512):
    t, e = y.shape
    d = w.shape[1]

    def body(y_ref, w_ref, x_ref, o_ref):
        o_ref[...] = x_ref[...] + jnp.dot(y_ref[...], w_ref[...], preferred_element_type=F32)

    return pl.pallas_call(
        body, name=name, grid=(t // tm,),
        in_specs=[pl.BlockSpec((tm, e), lambda i: (i, 0)), pl.BlockSpec((e, d), lambda i: (0, 0)),
                  pl.BlockSpec((tm, d), lambda i: (i, 0))],
        out_specs=pl.BlockSpec((tm, d), lambda i: (i, 0)),
        out_shape=SDS((t, d), F32),
        compiler_params=_cparams("parallel"),
    )(y, w, x)


def _matmul_nt(a, w, name, tm=512):
    t, d = a.shape
    e = w.shape[0]

    def body(a_ref, w_ref, o_ref):
        o_ref[...] = lax.dot_general(a_ref[...].astype(BF16), w_ref[...], NT_DIMS,
                                     preferred_element_type=F32).astype(o_ref.dtype)

    return pl.pallas_call(
        body, name=name, grid=(t // tm,),
        in_specs=[pl.BlockSpec((tm, d), lambda i: (i, 0)), pl.BlockSpec((e, d), lambda i: (0, 0))],
        out_specs=pl.BlockSpec((tm, e), lambda i: (i, 0)),
        out_shape=SDS((t, e), BF16),
        compiler_params=_cparams("parallel"),
    )(a, w)


def _matmul_tn(a, b, name, out_dtype=BF16, tm=1024, tn=1024, tk=512):
    t, m = a.shape
    n = b.shape[1]
    tm, tn = min(tm, m), _divisor_tile(n, 2 * tn)
    nk = t // tk

    def body(a_ref, b_ref, o_ref, acc_ref):
        k = pl.program_id(2)

        @pl.when(k == 0)
        def _():
            acc_ref[...] = jnp.zeros_like(acc_ref)

        acc_ref[...] += lax.dot_general(a_ref[...].astype(BF16), b_ref[...].astype(BF16), TN_DIMS,
                                        preferred_element_type=F32)

        @pl.when(k == nk - 1)
        def _():
            o_ref[...] = acc_ref[...].astype(o_ref.dtype)

    return pl.pallas_call(
        body, name=name, grid=(m // tm, n // tn, nk),
        in_specs=[pl.BlockSpec((tk, tm), lambda i, j, k: (k, i)), pl.BlockSpec((tk, tn), lambda i, j, k: (k, j))],
        out_specs=pl.BlockSpec((tm, tn), lambda i, j, k: (i, j)),
        out_shape=SDS((m, n), out_dtype),
        scratch_shapes=[pltpu.VMEM((tm, tn), F32)],
        compiler_params=_cparams("parallel", "parallel", "arbitrary"),
    )(a, b)


def _dproj_matmul_normbwd(dproj, w, x, g, dxn, name, tm=1024, tk=1024, ride=None):
    t, n = dproj.shape
    d = w.shape[0]
    tm, tk = min(tm, t), _divisor_tile(n, tk)
    nk = n // tk
    ni = t // tm
    nr = ride.n if ride else 0

    def body(*refs):
        dp_ref, w_ref, x_ref, g_ref, dxn_ref = refs[:5]
        dx_ref, dg_ref = refs[5 + nr:7 + nr]
        rin, rout = refs[5:5 + nr], refs[7 + nr:7 + 2 * nr]
        sems, acc_ref = refs[7 + 2 * nr:-1], refs[-1]
        i, k = pl.program_id(0), pl.program_id(1)
        if ride:
            @pl.when((i == 0) & (k == 0))
            def _():
                ride.start(rin, rout, sems)

        @pl.when(k == 0)
        def _():
            acc_ref[...] = jnp.zeros_like(acc_ref)

        acc_ref[...] += lax.dot_general(dp_ref[...], w_ref[...], NT_DIMS, preferred_element_type=F32)

        @pl.when(k == nk - 1)
        def _():
            dhn = acc_ref[...]
            xf = x_ref[...]
            r = lax.rsqrt(jnp.mean(xf * xf, axis=-1, keepdims=True) + NORM_EPS)
            xh = xf * r
            dy = dhn * g_ref[...]
            dx_ref[...] = dxn_ref[...] + r * (dy - xh * jnp.mean(dy * xh, axis=-1, keepdims=True))
            part = jnp.sum(dhn * xh, axis=0, keepdims=True)

            @pl.when(i == 0)
            def _():
                dg_ref[...] = part

            @pl.when(i > 0)
            def _():
                dg_ref[...] += part

        if ride:
            @pl.when((i == ni - 1) & (k == nk - 1))
            def _():
                ride.wait(rin, rout, sems)

    return pl.pallas_call(
        body, name=name, grid=(ni, nk),
        in_specs=[pl.BlockSpec((tm, tk), lambda i, k: (i, k)), pl.BlockSpec((d, tk), lambda i, k: (0, k)),
                  pl.BlockSpec((tm, d), lambda i, k: (i, 0)), pl.BlockSpec((1, d), lambda i, k: (0, 0)),
                  pl.BlockSpec((tm, d), lambda i, k: (i, 0))] + (ride.in_specs if ride else []),
        out_specs=[pl.BlockSpec((tm, d), lambda i, k: (i, 0)), pl.BlockSpec((1, d), lambda i, k: (0, 0))]
        + (ride.out_specs if ride else []),
        out_shape=[SDS((t, d), F32), SDS((1, d), F32)] + (ride.out_shape if ride else []),
        scratch_shapes=(ride.scratch if ride else []) + [pltpu.VMEM((tm, d), F32)],
        compiler_params=_cparams("arbitrary", "arbitrary"),
    )(dproj, w, x, g, dxn, *(ride.arrays if ride else []))


def _out_matmul_loss(y, w, x, target, name, tm=512):
    t, e = y.shape
    d = w.shape[1]
    inv_d = 1.0 / d

    def body(y_ref, w_ref, x_ref, t_ref, part_ref, dy_ref):
        i = pl.program_id(0)
        err = (x_ref[...] + jnp.dot(y_ref[...], w_ref[...], preferred_element_type=F32)) - t_ref[...]
        dy_ref[...] = err * inv_d
        part = jnp.sum(err * err, axis=0, keepdims=True) * (0.5 * inv_d)

        @pl.when(i == 0)
        def _():
            part_ref[...] = part

        @pl.when(i > 0)
        def _():
            part_ref[...] += part

    return pl.pallas_call(
        body, name=name, grid=(t // tm,),
        in_specs=[pl.BlockSpec((tm, e), lambda i: (i, 0)), pl.BlockSpec((e, d), lambda i: (0, 0)),
                  pl.BlockSpec((tm, d), lambda i: (i, 0)), pl.BlockSpec((tm, d), lambda i: (i, 0))],
        out_specs=[pl.BlockSpec((1, d), lambda i: (0, 0)), pl.BlockSpec((tm, d), lambda i: (i, 0))],
        out_shape=[SDS((1, d), F32), SDS((t, d), F32)],
        compiler_params=_cparams("arbitrary"),
    )(y, w, x, target)


CONV_ROWS = 32
CONV_COLS = 512


def _conv_chunk(src_ref, base, w_ref, width, r0, c0, flip):
    acc = None
    for k in range(width):
        off = base + r0 + ((width - 1 - k) if flip else (k - (width - 1)))
        term = src_ref[pl.ds(off, CONV_ROWS), pl.ds(c0, CONV_COLS)] * w_ref[pl.ds(k, 1), pl.ds(c0, CONV_COLS)]
        acc = term if acc is None else acc + term
    return acc


def _conv_weight_grad(dw_ref, d_ref, src_ref, base, width, tt, e):
    for c0 in range(0, e, CONV_COLS):
        for k in range(width):
            acc = None
            for r0 in range(0, tt, CONV_ROWS):
                prod = (d_ref[pl.ds(r0, CONV_ROWS), pl.ds(c0, CONV_COLS)]
                        * src_ref[pl.ds(base + r0 - (width - 1) + k, CONV_ROWS), pl.ds(c0, CONV_COLS)])
                part = _rows8(prod)
                acc = part if acc is None else acc + part
            dw_ref[pl.ds(k, 1), pl.ds(c0, CONV_COLS)] += jnp.sum(acc, axis=0, keepdims=True)


def _shifted_copies(dst_ref, src_ref, c0, length, sign):
    lo, hi = (SUBLANES, length) if sign < 0 else (0, length - SUBLANES)
    for b in range(SUBLANES):
        for r0 in range(lo, hi, CONV_ROWS):
            n = min(CONV_ROWS, hi - r0)
            dst_ref[b, pl.ds(r0, n), :] = src_ref[pl.ds(r0 + sign * b, n), pl.ds(c0, CONV_COLS)]


def _conv_aligned(copies_ref, base, w_ref, width, r0, c0, sign):
    acc = None
    for d in range(width):
        a, b = divmod(d, SUBLANES)
        term = (copies_ref[b, pl.ds(base + r0 + sign * SUBLANES * a, CONV_ROWS), :]
                * w_ref[pl.ds(width - 1 - d, 1), pl.ds(c0, CONV_COLS)])
        acc = term if acc is None else acc + term
    return acc


def _conv_weight_grad_aligned(dw_ref, dcopies_ref, src_ref, base, width, tt, c0):
    for d in range(width):
        a, b = divmod(d, SUBLANES)
        acc = None
        for r0 in range(0, tt, CONV_ROWS):
            prod = (dcopies_ref[b, pl.ds(r0, CONV_ROWS), :]
                    * src_ref[pl.ds(base + r0 - SUBLANES * a, CONV_ROWS), pl.ds(c0, CONV_COLS)])
            part = _rows8(prod)
            acc = part if acc is None else acc + part
        dw_ref[pl.ds(width - 1 - d, 1), pl.ds(c0, CONV_COLS)] += jnp.sum(acc, axis=0, keepdims=True)


LN_ROWS = 16


def _a_mid_fwd(proj, cw, cb, lg, lb, width, name, tt=256, ride=None):
    t, e3 = proj.shape
    e = e3 // 3
    halo = CONF_HALO
    nt = t // tt
    nr = ride.n if ride else 0

    def body(*refs):
        rin, rout = refs[5:5 + nr], refs[7 + nr:7 + 2 * nr]
        scratch = refs[7 + 2 * nr:]
        sems, own = (scratch[:3], scratch[3:]) if ride else ((), scratch)
        if ride:
            @pl.when(pl.program_id(0) == 0)
            def _():
                ride.start(rin, rout, sems)

        tile(*refs[:5], *refs[5 + nr:7 + nr], *own)
        if ride:
            @pl.when(pl.program_id(0) == nt - 1)
            def _():
                ride.wait(rin, rout, sems)

    def tile(p_ref, cw_ref, cb_ref, lg_ref, lb_ref, y_ref, u1_ref, ubuf, shifted):
        i = pl.program_id(0)

        @pl.when(i == 0)
        def _():
            ubuf[pl.ds(0, halo), :] = jnp.zeros((halo, e), F32)

        @pl.when(i > 0)
        def _():
            ubuf[pl.ds(0, halo), :] = ubuf[pl.ds(tt, halo), :]

        for r0 in range(0, tt, CONV_ROWS):
            val = p_ref[pl.ds(r0, CONV_ROWS), pl.ds(0, e)].astype(F32)
            gate = p_ref[pl.ds(r0, CONV_ROWS), pl.ds(e, e)].astype(F32)
            ubuf[pl.ds(halo + r0, CONV_ROWS), :] = val * _sigmoid(gate)
        for c0 in range(0, e, CONV_COLS):
            _shifted_copies(shifted, ubuf, c0, halo + tt, -1)
            for r0 in range(0, tt, CONV_ROWS):
                acc = _conv_aligned(shifted, halo, cw_ref, width, r0, c0, -1)
                u1_ref[pl.ds(r0, CONV_ROWS), pl.ds(c0, CONV_COLS)] = acc + cb_ref[:, pl.ds(c0, CONV_COLS)]
        for r0 in range(0, tt, LN_ROWS):
            u = u1_ref[pl.ds(r0, LN_ROWS), :]
            mu = jnp.mean(u, axis=-1, keepdims=True)
            dlt = u - mu
            var = jnp.mean(dlt * dlt, axis=-1, keepdims=True)
            u2 = (dlt * lax.rsqrt(var + NORM_EPS)) * lg_ref[...] + lb_ref[...]
            z = p_ref[pl.ds(r0, LN_ROWS), pl.ds(2 * e, e)].astype(F32)
            y_ref[pl.ds(r0, LN_ROWS), :] = (_silu(u2) * _silu(z)).astype(BF16)

    return pl.pallas_call(
        body, name=name, grid=(nt,),
        in_specs=[pl.BlockSpec((tt, e3), lambda i: (i, 0)), pl.BlockSpec(cw.shape, lambda i: (0, 0)),
                  pl.BlockSpec((1, e), lambda i: (0, 0)), pl.BlockSpec((1, e), lambda i: (0, 0)),
                  pl.BlockSpec((1, e), lambda i: (0, 0))] + (ride.in_specs if ride else []),
        out_specs=[pl.BlockSpec((tt, e), lambda i: (i, 0)), pl.BlockSpec((tt, e), lambda i: (i, 0))]
        + (ride.out_specs if ride else []),
        out_shape=[SDS((t, e), BF16), SDS((t, e), F32)] + (ride.out_shape if ride else []),
        scratch_shapes=(ride.scratch if ride else [])
        + [pltpu.VMEM((halo + tt, e), F32), pltpu.VMEM((SUBLANES, halo + tt, CONV_COLS), F32)],
        compiler_params=_cparams("arbitrary"),
    )(proj, cw, cb, lg, lb, *(ride.arrays if ride else []))


def _a_mid_bwd(proj, u1, dyz, cw, lg, lb, width, name, tt=256, ride=None):
    t, e3 = proj.shape
    e = e3 // 3
    halo = CONF_HALO
    nt = t // tt
    hb = tt // halo
    nr = ride.n if ride else 0

    def body(*refs):
        rin, rout = refs[7:7 + nr], refs[12 + nr:12 + 2 * nr]
        scratch = refs[12 + 2 * nr:]
        sems, own = (scratch[:3], scratch[3:]) if ride else ((), scratch)
        if ride:
            @pl.when(pl.program_id(0) == 0)
            def _():
                ride.start(rin, rout, sems)

        tile(*refs[:7], *refs[7 + nr:12 + nr], *own)
        if ride:
            @pl.when(pl.program_id(0) == nt - 1)
            def _():
                ride.wait(rin, rout, sems)

    def tile(p_ref, pp_ref, u1_ref, dy_ref, cw_ref, lg_ref, lb_ref,
             dp_ref, dcw_ref, dcb_ref, dlg_ref, dlb_ref, ubuf, dbuf, shifted, acc_cb, acc_lg, acc_lb):
        i = pl.program_id(0)
        ti = nt - 1 - i

        @pl.when(i == 0)
        def _():
            dbuf[pl.ds(tt, halo), :] = jnp.zeros((halo, e), F32)
            dcw_ref[...] = jnp.zeros_like(dcw_ref)
            acc_cb[...] = jnp.zeros_like(acc_cb)
            acc_lg[...] = jnp.zeros_like(acc_lg)
            acc_lb[...] = jnp.zeros_like(acc_lb)

        @pl.when(i > 0)
        def _():
            dbuf[pl.ds(tt, halo), :] = dbuf[pl.ds(0, halo), :]

        keep = (ti > 0).astype(F32)
        ubuf[pl.ds(0, halo), :] = keep * (pp_ref[:, pl.ds(0, e)].astype(F32) * _sigmoid(pp_ref[:, pl.ds(e, e)].astype(F32)))
        for r0 in range(0, tt, CONV_ROWS):
            val = p_ref[pl.ds(r0, CONV_ROWS), pl.ds(0, e)].astype(F32)
            gate = p_ref[pl.ds(r0, CONV_ROWS), pl.ds(e, e)].astype(F32)
            ubuf[pl.ds(halo + r0, CONV_ROWS), :] = val * _sigmoid(gate)

        for r0 in range(0, tt, LN_ROWS):
            rows = pl.ds(r0, LN_ROWS)
            u = u1_ref[rows, :]
            mu = jnp.mean(u, axis=-1, keepdims=True)
            dlt = u - mu
            var = jnp.mean(dlt * dlt, axis=-1, keepdims=True)
            rstd = lax.rsqrt(var + NORM_EPS)
            xh = dlt * rstd
            u2 = xh * lg_ref[...] + lb_ref[...]
            s2 = _sigmoid(u2)
            u3 = u2 * s2
            z = p_ref[rows, pl.ds(2 * e, e)].astype(F32)
            sz = _sigmoid(z)
            dy = dy_ref[rows, :].astype(F32)
            dp_ref[rows, pl.ds(2 * e, e)] = (dy * u3 * (sz * (1.0 + z * (1.0 - sz)))).astype(BF16)
            du2 = (dy * (z * sz)) * (s2 * (1.0 + u2 * (1.0 - s2)))
            acc_lg[...] += _rows8(du2 * xh)
            acc_lb[...] += _rows8(du2)
            dxh = du2 * lg_ref[...]
            m1 = jnp.mean(dxh, axis=-1, keepdims=True)
            m2 = jnp.mean(dxh * xh, axis=-1, keepdims=True)
            du1 = rstd * (dxh - m1 - xh * m2)
            dbuf[rows, :] = du1
            acc_cb[...] += _rows8(du1)

        for c0 in range(0, e, CONV_COLS):
            _shifted_copies(shifted, dbuf, c0, tt + halo, 1)
            for r0 in range(0, tt, CONV_ROWS):
                du0 = _conv_aligned(shifted, 0, cw_ref, width, r0, c0, 1)
                rows, cols = pl.ds(r0, CONV_ROWS), pl.ds(c0, CONV_COLS)
                val = p_ref[rows, cols].astype(F32)
                sg = _sigmoid(p_ref[rows, pl.ds(e + c0, CONV_COLS)].astype(F32))
                dp_ref[rows, cols] = (du0 * sg).astype(BF16)
                dp_ref[rows, pl.ds(e + c0, CONV_COLS)] = (du0 * val * sg * (1.0 - sg)).astype(BF16)
            _conv_weight_grad_aligned(dcw_ref, shifted, ubuf, halo, width, tt, c0)

        @pl.when(i == nt - 1)
        def _():
            dcb_ref[...] = jnp.sum(acc_cb[...], axis=0, keepdims=True)
            dlg_ref[...] = jnp.sum(acc_lg[...], axis=0, keepdims=True)
            dlb_ref[...] = jnp.sum(acc_lb[...], axis=0, keepdims=True)

    vec = pl.BlockSpec((1, e), lambda i: (0, 0))
    return pl.pallas_call(
        body, name=name, grid=(nt,),
        in_specs=[pl.BlockSpec((tt, e3), lambda i: (nt - 1 - i, 0)),
                  pl.BlockSpec((halo, e3), lambda i: (jnp.maximum((nt - 1 - i) * hb - 1, 0), 0)),
                  pl.BlockSpec((tt, e), lambda i: (nt - 1 - i, 0)), pl.BlockSpec((tt, e), lambda i: (nt - 1 - i, 0)),
                  pl.BlockSpec(cw.shape, lambda i: (0, 0)), vec, vec] + (ride.in_specs if ride else []),
        out_specs=[pl.BlockSpec((tt, e3), lambda i: (nt - 1 - i, 0)), pl.BlockSpec(cw.shape, lambda i: (0, 0)), vec, vec, vec]
        + (ride.out_specs if ride else []),
        out_shape=[SDS((t, e3), BF16), SDS(cw.shape, F32), SDS((1, e), F32), SDS((1, e), F32), SDS((1, e), F32)]
        + (ride.out_shape if ride else []),
        scratch_shapes=(ride.scratch if ride else [])
        + [pltpu.VMEM((halo + tt, e), F32), pltpu.VMEM((tt + halo, e), F32),
           pltpu.VMEM((SUBLANES, halo + tt, CONV_COLS), F32),
           pltpu.VMEM((SUBLANES, e), F32), pltpu.VMEM((SUBLANES, e), F32), pltpu.VMEM((SUBLANES, e), F32)],
        compiler_params=_cparams("arbitrary"),
    )(proj, proj, u1, dyz, cw, lg, lb, *(ride.arrays if ride else []))


def _c_mid_fwd(proj, cw, width, name, tt=256):
    t, e4 = proj.shape
    e = e4 // 4
    halo = SHORT_HALO

    def body(p_ref, cw_ref, y_ref, wbuf):
        i = pl.program_id(0)

        @pl.when(i == 0)
        def _():
            wbuf[pl.ds(0, halo), :] = jnp.zeros((halo, e), F32)

        @pl.when(i > 0)
        def _():
            wbuf[pl.ds(0, halo), :] = wbuf[pl.ds(tt, halo), :]

        for r0 in range(0, tt, CONV_ROWS):
            rows = pl.ds(r0, CONV_ROWS)
            wbuf[pl.ds(halo + r0, CONV_ROWS), :] = p_ref[rows, pl.ds(2 * e, e)].astype(F32) * p_ref[rows, pl.ds(0, e)].astype(F32)
        for c0 in range(0, e, CONV_COLS):
            for r0 in range(0, tt, CONV_ROWS):
                rows = pl.ds(r0, CONV_ROWS)
                cv = _conv_chunk(wbuf, halo, cw_ref, width, r0, c0, False)
                bg = p_ref[rows, pl.ds(e + c0, CONV_COLS)].astype(F32)
                z = p_ref[rows, pl.ds(3 * e + c0, CONV_COLS)].astype(F32)
                y_ref[rows, pl.ds(c0, CONV_COLS)] = ((bg * cv) * _silu(z)).astype(BF16)

    return pl.pallas_call(
        body, name=name, grid=(t // tt,),
        in_specs=[pl.BlockSpec((tt, e4), lambda i: (i, 0)), pl.BlockSpec(cw.shape, lambda i: (0, 0))],
        out_specs=pl.BlockSpec((tt, e), lambda i: (i, 0)),
        out_shape=SDS((t, e), BF16),
        scratch_shapes=[pltpu.VMEM((halo + tt, e), F32)],
        compiler_params=_cparams("arbitrary"),
    )(proj, cw)


def _c_mid_bwd(proj, dyz, cw, width, name, tt=256):
    t, e4 = proj.shape
    e = e4 // 4
    halo = SHORT_HALO
    nt = t // tt
    hb = tt // halo

    def body(p_ref, pp_ref, dy_ref, cw_ref, dp_ref, dcw_ref, wbuf, dbuf):
        i = pl.program_id(0)
        ti = nt - 1 - i

        @pl.when(i == 0)
        def _():
            dbuf[pl.ds(tt, halo), :] = jnp.zeros((halo, e), F32)
            dcw_ref[...] = jnp.zeros_like(dcw_ref)

        @pl.when(i > 0)
        def _():
            dbuf[pl.ds(tt, halo), :] = dbuf[pl.ds(0, halo), :]

        keep = (ti > 0).astype(F32)
        wbuf[pl.ds(0, halo), :] = keep * (pp_ref[:, pl.ds(2 * e, e)].astype(F32) * pp_ref[:, pl.ds(0, e)].astype(F32))
        for r0 in range(0, tt, CONV_ROWS):
            rows = pl.ds(r0, CONV_ROWS)
            wbuf[pl.ds(halo + r0, CONV_ROWS), :] = p_ref[rows, pl.ds(2 * e, e)].astype(F32) * p_ref[rows, pl.ds(0, e)].astype(F32)
        for c0 in range(0, e, CONV_COLS):
            for r0 in range(0, tt, CONV_ROWS):
                rows, cols = pl.ds(r0, CONV_ROWS), pl.ds(c0, CONV_COLS)
                cv = _conv_chunk(wbuf, halo, cw_ref, width, r0, c0, False)
                bg = p_ref[rows, pl.ds(e + c0, CONV_COLS)].astype(F32)
                z = p_ref[rows, pl.ds(3 * e + c0, CONV_COLS)].astype(F32)
                sz = _sigmoid(z)
                dyz_c = dy_ref[rows, cols].astype(F32)
                dy = dyz_c * (z * sz)
                dp_ref[rows, pl.ds(3 * e + c0, CONV_COLS)] = (dyz_c * (bg * cv) * (sz * (1.0 + z * (1.0 - sz)))).astype(BF16)
                dp_ref[rows, pl.ds(e + c0, CONV_COLS)] = (dy * cv).astype(BF16)
                dbuf[rows, cols] = dy * bg
        for c0 in range(0, e, CONV_COLS):
            for r0 in range(0, tt, CONV_ROWS):
                rows, cols = pl.ds(r0, CONV_ROWS), pl.ds(c0, CONV_COLS)
                dw = _conv_chunk(dbuf, 0, cw_ref, width, r0, c0, True)
                dp_ref[rows, pl.ds(2 * e + c0, CONV_COLS)] = (dw * p_ref[rows, cols].astype(F32)).astype(BF16)
                dp_ref[rows, cols] = (dw * p_ref[rows, pl.ds(2 * e + c0, CONV_COLS)].astype(F32)).astype(BF16)
        _conv_weight_grad(dcw_ref, dbuf, wbuf, halo, width, tt, e)

    return pl.pallas_call(
        body, name=name, grid=(nt,),
        in_specs=[pl.BlockSpec((tt, e4), lambda i: (nt - 1 - i, 0)),
                  pl.BlockSpec((halo, e4), lambda i: (jnp.maximum((nt - 1 - i) * hb - 1, 0), 0)),
                  pl.BlockSpec((tt, e), lambda i: (nt - 1 - i, 0)), pl.BlockSpec(cw.shape, lambda i: (0, 0))],
        out_specs=[pl.BlockSpec((tt, e4), lambda i: (nt - 1 - i, 0)), pl.BlockSpec(cw.shape, lambda i: (0, 0))],
        out_shape=[SDS((t, e4), BF16), SDS(cw.shape, F32)],
        scratch_shapes=[pltpu.VMEM((halo + tt, e), F32), pltpu.VMEM((tt + halo, e), F32)],
        compiler_params=_cparams("arbitrary"),
    )(proj, proj, dyz, cw)


def _b_prep_fwd(proj, flog, fbias, qg, kg, heads, name, tt=256):
    t, e4 = proj.shape
    e = e4 // 4
    scale = HEAD_DIM ** -0.5 * LOG2E

    def body(q_ref, k_ref, fl_ref, fb_ref, qg_ref, kg_ref, qs_ref, kn_ref, knt_ref, c_ref, ct_ref, carry):
        i = pl.program_id(0)

        @pl.when(i == 0)
        def _():
            carry[...] = jnp.zeros_like(carry)

        for h in range(heads):
            cols = pl.ds(h * HEAD_DIM, HEAD_DIM)
            qh = q_ref[:, cols].astype(F32)
            r = lax.rsqrt(jnp.mean(qh * qh, axis=-1, keepdims=True) + NORM_EPS)
            qs_ref[:, cols] = (((qh * r) * qg_ref[:, cols]) * scale).astype(BF16)
            kh = k_ref[:, cols].astype(F32)
            r = lax.rsqrt(jnp.mean(kh * kh, axis=-1, keepdims=True) + NORM_EPS)
            kn = (kh * r) * kg_ref[:, cols]
            kn_ref[:, cols] = kn.astype(BF16)
            knt_ref[cols, :] = kn.T.astype(BF16)

        a = fl_ref[...] + fb_ref[...]
        lf = jnp.minimum(a, 0.0) - jnp.log(1.0 + jnp.exp(-jnp.abs(a)))
        tri = (lax.broadcasted_iota(jnp.int32, (tt, tt), 0) >= lax.broadcasted_iota(jnp.int32, (tt, tt), 1)).astype(BF16)
        c = _tri_matmul(tri, lf) + carry[...]
        c_ref[...] = c
        ct_ref[...] = (c * LOG2E).T
        carry[...] = c_ref[pl.ds(tt - 1, 1), :]

    return pl.pallas_call(
        body, name=name, grid=(t // tt,),
        in_specs=[pl.BlockSpec((tt, e), lambda i: (i, 0)), pl.BlockSpec((tt, e), lambda i: (i, 1)),
                  pl.BlockSpec((tt, LANES), lambda i: (i, 0)), pl.BlockSpec((1, LANES), lambda i: (0, 0)),
                  pl.BlockSpec((1, e), lambda i: (0, 0)), pl.BlockSpec((1, e), lambda i: (0, 0))],
        out_specs=[pl.BlockSpec((tt, e), lambda i: (i, 0)), pl.BlockSpec((tt, e), lambda i: (i, 0)),
                   pl.BlockSpec((e, tt), lambda i: (0, i)),
                   pl.BlockSpec((tt, LANES), lambda i: (i, 0)), pl.BlockSpec((LANES, tt), lambda i: (0, i))],
        out_shape=[SDS((t, e), BF16), SDS((t, e), BF16), SDS((e, t), BF16), SDS((t, LANES), F32), SDS((LANES, t), F32)],
        scratch_shapes=[pltpu.VMEM((1, LANES), F32)],
        compiler_params=_cparams("arbitrary"),
    )(proj, proj, flog, fbias, qg, kg)


ATT_BLOCK = 1024
ATT_CHUNK = 512
NEG_BIG = -1e30
LOG2E = 1.4426950408889634
LN2 = 0.6931471805599453


def _flash_fwd(qs, kn, proj, ck, heads, name):
    t, e = qs.shape
    blk = min(ATT_BLOCK, t)
    cw = min(ATT_CHUNK, blk // 2)
    nq, nch = t // blk, blk // cw
    assert nch % 2 == 0

    def body(q_ref, k_ref, v_ref, ck_ref, z_ref, o_ref, y_ref, m_ref, l_ref, s_a, s_b):
        i = pl.program_id(1)
        q = q_ref[...]
        bufs = (s_a, s_b)

        def key_rows(j, c):
            return pl.ds(pl.multiple_of(j * blk, blk) + c * cw, cw)

        def logits(j, c):
            bufs[c % 2][...] = (lax.dot_general(q, k_ref[key_rows(j, c), :], NT_DIMS, preferred_element_type=F32)
                            - ck_ref[j][:, c * cw:(c + 1) * cw])

        def weights(c, m, masked):
            s = bufs[c % 2][...]
            if masked:
                keep = lax.broadcasted_iota(jnp.int32, (blk, cw), 0) >= (lax.broadcasted_iota(jnp.int32, (blk, cw), 1) + c * cw)
                s = jnp.where(keep, s, NEG_BIG)
            m_new = jnp.maximum(m, jnp.ceil(jnp.max(s, axis=-1, keepdims=True)))
            return m_new, jnp.exp2(m - m_new), jnp.exp2(s - m_new).astype(BF16)

        ones = jnp.ones((cw, HEAD_DIM), BF16)

        def block(j, carry, masked):
            m, acc = carry
            for c in range(nch):
                if c + 1 < nch:
                    logits(j, c + 1)
                elif not masked:
                    logits(j + 1, 0)
                m, alpha, p = weights(c, m, masked)
                v1 = jnp.concatenate([v_ref[key_rows(j, c), :], ones], axis=1)
                acc = alpha * acc + jnp.dot(p, v1, preferred_element_type=F32)
            return m, acc

        logits(0, 0)
        carry = (jnp.full((blk, 1), NEG_BIG, F32), jnp.zeros((blk, 2 * HEAD_DIM), F32))
        carry = lax.fori_loop(0, i, lambda j, cr: block(j, cr, False), carry)
        m, acc = block(i, carry, True)
        l = acc[:, HEAD_DIM:HEAD_DIM + 1]
        o = acc[:, :HEAD_DIM] / l
        o_ref[...] = o
        y_ref[...] = (o * _silu(z_ref[...].astype(F32))).astype(BF16)
        m_ref[...] = jnp.broadcast_to(m, (blk, LANES))
        l_ref[...] = jnp.broadcast_to(l, (blk, LANES))

    head_all = pl.BlockSpec((t, HEAD_DIM), lambda h, i: (0, h))
    tile = pl.BlockSpec((blk, HEAD_DIM), lambda h, i: (i, h))
    stat = pl.BlockSpec((None, blk, LANES), lambda h, i: (h, i, 0))
    return pl.pallas_call(
        body, name=name, grid=(heads, nq),
        in_specs=[tile, head_all, pl.BlockSpec((t, HEAD_DIM), lambda h, i: (0, 2 * heads + h)),
                  pl.BlockSpec((None, nq, 1, blk), lambda h, i: (h, 0, 0, 0)),
                  pl.BlockSpec((blk, HEAD_DIM), lambda h, i: (i, 3 * heads + h))],
        out_specs=[tile, tile, stat, stat],
        out_shape=[SDS((t, e), F32), SDS((t, e), BF16), SDS((heads, t, LANES), F32), SDS((heads, t, LANES), F32)],
        scratch_shapes=[pltpu.VMEM((blk, cw), F32), pltpu.VMEM((blk, cw), F32)],
        compiler_params=_cparams("parallel", "arbitrary"),
    )(qs, kn, proj, ck, proj)


def _flash_bwd(qs, kn, knt, proj, ck, dyz, o, mstat, lstat, heads, name):
    t, e = qs.shape
    blk = min(ATT_BLOCK, t)
    cw = min(ATT_CHUNK, blk // 2)
    nq, nch = t // blk, blk // cw
    assert nch % 2 == 0

    def body(q_ref, dy_ref, o_ref, z_ref, m_ref, l_ref, k_ref, kt_ref, v_ref, ck_ref,
             dq_ref, dk_ref, dv_ref, dc_ref, dz_ref, s_a, s_b, d_a, d_b):
        i = pl.program_id(1)

        @pl.when(i == 0)
        def _():
            dk_ref[...] = jnp.zeros_like(dk_ref)
            dv_ref[...] = jnp.zeros_like(dv_ref)
            dc_ref[...] = jnp.zeros_like(dc_ref)

        z = z_ref[...].astype(F32)
        sz = _sigmoid(z)
        dy = dy_ref[...].astype(F32)
        of = o_ref[...]
        do = ((dy * (z * sz)) / l_ref[:, 0:1]).astype(BF16)
        dz_ref[...] = (dy * of * (sz * (1.0 + z * (1.0 - sz)))).astype(BF16)
        dl = jnp.sum(do.astype(F32) * of, axis=-1, keepdims=True)
        q = q_ref[...]
        q_t = q.astype(F32).T.astype(BF16)
        do_t = do.astype(F32).T.astype(BF16)
        mrow = m_ref[:, 0:1]
        sbuf, dbuf = (s_a, s_b), (d_a, d_b)

        def key_rows(j, c):
            return pl.ds(pl.multiple_of(j * blk, blk) + c * cw, cw)

        def products(j, c):
            rows = key_rows(j, c)
            sbuf[c % 2][...] = (lax.dot_general(q, k_ref[rows, :], NT_DIMS, preferred_element_type=F32)
                            - ck_ref[j][:, c * cw:(c + 1) * cw])
            dbuf[c % 2][...] = lax.dot_general(do, v_ref[rows, :], NT_DIMS, preferred_element_type=F32)

        def weights(c, masked):
            p = jnp.exp2(sbuf[c % 2][...] - mrow)
            if masked:
                keep = lax.broadcasted_iota(jnp.int32, (blk, cw), 0) >= (lax.broadcasted_iota(jnp.int32, (blk, cw), 1) + c * cw)
                p = jnp.where(keep, p, 0.0)
            p = p.astype(BF16)
            ds = p.astype(F32) * (dbuf[c % 2][...] - dl)
            return p, ds.astype(BF16), jnp.sum(ds, axis=0, keepdims=True)

        def outputs(j, c, p, dsb, colsum, dq):
            rows = key_rows(j, c)
            dv_ref[:, rows] += jnp.dot(do_t, p, preferred_element_type=F32)
            dk_ref[:, rows] += jnp.dot(q_t, dsb, preferred_element_type=F32)
            dc_ref[j, :, pl.ds(c * cw, cw)] -= colsum
            return dq + lax.dot_general(kt_ref[:, rows], dsb, NT_DIMS, preferred_element_type=F32)

        def block(j, dq, masked):
            products(j, 1)
            for c in range(nch):
                p, dsb, colsum = weights(c, masked)
                if c + 2 < nch:
                    products(j, c + 2)
                elif c + 2 == nch and not masked:
                    products(j + 1, 0)
                dq = outputs(j, c, p, dsb, colsum, dq)
            return dq

        products(0, 0)
        dq = lax.fori_loop(0, i, lambda j, acc: block(j, acc, False), jnp.zeros((HEAD_DIM, blk), F32))
        dq_ref[...] = block(i, dq, True).T

    tile = pl.BlockSpec((blk, HEAD_DIM), lambda h, i: (i, h))
    stat = pl.BlockSpec((None, blk, LANES), lambda h, i: (h, i, 0))
    head_all = pl.BlockSpec((t, HEAD_DIM), lambda h, i: (0, h))
    head_all_t = pl.BlockSpec((HEAD_DIM, t), lambda h, i: (h, 0))
    cspec = pl.BlockSpec((None, nq, 1, blk), lambda h, i: (h, 0, 0, 0))
    return pl.pallas_call(
        body, name=name, grid=(heads, nq),
        in_specs=[tile, tile, tile, pl.BlockSpec((blk, HEAD_DIM), lambda h, i: (i, 3 * heads + h)), stat, stat, head_all,
                  head_all_t, pl.BlockSpec((t, HEAD_DIM), lambda h, i: (0, 2 * heads + h)), cspec],
        out_specs=[tile, head_all_t, head_all_t, cspec, tile],
        out_shape=[SDS((t, e), F32), SDS((e, t), F32), SDS((e, t), F32), SDS((heads, nq, 1, blk), F32), SDS((t, e), BF16)],
        scratch_shapes=[pltpu.VMEM((blk, cw), F32)] * 4,
        compiler_params=_cparams("parallel", "arbitrary"),
    )(qs, dyz, o, proj, mstat, lstat, kn, knt, proj, ck)


def _b_prep_bwd(dqs, dknt, dvt, dz, proj, qg, kg, dct, flog, fbias, heads, name, tt=256):
    t, e4 = proj.shape
    e = e4 // 4
    nt = t // tt
    scale = HEAD_DIM ** -0.5

    def body(dq_ref, dkt_ref, dvt_ref, dz_ref, q_ref, k_ref, qg_ref, kg_ref, dc_ref, fl_ref, fb_ref,
             dp_ref, dqg_ref, dkg_ref, dfb_ref, carry, dlf, acc_q, acc_k, acc_f):
        i = pl.program_id(0)

        @pl.when(i == 0)
        def _():
            carry[...] = jnp.zeros_like(carry)
            acc_q[...] = jnp.zeros_like(acc_q)
            acc_k[...] = jnp.zeros_like(acc_k)
            acc_f[...] = jnp.zeros_like(acc_f)

        for h in range(heads):
            cols = pl.ds(h * HEAD_DIM, HEAD_DIM)
            for src_ref, d, g_ref, acc, mult, off in ((q_ref, dq_ref[:, cols], qg_ref, acc_q, scale, 0),
                                                      (k_ref, dkt_ref[cols, :].T, kg_ref, acc_k, LN2, e)):
                xf = src_ref[:, cols].astype(F32)
                r = lax.rsqrt(jnp.mean(xf * xf, axis=-1, keepdims=True) + NORM_EPS)
                xh = xf * r
                dn = d * mult
                acc[...] += _rows8(dn * xh)
                dxh = dn * g_ref[:, cols]
                dp_ref[:, pl.ds(off + h * HEAD_DIM, HEAD_DIM)] = (
                    r * (dxh - xh * jnp.mean(dxh * xh, axis=-1, keepdims=True))).astype(BF16)
                if off:
                    dp_ref[:, pl.ds(2 * e + h * HEAD_DIM, HEAD_DIM)] = dvt_ref[cols, :].T.astype(BF16)
        dp_ref[:, pl.ds(3 * e, e)] = dz_ref[...]

        tri = (lax.broadcasted_iota(jnp.int32, (tt, tt), 0) <= lax.broadcasted_iota(jnp.int32, (tt, tt), 1)).astype(BF16)
        dlf[...] = _tri_matmul(tri, dc_ref[...]) + carry[...]
        carry[...] = dlf[pl.ds(0, 1), :]
        a = fl_ref[...] + fb_ref[...]
        dfl = dlf[...] * _sigmoid(-a)
        dp_ref[:, pl.ds(4 * e, LANES)] = dfl.astype(BF16)
        acc_f[...] += _rows8(dfl)

        @pl.when(i == nt - 1)
        def _():
            dqg_ref[...] = jnp.sum(acc_q[...], axis=0, keepdims=True)
            dkg_ref[...] = jnp.sum(acc_k[...], axis=0, keepdims=True)
            dfb_ref[...] = jnp.sum(acc_f[...], axis=0, keepdims=True)

    rev = lambda i: (nt - 1 - i, 0)
    vec_e = pl.BlockSpec((1, e), lambda i: (0, 0))
    vec = pl.BlockSpec((1, LANES), lambda i: (0, 0))
    wide = pl.BlockSpec((tt, e), rev)
    tall = pl.BlockSpec((e, tt), lambda i: (0, nt - 1 - i))
    lane = pl.BlockSpec((tt, LANES), rev)
    return pl.pallas_call(
        body, name=name, grid=(nt,),
        in_specs=[wide, tall, tall, wide, wide, pl.BlockSpec((tt, e), lambda i: (nt - 1 - i, 1)), vec_e, vec_e, lane, lane, vec],
        out_specs=[pl.BlockSpec((tt, e4 + LANES), rev), vec, vec, vec],
        out_shape=[SDS((t, e4 + LANES), BF16), SDS((1, LANES), F32), SDS((1, LANES), F32), SDS((1, LANES), F32)],
        scratch_shapes=[pltpu.VMEM((1, LANES), F32), pltpu.VMEM((tt, LANES), F32), pltpu.VMEM((SUBLANES, LANES), F32),
                        pltpu.VMEM((SUBLANES, LANES), F32), pltpu.VMEM((SUBLANES, LANES), F32)],
        compiler_params=_cparams("arbitrary"),
    )(dqs, dknt, dvt, dz, proj, proj, qg, kg, dct, flog, fbias)


def _b_fwd(h, b_norm, wb_pad, wb_out, qg, kg, fbias, heads, tag, ride=None):
    t = h.shape[0]
    e = wb_out.shape[0]
    blk = min(ATT_BLOCK, t)
    hn, proj, *rode = _norm_matmul(h, b_norm, wb_pad, f"b_in_proj_{tag}", ride=ride, n_cols=4 * e)
    flog = _matmul_f32out(hn, wb_pad[:, 4 * e:], f"b_forget_proj_{tag}")
    qs, kn, knt, _, ct = _b_prep_fwd(proj, flog, fbias, qg, kg, heads, f"b_prep_fwd_{tag}")
    ck = ct.reshape(LANES, t // blk, 1, blk)
    o, y, mstat, lstat = _flash_fwd(qs, kn, proj, ck, heads, f"b_attention_fwd_{tag}")
    sv = dict(x=h, hn=hn, proj=proj, flog=flog, qs=qs, kn=kn, knt=knt, ck=ck, o=o, y=y, mstat=mstat, lstat=lstat)
    return y, sv, rode


def _b_bwd(dh, sv, b_norm, wb_pad, wb_out, qg, kg, fbias, heads, tag, ride=None):
    t = dh.shape[0]
    e = wb_out.shape[0]
    gb = {}
    dyz = _matmul_nt(dh, wb_out, f"b_out_bwd_{tag}")
    gb["w_out"] = _matmul_tn(sv["y"], dh, f"b_out_wgrad_{tag}", tm=2048)
    dqs, dknt, dvt, dc, dz = _flash_bwd(sv["qs"], sv["kn"], sv["knt"], sv["proj"], sv["ck"], dyz, sv["o"], sv["mstat"],
                                        sv["lstat"], heads, f"b_attention_bwd_{tag}")
    dct = jnp.pad(dc.reshape(heads, t).T, ((0, 0), (0, LANES - heads)))
    dproj, dqg, dkg, dfb = _b_prep_bwd(dqs, dknt, dvt, dz, sv["proj"], qg, kg, dct, sv["flog"], fbias, heads,
                                       f"b_prep_bwd_{tag}")
    gb["w_in"] = _matmul_tn(sv["hn"], dproj, f"b_in_wgrad_{tag}")[:, :4 * e + heads]
    dh, dg, *rode = _dproj_matmul_normbwd(dproj, wb_pad, sv["x"], b_norm, dh, f"b_in_bwd_{tag}", ride=ride)
    gb["norm"], gb["q_norm"], gb["k_norm"], gb["f_bias"] = dg, dqg, dkg, dfb[:, :heads]
    return dh, gb, rode


def _sum_adamw(recv, w, m, v, name, tr=256):
    nl, r, c = w.shape
    tr = min(tr, r)

    def body(g_ref, w_ref, m_ref, v_ref, go_ref, d_ref, mo_ref, vo_ref):
        g = g_ref[0].astype(F32)
        for s in range(1, N_DEV):
            g = g + g_ref[s].astype(F32)
        go_ref[...] = g
        mn = ADAM_B1 * m_ref[...] + (1.0 - ADAM_B1) * g
        vn = ADAM_B2 * v_ref[...] + (1.0 - ADAM_B2) * (g * g)
        m_hat = mn / (1.0 - ADAM_B1 ** ADAM_STEP)
        v_hat = vn / (1.0 - ADAM_B2 ** ADAM_STEP)
        d_ref[...] = -ADAM_LR * (m_hat / (jnp.sqrt(v_hat) + ADAM_EPS) + ADAM_WD * w_ref[...])
        mo_ref[...] = mn
        vo_ref[...] = vn

    blk = pl.BlockSpec((None, tr, c), lambda l, i: (l, i, 0))
    return pl.pallas_call(
        body, name=name, grid=(nl, r // tr),
        in_specs=[pl.BlockSpec((N_DEV, None, tr, c), lambda l, i: (0, l, i, 0)), blk, blk, blk],
        out_specs=[blk, blk, blk, blk],
        out_shape=[SDS(w.shape, F32)] * 4,
        compiler_params=_cparams("parallel", "parallel"),
    )(recv, w, m, v)


def _unshard(g, axis):
    g = jnp.moveaxis(g, 0, axis)
    return g.reshape(g.shape[:axis] + (g.shape[axis] * g.shape[axis + 1],) + g.shape[axis + 2:])


def _to_slabs(full, axis):
    n = full.shape[axis]
    s = full.reshape(full.shape[:axis] + (N_DEV, n // N_DEV) + full.shape[axis + 1:])
    return jnp.moveaxis(s, axis, 0)


def _pack_rows(parts, lead):
    flat = [p.reshape(p.shape[:lead] + (-1,)) for p in parts]
    cat = jnp.concatenate(flat, axis=-1)
    n = cat.shape[-1]
    pad = (-n) % (SUBLANES * LANES)
    cat = jnp.pad(cat, [(0, 0)] * lead + [(0, pad)])
    return cat.reshape(cat.shape[:lead] + ((n + pad) // LANES, LANES))


def _unpack_rows(packed, shapes, lead):
    flat = packed.reshape(packed.shape[:lead] + (-1,))
    out, off = [], 0
    for shp in shapes:
        size = int(np.prod(shp))
        out.append(flat[..., off:off + size].reshape(packed.shape[:lead] + tuple(shp)))
        off += size
    return out


def _pad_rows(w, rows):
    return jnp.pad(w, ((0, rows - w.shape[0]), (0, 0)))


def kernel(x, a_norm, a_w_in, a_conv_w, a_conv_b, a_ln_g, a_ln_b, a_w_out, b_norm, b_w_in, b_f_bias, b_q_norm, b_k_norm, b_w_out, c_norm, c_w_in, c_conv_w, c_w_out, loss_target, m_a_norm, m_a_w_in, m_a_conv_w, m_a_conv_b, m_a_ln_g, m_a_ln_b, m_a_w_out, m_b_norm, m_b_w_in, m_b_f_bias, m_b_q_norm, m_b_k_norm, m_b_w_out, m_c_norm, m_c_w_in, m_c_conv_w, m_c_w_out, v_a_norm, v_a_w_in, v_a_conv_w, v_a_conv_b, v_a_ln_g, v_a_ln_b, v_a_w_out, v_b_norm, v_b_w_in, v_b_f_bias, v_b_q_norm, v_b_k_norm, v_b_w_out, v_c_norm, v_c_w_in, v_c_conv_w, v_c_w_out):
    t, d = x.shape[1], x.shape[2]
    e = a_w_out.shape[1] * N_DEV
    heads = b_f_bias.shape[1]
    n_a, n_b, n_c = a_norm.shape[0], b_norm.shape[0], c_norm.shape[0]
    depth = n_a + n_b + n_c
    ka, kc = a_conv_w.shape[1], c_conv_w.shape[1]
    assert e == heads * HEAD_DIM and n_b == 1 and n_c == 1 and x.shape[0] == 1

    layers = [(i % 3, i // 3) for i in range(depth)]

    def mat_shards(kind, j):
        w_in, w_out = ((a_w_in, a_w_out), (b_w_in, b_w_out), (c_w_in, c_w_out))[kind]
        return [w_in[j].astype(BF16), w_out[j].astype(BF16)]

    def full_mats(kind, gathered):
        w_in, w_out = _unshard(gathered[0], 1), _unshard(gathered[1], 0)
        return (jnp.pad(w_in, ((0, 0), (0, LANES - heads))) if kind == 1 else w_in), w_out

    small_names = ["a_norm", "a_conv_w", "a_conv_b", "a_ln_g", "a_ln_b", "c_norm", "c_conv_w"]
    small = dict(a_norm=a_norm, a_conv_w=a_conv_w, a_conv_b=a_conv_b, a_ln_g=a_ln_g, a_ln_b=a_ln_b,
                 c_norm=c_norm, c_conv_w=c_conv_w)
    small_pack = _pack_rows([small[k] for k in small_names], 0)
    first = _all_gather(mat_shards(*layers[0]) + [small_pack], "all_gather_first_layer")
    sm = _unpack_rows(first[2], [small[k].shape for k in small_names], 1)
    g_a_norm = _unshard(sm[0], 1)
    g_a_conv_w = _unshard(sm[1], 2)
    g_a_conv_b = _unshard(sm[2], 1)
    g_a_ln_g = _unshard(sm[3], 1)
    g_a_ln_b = _unshard(sm[4], 1)
    g_c_norm = _unshard(sm[5], 1)
    g_c_conv_w = _unshard(sm[6], 2)

    cw_a = [_pad_rows(g_a_conv_w[j], CONF_HALO) for j in range(n_a)]
    cw_c = _pad_rows(g_c_conv_w[0], SHORT_HALO)
    qg = jnp.tile(b_q_norm, (1, heads))
    kg = jnp.tile(b_k_norm, (1, heads))
    fbias = jnp.pad(b_f_bias, ((0, 0), (0, LANES - heads)))

    h = x[0]
    saved, weights = [], [full_mats(layers[0][0], first[:2])]
    for i, (kind, j) in enumerate(layers):
        tag = f"l{i}"
        w_in, w_out = weights[i]
        ride = _Ride("gather", mat_shards(*layers[i + 1])) if i + 1 < depth else None
        if kind == 0:
            hn, proj = _norm_matmul(h, g_a_norm[j:j + 1], w_in, f"a_in_proj_{tag}")
            y, u1, *rode = _a_mid_fwd(proj, cw_a[j], g_a_conv_b[j:j + 1], g_a_ln_g[j:j + 1], g_a_ln_b[j:j + 1], ka,
                                      f"a_mid_fwd_{tag}", ride=ride)
            saved.append(dict(x=h, hn=hn, proj=proj, u1=u1, y=y))
        elif kind == 1:
            y, sv, rode = _b_fwd(h, b_norm, w_in, w_out, qg, kg, fbias, heads, tag, ride=ride)
            saved.append(sv)
        else:
            hn, proj, *rode = _norm_matmul(h, g_c_norm, w_in, f"c_in_proj_{tag}", ride=ride)
            y = _c_mid_fwd(proj, cw_c, kc, f"c_mid_fwd_{tag}")
            saved.append(dict(x=h, hn=hn, proj=proj, y=y))
        if ride:
            weights.append(full_mats(layers[i + 1][0], rode))
            h = _out_matmul_residual(y, w_out, h, f"{'abc'[kind]}_out_proj_{tag}")
        else:
            loss_part, dh = _out_matmul_loss(y, w_out, h, loss_target[0], f"{'abc'[kind]}_out_proj_loss_{tag}")
    loss_local = jnp.sum(loss_part).reshape(1, 1)

    ga = dict(norm=[None] * n_a, conv_w=[None] * n_a, conv_b=[None] * n_a, ln_g=[None] * n_a, ln_b=[None] * n_a)
    gb, gc = None, {}
    recv_mats = [None] * depth
    pending = None
    for i in reversed(range(depth)):
        kind, j = layers[i]
        tag = f"l{i}"
        sv = saved[i]
        w_in, w_out = weights[i]
        ride = _Ride("exchange", pending) if pending is not None else None
        if kind == 0:
            dyz = _matmul_nt(dh, w_out, f"a_out_bwd_{tag}")
            gw_out = _matmul_tn(sv["y"], dh, f"a_out_wgrad_{tag}", tm=2048)
            mid_ride, ride = (ride, None) if i == 0 else (None, ride)
            dproj, dcw, dcb, dlg, dlb, *mid_rode = _a_mid_bwd(sv["proj"], sv["u1"], dyz, cw_a[j], g_a_ln_g[j:j + 1],
                                                              g_a_ln_b[j:j + 1], ka, f"a_mid_bwd_{tag}", ride=mid_ride)
            gw_in = _matmul_tn(sv["hn"], dproj, f"a_in_wgrad_{tag}")
            if i == 0:
                ride = _Ride("exchange", [_to_slabs(gw_in, 1), _to_slabs(gw_out, 0)])
            dh, dg, *rode = _dproj_matmul_normbwd(dproj, w_in, sv["x"], g_a_norm[j:j + 1], dh, f"a_in_bwd_{tag}", ride=ride)
            if i == 0:
                if mid_ride:
                    recv_mats[1] = mid_rode
                recv_mats[0], ride = rode, None
            ga["norm"][j], ga["conv_w"][j], ga["conv_b"][j], ga["ln_g"][j], ga["ln_b"][j] = dg[0], dcw[:ka], dcb[0], dlg[0], dlb[0]
        elif kind == 1:
            dh, gb, rode = _b_bwd(dh, sv, b_norm, w_in, w_out, qg, kg, fbias, heads, tag, ride=ride)
            gw_in, gw_out = gb["w_in"], gb["w_out"]
        else:
            dyz = _matmul_nt(dh, w_out, f"c_out_bwd_{tag}")
            gw_out = _matmul_tn(sv["y"], dh, f"c_out_wgrad_{tag}", tm=2048)
            dproj, dcw = _c_mid_bwd(sv["proj"], dyz, cw_c, kc, f"c_mid_bwd_{tag}")
            gw_in = _matmul_tn(sv["hn"], dproj, f"c_in_wgrad_{tag}")
            dh, dg, *rode = _dproj_matmul_normbwd(dproj, w_in, sv["x"], g_c_norm, dh, f"c_in_bwd_{tag}", ride=ride)
            gc["norm"], gc["conv_w"] = dg, dcw[:kc][None]
        if ride:
            recv_mats[i + 1] = rode
        pending = [_to_slabs(gw_in, 1), _to_slabs(gw_out, 0)]
    grad_x = dh[None]

    sharded_small = [(jnp.stack(ga["norm"]), 1), (jnp.stack(ga["conv_w"]), 2), (jnp.stack(ga["conv_b"]), 1),
                     (jnp.stack(ga["ln_g"]), 1), (jnp.stack(ga["ln_b"]), 1), (gc["norm"], 1), (gc["conv_w"], 2)]
    repl_small = [gb["norm"], gb["f_bias"], gb["q_norm"], gb["k_norm"], loss_local]
    small_slabs = _pack_rows([_to_slabs(g, ax) for g, ax in sharded_small]
                             + [jnp.broadcast_to(g[None], (N_DEV,) + g.shape) for g in repl_small], 1)
    recv_small, = _exchange([small_slabs], "exchange_vector_gradients")

    outs = {}
    mat_w = dict(a_w_in=(a_w_in, m_a_w_in, v_a_w_in), a_w_out=(a_w_out, m_a_w_out, v_a_w_out),
                 b_w_in=(b_w_in, m_b_w_in, v_b_w_in), b_w_out=(b_w_out, m_b_w_out, v_b_w_out),
                 c_w_in=(c_w_in, m_c_w_in, v_c_w_in), c_w_out=(c_w_out, m_c_w_out, v_c_w_out))
    for kind, prefix in enumerate("abc"):
        members = [i for i, (k, _) in enumerate(layers) if k == kind]
        for which, name in enumerate((f"{prefix}_w_in", f"{prefix}_w_out")):
            recv = jnp.stack([recv_mats[i][which] for i in members], axis=1)
            outs[name] = _sum_adamw(recv, *mat_w[name], f"adamw_{name}")

    small_order = small_names + ["b_norm", "b_f_bias", "b_q_norm", "b_k_norm", "loss"]
    no_state = jnp.zeros((1, 1), F32)
    small_w = dict(a_norm=(a_norm, m_a_norm, v_a_norm), a_conv_w=(a_conv_w, m_a_conv_w, v_a_conv_w),
                   a_conv_b=(a_conv_b, m_a_conv_b, v_a_conv_b), a_ln_g=(a_ln_g, m_a_ln_g, v_a_ln_g),
                   a_ln_b=(a_ln_b, m_a_ln_b, v_a_ln_b), c_norm=(c_norm, m_c_norm, v_c_norm),
                   c_conv_w=(c_conv_w, m_c_conv_w, v_c_conv_w), b_norm=(b_norm, m_b_norm, v_b_norm),
                   b_f_bias=(b_f_bias, m_b_f_bias, v_b_f_bias), b_q_norm=(b_q_norm, m_b_q_norm, v_b_q_norm),
                   b_k_norm=(b_k_norm, m_b_k_norm, v_b_k_norm), loss=(no_state, no_state, no_state))
    packs = [_pack_rows([small_w[k][q] for k in small_order], 0)[None] for q in range(3)]
    small_out = _sum_adamw(recv_small[:, None], *packs, "adamw_vectors")
    shapes = [small_w[k][0].shape for k in small_order]
    unpacked = [_unpack_rows(o[0], shapes, 0) for o in small_out]
    for idx, name in enumerate(small_order):
        outs[name] = tuple(unpacked[q][idx] for q in range(4))

    loss = outs["loss"][0].reshape(())
    order = ["a_norm", "a_w_in", "a_conv_w", "a_conv_b", "a_ln_g", "a_ln_b", "a_w_out", "b_norm", "b_w_in", "b_f_bias",
             "b_q_norm", "b_k_norm", "b_w_out", "c_norm", "c_w_in", "c_conv_w", "c_w_out"]
    return (loss, grad_x, *[outs[k][0] for k in order], *[outs[k][1] for k in order],
            *[outs[k][2] for k in order], *[outs[k][3] for k in order])
```

```python
import jax
import jax.numpy as jnp
import numpy as np
from jax import lax
from jax.experimental import pallas as pl
from jax.experimental.pallas import tpu as pltpu

F32 = jnp.float32
BF16 = jnp.bfloat16
SDS = jax.ShapeDtypeStruct

NORM_EPS = 1e-6
ADAM_LR = 0.001
ADAM_B1 = 0.9
ADAM_B2 = 0.999
ADAM_EPS = 1e-08
ADAM_WD = 0.01
ADAM_STEP = 10

N_DEV = 8
LANES = 128
SUBLANES = 8
HEAD_DIM = 128
CONF_HALO = 32
SHORT_HALO = 8
VMEM_LIMIT = 56 * 1024 * 1024

NT_DIMS = (((1,), (1,)), ((), ()))
TN_DIMS = (((0,), (0,)), ((), ()))
MESH = pl.DeviceIdType.MESH
ANY = pl.BlockSpec(memory_space=pl.ANY)


def _cparams(*sem):
    return pltpu.CompilerParams(dimension_semantics=sem, vmem_limit_bytes=VMEM_LIMIT)


def _divisor_tile(n, cap):
    return max(m for m in range(LANES, min(n, cap) + 1, LANES) if n % m == 0)


def _sigmoid(x):
    return 1.0 / (1.0 + jnp.exp(-x))


def _silu(x):
    return x * _sigmoid(x)


def _dsilu(x):
    s = _sigmoid(x)
    return s * (1.0 + x * (1.0 - s))


def _rows8(v):
    out = v[0:SUBLANES]
    for a in range(1, v.shape[0] // SUBLANES):
        out = out + v[a * SUBLANES:(a + 1) * SUBLANES]
    return out


def _split3(v):
    hi = v.astype(BF16)
    r1 = v - hi.astype(F32)
    mid = r1.astype(BF16)
    lo = (r1 - mid.astype(F32)).astype(BF16)
    return hi, mid, lo


def _tri_matmul(tri, v):
    hi, mid, lo = _split3(v)
    return (jnp.dot(tri, hi, preferred_element_type=F32) + jnp.dot(tri, mid, preferred_element_type=F32)
            + jnp.dot(tri, lo, preferred_element_type=F32))


def _position():
    return lax.axis_index("x"), lax.axis_index("y"), lax.axis_index("c")


def _all_gather(shards, name):
    n = len(shards)

    def body(*refs):
        xs, outs = refs[:n], refs[n:2 * n]
        send_sems, recv_sems, local_sems = refs[2 * n:]
        x, y, c = _position()
        me, sibling = (x, y, c), (x, y, 1 - c)
        chips = [(1 - x, y), (x, 1 - y), (1 - x, 1 - y)]

        def slot(a, px, py, pc):
            return outs[a].at[4 * px + 2 * py + pc]

        def copy(a, k, block, to, src=None):
            return pltpu.make_async_remote_copy(
                src_ref=slot(a, *block) if src is None else src, dst_ref=slot(a, *block),
                send_sem=send_sems.at[a, k], recv_sem=recv_sems.at[a, k], device_id=to, device_id_type=MESH)

        started = []
        mine = []
        for a in range(n):
            cp = pltpu.make_async_copy(xs[a], slot(a, *me), local_sems.at[a])
            cp.start()
            mine.append(cp)
        for a in range(n):
            first = [copy(a, 0, me, sibling, src=xs[a])]
            first += [copy(a, 1 + j, me, (*chip, c), src=xs[a]) for j, chip in enumerate(chips)]
            for cp in first:
                cp.start()
            started += first
        for a in range(n):
            for j, chip in enumerate(chips):
                copy(a, 1 + j, (*chip, c), me).wait_recv()
                fwd = copy(a, 4 + j, (*chip, c), sibling)
                fwd.start()
                started.append(fwd)
        for a in range(n):
            copy(a, 0, sibling, me).wait_recv()
            for j, chip in enumerate(chips):
                copy(a, 4 + j, (*chip, 1 - c), me).wait_recv()
        for cp in started:
            cp.wait_send()
        for cp in mine:
            cp.wait()

    return pl.pallas_call(
        body, name=name,
        out_shape=[SDS((N_DEV,) + s.shape, s.dtype) for s in shards],
        in_specs=[ANY] * n, out_specs=[ANY] * n,
        scratch_shapes=[pltpu.SemaphoreType.DMA((n, 7)), pltpu.SemaphoreType.DMA((n, 7)), pltpu.SemaphoreType.DMA((n,))],
    )(*shards)


def _exchange(slabs, name):
    n = len(slabs)

    def body(*refs):
        ins, outs = refs[:n], refs[n:2 * n]
        send_sems, recv_sems, local_sems = refs[2 * n:]
        x, y, c = _position()
        me = 4 * x + 2 * y + c
        peers = [(x ^ bx, y ^ by, c ^ bc) for bx in (0, 1) for by in (0, 1) for bc in (0, 1)][1:]

        def copy(a, k, peer):
            pid = 4 * peer[0] + 2 * peer[1] + peer[2]
            return pltpu.make_async_remote_copy(
                src_ref=ins[a].at[pid], dst_ref=outs[a].at[me],
                send_sem=send_sems.at[a, k], recv_sem=recv_sems.at[a, k], device_id=peer, device_id_type=MESH)

        def arrival(a, k, peer):
            pid = 4 * peer[0] + 2 * peer[1] + peer[2]
            return pltpu.make_async_remote_copy(
                src_ref=ins[a].at[pid], dst_ref=outs[a].at[pid],
                send_sem=send_sems.at[a, k], recv_sem=recv_sems.at[a, k], device_id=peer, device_id_type=MESH)

        mine = []
        for a in range(n):
            cp = pltpu.make_async_copy(ins[a].at[me], outs[a].at[me], local_sems.at[a])
            cp.start()
            mine.append(cp)
        started = []
        for a in range(n):
            for k, peer in enumerate(peers):
                cp = copy(a, k, peer)
                cp.start()
                started.append(cp)
        for a in range(n):
            for k, peer in enumerate(peers):
                arrival(a, k, peer).wait_recv()
        for cp in started:
            cp.wait_send()
        for cp in mine:
            cp.wait()

    return pl.pallas_call(
        body, name=name,
        out_shape=[SDS(s.shape, s.dtype) for s in slabs],
        in_specs=[ANY] * n, out_specs=[ANY] * n,
        scratch_shapes=[pltpu.SemaphoreType.DMA((n, 7)), pltpu.SemaphoreType.DMA((n, 7)), pltpu.SemaphoreType.DMA((n,))],
    )(*slabs)


class _Ride:
    def __init__(self, kind, arrays):
        self.kind, self.arrays, self.n = kind, list(arrays), len(arrays)
        self.in_specs = [ANY] * self.n
        self.out_specs = [ANY] * self.n
        self.out_shape = [SDS(((N_DEV,) + a.shape) if kind == "gather" else a.shape, a.dtype) for a in self.arrays]
        self.scratch = [pltpu.SemaphoreType.DMA((self.n, 7)), pltpu.SemaphoreType.DMA((self.n, 7)),
                        pltpu.SemaphoreType.DMA((self.n,))]

    def _copies(self, ins, outs, sems, arriving):
        send_sems, recv_sems, local_sems = sems
        x, y, c = _position()
        me = 4 * x + 2 * y + c
        peers = [(x ^ bx, y ^ by, c ^ bc) for bx in (0, 1) for by in (0, 1) for bc in (0, 1)][1:]
        local, remote = [], []
        for a in range(self.n):
            own = ins[a] if self.kind == "gather" else ins[a].at[me]
            local.append(pltpu.make_async_copy(own, outs[a].at[me], local_sems.at[a]))
            for k, peer in enumerate(peers):
                pid = 4 * peer[0] + 2 * peer[1] + peer[2]
                remote.append(pltpu.make_async_remote_copy(
                    src_ref=ins[a] if self.kind == "gather" else ins[a].at[pid],
                    dst_ref=outs[a].at[pid if arriving else me],
                    send_sem=send_sems.at[a, k], recv_sem=recv_sems.at[a, k], device_id=peer, device_id_type=MESH))
        return local, remote

    def start(self, ins, outs, sems):
        local, sends = self._copies(ins, outs, sems, False)
        for cp in local + sends:
            cp.start()

    def wait(self, ins, outs, sems):
        local, arrivals = self._copies(ins, outs, sems, True)
        for cp in arrivals:
            cp.wait_recv()
        for cp in arrivals:
            cp.wait_send()
        for cp in local:
            cp.wait()


def _norm_matmul(x, g, w, name, tm=2048, tn=1024, ride=None, n_cols=None):
    t, d = x.shape
    n = n_cols or w.shape[1]
    tm, tn = min(tm, t), min(tn, n)
    ni, nj = t // tm, n // tn
    nr = ride.n if ride else 0

    def body(*refs):
        x_ref, g_ref, w_ref = refs[:3]
        hn_ref, o_ref = refs[3 + nr:5 + nr]
        rin, rout, sems = refs[3:3 + nr], refs[5 + nr:5 + 2 * nr], refs[5 + 2 * nr:]
        i, j = pl.program_id(0), pl.program_id(1)
        if ride:
            @pl.when((i == 0) & (j == 0))
            def _():
                ride.start(rin, rout, sems)

        @pl.when(j == 0)
        def _():
            xf = x_ref[...]
            r = lax.rsqrt(jnp.mean(xf * xf, axis=-1, keepdims=True) + NORM_EPS)
            hn_ref[...] = ((xf * r) * g_ref[...]).astype(BF16)

        o_ref[...] = jnp.dot(hn_ref[...], w_ref[...], preferred_element_type=F32).astype(o_ref.dtype)
        if ride:
            @pl.when((i == ni - 1) & (j == nj - 1))
            def _():
                ride.wait(rin, rout, sems)

    return pl.pallas_call(
        body, name=name, grid=(ni, nj),
        in_specs=[pl.BlockSpec((tm, d), lambda i, j: (i, 0)), pl.BlockSpec((1, d), lambda i, j: (0, 0)),
                  pl.BlockSpec((d, tn), lambda i, j: (0, j))] + (ride.in_specs if ride else []),
        out_specs=[pl.BlockSpec((tm, d), lambda i, j: (i, 0)), pl.BlockSpec((tm, tn), lambda i, j: (i, j))]
        + (ride.out_specs if ride else []),
        out_shape=[SDS((t, d), BF16), SDS((t, n), BF16)] + (ride.out_shape if ride else []),
        scratch_shapes=ride.scratch if ride else [],
        compiler_params=_cparams("arbitrary", "arbitrary") if ride else _cparams("parallel", "arbitrary"),
    )(x, g, w, *(ride.arrays if ride else []))


def _matmul_f32out(a, w, name, tm=512):
    t, k = a.shape
    n = w.shape[1]

    def body(a_ref, w_ref, o_ref):
        o_ref[...] = jnp.dot(a_ref[...], w_ref[...], preferred_element_type=F32)

    return pl.pallas_call(
        body, name=name, grid=(t // tm,),
        in_specs=[pl.BlockSpec((tm, k), lambda i: (i, 0)), pl.BlockSpec((k, n), lambda i: (0, 0))],
        out_specs=pl.BlockSpec((tm, n), lambda i: (i, 0)),
        out_shape=SDS((t, n), F32),
        compiler_params=_cparams("parallel"),
    )(a, w)


def _out_matmul_residual(y, w, x, name, tm=512):
    t, e = y.shape
    d = w.shape[1]

    def body(y_ref, w_ref, x_ref, o_ref):
        o_ref[...] = x_ref[...] + jnp.dot(y_ref[...], w_ref[...], preferred_element_type=F32)

    return pl.pallas_call(
        body, name=name, grid=(t // tm,),
        in_specs=[pl.BlockSpec((tm, e), lambda i: (i, 0)), pl.BlockSpec((e, d), lambda i: (0, 0)),
                  pl.BlockSpec((tm, d), lambda i: (i, 0))],
        out_specs=pl.BlockSpec((tm, d), lambda i: (i, 0)),
        out_shape=SDS((t, d), F32),
        compiler_params=_cparams("parallel"),
    )(y, w, x)


def _matmul_nt(a, w, name, tm=512):
    t, d = a.shape
    e = w.shape[0]

    def body(a_ref, w_ref, o_ref):
        o_ref[...] = lax.dot_general(a_ref[...].astype(BF16), w_ref[...], NT_DIMS,
                                     preferred_element_type=F32).astype(o_ref.dtype)

    return pl.pallas_call(
        body, name=name, grid=(t // tm,),
        in_specs=[pl.BlockSpec((tm, d), lambda i: (i, 0)), pl.BlockSpec((e, d), lambda i: (0, 0))],
        out_specs=pl.BlockSpec((tm, e), lambda i: (i, 0)),
        out_shape=SDS((t, e), BF16),
        compiler_params=_cparams("parallel"),
    )(a, w)


def _matmul_tn(a, b, name, out_dtype=BF16, tm=1024, tn=1024, tk=512):
    t, m = a.shape
    n = b.shape[1]
    tm, tn = min(tm, m), _divisor_tile(n, 2 * tn)
    nk = t // tk

    def body(a_ref, b_ref, o_ref, acc_ref):
        k = pl.program_id(2)

        @pl.when(k == 0)
        def _():
            acc_ref[...] = jnp.zeros_like(acc_ref)

        acc_ref[...] += lax.dot_general(a_ref[...].astype(BF16), b_ref[...].astype(BF16), TN_DIMS,
                                        preferred_element_type=F32)

        @pl.when(k == nk - 1)
        def _():
            o_ref[...] = acc_ref[...].astype(o_ref.dtype)

    return pl.pallas_call(
        body, name=name, grid=(m // tm, n // tn, nk),
        in_specs=[pl.BlockSpec((tk, tm), lambda i, j, k: (k, i)), pl.BlockSpec((tk, tn), lambda i, j, k: (k, j))],
        out_specs=pl.BlockSpec((tm, tn), lambda i, j, k: (i, j)),
        out_shape=SDS((m, n), out_dtype),
        scratch_shapes=[pltpu.VMEM((tm, tn), F32)],
        compiler_params=_cparams("parallel", "parallel", "arbitrary"),
    )(a, b)


def _dproj_matmul_normbwd(dproj, w, x, g, dxn, name, tm=1024, tk=1024, ride=None):
    t, n = dproj.shape
    d = w.shape[0]
    tm, tk = min(tm, t), _divisor_tile(n, tk)
    nk = n // tk
    ni = t // tm
    nr = ride.n if ride else 0

    def body(*refs):
        dp_ref, w_ref, x_ref, g_ref, dxn_ref = refs[:5]
        dx_ref, dg_ref = refs[5 + nr:7 + nr]
        rin, rout = refs[5:5 + nr], refs[7 + nr:7 + 2 * nr]
        sems, acc_ref = refs[7 + 2 * nr:-1], refs[-1]
        i, k = pl.program_id(0), pl.program_id(1)
        if ride:
            @pl.when((i == 0) & (k == 0))
            def _():
                ride.start(rin, rout, sems)

        @pl.when(k == 0)
        def _():
            acc_ref[...] = jnp.zeros_like(acc_ref)

        acc_ref[...] += lax.dot_general(dp_ref[...], w_ref[...], NT_DIMS, preferred_element_type=F32)

        @pl.when(k == nk - 1)
        def _():
            dhn = acc_ref[...]
            xf = x_ref[...]
            r = lax.rsqrt(jnp.mean(xf * xf, axis=-1, keepdims=True) + NORM_EPS)
            xh = xf * r
            dy = dhn * g_ref[...]
            dx_ref[...] = dxn_ref[...] + r * (dy - xh * jnp.mean(dy * xh, axis=-1, keepdims=True))
            part = jnp.sum(dhn * xh, axis=0, keepdims=True)

            @pl.when(i == 0)
            def _():
                dg_ref[...] = part

            @pl.when(i > 0)
            def _():
                dg_ref[...] += part

        if ride:
            @pl.when((i == ni - 1) & (k == nk - 1))
            def _():
                ride.wait(rin, rout, sems)

    return pl.pallas_call(
        body, name=name, grid=(ni, nk),
        in_specs=[pl.BlockSpec((tm, tk), lambda i, k: (i, k)), pl.BlockSpec((d, tk), lambda i, k: (0, k)),
                  pl.BlockSpec((tm, d), lambda i, k: (i, 0)), pl.BlockSpec((1, d), lambda i, k: (0, 0)),
                  pl.BlockSpec((tm, d), lambda i, k: (i, 0))] + (ride.in_specs if ride else []),
        out_specs=[pl.BlockSpec((tm, d), lambda i, k: (i, 0)), pl.BlockSpec((1, d), lambda i, k: (0, 0))]
        + (ride.out_specs if ride else []),
        out_shape=[SDS((t, d), F32), SDS((1, d), F32)] + (ride.out_shape if ride else []),
        scratch_shapes=(ride.scratch if ride else []) + [pltpu.VMEM((tm, d), F32)],
        compiler_params=_cparams("arbitrary", "arbitrary"),
    )(dproj, w, x, g, dxn, *(ride.arrays if ride else []))


def _out_matmul_loss(y, w, x, target, name, tm=512):
    t, e = y.shape
    d = w.shape[1]
    inv_d = 1.0 / d

    def body(y_ref, w_ref, x_ref, t_ref, part_ref, dy_ref):
        i = pl.program_id(0)
        err = (x_ref[...] + jnp.dot(y_ref[...], w_ref[...], preferred_element_type=F32)) - t_ref[...]
        dy_ref[...] = err * inv_d
        part = jnp.sum(err * err, axis=0, keepdims=True) * (0.5 * inv_d)

        @pl.when(i == 0)
        def _():
            part_ref[...] = part

        @pl.when(i > 0)
        def _():
            part_ref[...] += part

    return pl.pallas_call(
        body, name=name, grid=(t // tm,),
        in_specs=[pl.BlockSpec((tm, e), lambda i: (i, 0)), pl.BlockSpec((e, d), lambda i: (0, 0)),
                  pl.BlockSpec((tm, d), lambda i: (i, 0)), pl.BlockSpec((tm, d), lambda i: (i, 0))],
        out_specs=[pl.BlockSpec((1, d), lambda i: (0, 0)), pl.BlockSpec((tm, d), lambda i: (i, 0))],
        out_shape=[SDS((1, d), F32), SDS((t, d), F32)],
        compiler_params=_cparams("arbitrary"),
    )(y, w, x, target)


CONV_ROWS = 32
CONV_COLS = 512


def _conv_chunk(src_ref, base, w_ref, width, r0, c0, flip):
    acc = None
    for k in range(width):
        off = base + r0 + ((width - 1 - k) if flip else (k - (width - 1)))
        term = src_ref[pl.ds(off, CONV_ROWS), pl.ds(c0, CONV_COLS)] * w_ref[pl.ds(k, 1), pl.ds(c0, CONV_COLS)]
        acc = term if acc is None else acc + term
    return acc


def _conv_weight_grad(dw_ref, d_ref, src_ref, base, width, tt, e):
    for c0 in range(0, e, CONV_COLS):
        for k in range(width):
            acc = None
            for r0 in range(0, tt, CONV_ROWS):
                prod = (d_ref[pl.ds(r0, CONV_ROWS), pl.ds(c0, CONV_COLS)]
                        * src_ref[pl.ds(base + r0 - (width - 1) + k, CONV_ROWS), pl.ds(c0, CONV_COLS)])
                part = _rows8(prod)
                acc = part if acc is None else acc + part
            dw_ref[pl.ds(k, 1), pl.ds(c0, CONV_COLS)] += jnp.sum(acc, axis=0, keepdims=True)


def _shifted_copies(dst_ref, src_ref, c0, length, sign):
    lo, hi = (SUBLANES, length) if sign < 0 else (0, length - SUBLANES)
    for b in range(SUBLANES):
        for r0 in range(lo, hi, CONV_ROWS):
            n = min(CONV_ROWS, hi - r0)
            dst_ref[b, pl.ds(r0, n), :] = src_ref[pl.ds(r0 + sign * b, n), pl.ds(c0, CONV_COLS)]


def _conv_aligned(copies_ref, base, w_ref, width, r0, c0, sign):
    acc = None
    for d in range(width):
        a, b = divmod(d, SUBLANES)
        term = (copies_ref[b, pl.ds(base + r0 + sign * SUBLANES * a, CONV_ROWS), :]
                * w_ref[pl.ds(width - 1 - d, 1), pl.ds(c0, CONV_COLS)])
        acc = term if acc is None else acc + term
    return acc


def _conv_weight_grad_aligned(dw_ref, dcopies_ref, src_ref, base, width, tt, c0):
    for d in range(width):
        a, b = divmod(d, SUBLANES)
        acc = None
        for r0 in range(0, tt, CONV_ROWS):
            prod = (dcopies_ref[b, pl.ds(r0, CONV_ROWS), :]
                    * src_ref[pl.ds(base + r0 - SUBLANES * a, CONV_ROWS), pl.ds(c0, CONV_COLS)])
            part = _rows8(prod)
            acc = part if acc is None else acc + part
        dw_ref[pl.ds(width - 1 - d, 1), pl.ds(c0, CONV_COLS)] += jnp.sum(acc, axis=0, keepdims=True)


LN_ROWS = 16


def _a_mid_fwd(proj, cw, cb, lg, lb, width, name, tt=256, ride=None):
    t, e3 = proj.shape
    e = e3 // 3
    halo = CONF_HALO
    nt = t // tt
    nr = ride.n if ride else 0

    def body(*refs):
        rin, rout = refs[5:5 + nr], refs[7 + nr:7 + 2 * nr]
        scratch = refs[7 + 2 * nr:]
        sems, own = (scratch[:3], scratch[3:]) if ride else ((), scratch)
        if ride:
            @pl.when(pl.program_id(0) == 0)
            def _():
                ride.start(rin, rout, sems)

        tile(*refs[:5], *refs[5 + nr:7 + nr], *own)
        if ride:
            @pl.when(pl.program_id(0) == nt - 1)
            def _():
                ride.wait(rin, rout, sems)

    def tile(p_ref, cw_ref, cb_ref, lg_ref, lb_ref, y_ref, u1_ref, ubuf, shifted):
        i = pl.program_id(0)

        @pl.when(i == 0)
        def _():
            ubuf[pl.ds(0, halo), :] = jnp.zeros((halo, e), F32)

        @pl.when(i > 0)
        def _():
            ubuf[pl.ds(0, halo), :] = ubuf[pl.ds(tt, halo), :]

        for r0 in range(0, tt, CONV_ROWS):
            val = p_ref[pl.ds(r0, CONV_ROWS), pl.ds(0, e)].astype(F32)
            gate = p_ref[pl.ds(r0, CONV_ROWS), pl.ds(e, e)].astype(F32)
            ubuf[pl.ds(halo + r0, CONV_ROWS), :] = val * _sigmoid(gate)
        for c0 in range(0, e, CONV_COLS):
            _shifted_copies(shifted, ubuf, c0, halo + tt, -1)
            for r0 in range(0, tt, CONV_ROWS):
                acc = _conv_aligned(shifted, halo, cw_ref, width, r0, c0, -1)
                u1_ref[pl.ds(r0, CONV_ROWS), pl.ds(c0, CONV_COLS)] = acc + cb_ref[:, pl.ds(c0, CONV_COLS)]
        for r0 in range(0, tt, LN_ROWS):
            u = u1_ref[pl.ds(r0, LN_ROWS), :]
            mu = jnp.mean(u, axis=-1, keepdims=True)
            dlt = u - mu
            var = jnp.mean(dlt * dlt, axis=-1, keepdims=True)
            u2 = (dlt * lax.rsqrt(var + NORM_EPS)) * lg_ref[...] + lb_ref[...]
            z = p_ref[pl.ds(r0, LN_ROWS), pl.ds(2 * e, e)].astype(F32)
            y_ref[pl.ds(r0, LN_ROWS), :] = (_silu(u2) * _silu(z)).astype(BF16)

    return pl.pallas_call(
        body, name=name, grid=(nt,),
        in_specs=[pl.BlockSpec((tt, e3), lambda i: (i, 0)), pl.BlockSpec(cw.shape, lambda i: (0, 0)),
                  pl.BlockSpec((1, e), lambda i: (0, 0)), pl.BlockSpec((1, e), lambda i: (0, 0)),
                  pl.BlockSpec((1, e), lambda i: (0, 0))] + (ride.in_specs if ride else []),
        out_specs=[pl.BlockSpec((tt, e), lambda i: (i, 0)), pl.BlockSpec((tt, e), lambda i: (i, 0))]
        + (ride.out_specs if ride else []),
        out_shape=[SDS((t, e), BF16), SDS((t, e), F32)] + (ride.out_shape if ride else []),
        scratch_shapes=(ride.scratch if ride else [])
        + [pltpu.VMEM((halo + tt, e), F32), pltpu.VMEM((SUBLANES, halo + tt, CONV_COLS), F32)],
        compiler_params=_cparams("arbitrary"),
    )(proj, cw, cb, lg, lb, *(ride.arrays if ride else []))


def _a_mid_bwd(proj, u1, dyz, cw, lg, lb, width, name, tt=256, ride=None):
    t, e3 = proj.shape
    e = e3 // 3
    halo = CONF_HALO
    nt = t // tt
    hb = tt // halo
    nr = ride.n if ride else 0

    def body(*refs):
        rin, rout = refs[7:7 + nr], refs[12 + nr:12 + 2 * nr]
        scratch = refs[12 + 2 * nr:]
        sems, own = (scratch[:3], scratch[3:]) if ride else ((), scratch)
        if ride:
            @pl.when(pl.program_id(0) == 0)
            def _():
                ride.start(rin, rout, sems)

        tile(*refs[:7], *refs[7 + nr:12 + nr], *own)
        if ride:
            @pl.when(pl.program_id(0) == nt - 1)
            def _():
                ride.wait(rin, rout, sems)

    def tile(p_ref, pp_ref, u1_ref, dy_ref, cw_ref, lg_ref, lb_ref,
             dp_ref, dcw_ref, dcb_ref, dlg_ref, dlb_ref, ubuf, dbuf, shifted, acc_cb, acc_lg, acc_lb):
        i = pl.program_id(0)
        ti = nt - 1 - i

        @pl.when(i == 0)
        def _():
            dbuf[pl.ds(tt, halo), :] = jnp.zeros((halo, e), F32)
            dcw_ref[...] = jnp.zeros_like(dcw_ref)
            acc_cb[...] = jnp.zeros_like(acc_cb)
            acc_lg[...] = jnp.zeros_like(acc_lg)
            acc_lb[...] = jnp.zeros_like(acc_lb)

        @pl.when(i > 0)
        def _():
            dbuf[pl.ds(tt, halo), :] = dbuf[pl.ds(0, halo), :]

        keep = (ti > 0).astype(F32)
        ubuf[pl.ds(0, halo), :] = keep * (pp_ref[:, pl.ds(0, e)].astype(F32) * _sigmoid(pp_ref[:, pl.ds(e, e)].astype(F32)))
        for r0 in range(0, tt, CONV_ROWS):
            val = p_ref[pl.ds(r0, CONV_ROWS), pl.ds(0, e)].astype(F32)
            gate = p_ref[pl.ds(r0, CONV_ROWS), pl.ds(e, e)].astype(F32)
            ubuf[pl.ds(halo + r0, CONV_ROWS), :] = val * _sigmoid(gate)

        for r0 in range(0, tt, LN_ROWS):
            rows = pl.ds(r0, LN_ROWS)
            u = u1_ref[rows, :]
            mu = jnp.mean(u, axis=-1, keepdims=True)
            dlt = u - mu
            var = jnp.mean(dlt * dlt, axis=-1, keepdims=True)
            rstd = lax.rsqrt(var + NORM_EPS)
            xh = dlt * rstd
            u2 = xh * lg_ref[...] + lb_ref[...]
            s2 = _sigmoid(u2)
            u3 = u2 * s2
            z = p_ref[rows, pl.ds(2 * e, e)].astype(F32)
            sz = _sigmoid(z)
            dy = dy_ref[rows, :].astype(F32)
            dp_ref[rows, pl.ds(2 * e, e)] = (dy * u3 * (sz * (1.0 + z * (1.0 - sz)))).astype(BF16)
            du2 = (dy * (z * sz)) * (s2 * (1.0 + u2 * (1.0 - s2)))
            acc_lg[...] += _rows8(du2 * xh)
            acc_lb[...] += _rows8(du2)
            dxh = du2 * lg_ref[...]
            m1 = jnp.mean(dxh, axis=-1, keepdims=True)
            m2 = jnp.mean(dxh * xh, axis=-1, keepdims=True)
            du1 = rstd * (dxh - m1 - xh * m2)
            dbuf[rows, :] = du1
            acc_cb[...] += _rows8(du1)

        for c0 in range(0, e, CONV_COLS):
            _shifted_copies(shifted, dbuf, c0, tt + halo, 1)
            for r0 in range(0, tt, CONV_ROWS):
                du0 = _conv_aligned(shifted, 0, cw_ref, width, r0, c0, 1)
                rows, cols = pl.ds(r0, CONV_ROWS), pl.ds(c0, CONV_COLS)
                val = p_ref[rows, cols].astype(F32)
                sg = _sigmoid(p_ref[rows, pl.ds(e + c0, CONV_COLS)].astype(F32))
                dp_ref[rows, cols] = (du0 * sg).astype(BF16)
                dp_ref[rows, pl.ds(e + c0, CONV_COLS)] = (du0 * val * sg * (1.0 - sg)).astype(BF16)
            _conv_weight_grad_aligned(dcw_ref, shifted, ubuf, halo, width, tt, c0)

        @pl.when(i == nt - 1)
        def _():
            dcb_ref[...] = jnp.sum(acc_cb[...], axis=0, keepdims=True)
            dlg_ref[...] = jnp.sum(acc_lg[...], axis=0, keepdims=True)
            dlb_ref[...] = jnp.sum(acc_lb[...], axis=0, keepdims=True)

    vec = pl.BlockSpec((1, e), lambda i: (0, 0))
    return pl.pallas_call(
        body, name=name, grid=(nt,),
        in_specs=[pl.BlockSpec((tt, e3), lambda i: (nt - 1 - i, 0)),
                  pl.BlockSpec((halo, e3), lambda i: (jnp.maximum((nt - 1 - i) * hb - 1, 0), 0)),
                  pl.BlockSpec((tt, e), lambda i: (nt - 1 - i, 0)), pl.BlockSpec((tt, e), lambda i: (nt - 1 - i, 0)),
                  pl.BlockSpec(cw.shape, lambda i: (0, 0)), vec, vec] + (ride.in_specs if ride else []),
        out_specs=[pl.BlockSpec((tt, e3), lambda i: (nt - 1 - i, 0)), pl.BlockSpec(cw.shape, lambda i: (0, 0)), vec, vec, vec]
        + (ride.out_specs if ride else []),
        out_shape=[SDS((t, e3), BF16), SDS(cw.shape, F32), SDS((1, e), F32), SDS((1, e), F32), SDS((1, e), F32)]
        + (ride.out_shape if ride else []),
        scratch_shapes=(ride.scratch if ride else [])
        + [pltpu.VMEM((halo + tt, e), F32), pltpu.VMEM((tt + halo, e), F32),
           pltpu.VMEM((SUBLANES, halo + tt, CONV_COLS), F32),
           pltpu.VMEM((SUBLANES, e), F32), pltpu.VMEM((SUBLANES, e), F32), pltpu.VMEM((SUBLANES, e), F32)],
        compiler_params=_cparams("arbitrary"),
    )(proj, proj, u1, dyz, cw, lg, lb, *(ride.arrays if ride else []))


def _c_mid_fwd(proj, cw, width, name, tt=256):
    t, e4 = proj.shape
    e = e4 // 4
    halo = SHORT_HALO

    def body(p_ref, cw_ref, y_ref, wbuf):
        i = pl.program_id(0)

        @pl.when(i == 0)
        def _():
            wbuf[pl.ds(0, halo), :] = jnp.zeros((halo, e), F32)

        @pl.when(i > 0)
        def _():
            wbuf[pl.ds(0, halo), :] = wbuf[pl.ds(tt, halo), :]

        for r0 in range(0, tt, CONV_ROWS):
            rows = pl.ds(r0, CONV_ROWS)
            wbuf[pl.ds(halo + r0, CONV_ROWS), :] = p_ref[rows, pl.ds(2 * e, e)].astype(F32) * p_ref[rows, pl.ds(0, e)].astype(F32)
        for c0 in range(0, e, CONV_COLS):
            for r0 in range(0, tt, CONV_ROWS):
                rows = pl.ds(r0, CONV_ROWS)
                cv = _conv_chunk(wbuf, halo, cw_ref, width, r0, c0, False)
                bg = p_ref[rows, pl.ds(e + c0, CONV_COLS)].astype(F32)
                z = p_ref[rows, pl.ds(3 * e + c0, CONV_COLS)].astype(F32)
                y_ref[rows, pl.ds(c0, CONV_COLS)] = ((bg * cv) * _silu(z)).astype(BF16)

    return pl.pallas_call(
        body, name=name, grid=(t // tt,),
        in_specs=[pl.BlockSpec((tt, e4), lambda i: (i, 0)), pl.BlockSpec(cw.shape, lambda i: (0, 0))],
        out_specs=pl.BlockSpec((tt, e), lambda i: (i, 0)),
        out_shape=SDS((t, e), BF16),
        scratch_shapes=[pltpu.VMEM((halo + tt, e), F32)],
        compiler_params=_cparams("arbitrary"),
    )(proj, cw)


def _c_mid_bwd(proj, dyz, cw, width, name, tt=256):
    t, e4 = proj.shape
    e = e4 // 4
    halo = SHORT_HALO
    nt = t // tt
    hb = tt // halo

    def body(p_ref, pp_ref, dy_ref, cw_ref, dp_ref, dcw_ref, wbuf, dbuf):
        i = pl.program_id(0)
        ti = nt - 1 - i

        @pl.when(i == 0)
        def _():
            dbuf[pl.ds(tt, halo), :] = jnp.zeros((halo, e), F32)
            dcw_ref[...] = jnp.zeros_like(dcw_ref)

        @pl.when(i > 0)
        def _():
            dbuf[pl.ds(tt, halo), :] = dbuf[pl.ds(0, halo), :]

        keep = (ti > 0).astype(F32)
        wbuf[pl.ds(0, halo), :] = keep * (pp_ref[:, pl.ds(2 * e, e)].astype(F32) * pp_ref[:, pl.ds(0, e)].astype(F32))
        for r0 in range(0, tt, CONV_ROWS):
            rows = pl.ds(r0, CONV_ROWS)
            wbuf[pl.ds(halo + r0, CONV_ROWS), :] = p_ref[rows, pl.ds(2 * e, e)].astype(F32) * p_ref[rows, pl.ds(0, e)].astype(F32)
        for c0 in range(0, e, CONV_COLS):
            for r0 in range(0, tt, CONV_ROWS):
                rows, cols = pl.ds(r0, CONV_ROWS), pl.ds(c0, CONV_COLS)
                cv = _conv_chunk(wbuf, halo, cw_ref, width, r0, c0, False)
                bg = p_ref[rows, pl.ds(e + c0, CONV_COLS)].astype(F32)
                z = p_ref[rows, pl.ds(3 * e + c0, CONV_COLS)].astype(F32)
                sz = _sigmoid(z)
                dyz_c = dy_ref[rows, cols].astype(F32)
                dy = dyz_c * (z * sz)
                dp_ref[rows, pl.ds(3 * e + c0, CONV_COLS)] = (dyz_c * (bg * cv) * (sz * (1.0 + z * (1.0 - sz)))).astype(BF16)
                dp_ref[rows, pl.ds(e + c0, CONV_COLS)] = (dy * cv).astype(BF16)
                dbuf[rows, cols] = dy * bg
        for c0 in range(0, e, CONV_COLS):
            for r0 in range(0, tt, CONV_ROWS):
                rows, cols = pl.ds(r0, CONV_ROWS), pl.ds(c0, CONV_COLS)
                dw = _conv_chunk(dbuf, 0, cw_ref, width, r0, c0, True)
                dp_ref[rows, pl.ds(2 * e + c0, CONV_COLS)] = (dw * p_ref[rows, cols].astype(F32)).astype(BF16)
                dp_ref[rows, cols] = (dw * p_ref[rows, pl.ds(2 * e + c0, CONV_COLS)].astype(F32)).astype(BF16)
        _conv_weight_grad(dcw_ref, dbuf, wbuf, halo, width, tt, e)

    return pl.pallas_call(
        body, name=name, grid=(nt,),
        in_specs=[pl.BlockSpec((tt, e4), lambda i: (nt - 1 - i, 0)),
                  pl.BlockSpec((halo, e4), lambda i: (jnp.maximum((nt - 1 - i) * hb - 1, 0), 0)),
                  pl.BlockSpec((tt, e), lambda i: (nt - 1 - i, 0)), pl.BlockSpec(cw.shape, lambda i: (0, 0))],
        out_specs=[pl.BlockSpec((tt, e4), lambda i: (nt - 1 - i, 0)), pl.BlockSpec(cw.shape, lambda i: (0, 0))],
        out_shape=[SDS((t, e4), BF16), SDS(cw.shape, F32)],
        scratch_shapes=[pltpu.VMEM((halo + tt, e), F32), pltpu.VMEM((tt + halo, e), F32)],
        compiler_params=_cparams("arbitrary"),
    )(proj, proj, dyz, cw)


def _b_prep_fwd(proj, flog, fbias, qg, kg, heads, name, tt=256):
    t, e4 = proj.shape
    e = e4 // 4
    scale = HEAD_DIM ** -0.5 * LOG2E

    def body(q_ref, k_ref, fl_ref, fb_ref, qg_ref, kg_ref, qs_ref, kn_ref, knt_ref, c_ref, ct_ref, carry):
        i = pl.program_id(0)

        @pl.when(i == 0)
        def _():
            carry[...] = jnp.zeros_like(carry)

        for h in range(heads):
            cols = pl.ds(h * HEAD_DIM, HEAD_DIM)
            qh = q_ref[:, cols].astype(F32)
            r = lax.rsqrt(jnp.mean(qh * qh, axis=-1, keepdims=True) + NORM_EPS)
            qs_ref[:, cols] = (((qh * r) * qg_ref[:, cols]) * scale).astype(BF16)
            kh = k_ref[:, cols].astype(F32)
            r = lax.rsqrt(jnp.mean(kh * kh, axis=-1, keepdims=True) + NORM_EPS)
            kn = (kh * r) * kg_ref[:, cols]
            kn_ref[:, cols] = kn.astype(BF16)
            knt_ref[cols, :] = kn.T.astype(BF16)

        a = fl_ref[...] + fb_ref[...]
        lf = jnp.minimum(a, 0.0) - jnp.log(1.0 + jnp.exp(-jnp.abs(a)))
        tri = (lax.broadcasted_iota(jnp.int32, (tt, tt), 0) >= lax.broadcasted_iota(jnp.int32, (tt, tt), 1)).astype(BF16)
        c = _tri_matmul(tri, lf) + carry[...]
        c_ref[...] = c
        ct_ref[...] = (c * LOG2E).T
        carry[...] = c_ref[pl.ds(tt - 1, 1), :]

    return pl.pallas_call(
        body, name=name, grid=(t // tt,),
        in_specs=[pl.BlockSpec((tt, e), lambda i: (i, 0)), pl.BlockSpec((tt, e), lambda i: (i, 1)),
                  pl.BlockSpec((tt, LANES), lambda i: (i, 0)), pl.BlockSpec((1, LANES), lambda i: (0, 0)),
                  pl.BlockSpec((1, e), lambda i: (0, 0)), pl.BlockSpec((1, e), lambda i: (0, 0))],
        out_specs=[pl.BlockSpec((tt, e), lambda i: (i, 0)), pl.BlockSpec((tt, e), lambda i: (i, 0)),
                   pl.BlockSpec((e, tt), lambda i: (0, i)),
                   pl.BlockSpec((tt, LANES), lambda i: (i, 0)), pl.BlockSpec((LANES, tt), lambda i: (0, i))],
        out_shape=[SDS((t, e), BF16), SDS((t, e), BF16), SDS((e, t), BF16), SDS((t, LANES), F32), SDS((LANES, t), F32)],
        scratch_shapes=[pltpu.VMEM((1, LANES), F32)],
        compiler_params=_cparams("arbitrary"),
    )(proj, proj, flog, fbias, qg, kg)


ATT_BLOCK = 1024
ATT_CHUNK_FWD = 512
ATT_CHUNK_BWD = 256
NEG_BIG = -1e30
LOG2E = 1.4426950408889634
LN2 = 0.6931471805599453


def _flash_fwd(qs, kn, proj, ck, heads, name):
    t, e = qs.shape
    blk = min(ATT_BLOCK, t)
    cw = min(ATT_CHUNK_FWD, blk // 2)
    nq, nch = t // blk, blk // cw
    assert nch % 2 == 0

    def body(q_ref, k_ref, v_ref, ck_ref, z_ref, o_ref, y_ref, m_ref, l_ref, s_a, s_b):
        i = pl.program_id(1)
        q = q_ref[...]
        bufs = (s_a, s_b)

        def key_rows(j, c):
            return pl.ds(pl.multiple_of(j * blk, blk) + c * cw, cw)

        def logits(j, c):
            bufs[c % 2][...] = (lax.dot_general(q, k_ref[key_rows(j, c), :], NT_DIMS, preferred_element_type=F32)
                            - ck_ref[j][:, c * cw:(c + 1) * cw])

        def weights(c, m, masked):
            s = bufs[c % 2][...]
            if masked:
                keep = lax.broadcasted_iota(jnp.int32, (blk, cw), 0) >= (lax.broadcasted_iota(jnp.int32, (blk, cw), 1) + c * cw)
                s = jnp.where(keep, s, NEG_BIG)
            m_new = jnp.maximum(m, jnp.ceil(jnp.max(s, axis=-1, keepdims=True)))
            return m_new, jnp.exp2(m - m_new), jnp.exp2(s - m_new).astype(BF16)

        ones = jnp.ones((cw, HEAD_DIM), BF16)

        def block(j, carry, masked):
            m, acc = carry
            for c in range(nch):
                if c + 1 < nch:
                    logits(j, c + 1)
                elif not masked:
                    logits(j + 1, 0)
                m, alpha, p = weights(c, m, masked)
                v1 = jnp.concatenate([v_ref[key_rows(j, c), :], ones], axis=1)
                acc = alpha * acc + jnp.dot(p, v1, preferred_element_type=F32)
            return m, acc

        logits(0, 0)
        carry = (jnp.full((blk, 1), NEG_BIG, F32), jnp.zeros((blk, 2 * HEAD_DIM), F32))
        carry = lax.fori_loop(0, i, lambda j, cr: block(j, cr, False), carry)
        m, acc = block(i, carry, True)
        l = acc[:, HEAD_DIM:HEAD_DIM + 1]
        o = acc[:, :HEAD_DIM] / l
        o_ref[...] = o
        y_ref[...] = (o * _silu(z_ref[...].astype(F32))).astype(BF16)
        m_ref[...] = jnp.broadcast_to(m, (blk, LANES))
        l_ref[...] = jnp.broadcast_to(l, (blk, LANES))

    head_all = pl.BlockSpec((t, HEAD_DIM), lambda h, i: (0, h))
    tile = pl.BlockSpec((blk, HEAD_DIM), lambda h, i: (i, h))
    stat = pl.BlockSpec((None, blk, LANES), lambda h, i: (h, i, 0))
    return pl.pallas_call(
        body, name=name, grid=(heads, nq),
        in_specs=[tile, head_all, pl.BlockSpec((t, HEAD_DIM), lambda h, i: (0, 2 * heads + h)),
                  pl.BlockSpec((None, nq, 1, blk), lambda h, i: (h, 0, 0, 0)),
                  pl.BlockSpec((blk, HEAD_DIM), lambda h, i: (i, 3 * heads + h))],
        out_specs=[tile, tile, stat, stat],
        out_shape=[SDS((t, e), F32), SDS((t, e), BF16), SDS((heads, t, LANES), F32), SDS((heads, t, LANES), F32)],
        scratch_shapes=[pltpu.VMEM((blk, cw), F32), pltpu.VMEM((blk, cw), F32)],
        compiler_params=_cparams("parallel", "arbitrary"),
    )(qs, kn, proj, ck, proj)


def _flash_bwd(qs, kn, knt, proj, ck, dyz, o, mstat, lstat, heads, name):
    t, e = qs.shape
    blk = min(ATT_BLOCK, t)
    cw = min(ATT_CHUNK_BWD, blk // 2)
    nq, nch = t // blk, blk // cw
    assert nch % 2 == 0

    def body(q_ref, dy_ref, o_ref, z_ref, m_ref, l_ref, k_ref, kt_ref, v_ref, ck_ref,
             dq_ref, dk_ref, dv_ref, dc_ref, dz_ref, s_a, s_b, d_a, d_b):
        i = pl.program_id(1)

        @pl.when(i == 0)
        def _():
            dk_ref[...] = jnp.zeros_like(dk_ref)
            dv_ref[...] = jnp.zeros_like(dv_ref)
            dc_ref[...] = jnp.zeros_like(dc_ref)

        z = z_ref[...].astype(F32)
        sz = _sigmoid(z)
        dy = dy_ref[...].astype(F32)
        of = o_ref[...]
        do = ((dy * (z * sz)) / l_ref[:, 0:1]).astype(BF16)
        dz_ref[...] = (dy * of * (sz * (1.0 + z * (1.0 - sz)))).astype(BF16)
        dl = jnp.sum(do.astype(F32) * of, axis=-1, keepdims=True)
        q = q_ref[...]
        q_t = q.astype(F32).T.astype(BF16)
        do_t = do.astype(F32).T.astype(BF16)
        mrow = m_ref[:, 0:1]
        sbuf, dbuf = (s_a, s_b), (d_a, d_b)

        def key_rows(j, c):
            return pl.ds(pl.multiple_of(j * blk, blk) + c * cw, cw)

        def products(j, c):
            rows = key_rows(j, c)
            sbuf[c % 2][...] = (lax.dot_general(q, k_ref[rows, :], NT_DIMS, preferred_element_type=F32)
                            - ck_ref[j][:, c * cw:(c + 1) * cw])
            dbuf[c % 2][...] = lax.dot_general(do, v_ref[rows, :], NT_DIMS, preferred_element_type=F32)

        def weights(c, masked):
            p = jnp.exp2(sbuf[c % 2][...] - mrow)
            if masked:
                keep = lax.broadcasted_iota(jnp.int32, (blk, cw), 0) >= (lax.broadcasted_iota(jnp.int32, (blk, cw), 1) + c * cw)
                p = jnp.where(keep, p, 0.0)
            p = p.astype(BF16)
            ds = p.astype(F32) * (dbuf[c % 2][...] - dl)
            return p, ds.astype(BF16), jnp.sum(ds, axis=0, keepdims=True)

        def outputs(j, c, p, dsb, colsum, dq):
            rows = key_rows(j, c)
            dv_ref[:, rows] += jnp.dot(do_t, p, preferred_element_type=F32)
            dk_ref[:, rows] += jnp.dot(q_t, dsb, preferred_element_type=F32)
            dc_ref[j, :, pl.ds(c * cw, cw)] -= colsum
            return dq + lax.dot_general(kt_ref[:, rows], dsb, NT_DIMS, preferred_element_type=F32)

        def block(j, dq, masked):
            products(j, 1)
            for c in range(nch):
                p, dsb, colsum = weights(c, masked)
                if c + 2 < nch:
                    products(j, c + 2)
                elif c + 2 == nch and not masked:
                    products(j + 1, 0)
                dq = outputs(j, c, p, dsb, colsum, dq)
            return dq

        products(0, 0)
        dq = lax.fori_loop(0, i, lambda j, acc: block(j, acc, False), jnp.zeros((HEAD_DIM, blk), F32))
        dq_ref[...] = block(i, dq, True).T

    tile = pl.BlockSpec((blk, HEAD_DIM), lambda h, i: (i, h))
    stat = pl.BlockSpec((None, blk, LANES), lambda h, i: (h, i, 0))
    head_all = pl.BlockSpec((t, HEAD_DIM), lambda h, i: (0, h))
    head_all_t = pl.BlockSpec((HEAD_DIM, t), lambda h, i: (h, 0))
    cspec = pl.BlockSpec((None, nq, 1, blk), lambda h, i: (h, 0, 0, 0))
    return pl.pallas_call(
        body, name=name, grid=(heads, nq),
        in_specs=[tile, tile, tile, pl.BlockSpec((blk, HEAD_DIM), lambda h, i: (i, 3 * heads + h)), stat, stat, head_all,
                  head_all_t, pl.BlockSpec((t, HEAD_DIM), lambda h, i: (0, 2 * heads + h)), cspec],
        out_specs=[tile, head_all_t, head_all_t, cspec, tile],
        out_shape=[SDS((t, e), F32), SDS((e, t), F32), SDS((e, t), F32), SDS((heads, nq, 1, blk), F32), SDS((t, e), BF16)],
        scratch_shapes=[pltpu.VMEM((blk, cw), F32)] * 4,
        compiler_params=_cparams("parallel", "arbitrary"),
    )(qs, dyz, o, proj, mstat, lstat, kn, knt, proj, ck)


def _b_prep_bwd(dqs, dknt, dvt, dz, proj, qg, kg, dct, flog, fbias, heads, name, tt=256):
    t, e4 = proj.shape
    e = e4 // 4
    nt = t // tt
    scale = HEAD_DIM ** -0.5

    def body(dq_ref, dkt_ref, dvt_ref, dz_ref, q_ref, k_ref, qg_ref, kg_ref, dc_ref, fl_ref, fb_ref,
             dp_ref, dqg_ref, dkg_ref, dfb_ref, carry, dlf, acc_q, acc_k, acc_f):
        i = pl.program_id(0)

        @pl.when(i == 0)
        def _():
            carry[...] = jnp.zeros_like(carry)
            acc_q[...] = jnp.zeros_like(acc_q)
            acc_k[...] = jnp.zeros_like(acc_k)
            acc_f[...] = jnp.zeros_like(acc_f)

        for h in range(heads):
            cols = pl.ds(h * HEAD_DIM, HEAD_DIM)
            for src_ref, d, g_ref, acc, mult, off in ((q_ref, dq_ref[:, cols], qg_ref, acc_q, scale, 0),
                                                      (k_ref, dkt_ref[cols, :].T, kg_ref, acc_k, LN2, e)):
                xf = src_ref[:, cols].astype(F32)
                r = lax.rsqrt(jnp.mean(xf * xf, axis=-1, keepdims=True) + NORM_EPS)
                xh = xf * r
                dn = d * mult
                acc[...] += _rows8(dn * xh)
                dxh = dn * g_ref[:, cols]
                dp_ref[:, pl.ds(off + h * HEAD_DIM, HEAD_DIM)] = (
                    r * (dxh - xh * jnp.mean(dxh * xh, axis=-1, keepdims=True))).astype(BF16)
                if off:
                    dp_ref[:, pl.ds(2 * e + h * HEAD_DIM, HEAD_DIM)] = dvt_ref[cols, :].T.astype(BF16)
        dp_ref[:, pl.ds(3 * e, e)] = dz_ref[...]

        tri = (lax.broadcasted_iota(jnp.int32, (tt, tt), 0) <= lax.broadcasted_iota(jnp.int32, (tt, tt), 1)).astype(BF16)
        dlf[...] = _tri_matmul(tri, dc_ref[...]) + carry[...]
        carry[...] = dlf[pl.ds(0, 1), :]
        a = fl_ref[...] + fb_ref[...]
        dfl = dlf[...] * _sigmoid(-a)
        dp_ref[:, pl.ds(4 * e, LANES)] = dfl.astype(BF16)
        acc_f[...] += _rows8(dfl)

        @pl.when(i == nt - 1)
        def _():
            dqg_ref[...] = jnp.sum(acc_q[...], axis=0, keepdims=True)
            dkg_ref[...] = jnp.sum(acc_k[...], axis=0, keepdims=True)
            dfb_ref[...] = jnp.sum(acc_f[...], axis=0, keepdims=True)

    rev = lambda i: (nt - 1 - i, 0)
    vec_e = pl.BlockSpec((1, e), lambda i: (0, 0))
    vec = pl.BlockSpec((1, LANES), lambda i: (0, 0))
    wide = pl.BlockSpec((tt, e), rev)
    tall = pl.BlockSpec((e, tt), lambda i: (0, nt - 1 - i))
    lane = pl.BlockSpec((tt, LANES), rev)
    return pl.pallas_call(
        body, name=name, grid=(nt,),
        in_specs=[wide, tall, tall, wide, wide, pl.BlockSpec((tt, e), lambda i: (nt - 1 - i, 1)), vec_e, vec_e, lane, lane, vec],
        out_specs=[pl.BlockSpec((tt, e4 + LANES), rev), vec, vec, vec],
        out_shape=[SDS((t, e4 + LANES), BF16), SDS((1, LANES), F32), SDS((1, LANES), F32), SDS((1, LANES), F32)],
        scratch_shapes=[pltpu.VMEM((1, LANES), F32), pltpu.VMEM((tt, LANES), F32), pltpu.VMEM((SUBLANES, LANES), F32),
                        pltpu.VMEM((SUBLANES, LANES), F32), pltpu.VMEM((SUBLANES, LANES), F32)],
        compiler_params=_cparams("arbitrary"),
    )(dqs, dknt, dvt, dz, proj, proj, qg, kg, dct, flog, fbias)


def _b_fwd(h, b_norm, wb_pad, wb_out, qg, kg, fbias, heads, tag, ride=None):
    t = h.shape[0]
    e = wb_out.shape[0]
    blk = min(ATT_BLOCK, t)
    hn, proj, *rode = _norm_matmul(h, b_norm, wb_pad, f"b_in_proj_{tag}", ride=ride, n_cols=4 * e)
    flog = _matmul_f32out(hn, wb_pad[:, 4 * e:], f"b_forget_proj_{tag}")
    qs, kn, knt, _, ct = _b_prep_fwd(proj, flog, fbias, qg, kg, heads, f"b_prep_fwd_{tag}")
    ck = ct.reshape(LANES, t // blk, 1, blk)
    o, y, mstat, lstat = _flash_fwd(qs, kn, proj, ck, heads, f"b_attention_fwd_{tag}")
    sv = dict(x=h, hn=hn, proj=proj, flog=flog, qs=qs, kn=kn, knt=knt, ck=ck, o=o, y=y, mstat=mstat, lstat=lstat)
    return y, sv, rode


def _b_bwd(dh, sv, b_norm, wb_pad, wb_out, qg, kg, fbias, heads, tag, ride=None):
    t = dh.shape[0]
    e = wb_out.shape[0]
    gb = {}
    dyz = _matmul_nt(dh, wb_out, f"b_out_bwd_{tag}")
    gb["w_out"] = _matmul_tn(sv["y"], dh, f"b_out_wgrad_{tag}", tm=2048)
    dqs, dknt, dvt, dc, dz = _flash_bwd(sv["qs"], sv["kn"], sv["knt"], sv["proj"], sv["ck"], dyz, sv["o"], sv["mstat"],
                                        sv["lstat"], heads, f"b_attention_bwd_{tag}")
    dct = jnp.pad(dc.reshape(heads, t).T, ((0, 0), (0, LANES - heads)))
    dproj, dqg, dkg, dfb = _b_prep_bwd(dqs, dknt, dvt, dz, sv["proj"], qg, kg, dct, sv["flog"], fbias, heads,
                                       f"b_prep_bwd_{tag}")
    gb["w_in"] = _matmul_tn(sv["hn"], dproj, f"b_in_wgrad_{tag}")[:, :4 * e + heads]
    dh, dg, *rode = _dproj_matmul_normbwd(dproj, wb_pad, sv["x"], b_norm, dh, f"b_in_bwd_{tag}", ride=ride)
    gb["norm"], gb["q_norm"], gb["k_norm"], gb["f_bias"] = dg, dqg, dkg, dfb[:, :heads]
    return dh, gb, rode


def _sum_adamw(recv, w, m, v, name, tr=256):
    nl, r, c = w.shape
    tr = min(tr, r)

    def body(g_ref, w_ref, m_ref, v_ref, go_ref, d_ref, mo_ref, vo_ref):
        g = g_ref[0].astype(F32)
        for s in range(1, N_DEV):
            g = g + g_ref[s].astype(F32)
        go_ref[...] = g
        mn = ADAM_B1 * m_ref[...] + (1.0 - ADAM_B1) * g
        vn = ADAM_B2 * v_ref[...] + (1.0 - ADAM_B2) * (g * g)
        m_hat = mn / (1.0 - ADAM_B1 ** ADAM_STEP)
        v_hat = vn / (1.0 - ADAM_B2 ** ADAM_STEP)
        d_ref[...] = -ADAM_LR * (m_hat / (jnp.sqrt(v_hat) + ADAM_EPS) + ADAM_WD * w_ref[...])
        mo_ref[...] = mn
        vo_ref[...] = vn

    blk = pl.BlockSpec((None, tr, c), lambda l, i: (l, i, 0))
    return pl.pallas_call(
        body, name=name, grid=(nl, r // tr),
        in_specs=[pl.BlockSpec((N_DEV, None, tr, c), lambda l, i: (0, l, i, 0)), blk, blk, blk],
        out_specs=[blk, blk, blk, blk],
        out_shape=[SDS(w.shape, F32)] * 4,
        compiler_params=_cparams("parallel", "parallel"),
    )(recv, w, m, v)


def _unshard(g, axis):
    g = jnp.moveaxis(g, 0, axis)
    return g.reshape(g.shape[:axis] + (g.shape[axis] * g.shape[axis + 1],) + g.shape[axis + 2:])


def _to_slabs(full, axis):
    n = full.shape[axis]
    s = full.reshape(full.shape[:axis] + (N_DEV, n // N_DEV) + full.shape[axis + 1:])
    return jnp.moveaxis(s, axis, 0)


def _pack_rows(parts, lead):
    flat = [p.reshape(p.shape[:lead] + (-1,)) for p in parts]
    cat = jnp.concatenate(flat, axis=-1)
    n = cat.shape[-1]
    pad = (-n) % (SUBLANES * LANES)
    cat = jnp.pad(cat, [(0, 0)] * lead + [(0, pad)])
    return cat.reshape(cat.shape[:lead] + ((n + pad) // LANES, LANES))


def _unpack_rows(packed, shapes, lead):
    flat = packed.reshape(packed.shape[:lead] + (-1,))
    out, off = [], 0
    for shp in shapes:
        size = int(np.prod(shp))
        out.append(flat[..., off:off + size].reshape(packed.shape[:lead] + tuple(shp)))
        off += size
    return out


def _pad_rows(w, rows):
    return jnp.pad(w, ((0, rows - w.shape[0]), (0, 0)))


def kernel(x, a_norm, a_w_in, a_conv_w, a_conv_b, a_ln_g, a_ln_b, a_w_out, b_norm, b_w_in, b_f_bias, b_q_norm, b_k_norm, b_w_out, c_norm, c_w_in, c_conv_w, c_w_out, loss_target, m_a_norm, m_a_w_in, m_a_conv_w, m_a_conv_b, m_a_ln_g, m_a_ln_b, m_a_w_out, m_b_norm, m_b_w_in, m_b_f_bias, m_b_q_norm, m_b_k_norm, m_b_w_out, m_c_norm, m_c_w_in, m_c_conv_w, m_c_w_out, v_a_norm, v_a_w_in, v_a_conv_w, v_a_conv_b, v_a_ln_g, v_a_ln_b, v_a_w_out, v_b_norm, v_b_w_in, v_b_f_bias, v_b_q_norm, v_b_k_norm, v_b_w_out, v_c_norm, v_c_w_in, v_c_conv_w, v_c_w_out):
    t, d = x.shape[1], x.shape[2]
    e = a_w_out.shape[1] * N_DEV
    heads = b_f_bias.shape[1]
    n_a, n_b, n_c = a_norm.shape[0], b_norm.shape[0], c_norm.shape[0]
    depth = n_a + n_b + n_c
    ka, kc = a_conv_w.shape[1], c_conv_w.shape[1]
    assert e == heads * HEAD_DIM and n_b == 1 and n_c == 1 and x.shape[0] == 1

    layers = [(i % 3, i // 3) for i in range(depth)]

    def mat_shards(kind, j):
        w_in, w_out = ((a_w_in, a_w_out), (b_w_in, b_w_out), (c_w_in, c_w_out))[kind]
        return [w_in[j].astype(BF16), w_out[j].astype(BF16)]

    def full_mats(kind, gathered):
        w_in, w_out = _unshard(gathered[0], 1), _unshard(gathered[1], 0)
        return (jnp.pad(w_in, ((0, 0), (0, LANES - heads))) if kind == 1 else w_in), w_out

    small_names = ["a_norm", "a_conv_w", "a_conv_b", "a_ln_g", "a_ln_b", "c_norm", "c_conv_w"]
    small = dict(a_norm=a_norm, a_conv_w=a_conv_w, a_conv_b=a_conv_b, a_ln_g=a_ln_g, a_ln_b=a_ln_b,
                 c_norm=c_norm, c_conv_w=c_conv_w)
    small_pack = _pack_rows([small[k] for k in small_names], 0)
    first = _all_gather(mat_shards(*layers[0]) + [small_pack], "all_gather_first_layer")
    sm = _unpack_rows(first[2], [small[k].shape for k in small_names], 1)
    g_a_norm = _unshard(sm[0], 1)
    g_a_conv_w = _unshard(sm[1], 2)
    g_a_conv_b = _unshard(sm[2], 1)
    g_a_ln_g = _unshard(sm[3], 1)
    g_a_ln_b = _unshard(sm[4], 1)
    g_c_norm = _unshard(sm[5], 1)
    g_c_conv_w = _unshard(sm[6], 2)

    cw_a = [_pad_rows(g_a_conv_w[j], CONF_HALO) for j in range(n_a)]
    cw_c = _pad_rows(g_c_conv_w[0], SHORT_HALO)
    qg = jnp.tile(b_q_norm, (1, heads))
    kg = jnp.tile(b_k_norm, (1, heads))
    fbias = jnp.pad(b_f_bias, ((0, 0), (0, LANES - heads)))

    h = x[0]
    saved, weights = [], [full_mats(layers[0][0], first[:2])]
    for i, (kind, j) in enumerate(layers):
        tag = f"l{i}"
        w_in, w_out = weights[i]
        ride = _Ride("gather", mat_shards(*layers[i + 1])) if i + 1 < depth else None
        if kind == 0:
            hn, proj = _norm_matmul(h, g_a_norm[j:j + 1], w_in, f"a_in_proj_{tag}")
            y, u1, *rode = _a_mid_fwd(proj, cw_a[j], g_a_conv_b[j:j + 1], g_a_ln_g[j:j + 1], g_a_ln_b[j:j + 1], ka,
                                      f"a_mid_fwd_{tag}", ride=ride)
            saved.append(dict(x=h, hn=hn, proj=proj, u1=u1, y=y))
        elif kind == 1:
            y, sv, rode = _b_fwd(h, b_norm, w_in, w_out, qg, kg, fbias, heads, tag, ride=ride)
            saved.append(sv)
        else:
            hn, proj, *rode = _norm_matmul(h, g_c_norm, w_in, f"c_in_proj_{tag}", ride=ride)
            y = _c_mid_fwd(proj, cw_c, kc, f"c_mid_fwd_{tag}")
            saved.append(dict(x=h, hn=hn, proj=proj, y=y))
        if ride:
            weights.append(full_mats(layers[i + 1][0], rode))
            h = _out_matmul_residual(y, w_out, h, f"{'abc'[kind]}_out_proj_{tag}")
        else:
            loss_part, dh = _out_matmul_loss(y, w_out, h, loss_target[0], f"{'abc'[kind]}_out_proj_loss_{tag}")
    loss_local = jnp.sum(loss_part).reshape(1, 1)

    ga = dict(norm=[None] * n_a, conv_w=[None] * n_a, conv_b=[None] * n_a, ln_g=[None] * n_a, ln_b=[None] * n_a)
    gb, gc = None, {}
    recv_mats = [None] * depth
    pending = None
    for i in reversed(range(depth)):
        kind, j = layers[i]
        tag = f"l{i}"
        sv = saved[i]
        w_in, w_out = weights[i]
        ride = _Ride("exchange", pending) if pending is not None else None
        if kind == 0:
            dyz = _matmul_nt(dh, w_out, f"a_out_bwd_{tag}")
            gw_out = _matmul_tn(sv["y"], dh, f"a_out_wgrad_{tag}", tm=2048)
            mid_ride, ride = (ride, None) if i == 0 else (None, ride)
            dproj, dcw, dcb, dlg, dlb, *mid_rode = _a_mid_bwd(sv["proj"], sv["u1"], dyz, cw_a[j], g_a_ln_g[j:j + 1],
                                                              g_a_ln_b[j:j + 1], ka, f"a_mid_bwd_{tag}", ride=mid_ride)
            gw_in = _matmul_tn(sv["hn"], dproj, f"a_in_wgrad_{tag}")
            if i == 0:
                ride = _Ride("exchange", [_to_slabs(gw_in, 1), _to_slabs(gw_out, 0)])
            dh, dg, *rode = _dproj_matmul_normbwd(dproj, w_in, sv["x"], g_a_norm[j:j + 1], dh, f"a_in_bwd_{tag}", ride=ride)
            if i == 0:
                if mid_ride:
                    recv_mats[1] = mid_rode
                recv_mats[0], ride = rode, None
            ga["norm"][j], ga["conv_w"][j], ga["conv_b"][j], ga["ln_g"][j], ga["ln_b"][j] = dg[0], dcw[:ka], dcb[0], dlg[0], dlb[0]
        elif kind == 1:
            dh, gb, rode = _b_bwd(dh, sv, b_norm, w_in, w_out, qg, kg, fbias, heads, tag, ride=ride)
            gw_in, gw_out = gb["w_in"], gb["w_out"]
        else:
            dyz = _matmul_nt(dh, w_out, f"c_out_bwd_{tag}")
            gw_out = _matmul_tn(sv["y"], dh, f"c_out_wgrad_{tag}", tm=2048)
            dproj, dcw = _c_mid_bwd(sv["proj"], dyz, cw_c, kc, f"c_mid_bwd_{tag}")
            gw_in = _matmul_tn(sv["hn"], dproj, f"c_in_wgrad_{tag}")
            dh, dg, *rode = _dproj_matmul_normbwd(dproj, w_in, sv["x"], g_c_norm, dh, f"c_in_bwd_{tag}", ride=ride)
            gc["norm"], gc["conv_w"] = dg, dcw[:kc][None]
        if ride:
            recv_mats[i + 1] = rode
        pending = [_to_slabs(gw_in, 1), _to_slabs(gw_out, 0)]
    grad_x = dh[None]

    sharded_small = [(jnp.stack(ga["norm"]), 1), (jnp.stack(ga["conv_w"]), 2), (jnp.stack(ga["conv_b"]), 1),
                     (jnp.stack(ga["ln_g"]), 1), (jnp.stack(ga["ln_b"]), 1), (gc["norm"], 1), (gc["conv_w"], 2)]
    repl_small = [gb["norm"], gb["f_bias"], gb["q_norm"], gb["k_norm"], loss_local]
    small_slabs = _pack_rows([_to_slabs(g, ax) for g, ax in sharded_small]
                             + [jnp.broadcast_to(g[None], (N_DEV,) + g.shape) for g in repl_small], 1)
    recv_small, = _exchange([small_slabs], "exchange_vector_gradients")

    outs = {}
    mat_w = dict(a_w_in=(a_w_in, m_a_w_in, v_a_w_in), a_w_out=(a_w_out, m_a_w_out, v_a_w_out),
                 b_w_in=(b_w_in, m_b_w_in, v_b_w_in), b_w_out=(b_w_out, m_b_w_out, v_b_w_out),
                 c_w_in=(c_w_in, m_c_w_in, v_c_w_in), c_w_out=(c_w_out, m_c_w_out, v_c_w_out))
    for kind, prefix in enumerate("abc"):
        members = [i for i, (k, _) in enumerate(layers) if k == kind]
        for which, name in enumerate((f"{prefix}_w_in", f"{prefix}_w_out")):
            recv = jnp.stack([recv_mats[i][which] for i in members], axis=1)
            outs[name] = _sum_adamw(recv, *mat_w[name], f"adamw_{name}")

    small_order = small_names + ["b_norm", "b_f_bias", "b_q_norm", "b_k_norm", "loss"]
    no_state = jnp.zeros((1, 1), F32)
    small_w = dict(a_norm=(a_norm, m_a_norm, v_a_norm), a_conv_w=(a_conv_w, m_a_conv_w, v_a_conv_w),
                   a_conv_b=(a_conv_b, m_a_conv_b, v_a_conv_b), a_ln_g=(a_ln_g, m_a_ln_g, v_a_ln_g),
                   a_ln_b=(a_ln_b, m_a_ln_b, v_a_ln_b), c_norm=(c_norm, m_c_norm, v_c_norm),
                   c_conv_w=(c_conv_w, m_c_conv_w, v_c_conv_w), b_norm=(b_norm, m_b_norm, v_b_norm),
                   b_f_bias=(b_f_bias, m_b_f_bias, v_b_f_bias), b_q_norm=(b_q_norm, m_b_q_norm, v_b_q_norm),
                   b_k_norm=(b_k_norm, m_b_k_norm, v_b_k_norm), loss=(no_state, no_state, no_state))
    packs = [_pack_rows([small_w[k][q] for k in small_order], 0)[None] for q in range(3)]
    small_out = _sum_adamw(recv_small[:, None], *packs, "adamw_vectors")
    shapes = [small_w[k][0].shape for k in small_order]
    unpacked = [_unpack_rows(o[0], shapes, 0) for o in small_out]
    for idx, name in enumerate(small_order):
        outs[name] = tuple(unpacked[q][idx] for q in range(4))

    loss = outs["loss"][0].reshape(())
    order = ["a_norm", "a_w_in", "a_conv_w", "a_conv_b", "a_ln_g", "a_ln_b", "a_w_out", "b_norm", "b_w_in", "b_f_bias",
             "b_q_norm", "b_k_norm", "b_w_out", "c_norm", "c_w_in", "c_conv_w", "c_w_out"]
    return (loss, grad_x, *[outs[k][0] for k in order], *[outs[k][1] for k in order],
            *[outs[k][2] for k in order], *[outs[k][3] for k in order])
```

```python
import jax
import jax.numpy as jnp
import numpy as np
from jax import lax
from jax.experimental import pallas as pl
from jax.experimental.pallas import tpu as pltpu

F32 = jnp.float32
BF16 = jnp.bfloat16
SDS = jax.ShapeDtypeStruct

NORM_EPS = 1e-6
ADAM_LR = 0.001
ADAM_B1 = 0.9
ADAM_B2 = 0.999
ADAM_EPS = 1e-08
ADAM_WD = 0.01
ADAM_STEP = 10

N_DEV = 8
LANES = 128
SUBLANES = 8
HEAD_DIM = 128
CONF_HALO = 32
SHORT_HALO = 8
VMEM_LIMIT = 56 * 1024 * 1024

NT_DIMS = (((1,), (1,)), ((), ()))
TN_DIMS = (((0,), (0,)), ((), ()))
MESH = pl.DeviceIdType.MESH
ANY = pl.BlockSpec(memory_space=pl.ANY)


def _cparams(*sem):
    return pltpu.CompilerParams(dimension_semantics=sem, vmem_limit_bytes=VMEM_LIMIT)


def _divisor_tile(n, cap):
    return max(m for m in range(LANES, min(n, cap) + 1, LANES) if n % m == 0)


def _sigmoid(x):
    return 1.0 / (1.0 + jnp.exp(-x))


def _silu(x):
    return x * _sigmoid(x)


def _dsilu(x):
    s = _sigmoid(x)
    return s * (1.0 + x * (1.0 - s))


def _rows8(v):
    out = v[0:SUBLANES]
    for a in range(1, v.shape[0] // SUBLANES):
        out = out + v[a * SUBLANES:(a + 1) * SUBLANES]
    return out


def _split3(v):
    hi = v.astype(BF16)
    r1 = v - hi.astype(F32)
    mid = r1.astype(BF16)
    lo = (r1 - mid.astype(F32)).astype(BF16)
    return hi, mid, lo


def _tri_matmul(tri, v):
    hi, mid, lo = _split3(v)
    return (jnp.dot(tri, hi, preferred_element_type=F32) + jnp.dot(tri, mid, preferred_element_type=F32)
            + jnp.dot(tri, lo, preferred_element_type=F32))


def _position():
    return lax.axis_index("x"), lax.axis_index("y"), lax.axis_index("c")


def _all_gather(shards, name):
    n = len(shards)

    def body(*refs):
        xs, outs = refs[:n], refs[n:2 * n]
        send_sems, recv_sems, local_sems = refs[2 * n:]
        x, y, c = _position()
        me, sibling = (x, y, c), (x, y, 1 - c)
        chips = [(1 - x, y), (x, 1 - y), (1 - x, 1 - y)]

        def slot(a, px, py, pc):
            return outs[a].at[4 * px + 2 * py + pc]

        def copy(a, k, block, to, src=None):
            return pltpu.make_async_remote_copy(
                src_ref=slot(a, *block) if src is None else src, dst_ref=slot(a, *block),
                send_sem=send_sems.at[a, k], recv_sem=recv_sems.at[a, k], device_id=to, device_id_type=MESH)

        started = []
        mine = []
        for a in range(n):
            cp = pltpu.make_async_copy(xs[a], slot(a, *me), local_sems.at[a])
            cp.start()
            mine.append(cp)
        for a in range(n):
            first = [copy(a, 0, me, sibling, src=xs[a])]
            first += [copy(a, 1 + j, me, (*chip, c), src=xs[a]) for j, chip in enumerate(chips)]
            for cp in first:
                cp.start()
            started += first
        for a in range(n):
            for j, chip in enumerate(chips):
                copy(a, 1 + j, (*chip, c), me).wait_recv()
                fwd = copy(a, 4 + j, (*chip, c), sibling)
                fwd.start()
                started.append(fwd)
        for a in range(n):
            copy(a, 0, sibling, me).wait_recv()
            for j, chip in enumerate(chips):
                copy(a, 4 + j, (*chip, 1 - c), me).wait_recv()
        for cp in started:
            cp.wait_send()
        for cp in mine:
            cp.wait()

    return pl.pallas_call(
        body, name=name,
        out_shape=[SDS((N_DEV,) + s.shape, s.dtype) for s in shards],
        in_specs=[ANY] * n, out_specs=[ANY] * n,
        scratch_shapes=[pltpu.SemaphoreType.DMA((n, 7)), pltpu.SemaphoreType.DMA((n, 7)), pltpu.SemaphoreType.DMA((n,))],
    )(*shards)


def _exchange(slabs, name):
    n = len(slabs)

    def body(*refs):
        ins, outs = refs[:n], refs[n:2 * n]
        send_sems, recv_sems, local_sems = refs[2 * n:]
        x, y, c = _position()
        me = 4 * x + 2 * y + c
        peers = [(x ^ bx, y ^ by, c ^ bc) for bx in (0, 1) for by in (0, 1) for bc in (0, 1)][1:]

        def copy(a, k, peer):
            pid = 4 * peer[0] + 2 * peer[1] + peer[2]
            return pltpu.make_async_remote_copy(
                src_ref=ins[a].at[pid], dst_ref=outs[a].at[me],
                send_sem=send_sems.at[a, k], recv_sem=recv_sems.at[a, k], device_id=peer, device_id_type=MESH)

        def arrival(a, k, peer):
            pid = 4 * peer[0] + 2 * peer[1] + peer[2]
            return pltpu.make_async_remote_copy(
                src_ref=ins[a].at[pid], dst_ref=outs[a].at[pid],
                send_sem=send_sems.at[a, k], recv_sem=recv_sems.at[a, k], device_id=peer, device_id_type=MESH)

        mine = []
        for a in range(n):
            cp = pltpu.make_async_copy(ins[a].at[me], outs[a].at[me], local_sems.at[a])
            cp.start()
            mine.append(cp)
        started = []
        for a in range(n):
            for k, peer in enumerate(peers):
                cp = copy(a, k, peer)
                cp.start()
                started.append(cp)
        for a in range(n):
            for k, peer in enumerate(peers):
                arrival(a, k, peer).wait_recv()
        for cp in started:
            cp.wait_send()
        for cp in mine:
            cp.wait()

    return pl.pallas_call(
        body, name=name,
        out_shape=[SDS(s.shape, s.dtype) for s in slabs],
        in_specs=[ANY] * n, out_specs=[ANY] * n,
        scratch_shapes=[pltpu.SemaphoreType.DMA((n, 7)), pltpu.SemaphoreType.DMA((n, 7)), pltpu.SemaphoreType.DMA((n,))],
    )(*slabs)


class _Ride:
    def __init__(self, kind, arrays):
        self.kind, self.arrays, self.n = kind, list(arrays), len(arrays)
        self.in_specs = [ANY] * self.n
        self.out_specs = [ANY] * self.n
        self.out_shape = [SDS(((N_DEV,) + a.shape) if kind == "gather" else a.shape, a.dtype) for a in self.arrays]
        self.scratch = [pltpu.SemaphoreType.DMA((self.n, 7)), pltpu.SemaphoreType.DMA((self.n, 7)),
                        pltpu.SemaphoreType.DMA((self.n,))]

    def _copies(self, ins, outs, sems, arriving):
        send_sems, recv_sems, local_sems = sems
        x, y, c = _position()
        me = 4 * x + 2 * y + c
        peers = [(x ^ bx, y ^ by, c ^ bc) for bx in (0, 1) for by in (0, 1) for bc in (0, 1)][1:]
        local, remote = [], []
        for a in range(self.n):
            own = ins[a] if self.kind == "gather" else ins[a].at[me]
            local.append(pltpu.make_async_copy(own, outs[a].at[me], local_sems.at[a]))
            for k, peer in enumerate(peers):
                pid = 4 * peer[0] + 2 * peer[1] + peer[2]
                remote.append(pltpu.make_async_remote_copy(
                    src_ref=ins[a] if self.kind == "gather" else ins[a].at[pid],
                    dst_ref=outs[a].at[pid if arriving else me],
                    send_sem=send_sems.at[a, k], recv_sem=recv_sems.at[a, k], device_id=peer, device_id_type=MESH))
        return local, remote

    def start(self, ins, outs, sems):
        local, sends = self._copies(ins, outs, sems, False)
        for cp in local + sends:
            cp.start()

    def wait(self, ins, outs, sems):
        local, arrivals = self._copies(ins, outs, sems, True)
        for cp in arrivals:
            cp.wait_recv()
        for cp in arrivals:
            cp.wait_send()
        for cp in local:
            cp.wait()


def _norm_matmul(x, g, w, name, tm=2048, tn=1024, ride=None, n_cols=None):
    t, d = x.shape
    n = n_cols or w.shape[1]
    tm, tn = min(tm, t), min(tn, n)
    ni, nj = t // tm, n // tn
    nr = ride.n if ride else 0

    def body(*refs):
        x_ref, g_ref, w_ref = refs[:3]
        hn_ref, o_ref = refs[3 + nr:5 + nr]
        rin, rout, sems = refs[3:3 + nr], refs[5 + nr:5 + 2 * nr], refs[5 + 2 * nr:]
        i, j = pl.program_id(0), pl.program_id(1)
        if ride:
            @pl.when((i == 0) & (j == 0))
            def _():
                ride.start(rin, rout, sems)

        @pl.when(j == 0)
        def _():
            xf = x_ref[...]
            r = lax.rsqrt(jnp.mean(xf * xf, axis=-1, keepdims=True) + NORM_EPS)
            hn_ref[...] = ((xf * r) * g_ref[...]).astype(BF16)

        o_ref[...] = jnp.dot(hn_ref[...], w_ref[...], preferred_element_type=F32).astype(o_ref.dtype)
        if ride:
            @pl.when((i == ni - 1) & (j == nj - 1))
            def _():
                ride.wait(rin, rout, sems)

    return pl.pallas_call(
        body, name=name, grid=(ni, nj),
        in_specs=[pl.BlockSpec((tm, d), lambda i, j: (i, 0)), pl.BlockSpec((1, d), lambda i, j: (0, 0)),
                  pl.BlockSpec((d, tn), lambda i, j: (0, j))] + (ride.in_specs if ride else []),
        out_specs=[pl.BlockSpec((tm, d), lambda i, j: (i, 0)), pl.BlockSpec((tm, tn), lambda i, j: (i, j))]
        + (ride.out_specs if ride else []),
        out_shape=[SDS((t, d), BF16), SDS((t, n), BF16)] + (ride.out_shape if ride else []),
        scratch_shapes=ride.scratch if ride else [],
        compiler_params=_cparams("arbitrary", "arbitrary") if ride else _cparams("parallel", "arbitrary"),
    )(x, g, w, *(ride.arrays if ride else []))


def _matmul_f32out(a, w, name, tm=512):
    t, k = a.shape
    n = w.shape[1]

    def body(a_ref, w_ref, o_ref):
        o_ref[...] = jnp.dot(a_ref[...], w_ref[...], preferred_element_type=F32)

    return pl.pallas_call(
        body, name=name, grid=(t // tm,),
        in_specs=[pl.BlockSpec((tm, k), lambda i: (i, 0)), pl.BlockSpec((k, n), lambda i: (0, 0))],
        out_specs=pl.BlockSpec((tm, n), lambda i: (i, 0)),
        out_shape=SDS((t, n), F32),
        compiler_params=_cparams("parallel"),
    )(a, w)


def _out_matmul_residual(y, w, x, name, tm=1024):
    t, e = y.shape
    d = w.shape[1]
    tm = min(tm, t)

    def body(y_ref, w_ref, x_ref, o_ref):
        o_ref[...] = x_ref[...] + jnp.dot(y_ref[...], w_ref[...], preferred_element_type=F32)

    return pl.pallas_call(
        body, name=name, grid=(t // tm,),
        in_specs=[pl.BlockSpec((tm, e), lambda i: (i, 0)), pl.BlockSpec((e, d), lambda i: (0, 0)),
                  pl.BlockSpec((tm, d), lambda i: (i, 0))],
        out_specs=pl.BlockSpec((tm, d), lambda i: (i, 0)),
        out_shape=SDS((t, d), F32),
        compiler_params=_cparams("parallel"),
    )(y, w, x)


def _matmul_nt(a, w, name, tm=1024):
    t, d = a.shape
    e = w.shape[0]
    tm = min(tm, t)

    def body(a_ref, w_ref, o_ref):
        o_ref[...] = lax.dot_general(a_ref[...].astype(BF16), w_ref[...], NT_DIMS,
                                     preferred_element_type=F32).astype(o_ref.dtype)

    return pl.pallas_call(
        body, name=name, grid=(t // tm,),
        in_specs=[pl.BlockSpec((tm, d), lambda i: (i, 0)), pl.BlockSpec((e, d), lambda i: (0, 0))],
        out_specs=pl.BlockSpec((tm, e), lambda i: (i, 0)),
        out_shape=SDS((t, e), BF16),
        compiler_params=_cparams("parallel"),
    )(a, w)


def _matmul_tn(a, b, name, out_dtype=BF16, tm=1024, tn=1024, tk=1024):
    t, m = a.shape
    n = b.shape[1]
    tm, tn, tk = min(tm, m), _divisor_tile(n, 2 * tn), min(tk, t)
    nk = t // tk

    def body(a_ref, b_ref, o_ref, acc_ref):
        k = pl.program_id(2)

        @pl.when(k == 0)
        def _():
            acc_ref[...] = jnp.zeros_like(acc_ref)

        acc_ref[...] += lax.dot_general(a_ref[...].astype(BF16), b_ref[...].astype(BF16), TN_DIMS,
                                        preferred_element_type=F32)

        @pl.when(k == nk - 1)
        def _():
            o_ref[...] = acc_ref[...].astype(o_ref.dtype)

    return pl.pallas_call(
        body, name=name, grid=(m // tm, n // tn, nk),
        in_specs=[pl.BlockSpec((tk, tm), lambda i, j, k: (k, i)), pl.BlockSpec((tk, tn), lambda i, j, k: (k, j))],
        out_specs=pl.BlockSpec((tm, tn), lambda i, j, k: (i, j)),
        out_shape=SDS((m, n), out_dtype),
        scratch_shapes=[pltpu.VMEM((tm, tn), F32)],
        compiler_params=_cparams("parallel", "parallel", "arbitrary"),
    )(a, b)


def _dproj_matmul_normbwd(dproj, w, x, g, dxn, name, tm=1024, tk=1024, ride=None):
    t, n = dproj.shape
    d = w.shape[0]
    tm, tk = min(tm, t), _divisor_tile(n, tk)
    nk = n // tk
    ni = t // tm
    nr = ride.n if ride else 0

    def body(*refs):
        dp_ref, w_ref, x_ref, g_ref, dxn_ref = refs[:5]
        dx_ref, dg_ref = refs[5 + nr:7 + nr]
        rin, rout = refs[5:5 + nr], refs[7 + nr:7 + 2 * nr]
        sems, acc_ref = refs[7 + 2 * nr:-1], refs[-1]
        i, k = pl.program_id(0), pl.program_id(1)
        if ride:
            @pl.when((i == 0) & (k == 0))
            def _():
                ride.start(rin, rout, sems)

        @pl.when(k == 0)
        def _():
            acc_ref[...] = jnp.zeros_like(acc_ref)

        acc_ref[...] += lax.dot_general(dp_ref[...], w_ref[...], NT_DIMS, preferred_element_type=F32)

        @pl.when(k == nk - 1)
        def _():
            dhn = acc_ref[...]
            xf = x_ref[...]
            r = lax.rsqrt(jnp.mean(xf * xf, axis=-1, keepdims=True) + NORM_EPS)
            xh = xf * r
            dy = dhn * g_ref[...]
            dx_ref[...] = dxn_ref[...] + r * (dy - xh * jnp.mean(dy * xh, axis=-1, keepdims=True))
            part = jnp.sum(dhn * xh, axis=0, keepdims=True)

            @pl.when(i == 0)
            def _():
                dg_ref[...] = part

            @pl.when(i > 0)
            def _():
                dg_ref[...] += part

        if ride:
            @pl.when((i == ni - 1) & (k == nk - 1))
            def _():
                ride.wait(rin, rout, sems)

    return pl.pallas_call(
        body, name=name, grid=(ni, nk),
        in_specs=[pl.BlockSpec((tm, tk), lambda i, k: (i, k)), pl.BlockSpec((d, tk), lambda i, k: (0, k)),
                  pl.BlockSpec((tm, d), lambda i, k: (i, 0)), pl.BlockSpec((1, d), lambda i, k: (0, 0)),
                  pl.BlockSpec((tm, d), lambda i, k: (i, 0))] + (ride.in_specs if ride else []),
        out_specs=[pl.BlockSpec((tm, d), lambda i, k: (i, 0)), pl.BlockSpec((1, d), lambda i, k: (0, 0))]
        + (ride.out_specs if ride else []),
        out_shape=[SDS((t, d), F32), SDS((1, d), F32)] + (ride.out_shape if ride else []),
        scratch_shapes=(ride.scratch if ride else []) + [pltpu.VMEM((tm, d), F32)],
        compiler_params=_cparams("arbitrary", "arbitrary"),
    )(dproj, w, x, g, dxn, *(ride.arrays if ride else []))


def _out_matmul_loss(y, w, x, target, name, tm=1024):
    t, e = y.shape
    d = w.shape[1]
    tm = min(tm, t)
    inv_d = 1.0 / d

    def body(y_ref, w_ref, x_ref, t_ref, part_ref, dy_ref):
        i = pl.program_id(0)
        err = (x_ref[...] + jnp.dot(y_ref[...], w_ref[...], preferred_element_type=F32)) - t_ref[...]
        dy_ref[...] = err * inv_d
        part = jnp.sum(err * err, axis=0, keepdims=True) * (0.5 * inv_d)

        @pl.when(i == 0)
        def _():
            part_ref[...] = part

        @pl.when(i > 0)
        def _():
            part_ref[...] += part

    return pl.pallas_call(
        body, name=name, grid=(t // tm,),
        in_specs=[pl.BlockSpec((tm, e), lambda i: (i, 0)), pl.BlockSpec((e, d), lambda i: (0, 0)),
                  pl.BlockSpec((tm, d), lambda i: (i, 0)), pl.BlockSpec((tm, d), lambda i: (i, 0))],
        out_specs=[pl.BlockSpec((1, d), lambda i: (0, 0)), pl.BlockSpec((tm, d), lambda i: (i, 0))],
        out_shape=[SDS((1, d), F32), SDS((t, d), F32)],
        compiler_params=_cparams("arbitrary"),
    )(y, w, x, target)


CONV_ROWS = 32
CONV_COLS = 512


def _conv_chunk(src_ref, base, w_ref, width, r0, c0, flip):
    acc = None
    for k in range(width):
        off = base + r0 + ((width - 1 - k) if flip else (k - (width - 1)))
        term = src_ref[pl.ds(off, CONV_ROWS), pl.ds(c0, CONV_COLS)] * w_ref[pl.ds(k, 1), pl.ds(c0, CONV_COLS)]
        acc = term if acc is None else acc + term
    return acc


def _conv_weight_grad(dw_ref, d_ref, src_ref, base, width, tt, e):
    for c0 in range(0, e, CONV_COLS):
        for k in range(width):
            acc = None
            for r0 in range(0, tt, CONV_ROWS):
                prod = (d_ref[pl.ds(r0, CONV_ROWS), pl.ds(c0, CONV_COLS)]
                        * src_ref[pl.ds(base + r0 - (width - 1) + k, CONV_ROWS), pl.ds(c0, CONV_COLS)])
                part = _rows8(prod)
                acc = part if acc is None else acc + part
            dw_ref[pl.ds(k, 1), pl.ds(c0, CONV_COLS)] += jnp.sum(acc, axis=0, keepdims=True)


def _shifted_copies(dst_ref, src_ref, c0, length, sign):
    lo, hi = (SUBLANES, length) if sign < 0 else (0, length - SUBLANES)
    for b in range(SUBLANES):
        for r0 in range(lo, hi, CONV_ROWS):
            n = min(CONV_ROWS, hi - r0)
            dst_ref[b, pl.ds(r0, n), :] = src_ref[pl.ds(r0 + sign * b, n), pl.ds(c0, CONV_COLS)]


def _conv_aligned(copies_ref, base, w_ref, width, r0, c0, sign):
    acc = None
    for d in range(width):
        a, b = divmod(d, SUBLANES)
        term = (copies_ref[b, pl.ds(base + r0 + sign * SUBLANES * a, CONV_ROWS), :]
                * w_ref[pl.ds(width - 1 - d, 1), pl.ds(c0, CONV_COLS)])
        acc = term if acc is None else acc + term
    return acc


def _conv_weight_grad_aligned(dw_ref, dcopies_ref, src_ref, base, width, tt, c0):
    for d in range(width):
        a, b = divmod(d, SUBLANES)
        acc = None
        for r0 in range(0, tt, CONV_ROWS):
            prod = (dcopies_ref[b, pl.ds(r0, CONV_ROWS), :]
                    * src_ref[pl.ds(base + r0 - SUBLANES * a, CONV_ROWS), pl.ds(c0, CONV_COLS)])
            part = _rows8(prod)
            acc = part if acc is None else acc + part
        dw_ref[pl.ds(width - 1 - d, 1), pl.ds(c0, CONV_COLS)] += jnp.sum(acc, axis=0, keepdims=True)


LN_ROWS = 16


def _a_mid_fwd(proj, cw, cb, lg, lb, width, name, tt=256, ride=None):
    t, e3 = proj.shape
    e = e3 // 3
    halo = CONF_HALO
    nt = t // tt
    nr = ride.n if ride else 0

    def body(*refs):
        rin, rout = refs[5:5 + nr], refs[7 + nr:7 + 2 * nr]
        scratch = refs[7 + 2 * nr:]
        sems, own = (scratch[:3], scratch[3:]) if ride else ((), scratch)
        if ride:
            @pl.when(pl.program_id(0) == 0)
            def _():
                ride.start(rin, rout, sems)

        tile(*refs[:5], *refs[5 + nr:7 + nr], *own)
        if ride:
            @pl.when(pl.program_id(0) == nt - 1)
            def _():
                ride.wait(rin, rout, sems)

    def tile(p_ref, cw_ref, cb_ref, lg_ref, lb_ref, y_ref, u1_ref, ubuf, shifted):
        i = pl.program_id(0)

        @pl.when(i == 0)
        def _():
            ubuf[pl.ds(0, halo), :] = jnp.zeros((halo, e), F32)

        @pl.when(i > 0)
        def _():
            ubuf[pl.ds(0, halo), :] = ubuf[pl.ds(tt, halo), :]

        for r0 in range(0, tt, CONV_ROWS):
            val = p_ref[pl.ds(r0, CONV_ROWS), pl.ds(0, e)].astype(F32)
            gate = p_ref[pl.ds(r0, CONV_ROWS), pl.ds(e, e)].astype(F32)
            ubuf[pl.ds(halo + r0, CONV_ROWS), :] = val * _sigmoid(gate)
        for c0 in range(0, e, CONV_COLS):
            _shifted_copies(shifted, ubuf, c0, halo + tt, -1)
            for r0 in range(0, tt, CONV_ROWS):
                acc = _conv_aligned(shifted, halo, cw_ref, width, r0, c0, -1)
                u1_ref[pl.ds(r0, CONV_ROWS), pl.ds(c0, CONV_COLS)] = acc + cb_ref[:, pl.ds(c0, CONV_COLS)]
        for r0 in range(0, tt, LN_ROWS):
            u = u1_ref[pl.ds(r0, LN_ROWS), :]
            mu = jnp.mean(u, axis=-1, keepdims=True)
            dlt = u - mu
            var = jnp.mean(dlt * dlt, axis=-1, keepdims=True)
            u2 = (dlt * lax.rsqrt(var + NORM_EPS)) * lg_ref[...] + lb_ref[...]
            z = p_ref[pl.ds(r0, LN_ROWS), pl.ds(2 * e, e)].astype(F32)
            y_ref[pl.ds(r0, LN_ROWS), :] = (_silu(u2) * _silu(z)).astype(BF16)

    return pl.pallas_call(
        body, name=name, grid=(nt,),
        in_specs=[pl.BlockSpec((tt, e3), lambda i: (i, 0)), pl.BlockSpec(cw.shape, lambda i: (0, 0)),
                  pl.BlockSpec((1, e), lambda i: (0, 0)), pl.BlockSpec((1, e), lambda i: (0, 0)),
                  pl.BlockSpec((1, e), lambda i: (0, 0))] + (ride.in_specs if ride else []),
        out_specs=[pl.BlockSpec((tt, e), lambda i: (i, 0)), pl.BlockSpec((tt, e), lambda i: (i, 0))]
        + (ride.out_specs if ride else []),
        out_shape=[SDS((t, e), BF16), SDS((t, e), F32)] + (ride.out_shape if ride else []),
        scratch_shapes=(ride.scratch if ride else [])
        + [pltpu.VMEM((halo + tt, e), F32), pltpu.VMEM((SUBLANES, halo + tt, CONV_COLS), F32)],
        compiler_params=_cparams("arbitrary"),
    )(proj, cw, cb, lg, lb, *(ride.arrays if ride else []))


def _a_mid_bwd(proj, u1, dyz, cw, lg, lb, width, name, tt=256, ride=None):
    t, e3 = proj.shape
    e = e3 // 3
    halo = CONF_HALO
    nt = t // tt
    hb = tt // halo
    nr = ride.n if ride else 0

    def body(*refs):
        rin, rout = refs[7:7 + nr], refs[12 + nr:12 + 2 * nr]
        scratch = refs[12 + 2 * nr:]
        sems, own = (scratch[:3], scratch[3:]) if ride else ((), scratch)
        if ride:
            @pl.when(pl.program_id(0) == 0)
            def _():
                ride.start(rin, rout, sems)

        tile(*refs[:7], *refs[7 + nr:12 + nr], *own)
        if ride:
            @pl.when(pl.program_id(0) == nt - 1)
            def _():
                ride.wait(rin, rout, sems)

    def tile(p_ref, pp_ref, u1_ref, dy_ref, cw_ref, lg_ref, lb_ref,
             dp_ref, dcw_ref, dcb_ref, dlg_ref, dlb_ref, ubuf, dbuf, shifted, acc_cb, acc_lg, acc_lb):
        i = pl.program_id(0)
        ti = nt - 1 - i

        @pl.when(i == 0)
        def _():
            dbuf[pl.ds(tt, halo), :] = jnp.zeros((halo, e), F32)
            dcw_ref[...] = jnp.zeros_like(dcw_ref)
            acc_cb[...] = jnp.zeros_like(acc_cb)
            acc_lg[...] = jnp.zeros_like(acc_lg)
            acc_lb[...] = jnp.zeros_like(acc_lb)

        @pl.when(i > 0)
        def _():
            dbuf[pl.ds(tt, halo), :] = dbuf[pl.ds(0, halo), :]

        keep = (ti > 0).astype(F32)
        ubuf[pl.ds(0, halo), :] = keep * (pp_ref[:, pl.ds(0, e)].astype(F32) * _sigmoid(pp_ref[:, pl.ds(e, e)].astype(F32)))
        for r0 in range(0, tt, CONV_ROWS):
            val = p_ref[pl.ds(r0, CONV_ROWS), pl.ds(0, e)].astype(F32)
            gate = p_ref[pl.ds(r0, CONV_ROWS), pl.ds(e, e)].astype(F32)
            ubuf[pl.ds(halo + r0, CONV_ROWS), :] = val * _sigmoid(gate)

        for r0 in range(0, tt, LN_ROWS):
            rows = pl.ds(r0, LN_ROWS)
            u = u1_ref[rows, :]
            mu = jnp.mean(u, axis=-1, keepdims=True)
            dlt = u - mu
            var = jnp.mean(dlt * dlt, axis=-1, keepdims=True)
            rstd = lax.rsqrt(var + NORM_EPS)
            xh = dlt * rstd
            u2 = xh * lg_ref[...] + lb_ref[...]
            s2 = _sigmoid(u2)
            u3 = u2 * s2
            z = p_ref[rows, pl.ds(2 * e, e)].astype(F32)
            sz = _sigmoid(z)
            dy = dy_ref[rows, :].astype(F32)
            dp_ref[rows, pl.ds(2 * e, e)] = (dy * u3 * (sz * (1.0 + z * (1.0 - sz)))).astype(BF16)
            du2 = (dy * (z * sz)) * (s2 * (1.0 + u2 * (1.0 - s2)))
            acc_lg[...] += _rows8(du2 * xh)
            acc_lb[...] += _rows8(du2)
            dxh = du2 * lg_ref[...]
            m1 = jnp.mean(dxh, axis=-1, keepdims=True)
            m2 = jnp.mean(dxh * xh, axis=-1, keepdims=True)
            du1 = rstd * (dxh - m1 - xh * m2)
            dbuf[rows, :] = du1
            acc_cb[...] += _rows8(du1)

        for c0 in range(0, e, CONV_COLS):
            _shifted_copies(shifted, dbuf, c0, tt + halo, 1)
            for r0 in range(0, tt, CONV_ROWS):
                du0 = _conv_aligned(shifted, 0, cw_ref, width, r0, c0, 1)
                rows, cols = pl.ds(r0, CONV_ROWS), pl.ds(c0, CONV_COLS)
                val = p_ref[rows, cols].astype(F32)
                sg = _sigmoid(p_ref[rows, pl.ds(e + c0, CONV_COLS)].astype(F32))
                dp_ref[rows, cols] = (du0 * sg).astype(BF16)
                dp_ref[rows, pl.ds(e + c0, CONV_COLS)] = (du0 * val * sg * (1.0 - sg)).astype(BF16)
            _conv_weight_grad_aligned(dcw_ref, shifted, ubuf, halo, width, tt, c0)

        @pl.when(i == nt - 1)
        def _():
            dcb_ref[...] = jnp.sum(acc_cb[...], axis=0, keepdims=True)
            dlg_ref[...] = jnp.sum(acc_lg[...], axis=0, keepdims=True)
            dlb_ref[...] = jnp.sum(acc_lb[...], axis=0, keepdims=True)

    vec = pl.BlockSpec((1, e), lambda i: (0, 0))
    return pl.pallas_call(
        body, name=name, grid=(nt,),
        in_specs=[pl.BlockSpec((tt, e3), lambda i: (nt - 1 - i, 0)),
                  pl.BlockSpec((halo, e3), lambda i: (jnp.maximum((nt - 1 - i) * hb - 1, 0), 0)),
                  pl.BlockSpec((tt, e), lambda i: (nt - 1 - i, 0)), pl.BlockSpec((tt, e), lambda i: (nt - 1 - i, 0)),
                  pl.BlockSpec(cw.shape, lambda i: (0, 0)), vec, vec] + (ride.in_specs if ride else []),
        out_specs=[pl.BlockSpec((tt, e3), lambda i: (nt - 1 - i, 0)), pl.BlockSpec(cw.shape, lambda i: (0, 0)), vec, vec, vec]
        + (ride.out_specs if ride else []),
        out_shape=[SDS((t, e3), BF16), SDS(cw.shape, F32), SDS((1, e), F32), SDS((1, e), F32), SDS((1, e), F32)]
        + (ride.out_shape if ride else []),
        scratch_shapes=(ride.scratch if ride else [])
        + [pltpu.VMEM((halo + tt, e), F32), pltpu.VMEM((tt + halo, e), F32),
           pltpu.VMEM((SUBLANES, halo + tt, CONV_COLS), F32),
           pltpu.VMEM((SUBLANES, e), F32), pltpu.VMEM((SUBLANES, e), F32), pltpu.VMEM((SUBLANES, e), F32)],
        compiler_params=_cparams("arbitrary"),
    )(proj, proj, u1, dyz, cw, lg, lb, *(ride.arrays if ride else []))


def _c_mid_fwd(proj, cw, width, name, tt=256):
    t, e4 = proj.shape
    e = e4 // 4
    halo = SHORT_HALO

    def body(p_ref, cw_ref, y_ref, wbuf):
        i = pl.program_id(0)

        @pl.when(i == 0)
        def _():
            wbuf[pl.ds(0, halo), :] = jnp.zeros((halo, e), F32)

        @pl.when(i > 0)
        def _():
            wbuf[pl.ds(0, halo), :] = wbuf[pl.ds(tt, halo), :]

        for r0 in range(0, tt, CONV_ROWS):
            rows = pl.ds(r0, CONV_ROWS)
            wbuf[pl.ds(halo + r0, CONV_ROWS), :] = p_ref[rows, pl.ds(2 * e, e)].astype(F32) * p_ref[rows, pl.ds(0, e)].astype(F32)
        for c0 in range(0, e, CONV_COLS):
            for r0 in range(0, tt, CONV_ROWS):
                rows = pl.ds(r0, CONV_ROWS)
                cv = _conv_chunk(wbuf, halo, cw_ref, width, r0, c0, False)
                bg = p_ref[rows, pl.ds(e + c0, CONV_COLS)].astype(F32)
                z = p_ref[rows, pl.ds(3 * e + c0, CONV_COLS)].astype(F32)
                y_ref[rows, pl.ds(c0, CONV_COLS)] = ((bg * cv) * _silu(z)).astype(BF16)

    return pl.pallas_call(
        body, name=name, grid=(t // tt,),
        in_specs=[pl.BlockSpec((tt, e4), lambda i: (i, 0)), pl.BlockSpec(cw.shape, lambda i: (0, 0))],
        out_specs=pl.BlockSpec((tt, e), lambda i: (i, 0)),
        out_shape=SDS((t, e), BF16),
        scratch_shapes=[pltpu.VMEM((halo + tt, e), F32)],
        compiler_params=_cparams("arbitrary"),
    )(proj, cw)


def _c_mid_bwd(proj, dyz, cw, width, name, tt=256):
    t, e4 = proj.shape
    e = e4 // 4
    halo = SHORT_HALO
    nt = t // tt
    hb = tt // halo

    def body(p_ref, pp_ref, dy_ref, cw_ref, dp_ref, dcw_ref, wbuf, dbuf):
        i = pl.program_id(0)
        ti = nt - 1 - i

        @pl.when(i == 0)
        def _():
            dbuf[pl.ds(tt, halo), :] = jnp.zeros((halo, e), F32)
            dcw_ref[...] = jnp.zeros_like(dcw_ref)

        @pl.when(i > 0)
        def _():
            dbuf[pl.ds(tt, halo), :] = dbuf[pl.ds(0, halo), :]

        keep = (ti > 0).astype(F32)
        wbuf[pl.ds(0, halo), :] = keep * (pp_ref[:, pl.ds(2 * e, e)].astype(F32) * pp_ref[:, pl.ds(0, e)].astype(F32))
        for r0 in range(0, tt, CONV_ROWS):
            rows = pl.ds(r0, CONV_ROWS)
            wbuf[pl.ds(halo + r0, CONV_ROWS), :] = p_ref[rows, pl.ds(2 * e, e)].astype(F32) * p_ref[rows, pl.ds(0, e)].astype(F32)
        for c0 in range(0, e, CONV_COLS):
            for r0 in range(0, tt, CONV_ROWS):
                rows, cols = pl.ds(r0, CONV_ROWS), pl.ds(c0, CONV_COLS)
                cv = _conv_chunk(wbuf, halo, cw_ref, width, r0, c0, False)
                bg = p_ref[rows, pl.ds(e + c0, CONV_COLS)].astype(F32)
                z = p_ref[rows, pl.ds(3 * e + c0, CONV_COLS)].astype(F32)
                sz = _sigmoid(z)
                dyz_c = dy_ref[rows, cols].astype(F32)
                dy = dyz_c * (z * sz)
                dp_ref[rows, pl.ds(3 * e + c0, CONV_COLS)] = (dyz_c * (bg * cv) * (sz * (1.0 + z * (1.0 - sz)))).astype(BF16)
                dp_ref[rows, pl.ds(e + c0, CONV_COLS)] = (dy * cv).astype(BF16)
                dbuf[rows, cols] = dy * bg
        for c0 in range(0, e, CONV_COLS):
            for r0 in range(0, tt, CONV_ROWS):
                rows, cols = pl.ds(r0, CONV_ROWS), pl.ds(c0, CONV_COLS)
                dw = _conv_chunk(dbuf, 0, cw_ref, width, r0, c0, True)
                dp_ref[rows, pl.ds(2 * e + c0, CONV_COLS)] = (dw * p_ref[rows, cols].astype(F32)).astype(BF16)
                dp_ref[rows, cols] = (dw * p_ref[rows, pl.ds(2 * e + c0, CONV_COLS)].astype(F32)).astype(BF16)
        _conv_weight_grad(dcw_ref, dbuf, wbuf, halo, width, tt, e)

    return pl.pallas_call(
        body, name=name, grid=(nt,),
        in_specs=[pl.BlockSpec((tt, e4), lambda i: (nt - 1 - i, 0)),
                  pl.BlockSpec((halo, e4), lambda i: (jnp.maximum((nt - 1 - i) * hb - 1, 0), 0)),
                  pl.BlockSpec((tt, e), lambda i: (nt - 1 - i, 0)), pl.BlockSpec(cw.shape, lambda i: (0, 0))],
        out_specs=[pl.BlockSpec((tt, e4), lambda i: (nt - 1 - i, 0)), pl.BlockSpec(cw.shape, lambda i: (0, 0))],
        out_shape=[SDS((t, e4), BF16), SDS(cw.shape, F32)],
        scratch_shapes=[pltpu.VMEM((halo + tt, e), F32), pltpu.VMEM((tt + halo, e), F32)],
        compiler_params=_cparams("arbitrary"),
    )(proj, proj, dyz, cw)


def _b_prep_fwd(proj, flog, fbias, qg, kg, heads, name, tt=256):
    t, e4 = proj.shape
    e = e4 // 4
    scale = HEAD_DIM ** -0.5 * LOG2E

    def body(q_ref, k_ref, fl_ref, fb_ref, qg_ref, kg_ref, qs_ref, kn_ref, knt_ref, c_ref, ct_ref, carry):
        i = pl.program_id(0)

        @pl.when(i == 0)
        def _():
            carry[...] = jnp.zeros_like(carry)

        for h in range(heads):
            cols = pl.ds(h * HEAD_DIM, HEAD_DIM)
            qh = q_ref[:, cols].astype(F32)
            r = lax.rsqrt(jnp.mean(qh * qh, axis=-1, keepdims=True) + NORM_EPS)
            qs_ref[:, cols] = (((qh * r) * qg_ref[:, cols]) * scale).astype(BF16)
            kh = k_ref[:, cols].astype(F32)
            r = lax.rsqrt(jnp.mean(kh * kh, axis=-1, keepdims=True) + NORM_EPS)
            kn = (kh * r) * kg_ref[:, cols]
            kn_ref[:, cols] = kn.astype(BF16)
            knt_ref[cols, :] = kn.T.astype(BF16)

        a = fl_ref[...] + fb_ref[...]
        lf = jnp.minimum(a, 0.0) - jnp.log(1.0 + jnp.exp(-jnp.abs(a)))
        tri = (lax.broadcasted_iota(jnp.int32, (tt, tt), 0) >= lax.broadcasted_iota(jnp.int32, (tt, tt), 1)).astype(BF16)
        c = _tri_matmul(tri, lf) + carry[...]
        c_ref[...] = c
        ct_ref[...] = (c * LOG2E).T
        carry[...] = c_ref[pl.ds(tt - 1, 1), :]

    return pl.pallas_call(
        body, name=name, grid=(t // tt,),
        in_specs=[pl.BlockSpec((tt, e), lambda i: (i, 0)), pl.BlockSpec((tt, e), lambda i: (i, 1)),
                  pl.BlockSpec((tt, LANES), lambda i: (i, 0)), pl.BlockSpec((1, LANES), lambda i: (0, 0)),
                  pl.BlockSpec((1, e), lambda i: (0, 0)), pl.BlockSpec((1, e), lambda i: (0, 0))],
        out_specs=[pl.BlockSpec((tt, e), lambda i: (i, 0)), pl.BlockSpec((tt, e), lambda i: (i, 0)),
                   pl.BlockSpec((e, tt), lambda i: (0, i)),
                   pl.BlockSpec((tt, LANES), lambda i: (i, 0)), pl.BlockSpec((LANES, tt), lambda i: (0, i))],
        out_shape=[SDS((t, e), BF16), SDS((t, e), BF16), SDS((e, t), BF16), SDS((t, LANES), F32), SDS((LANES, t), F32)],
        scratch_shapes=[pltpu.VMEM((1, LANES), F32)],
        compiler_params=_cparams("arbitrary"),
    )(proj, proj, flog, fbias, qg, kg)


ATT_BLOCK = 1024
ATT_CHUNK_FWD = 512
ATT_CHUNK_BWD = 256
NEG_BIG = -1e30
LOG2E = 1.4426950408889634
LN2 = 0.6931471805599453


def _flash_fwd(qs, kn, proj, ck, heads, name):
    t, e = qs.shape
    blk = min(ATT_BLOCK, t)
    cw = min(ATT_CHUNK_FWD, blk // 2)
    nq, nch = t // blk, blk // cw
    assert nch % 2 == 0

    def body(q_ref, k_ref, v_ref, ck_ref, z_ref, o_ref, y_ref, m_ref, l_ref, s_a, s_b):
        i = pl.program_id(1)
        q = q_ref[...]
        bufs = (s_a, s_b)

        def key_rows(j, c):
            return pl.ds(pl.multiple_of(j * blk, blk) + c * cw, cw)

        def logits(j, c):
            bufs[c % 2][...] = (lax.dot_general(q, k_ref[key_rows(j, c), :], NT_DIMS, preferred_element_type=F32)
                            - ck_ref[j][:, c * cw:(c + 1) * cw])

        def weights(c, m, masked):
            s = bufs[c % 2][...]
            if masked:
                keep = lax.broadcasted_iota(jnp.int32, (blk, cw), 0) >= (lax.broadcasted_iota(jnp.int32, (blk, cw), 1) + c * cw)
                s = jnp.where(keep, s, NEG_BIG)
            m_new = jnp.maximum(m, jnp.ceil(jnp.max(s, axis=-1, keepdims=True)))
            return m_new, jnp.exp2(m - m_new), jnp.exp2(s - m_new).astype(BF16)

        ones = jnp.ones((cw, HEAD_DIM), BF16)

        def block(j, carry, masked):
            m, acc = carry
            for c in range(nch):
                if c + 1 < nch:
                    logits(j, c + 1)
                elif not masked:
                    logits(j + 1, 0)
                m, alpha, p = weights(c, m, masked)
                v1 = jnp.concatenate([v_ref[key_rows(j, c), :], ones], axis=1)
                acc = alpha * acc + jnp.dot(p, v1, preferred_element_type=F32)
            return m, acc

        logits(0, 0)
        carry = (jnp.full((blk, 1), NEG_BIG, F32), jnp.zeros((blk, 2 * HEAD_DIM), F32))
        carry = lax.fori_loop(0, i, lambda j, cr: block(j, cr, False), carry)
        m, acc = block(i, carry, True)
        l = acc[:, HEAD_DIM:HEAD_DIM + 1]
        o = acc[:, :HEAD_DIM] / l
        o_ref[...] = o
        y_ref[...] = (o * _silu(z_ref[...].astype(F32))).astype(BF16)
        m_ref[...] = jnp.broadcast_to(m, (blk, LANES))
        l_ref[...] = jnp.broadcast_to(l, (blk, LANES))

    head_all = pl.BlockSpec((t, HEAD_DIM), lambda h, i: (0, h))
    tile = pl.BlockSpec((blk, HEAD_DIM), lambda h, i: (i, h))
    stat = pl.BlockSpec((None, blk, LANES), lambda h, i: (h, i, 0))
    return pl.pallas_call(
        body, name=name, grid=(heads, nq),
        in_specs=[tile, head_all, pl.BlockSpec((t, HEAD_DIM), lambda h, i: (0, 2 * heads + h)),
                  pl.BlockSpec((None, nq, 1, blk), lambda h, i: (h, 0, 0, 0)),
                  pl.BlockSpec((blk, HEAD_DIM), lambda h, i: (i, 3 * heads + h))],
        out_specs=[tile, tile, stat, stat],
        out_shape=[SDS((t, e), F32), SDS((t, e), BF16), SDS((heads, t, LANES), F32), SDS((heads, t, LANES), F32)],
        scratch_shapes=[pltpu.VMEM((blk, cw), F32), pltpu.VMEM((blk, cw), F32)],
        compiler_params=_cparams("parallel", "arbitrary"),
    )(qs, kn, proj, ck, proj)


def _flash_bwd(qs, kn, knt, proj, ck, dyz, o, mstat, lstat, heads, name):
    t, e = qs.shape
    blk = min(ATT_BLOCK, t)
    cw = min(ATT_CHUNK_BWD, blk // 2)
    nq, nch = t // blk, blk // cw
    assert nch % 2 == 0

    def body(q_ref, dy_ref, o_ref, z_ref, m_ref, l_ref, k_ref, kt_ref, v_ref, ck_ref,
             dq_ref, dk_ref, dv_ref, dc_ref, dz_ref, s_a, s_b, d_a, d_b):
        i = pl.program_id(1)

        @pl.when(i == 0)
        def _():
            dk_ref[...] = jnp.zeros_like(dk_ref)
            dv_ref[...] = jnp.zeros_like(dv_ref)
            dc_ref[...] = jnp.zeros_like(dc_ref)

        z = z_ref[...].astype(F32)
        sz = _sigmoid(z)
        dy = dy_ref[...].astype(F32)
        of = o_ref[...]
        do = ((dy * (z * sz)) / l_ref[:, 0:1]).astype(BF16)
        dz_ref[...] = (dy * of * (sz * (1.0 + z * (1.0 - sz)))).astype(BF16)
        dl = jnp.sum(do.astype(F32) * of, axis=-1, keepdims=True)
        q = q_ref[...]
        q_t = q.astype(F32).T.astype(BF16)
        do_t = do.astype(F32).T.astype(BF16)
        mrow = m_ref[:, 0:1]
        sbuf, dbuf = (s_a, s_b), (d_a, d_b)

        def key_rows(j, c):
            return pl.ds(pl.multiple_of(j * blk, blk) + c * cw, cw)

        def products(j, c):
            rows = key_rows(j, c)
            sbuf[c % 2][...] = (lax.dot_general(q, k_ref[rows, :], NT_DIMS, preferred_element_type=F32)
                            - ck_ref[j][:, c * cw:(c + 1) * cw])
            dbuf[c % 2][...] = lax.dot_general(do, v_ref[rows, :], NT_DIMS, preferred_element_type=F32)

        def weights(c, masked):
            p = jnp.exp2(sbuf[c % 2][...] - mrow)
            if masked:
                keep = lax.broadcasted_iota(jnp.int32, (blk, cw), 0) >= (lax.broadcasted_iota(jnp.int32, (blk, cw), 1) + c * cw)
                p = jnp.where(keep, p, 0.0)
            p = p.astype(BF16)
            ds = p.astype(F32) * (dbuf[c % 2][...] - dl)
            return p, ds.astype(BF16), jnp.sum(ds, axis=0, keepdims=True)

        def outputs(j, c, p, dsb, colsum, dq):
            rows = key_rows(j, c)
            dv_ref[:, rows] += jnp.dot(do_t, p, preferred_element_type=F32)
            dk_ref[:, rows] += jnp.dot(q_t, dsb, preferred_element_type=F32)
            dc_ref[j, :, pl.ds(c * cw, cw)] -= colsum
            return dq + lax.dot_general(kt_ref[:, rows], dsb, NT_DIMS, preferred_element_type=F32)

        def block(j, dq, masked):
            products(j, 1)
            for c in range(nch):
                p, dsb, colsum = weights(c, masked)
                if c + 2 < nch:
                    products(j, c + 2)
                elif c + 2 == nch and not masked:
                    products(j + 1, 0)
                dq = outputs(j, c, p, dsb, colsum, dq)
            return dq

        products(0, 0)
        dq = lax.fori_loop(0, i, lambda j, acc: block(j, acc, False), jnp.zeros((HEAD_DIM, blk), F32))
        dq_ref[...] = block(i, dq, True).T

    tile = pl.BlockSpec((blk, HEAD_DIM), lambda h, i: (i, h))
    stat = pl.BlockSpec((None, blk, LANES), lambda h, i: (h, i, 0))
    head_all = pl.BlockSpec((t, HEAD_DIM), lambda h, i: (0, h))
    head_all_t = pl.BlockSpec((HEAD_DIM, t), lambda h, i: (h, 0))
    cspec = pl.BlockSpec((None, nq, 1, blk), lambda h, i: (h, 0, 0, 0))
    return pl.pallas_call(
        body, name=name, grid=(heads, nq),
        in_specs=[tile, tile, tile, pl.BlockSpec((blk, HEAD_DIM), lambda h, i: (i, 3 * heads + h)), stat, stat, head_all,
                  head_all_t, pl.BlockSpec((t, HEAD_DIM), lambda h, i: (0, 2 * heads + h)), cspec],
        out_specs=[tile, head_all_t, head_all_t, cspec, tile],
        out_shape=[SDS((t, e), F32), SDS((e, t), F32), SDS((e, t), F32), SDS((heads, nq, 1, blk), F32), SDS((t, e), BF16)],
        scratch_shapes=[pltpu.VMEM((blk, cw), F32)] * 4,
        compiler_params=_cparams("parallel", "arbitrary"),
    )(qs, dyz, o, proj, mstat, lstat, kn, knt, proj, ck)


def _b_prep_bwd(dqs, dknt, dvt, dz, proj, qg, kg, dct, flog, fbias, heads, name, tt=256):
    t, e4 = proj.shape
    e = e4 // 4
    nt = t // tt
    scale = HEAD_DIM ** -0.5

    def body(dq_ref, dkt_ref, dvt_ref, dz_ref, q_ref, k_ref, qg_ref, kg_ref, dc_ref, fl_ref, fb_ref,
             dp_ref, dqg_ref, dkg_ref, dfb_ref, carry, dlf, acc_q, acc_k, acc_f):
        i = pl.program_id(0)

        @pl.when(i == 0)
        def _():
            carry[...] = jnp.zeros_like(carry)
            acc_q[...] = jnp.zeros_like(acc_q)
            acc_k[...] = jnp.zeros_like(acc_k)
            acc_f[...] = jnp.zeros_like(acc_f)

        for h in range(heads):
            cols = pl.ds(h * HEAD_DIM, HEAD_DIM)
            for src_ref, d, g_ref, acc, mult, off in ((q_ref, dq_ref[:, cols], qg_ref, acc_q, scale, 0),
                                                      (k_ref, dkt_ref[cols, :].T, kg_ref, acc_k, LN2, e)):
                xf = src_ref[:, cols].astype(F32)
                r = lax.rsqrt(jnp.mean(xf * xf, axis=-1, keepdims=True) + NORM_EPS)
                xh = xf * r
                dn = d * mult
                acc[...] += _rows8(dn * xh)
                dxh = dn * g_ref[:, cols]
                dp_ref[:, pl.ds(off + h * HEAD_DIM, HEAD_DIM)] = (
                    r * (dxh - xh * jnp.mean(dxh * xh, axis=-1, keepdims=True))).astype(BF16)
                if off:
                    dp_ref[:, pl.ds(2 * e + h * HEAD_DIM, HEAD_DIM)] = dvt_ref[cols, :].T.astype(BF16)
        dp_ref[:, pl.ds(3 * e, e)] = dz_ref[...]

        tri = (lax.broadcasted_iota(jnp.int32, (tt, tt), 0) <= lax.broadcasted_iota(jnp.int32, (tt, tt), 1)).astype(BF16)
        dlf[...] = _tri_matmul(tri, dc_ref[...]) + carry[...]
        carry[...] = dlf[pl.ds(0, 1), :]
        a = fl_ref[...] + fb_ref[...]
        dfl = dlf[...] * _sigmoid(-a)
        dp_ref[:, pl.ds(4 * e, LANES)] = dfl.astype(BF16)
        acc_f[...] += _rows8(dfl)

        @pl.when(i == nt - 1)
        def _():
            dqg_ref[...] = jnp.sum(acc_q[...], axis=0, keepdims=True)
            dkg_ref[...] = jnp.sum(acc_k[...], axis=0, keepdims=True)
            dfb_ref[...] = jnp.sum(acc_f[...], axis=0, keepdims=True)

    rev = lambda i: (nt - 1 - i, 0)
    vec_e = pl.BlockSpec((1, e), lambda i: (0, 0))
    vec = pl.BlockSpec((1, LANES), lambda i: (0, 0))
    wide = pl.BlockSpec((tt, e), rev)
    tall = pl.BlockSpec((e, tt), lambda i: (0, nt - 1 - i))
    lane = pl.BlockSpec((tt, LANES), rev)
    return pl.pallas_call(
        body, name=name, grid=(nt,),
        in_specs=[wide, tall, tall, wide, wide, pl.BlockSpec((tt, e), lambda i: (nt - 1 - i, 1)), vec_e, vec_e, lane, lane, vec],
        out_specs=[pl.BlockSpec((tt, e4 + LANES), rev), vec, vec, vec],
        out_shape=[SDS((t, e4 + LANES), BF16), SDS((1, LANES), F32), SDS((1, LANES), F32), SDS((1, LANES), F32)],
        scratch_shapes=[pltpu.VMEM((1, LANES), F32), pltpu.VMEM((tt, LANES), F32), pltpu.VMEM((SUBLANES, LANES), F32),
                        pltpu.VMEM((SUBLANES, LANES), F32), pltpu.VMEM((SUBLANES, LANES), F32)],
        compiler_params=_cparams("arbitrary"),
    )(dqs, dknt, dvt, dz, proj, proj, qg, kg, dct, flog, fbias)


def _b_fwd(h, b_norm, wb_pad, wb_out, qg, kg, fbias, heads, tag, ride=None):
    t = h.shape[0]
    e = wb_out.shape[0]
    blk = min(ATT_BLOCK, t)
    hn, proj, *rode = _norm_matmul(h, b_norm, wb_pad, f"b_in_proj_{tag}", ride=ride, n_cols=4 * e)
    flog = _matmul_f32out(hn, wb_pad[:, 4 * e:], f"b_forget_proj_{tag}")
    qs, kn, knt, _, ct = _b_prep_fwd(proj, flog, fbias, qg, kg, heads, f"b_prep_fwd_{tag}")
    ck = ct.reshape(LANES, t // blk, 1, blk)
    o, y, mstat, lstat = _flash_fwd(qs, kn, proj, ck, heads, f"b_attention_fwd_{tag}")
    sv = dict(x=h, hn=hn, proj=proj, flog=flog, qs=qs, kn=kn, knt=knt, ck=ck, o=o, y=y, mstat=mstat, lstat=lstat)
    return y, sv, rode


def _b_bwd(dh, sv, b_norm, wb_pad, wb_out, qg, kg, fbias, heads, tag, ride=None):
    t = dh.shape[0]
    e = wb_out.shape[0]
    gb = {}
    dyz = _matmul_nt(dh, wb_out, f"b_out_bwd_{tag}")
    gb["w_out"] = _matmul_tn(sv["y"], dh, f"b_out_wgrad_{tag}", tm=2048)
    dqs, dknt, dvt, dc, dz = _flash_bwd(sv["qs"], sv["kn"], sv["knt"], sv["proj"], sv["ck"], dyz, sv["o"], sv["mstat"],
                                        sv["lstat"], heads, f"b_attention_bwd_{tag}")
    dct = jnp.pad(dc.reshape(heads, t).T, ((0, 0), (0, LANES - heads)))
    dproj, dqg, dkg, dfb = _b_prep_bwd(dqs, dknt, dvt, dz, sv["proj"], qg, kg, dct, sv["flog"], fbias, heads,
                                       f"b_prep_bwd_{tag}")
    gb["w_in"] = _matmul_tn(sv["hn"], dproj, f"b_in_wgrad_{tag}")[:, :4 * e + heads]
    dh, dg, *rode = _dproj_matmul_normbwd(dproj, wb_pad, sv["x"], b_norm, dh, f"b_in_bwd_{tag}", ride=ride)
    gb["norm"], gb["q_norm"], gb["k_norm"], gb["f_bias"] = dg, dqg, dkg, dfb[:, :heads]
    return dh, gb, rode


def _sum_adamw(recv, w, m, v, name, tr=256):
    nl, r, c = w.shape
    tr = min(tr, r)

    def body(g_ref, w_ref, m_ref, v_ref, go_ref, d_ref, mo_ref, vo_ref):
        g = g_ref[0].astype(F32)
        for s in range(1, N_DEV):
            g = g + g_ref[s].astype(F32)
        go_ref[...] = g
        mn = ADAM_B1 * m_ref[...] + (1.0 - ADAM_B1) * g
        vn = ADAM_B2 * v_ref[...] + (1.0 - ADAM_B2) * (g * g)
        m_hat = mn / (1.0 - ADAM_B1 ** ADAM_STEP)
        v_hat = vn / (1.0 - ADAM_B2 ** ADAM_STEP)
        d_ref[...] = -ADAM_LR * (m_hat / (jnp.sqrt(v_hat) + ADAM_EPS) + ADAM_WD * w_ref[...])
        mo_ref[...] = mn
        vo_ref[...] = vn

    blk = pl.BlockSpec((None, tr, c), lambda l, i: (l, i, 0))
    return pl.pallas_call(
        body, name=name, grid=(nl, r // tr),
        in_specs=[pl.BlockSpec((N_DEV, None, tr, c), lambda l, i: (0, l, i, 0)), blk, blk, blk],
        out_specs=[blk, blk, blk, blk],
        out_shape=[SDS(w.shape, F32)] * 4,
        compiler_params=_cparams("parallel", "parallel"),
    )(recv, w, m, v)


def _unshard(g, axis):
    g = jnp.moveaxis(g, 0, axis)
    return g.reshape(g.shape[:axis] + (g.shape[axis] * g.shape[axis + 1],) + g.shape[axis + 2:])


def _to_slabs(full, axis):
    n = full.shape[axis]
    s = full.reshape(full.shape[:axis] + (N_DEV, n // N_DEV) + full.shape[axis + 1:])
    return jnp.moveaxis(s, axis, 0)


def _pack_rows(parts, lead):
    flat = [p.reshape(p.shape[:lead] + (-1,)) for p in parts]
    cat = jnp.concatenate(flat, axis=-1)
    n = cat.shape[-1]
    pad = (-n) % (SUBLANES * LANES)
    cat = jnp.pad(cat, [(0, 0)] * lead + [(0, pad)])
    return cat.reshape(cat.shape[:lead] + ((n + pad) // LANES, LANES))


def _unpack_rows(packed, shapes, lead):
    flat = packed.reshape(packed.shape[:lead] + (-1,))
    out, off = [], 0
    for shp in shapes:
        size = int(np.prod(shp))
        out.append(flat[..., off:off + size].reshape(packed.shape[:lead] + tuple(shp)))
        off += size
    return out


def _pad_rows(w, rows):
    return jnp.pad(w, ((0, rows - w.shape[0]), (0, 0)))


def kernel(x, a_norm, a_w_in, a_conv_w, a_conv_b, a_ln_g, a_ln_b, a_w_out, b_norm, b_w_in, b_f_bias, b_q_norm, b_k_norm, b_w_out, c_norm, c_w_in, c_conv_w, c_w_out, loss_target, m_a_norm, m_a_w_in, m_a_conv_w, m_a_conv_b, m_a_ln_g, m_a_ln_b, m_a_w_out, m_b_norm, m_b_w_in, m_b_f_bias, m_b_q_norm, m_b_k_norm, m_b_w_out, m_c_norm, m_c_w_in, m_c_conv_w, m_c_w_out, v_a_norm, v_a_w_in, v_a_conv_w, v_a_conv_b, v_a_ln_g, v_a_ln_b, v_a_w_out, v_b_norm, v_b_w_in, v_b_f_bias, v_b_q_norm, v_b_k_norm, v_b_w_out, v_c_norm, v_c_w_in, v_c_conv_w, v_c_w_out):
    t, d = x.shape[1], x.shape[2]
    e = a_w_out.shape[1] * N_DEV
    heads = b_f_bias.shape[1]
    n_a, n_b, n_c = a_norm.shape[0], b_norm.shape[0], c_norm.shape[0]
    depth = n_a + n_b + n_c
    ka, kc = a_conv_w.shape[1], c_conv_w.shape[1]
    assert e == heads * HEAD_DIM and n_b == 1 and n_c == 1 and x.shape[0] == 1

    layers = [(i % 3, i // 3) for i in range(depth)]

    def mat_shards(kind, j):
        w_in, w_out = ((a_w_in, a_w_out), (b_w_in, b_w_out), (c_w_in, c_w_out))[kind]
        return [w_in[j].astype(BF16), w_out[j].astype(BF16)]

    def full_mats(kind, gathered):
        w_in, w_out = _unshard(gathered[0], 1), _unshard(gathered[1], 0)
        return (jnp.pad(w_in, ((0, 0), (0, LANES - heads))) if kind == 1 else w_in), w_out

    small_names = ["a_norm", "a_conv_w", "a_conv_b", "a_ln_g", "a_ln_b", "c_norm", "c_conv_w"]
    small = dict(a_norm=a_norm, a_conv_w=a_conv_w, a_conv_b=a_conv_b, a_ln_g=a_ln_g, a_ln_b=a_ln_b,
                 c_norm=c_norm, c_conv_w=c_conv_w)
    small_pack = _pack_rows([small[k] for k in small_names], 0)
    first = _all_gather(mat_shards(*layers[0]) + [small_pack], "all_gather_first_layer")
    sm = _unpack_rows(first[2], [small[k].shape for k in small_names], 1)
    g_a_norm = _unshard(sm[0], 1)
    g_a_conv_w = _unshard(sm[1], 2)
    g_a_conv_b = _unshard(sm[2], 1)
    g_a_ln_g = _unshard(sm[3], 1)
    g_a_ln_b = _unshard(sm[4], 1)
    g_c_norm = _unshard(sm[5], 1)
    g_c_conv_w = _unshard(sm[6], 2)

    cw_a = [_pad_rows(g_a_conv_w[j], CONF_HALO) for j in range(n_a)]
    cw_c = _pad_rows(g_c_conv_w[0], SHORT_HALO)
    qg = jnp.tile(b_q_norm, (1, heads))
    kg = jnp.tile(b_k_norm, (1, heads))
    fbias = jnp.pad(b_f_bias, ((0, 0), (0, LANES - heads)))

    h = x[0]
    saved, weights = [], [full_mats(layers[0][0], first[:2])]
    for i, (kind, j) in enumerate(layers):
        tag = f"l{i}"
        w_in, w_out = weights[i]
        ride = _Ride("gather", mat_shards(*layers[i + 1])) if i + 1 < depth else None
        if kind == 0:
            hn, proj = _norm_matmul(h, g_a_norm[j:j + 1], w_in, f"a_in_proj_{tag}")
            y, u1, *rode = _a_mid_fwd(proj, cw_a[j], g_a_conv_b[j:j + 1], g_a_ln_g[j:j + 1], g_a_ln_b[j:j + 1], ka,
                                      f"a_mid_fwd_{tag}", ride=ride)
            saved.append(dict(x=h, hn=hn, proj=proj, u1=u1, y=y))
        elif kind == 1:
            y, sv, rode = _b_fwd(h, b_norm, w_in, w_out, qg, kg, fbias, heads, tag, ride=ride)
            saved.append(sv)
        else:
            hn, proj, *rode = _norm_matmul(h, g_c_norm, w_in, f"c_in_proj_{tag}", ride=ride)
            y = _c_mid_fwd(proj, cw_c, kc, f"c_mid_fwd_{tag}")
            saved.append(dict(x=h, hn=hn, proj=proj, y=y))
        if ride:
            weights.append(full_mats(layers[i + 1][0], rode))
            h = _out_matmul_residual(y, w_out, h, f"{'abc'[kind]}_out_proj_{tag}")
        else:
            loss_part, dh = _out_matmul_loss(y, w_out, h, loss_target[0], f"{'abc'[kind]}_out_proj_loss_{tag}")
    loss_local = jnp.sum(loss_part).reshape(1, 1)

    ga = dict(norm=[None] * n_a, conv_w=[None] * n_a, conv_b=[None] * n_a, ln_g=[None] * n_a, ln_b=[None] * n_a)
    gb, gc = None, {}
    recv_mats = [None] * depth
    pending = None
    for i in reversed(range(depth)):
        kind, j = layers[i]
        tag = f"l{i}"
        sv = saved[i]
        w_in, w_out = weights[i]
        ride = _Ride("exchange", pending) if pending is not None else None
        if kind == 0:
            dyz = _matmul_nt(dh, w_out, f"a_out_bwd_{tag}")
            gw_out = _matmul_tn(sv["y"], dh, f"a_out_wgrad_{tag}", tm=2048)
            mid_ride, ride = (ride, None) if i == 0 else (None, ride)
            dproj, dcw, dcb, dlg, dlb, *mid_rode = _a_mid_bwd(sv["proj"], sv["u1"], dyz, cw_a[j], g_a_ln_g[j:j + 1],
                                                              g_a_ln_b[j:j + 1], ka, f"a_mid_bwd_{tag}", ride=mid_ride)
            gw_in = _matmul_tn(sv["hn"], dproj, f"a_in_wgrad_{tag}")
            if i == 0:
                ride = _Ride("exchange", [_to_slabs(gw_in, 1), _to_slabs(gw_out, 0)])
            dh, dg, *rode = _dproj_matmul_normbwd(dproj, w_in, sv["x"], g_a_norm[j:j + 1], dh, f"a_in_bwd_{tag}", ride=ride)
            if i == 0:
                if mid_ride:
                    recv_mats[1] = mid_rode
                recv_mats[0], ride = rode, None
            ga["norm"][j], ga["conv_w"][j], ga["conv_b"][j], ga["ln_g"][j], ga["ln_b"][j] = dg[0], dcw[:ka], dcb[0], dlg[0], dlb[0]
        elif kind == 1:
            dh, gb, rode = _b_bwd(dh, sv, b_norm, w_in, w_out, qg, kg, fbias, heads, tag, ride=ride)
            gw_in, gw_out = gb["w_in"], gb["w_out"]
        else:
            dyz = _matmul_nt(dh, w_out, f"c_out_bwd_{tag}")
            gw_out = _matmul_tn(sv["y"], dh, f"c_out_wgrad_{tag}", tm=2048)
            dproj, dcw = _c_mid_bwd(sv["proj"], dyz, cw_c, kc, f"c_mid_bwd_{tag}")
            gw_in = _matmul_tn(sv["hn"], dproj, f"c_in_wgrad_{tag}")
            dh, dg, *rode = _dproj_matmul_normbwd(dproj, w_in, sv["x"], g_c_norm, dh, f"c_in_bwd_{tag}", ride=ride)
            gc["norm"], gc["conv_w"] = dg, dcw[:kc][None]
        if ride:
            recv_mats[i + 1] = rode
        pending = [_to_slabs(gw_in, 1), _to_slabs(gw_out, 0)]
    grad_x = dh[None]

    sharded_small = [(jnp.stack(ga["norm"]), 1), (jnp.stack(ga["conv_w"]), 2), (jnp.stack(ga["conv_b"]), 1),
                     (jnp.stack(ga["ln_g"]), 1), (jnp.stack(ga["ln_b"]), 1), (gc["norm"], 1), (gc["conv_w"], 2)]
    repl_small = [gb["norm"], gb["f_bias"], gb["q_norm"], gb["k_norm"], loss_local]
    small_slabs = _pack_rows([_to_slabs(g, ax) for g, ax in sharded_small]
                             + [jnp.broadcast_to(g[None], (N_DEV,) + g.shape) for g in repl_small], 1)
    recv_small, = _exchange([small_slabs], "exchange_vector_gradients")

    outs = {}
    mat_w = dict(a_w_in=(a_w_in, m_a_w_in, v_a_w_in), a_w_out=(a_w_out, m_a_w_out, v_a_w_out),
                 b_w_in=(b_w_in, m_b_w_in, v_b_w_in), b_w_out=(b_w_out, m_b_w_out, v_b_w_out),
                 c_w_in=(c_w_in, m_c_w_in, v_c_w_in), c_w_out=(c_w_out, m_c_w_out, v_c_w_out))
    for kind, prefix in enumerate("abc"):
        members = [i for i, (k, _) in enumerate(layers) if k == kind]
        for which, name in enumerate((f"{prefix}_w_in", f"{prefix}_w_out")):
            recv = jnp.stack([recv_mats[i][which] for i in members], axis=1)
            outs[name] = _sum_adamw(recv, *mat_w[name], f"adamw_{name}")

    small_order = small_names + ["b_norm", "b_f_bias", "b_q_norm", "b_k_norm", "loss"]
    no_state = jnp.zeros((1, 1), F32)
    small_w = dict(a_norm=(a_norm, m_a_norm, v_a_norm), a_conv_w=(a_conv_w, m_a_conv_w, v_a_conv_w),
                   a_conv_b=(a_conv_b, m_a_conv_b, v_a_conv_b), a_ln_g=(a_ln_g, m_a_ln_g, v_a_ln_g),
                   a_ln_b=(a_ln_b, m_a_ln_b, v_a_ln_b), c_norm=(c_norm, m_c_norm, v_c_norm),
                   c_conv_w=(c_conv_w, m_c_conv_w, v_c_conv_w), b_norm=(b_norm, m_b_norm, v_b_norm),
                   b_f_bias=(b_f_bias, m_b_f_bias, v_b_f_bias), b_q_norm=(b_q_norm, m_b_q_norm, v_b_q_norm),
                   b_k_norm=(b_k_norm, m_b_k_norm, v_b_k_norm), loss=(no_state, no_state, no_state))
    packs = [_pack_rows([small_w[k][q] for k in small_order], 0)[None] for q in range(3)]
    small_out = _sum_adamw(recv_small[:, None], *packs, "adamw_vectors")
    shapes = [small_w[k][0].shape for k in small_order]
    unpacked = [_unpack_rows(o[0], shapes, 0) for o in small_out]
    for idx, name in enumerate(small_order):
        outs[name] = tuple(unpacked[q][idx] for q in range(4))

    loss = outs["loss"][0].reshape(())
    order = ["a_norm", "a_w_in", "a_conv_w", "a_conv_b", "a_ln_g", "a_ln_b", "a_w_out", "b_norm", "b_w_in", "b_f_bias",
             "b_q_norm", "b_k_norm", "b_w_out", "c_norm", "c_w_in", "c_conv_w", "c_w_out"]
    return (loss, grad_x, *[outs[k][0] for k in order], *[outs[k][1] for k in order],
            *[outs[k][2] for k in order], *[outs[k][3] for k in order])
```

```python
import jax
import jax.numpy as jnp
import numpy as np
from jax import lax
from jax.experimental import pallas as pl
from jax.experimental.pallas import tpu as pltpu

F32 = jnp.float32
BF16 = jnp.bfloat16
SDS = jax.ShapeDtypeStruct

NORM_EPS = 1e-6
ADAM_LR = 0.001
ADAM_B1 = 0.9
ADAM_B2 = 0.999
ADAM_EPS = 1e-08
ADAM_WD = 0.01
ADAM_STEP = 10

N_DEV = 8
LANES = 128
SUBLANES = 8
HEAD_DIM = 128
CONF_HALO = 32
SHORT_HALO = 8
VMEM_LIMIT = 56 * 1024 * 1024

NT_DIMS = (((1,), (1,)), ((), ()))
TN_DIMS = (((0,), (0,)), ((), ()))
MESH = pl.DeviceIdType.MESH
ANY = pl.BlockSpec(memory_space=pl.ANY)


def _cparams(*sem):
    return pltpu.CompilerParams(dimension_semantics=sem, vmem_limit_bytes=VMEM_LIMIT)


def _divisor_tile(n, cap):
    return max(m for m in range(LANES, min(n, cap) + 1, LANES) if n % m == 0)


def _sigmoid(x):
    return 0.5 * jnp.tanh(0.5 * x) + 0.5


def _silu(x):
    return x * _sigmoid(x)


def _dsilu(x):
    s = _sigmoid(x)
    return s * (1.0 + x * (1.0 - s))


def _rows8(v):
    out = v[0:SUBLANES]
    for a in range(1, v.shape[0] // SUBLANES):
        out = out + v[a * SUBLANES:(a + 1) * SUBLANES]
    return out


def _split3(v):
    hi = v.astype(BF16)
    r1 = v - hi.astype(F32)
    mid = r1.astype(BF16)
    lo = (r1 - mid.astype(F32)).astype(BF16)
    return hi, mid, lo


def _tri_matmul(tri, v):
    hi, mid, lo = _split3(v)
    return (jnp.dot(tri, hi, preferred_element_type=F32) + jnp.dot(tri, mid, preferred_element_type=F32)
            + jnp.dot(tri, lo, preferred_element_type=F32))


def _position():
    return lax.axis_index("x"), lax.axis_index("y"), lax.axis_index("c")


def _all_gather(shards, name):
    n = len(shards)

    def body(*refs):
        xs, outs = refs[:n], refs[n:2 * n]
        send_sems, recv_sems, local_sems = refs[2 * n:]
        x, y, c = _position()
        me, sibling = (x, y, c), (x, y, 1 - c)
        chips = [(1 - x, y), (x, 1 - y), (1 - x, 1 - y)]

        def slot(a, px, py, pc):
            return outs[a].at[4 * px + 2 * py + pc]

        def copy(a, k, block, to, src=None):
            return pltpu.make_async_remote_copy(
                src_ref=slot(a, *block) if src is None else src, dst_ref=slot(a, *block),
                send_sem=send_sems.at[a, k], recv_sem=recv_sems.at[a, k], device_id=to, device_id_type=MESH)

        started = []
        mine = []
        for a in range(n):
            cp = pltpu.make_async_copy(xs[a], slot(a, *me), local_sems.at[a])
            cp.start()
            mine.append(cp)
        for a in range(n):
            first = [copy(a, 0, me, sibling, src=xs[a])]
            first += [copy(a, 1 + j, me, (*chip, c), src=xs[a]) for j, chip in enumerate(chips)]
            for cp in first:
                cp.start()
            started += first
        for a in range(n):
            for j, chip in enumerate(chips):
                copy(a, 1 + j, (*chip, c), me).wait_recv()
                fwd = copy(a, 4 + j, (*chip, c), sibling)
                fwd.start()
                started.append(fwd)
        for a in range(n):
            copy(a, 0, sibling, me).wait_recv()
            for j, chip in enumerate(chips):
                copy(a, 4 + j, (*chip, 1 - c), me).wait_recv()
        for cp in started:
            cp.wait_send()
        for cp in mine:
            cp.wait()

    return pl.pallas_call(
        body, name=name,
        out_shape=[SDS((N_DEV,) + s.shape, s.dtype) for s in shards],
        in_specs=[ANY] * n, out_specs=[ANY] * n,
        scratch_shapes=[pltpu.SemaphoreType.DMA((n, 7)), pltpu.SemaphoreType.DMA((n, 7)), pltpu.SemaphoreType.DMA((n,))],
    )(*shards)


def _exchange(slabs, name):
    n = len(slabs)

    def body(*refs):
        ins, outs = refs[:n], refs[n:2 * n]
        send_sems, recv_sems, local_sems = refs[2 * n:]
        x, y, c = _position()
        me = 4 * x + 2 * y + c
        peers = [(x ^ bx, y ^ by, c ^ bc) for bx in (0, 1) for by in (0, 1) for bc in (0, 1)][1:]

        def copy(a, k, peer):
            pid = 4 * peer[0] + 2 * peer[1] + peer[2]
            return pltpu.make_async_remote_copy(
                src_ref=ins[a].at[pid], dst_ref=outs[a].at[me],
                send_sem=send_sems.at[a, k], recv_sem=recv_sems.at[a, k], device_id=peer, device_id_type=MESH)

        def arrival(a, k, peer):
            pid = 4 * peer[0] + 2 * peer[1] + peer[2]
            return pltpu.make_async_remote_copy(
                src_ref=ins[a].at[pid], dst_ref=outs[a].at[pid],
                send_sem=send_sems.at[a, k], recv_sem=recv_sems.at[a, k], device_id=peer, device_id_type=MESH)

        mine = []
        for a in range(n):
            cp = pltpu.make_async_copy(ins[a].at[me], outs[a].at[me], local_sems.at[a])
            cp.start()
            mine.append(cp)
        started = []
        for a in range(n):
            for k, peer in enumerate(peers):
                cp = copy(a, k, peer)
                cp.start()
                started.append(cp)
        for a in range(n):
            for k, peer in enumerate(peers):
                arrival(a, k, peer).wait_recv()
        for cp in started:
            cp.wait_send()
        for cp in mine:
            cp.wait()

    return pl.pallas_call(
        body, name=name,
        out_shape=[SDS(s.shape, s.dtype) for s in slabs],
        in_specs=[ANY] * n, out_specs=[ANY] * n,
        scratch_shapes=[pltpu.SemaphoreType.DMA((n, 7)), pltpu.SemaphoreType.DMA((n, 7)), pltpu.SemaphoreType.DMA((n,))],
    )(*slabs)


class _Ride:
    def __init__(self, kind, arrays):
        self.kind, self.arrays, self.n = kind, list(arrays), len(arrays)
        self.in_specs = [ANY] * self.n
        self.out_specs = [ANY] * self.n
        self.out_shape = [SDS(((N_DEV,) + a.shape) if kind == "gather" else a.shape, a.dtype) for a in self.arrays]
        self.scratch = [pltpu.SemaphoreType.DMA((self.n, 7)), pltpu.SemaphoreType.DMA((self.n, 7)),
                        pltpu.SemaphoreType.DMA((self.n,))]

    def _copies(self, ins, outs, sems, arriving):
        send_sems, recv_sems, local_sems = sems
        x, y, c = _position()
        me = 4 * x + 2 * y + c
        peers = [(x ^ bx, y ^ by, c ^ bc) for bx in (0, 1) for by in (0, 1) for bc in (0, 1)][1:]
        local, remote = [], []
        for a in range(self.n):
            own = ins[a] if self.kind == "gather" else ins[a].at[me]
            local.append(pltpu.make_async_copy(own, outs[a].at[me], local_sems.at[a]))
            for k, peer in enumerate(peers):
                pid = 4 * peer[0] + 2 * peer[1] + peer[2]
                remote.append(pltpu.make_async_remote_copy(
                    src_ref=ins[a] if self.kind == "gather" else ins[a].at[pid],
                    dst_ref=outs[a].at[pid if arriving else me],
                    send_sem=send_sems.at[a, k], recv_sem=recv_sems.at[a, k], device_id=peer, device_id_type=MESH))
        return local, remote

    def start(self, ins, outs, sems):
        local, sends = self._copies(ins, outs, sems, False)
        for cp in local + sends:
            cp.start()

    def wait(self, ins, outs, sems):
        local, arrivals = self._copies(ins, outs, sems, True)
        for cp in arrivals:
            cp.wait_recv()
        for cp in arrivals:
            cp.wait_send()
        for cp in local:
            cp.wait()


def _norm_matmul(x, g, w, name, tm=2048, tn=1024, ride=None, n_cols=None):
    t, d = x.shape
    n = n_cols or w.shape[1]
    tm, tn = min(tm, t), min(tn, n)
    ni, nj = t // tm, n // tn
    nr = ride.n if ride else 0

    def body(*refs):
        x_ref, g_ref, w_ref = refs[:3]
        hn_ref, o_ref = refs[3 + nr:5 + nr]
        rin, rout, sems = refs[3:3 + nr], refs[5 + nr:5 + 2 * nr], refs[5 + 2 * nr:]
        i, j = pl.program_id(0), pl.program_id(1)
        if ride:
            @pl.when((i == 0) & (j == 0))
            def _():
                ride.start(rin, rout, sems)

        @pl.when(j == 0)
        def _():
            xf = x_ref[...]
            r = lax.rsqrt(jnp.mean(xf * xf, axis=-1, keepdims=True) + NORM_EPS)
            hn_ref[...] = ((xf * r) * g_ref[...]).astype(BF16)

        o_ref[...] = jnp.dot(hn_ref[...], w_ref[...], preferred_element_type=F32).astype(o_ref.dtype)
        if ride:
            @pl.when((i == ni - 1) & (j == nj - 1))
            def _():
                ride.wait(rin, rout, sems)

    return pl.pallas_call(
        body, name=name, grid=(ni, nj),
        in_specs=[pl.BlockSpec((tm, d), lambda i, j: (i, 0)), pl.BlockSpec((1, d), lambda i, j: (0, 0)),
                  pl.BlockSpec((d, tn), lambda i, j: (0, j))] + (ride.in_specs if ride else []),
        out_specs=[pl.BlockSpec((tm, d), lambda i, j: (i, 0)), pl.BlockSpec((tm, tn), lambda i, j: (i, j))]
        + (ride.out_specs if ride else []),
        out_shape=[SDS((t, d), BF16), SDS((t, n), BF16)] + (ride.out_shape if ride else []),
        scratch_shapes=ride.scratch if ride else [],
        compiler_params=_cparams("arbitrary", "arbitrary") if ride else _cparams("parallel", "arbitrary"),
    )(x, g, w, *(ride.arrays if ride else []))


def _matmul_f32out(a, w, name, tm=512):
    t, k = a.shape
    n = w.shape[1]

    def body(a_ref, w_ref, o_ref):
        o_ref[...] = jnp.dot(a_ref[...], w_ref[...], preferred_element_type=F32)

    return pl.pallas_call(
        body, name=name, grid=(t // tm,),
        in_specs=[pl.BlockSpec((tm, k), lambda i: (i, 0)), pl.BlockSpec((k, n), lambda i: (0, 0))],
        out_specs=pl.BlockSpec((tm, n), lambda i: (i, 0)),
        out_shape=SDS((t, n), F32),
        compiler_params=_cparams("parallel"),
    )(a, w)


def _out_matmul_residual(y, w, x, name, tm=1024):
    t, e = y.shape
    d = w.shape[1]
    tm = min(tm, t)

    def body(y_ref, w_ref, x_ref, o_ref):
        o_ref[...] = x_ref[...] + jnp.dot(y_ref[...], w_ref[...], preferred_element_type=F32)

    return pl.pallas_call(
        body, name=name, grid=(t // tm,),
        in_specs=[pl.BlockSpec((tm, e), lambda i: (i, 0)), pl.BlockSpec((e, d), lambda i: (0, 0)),
                  pl.BlockSpec((tm, d), lambda i: (i, 0))],
        out_specs=pl.BlockSpec((tm, d), lambda i: (i, 0)),
        out_shape=SDS((t, d), F32),
        compiler_params=_cparams("parallel"),
    )(y, w, x)


def _matmul_nt(a, w, name, tm=1024):
    t, d = a.shape
    e = w.shape[0]
    tm = min(tm, t)

    def body(a_ref, w_ref, o_ref):
        o_ref[...] = lax.dot_general(a_ref[...].astype(BF16), w_ref[...], NT_DIMS,
                                     preferred_element_type=F32).astype(o_ref.dtype)

    return pl.pallas_call(
        body, name=name, grid=(t // tm,),
        in_specs=[pl.BlockSpec((tm, d), lambda i: (i, 0)), pl.BlockSpec((e, d), lambda i: (0, 0))],
        out_specs=pl.BlockSpec((tm, e), lambda i: (i, 0)),
        out_shape=SDS((t, e), BF16),
        compiler_params=_cparams("parallel"),
    )(a, w)


def _matmul_tn(a, b, name, out_dtype=BF16, tm=1024, tn=1024, tk=1024):
    t, m = a.shape
    n = b.shape[1]
    tm, tn, tk = min(tm, m), _divisor_tile(n, 2 * tn), min(tk, t)
    nk = t // tk

    def body(a_ref, b_ref, o_ref, acc_ref):
        k = pl.program_id(2)

        @pl.when(k == 0)
        def _():
            acc_ref[...] = jnp.zeros_like(acc_ref)

        acc_ref[...] += lax.dot_general(a_ref[...].astype(BF16), b_ref[...].astype(BF16), TN_DIMS,
                                        preferred_element_type=F32)

        @pl.when(k == nk - 1)
        def _():
            o_ref[...] = acc_ref[...].astype(o_ref.dtype)

    return pl.pallas_call(
        body, name=name, grid=(m // tm, n // tn, nk),
        in_specs=[pl.BlockSpec((tk, tm), lambda i, j, k: (k, i)), pl.BlockSpec((tk, tn), lambda i, j, k: (k, j))],
        out_specs=pl.BlockSpec((tm, tn), lambda i, j, k: (i, j)),
        out_shape=SDS((m, n), out_dtype),
        scratch_shapes=[pltpu.VMEM((tm, tn), F32)],
        compiler_params=_cparams("parallel", "parallel", "arbitrary"),
    )(a, b)


def _dproj_matmul_normbwd(dproj, w, x, g, dxn, name, tm=1024, tk=1024, ride=None):
    t, n = dproj.shape
    d = w.shape[0]
    tm, tk = min(tm, t), _divisor_tile(n, tk)
    nk = n // tk
    ni = t // tm
    nr = ride.n if ride else 0

    def body(*refs):
        dp_ref, w_ref, x_ref, g_ref, dxn_ref = refs[:5]
        dx_ref, dg_ref = refs[5 + nr:7 + nr]
        rin, rout = refs[5:5 + nr], refs[7 + nr:7 + 2 * nr]
        sems, acc_ref = refs[7 + 2 * nr:-1], refs[-1]
        i, k = pl.program_id(0), pl.program_id(1)
        if ride:
            @pl.when((i == 0) & (k == 0))
            def _():
                ride.start(rin, rout, sems)

        @pl.when(k == 0)
        def _():
            acc_ref[...] = jnp.zeros_like(acc_ref)

        acc_ref[...] += lax.dot_general(dp_ref[...], w_ref[...], NT_DIMS, preferred_element_type=F32)

        @pl.when(k == nk - 1)
        def _():
            dhn = acc_ref[...]
            xf = x_ref[...]
            r = lax.rsqrt(jnp.mean(xf * xf, axis=-1, keepdims=True) + NORM_EPS)
            xh = xf * r
            dy = dhn * g_ref[...]
            dx_ref[...] = dxn_ref[...] + r * (dy - xh * jnp.mean(dy * xh, axis=-1, keepdims=True))
            part = jnp.sum(dhn * xh, axis=0, keepdims=True)

            @pl.when(i == 0)
            def _():
                dg_ref[...] = part

            @pl.when(i > 0)
            def _():
                dg_ref[...] += part

        if ride:
            @pl.when((i == ni - 1) & (k == nk - 1))
            def _():
                ride.wait(rin, rout, sems)

    return pl.pallas_call(
        body, name=name, grid=(ni, nk),
        in_specs=[pl.BlockSpec((tm, tk), lambda i, k: (i, k)), pl.BlockSpec((d, tk), lambda i, k: (0, k)),
                  pl.BlockSpec((tm, d), lambda i, k: (i, 0)), pl.BlockSpec((1, d), lambda i, k: (0, 0)),
                  pl.BlockSpec((tm, d), lambda i, k: (i, 0))] + (ride.in_specs if ride else []),
        out_specs=[pl.BlockSpec((tm, d), lambda i, k: (i, 0)), pl.BlockSpec((1, d), lambda i, k: (0, 0))]
        + (ride.out_specs if ride else []),
        out_shape=[SDS((t, d), F32), SDS((1, d), F32)] + (ride.out_shape if ride else []),
        scratch_shapes=(ride.scratch if ride else []) + [pltpu.VMEM((tm, d), F32)],
        compiler_params=_cparams("arbitrary", "arbitrary"),
    )(dproj, w, x, g, dxn, *(ride.arrays if ride else []))


def _out_matmul_loss(y, w, x, target, name, tm=1024):
    t, e = y.shape
    d = w.shape[1]
    tm = min(tm, t)
    inv_d = 1.0 / d

    def body(y_ref, w_ref, x_ref, t_ref, part_ref, dy_ref):
        i = pl.program_id(0)
        err = (x_ref[...] + jnp.dot(y_ref[...], w_ref[...], preferred_element_type=F32)) - t_ref[...]
        dy_ref[...] = err * inv_d
        part = jnp.sum(err * err, axis=0, keepdims=True) * (0.5 * inv_d)

        @pl.when(i == 0)
        def _():
            part_ref[...] = part

        @pl.when(i > 0)
        def _():
            part_ref[...] += part

    return pl.pallas_call(
        body, name=name, grid=(t // tm,),
        in_specs=[pl.BlockSpec((tm, e), lambda i: (i, 0)), pl.BlockSpec((e, d), lambda i: (0, 0)),
                  pl.BlockSpec((tm, d), lambda i: (i, 0)), pl.BlockSpec((tm, d), lambda i: (i, 0))],
        out_specs=[pl.BlockSpec((1, d), lambda i: (0, 0)), pl.BlockSpec((tm, d), lambda i: (i, 0))],
        out_shape=[SDS((1, d), F32), SDS((t, d), F32)],
        compiler_params=_cparams("arbitrary"),
    )(y, w, x, target)


CONV_ROWS = 32
CONV_COLS = 512


def _conv_chunk(src_ref, base, w_ref, width, r0, c0, flip):
    acc = None
    for k in range(width):
        off = base + r0 + ((width - 1 - k) if flip else (k - (width - 1)))
        term = src_ref[pl.ds(off, CONV_ROWS), pl.ds(c0, CONV_COLS)] * w_ref[pl.ds(k, 1), pl.ds(c0, CONV_COLS)]
        acc = term if acc is None else acc + term
    return acc


def _conv_weight_grad(dw_ref, d_ref, src_ref, base, width, tt, e):
    for c0 in range(0, e, CONV_COLS):
        for k in range(width):
            acc = None
            for r0 in range(0, tt, CONV_ROWS):
                prod = (d_ref[pl.ds(r0, CONV_ROWS), pl.ds(c0, CONV_COLS)]
                        * src_ref[pl.ds(base + r0 - (width - 1) + k, CONV_ROWS), pl.ds(c0, CONV_COLS)])
                part = _rows8(prod)
                acc = part if acc is None else acc + part
            dw_ref[pl.ds(k, 1), pl.ds(c0, CONV_COLS)] += jnp.sum(acc, axis=0, keepdims=True)


def _shifted_copies(dst_ref, src_ref, c0, length, sign):
    lo, hi = (SUBLANES, length) if sign < 0 else (0, length - SUBLANES)
    for b in range(SUBLANES):
        for r0 in range(lo, hi, CONV_ROWS):
            n = min(CONV_ROWS, hi - r0)
            dst_ref[b, pl.ds(r0, n), :] = src_ref[pl.ds(r0 + sign * b, n), pl.ds(c0, CONV_COLS)]


def _conv_aligned(copies_ref, base, w_ref, width, r0, c0, sign):
    acc = None
    for d in range(width):
        a, b = divmod(d, SUBLANES)
        term = (copies_ref[b, pl.ds(base + r0 + sign * SUBLANES * a, CONV_ROWS), :]
                * w_ref[pl.ds(width - 1 - d, 1), pl.ds(c0, CONV_COLS)])
        acc = term if acc is None else acc + term
    return acc


def _conv_weight_grad_aligned(dw_ref, dcopies_ref, src_ref, base, width, tt, c0):
    for d in range(width):
        a, b = divmod(d, SUBLANES)
        acc = None
        for r0 in range(0, tt, CONV_ROWS):
            prod = (dcopies_ref[b, pl.ds(r0, CONV_ROWS), :]
                    * src_ref[pl.ds(base + r0 - SUBLANES * a, CONV_ROWS), pl.ds(c0, CONV_COLS)])
            part = _rows8(prod)
            acc = part if acc is None else acc + part
        dw_ref[pl.ds(width - 1 - d, 1), pl.ds(c0, CONV_COLS)] += jnp.sum(acc, axis=0, keepdims=True)


LN_ROWS = 16


def _a_mid_fwd(proj, cw, cb, lg, lb, width, name, tt=256, ride=None):
    t, e3 = proj.shape
    e = e3 // 3
    halo = CONF_HALO
    nt = t // tt
    nr = ride.n if ride else 0

    def body(*refs):
        rin, rout = refs[5:5 + nr], refs[7 + nr:7 + 2 * nr]
        scratch = refs[7 + 2 * nr:]
        sems, own = (scratch[:3], scratch[3:]) if ride else ((), scratch)
        if ride:
            @pl.when(pl.program_id(0) == 0)
            def _():
                ride.start(rin, rout, sems)

        tile(*refs[:5], *refs[5 + nr:7 + nr], *own)
        if ride:
            @pl.when(pl.program_id(0) == nt - 1)
            def _():
                ride.wait(rin, rout, sems)

    def tile(p_ref, cw_ref, cb_ref, lg_ref, lb_ref, y_ref, u1_ref, ubuf, shifted):
        i = pl.program_id(0)

        @pl.when(i == 0)
        def _():
            ubuf[pl.ds(0, halo), :] = jnp.zeros((halo, e), F32)

        @pl.when(i > 0)
        def _():
            ubuf[pl.ds(0, halo), :] = ubuf[pl.ds(tt, halo), :]

        for r0 in range(0, tt, CONV_ROWS):
            val = p_ref[pl.ds(r0, CONV_ROWS), pl.ds(0, e)].astype(F32)
            gate = p_ref[pl.ds(r0, CONV_ROWS), pl.ds(e, e)].astype(F32)
            ubuf[pl.ds(halo + r0, CONV_ROWS), :] = val * _sigmoid(gate)
        for c0 in range(0, e, CONV_COLS):
            _shifted_copies(shifted, ubuf, c0, halo + tt, -1)
            for r0 in range(0, tt, CONV_ROWS):
                acc = _conv_aligned(shifted, halo, cw_ref, width, r0, c0, -1)
                u1_ref[pl.ds(r0, CONV_ROWS), pl.ds(c0, CONV_COLS)] = acc + cb_ref[:, pl.ds(c0, CONV_COLS)]
        for r0 in range(0, tt, LN_ROWS):
            u = u1_ref[pl.ds(r0, LN_ROWS), :]
            mu = jnp.mean(u, axis=-1, keepdims=True)
            dlt = u - mu
            var = jnp.mean(dlt * dlt, axis=-1, keepdims=True)
            u2 = (dlt * lax.rsqrt(var + NORM_EPS)) * lg_ref[...] + lb_ref[...]
            z = p_ref[pl.ds(r0, LN_ROWS), pl.ds(2 * e, e)].astype(F32)
            y_ref[pl.ds(r0, LN_ROWS), :] = (_silu(u2) * _silu(z)).astype(BF16)

    return pl.pallas_call(
        body, name=name, grid=(nt,),
        in_specs=[pl.BlockSpec((tt, e3), lambda i: (i, 0)), pl.BlockSpec(cw.shape, lambda i: (0, 0)),
                  pl.BlockSpec((1, e), lambda i: (0, 0)), pl.BlockSpec((1, e), lambda i: (0, 0)),
                  pl.BlockSpec((1, e), lambda i: (0, 0))] + (ride.in_specs if ride else []),
        out_specs=[pl.BlockSpec((tt, e), lambda i: (i, 0)), pl.BlockSpec((tt, e), lambda i: (i, 0))]
        + (ride.out_specs if ride else []),
        out_shape=[SDS((t, e), BF16), SDS((t, e), F32)] + (ride.out_shape if ride else []),
        scratch_shapes=(ride.scratch if ride else [])
        + [pltpu.VMEM((halo + tt, e), F32), pltpu.VMEM((SUBLANES, halo + tt, CONV_COLS), F32)],
        compiler_params=_cparams("arbitrary"),
    )(proj, cw, cb, lg, lb, *(ride.arrays if ride else []))


def _a_mid_bwd(proj, u1, dyz, cw, lg, lb, width, name, tt=256, ride=None):
    t, e3 = proj.shape
    e = e3 // 3
    halo = CONF_HALO
    nt = t // tt
    hb = tt // halo
    nr = ride.n if ride else 0

    def body(*refs):
        rin, rout = refs[7:7 + nr], refs[12 + nr:12 + 2 * nr]
        scratch = refs[12 + 2 * nr:]
        sems, own = (scratch[:3], scratch[3:]) if ride else ((), scratch)
        if ride:
            @pl.when(pl.program_id(0) == 0)
            def _():
                ride.start(rin, rout, sems)

        tile(*refs[:7], *refs[7 + nr:12 + nr], *own)
        if ride:
            @pl.when(pl.program_id(0) == nt - 1)
            def _():
                ride.wait(rin, rout, sems)

    def tile(p_ref, pp_ref, u1_ref, dy_ref, cw_ref, lg_ref, lb_ref,
             dp_ref, dcw_ref, dcb_ref, dlg_ref, dlb_ref, ubuf, dbuf, shifted, acc_cb, acc_lg, acc_lb):
        i = pl.program_id(0)
        ti = nt - 1 - i

        @pl.when(i == 0)
        def _():
            dbuf[pl.ds(tt, halo), :] = jnp.zeros((halo, e), F32)
            dcw_ref[...] = jnp.zeros_like(dcw_ref)
            acc_cb[...] = jnp.zeros_like(acc_cb)
            acc_lg[...] = jnp.zeros_like(acc_lg)
            acc_lb[...] = jnp.zeros_like(acc_lb)

        @pl.when(i > 0)
        def _():
            dbuf[pl.ds(tt, halo), :] = dbuf[pl.ds(0, halo), :]

        keep = (ti > 0).astype(F32)
        ubuf[pl.ds(0, halo), :] = keep * (pp_ref[:, pl.ds(0, e)].astype(F32) * _sigmoid(pp_ref[:, pl.ds(e, e)].astype(F32)))
        for r0 in range(0, tt, CONV_ROWS):
            val = p_ref[pl.ds(r0, CONV_ROWS), pl.ds(0, e)].astype(F32)
            gate = p_ref[pl.ds(r0, CONV_ROWS), pl.ds(e, e)].astype(F32)
            ubuf[pl.ds(halo + r0, CONV_ROWS), :] = val * _sigmoid(gate)

        for r0 in range(0, tt, LN_ROWS):
            rows = pl.ds(r0, LN_ROWS)
            u = u1_ref[rows, :]
            mu = jnp.mean(u, axis=-1, keepdims=True)
            dlt = u - mu
            var = jnp.mean(dlt * dlt, axis=-1, keepdims=True)
            rstd = lax.rsqrt(var + NORM_EPS)
            xh = dlt * rstd
            u2 = xh * lg_ref[...] + lb_ref[...]
            s2 = _sigmoid(u2)
            u3 = u2 * s2
            z = p_ref[rows, pl.ds(2 * e, e)].astype(F32)
            sz = _sigmoid(z)
            dy = dy_ref[rows, :].astype(F32)
            dp_ref[rows, pl.ds(2 * e, e)] = (dy * u3 * (sz * (1.0 + z * (1.0 - sz)))).astype(BF16)
            du2 = (dy * (z * sz)) * (s2 * (1.0 + u2 * (1.0 - s2)))
            acc_lg[...] += _rows8(du2 * xh)
            acc_lb[...] += _rows8(du2)
            dxh = du2 * lg_ref[...]
            m1 = jnp.mean(dxh, axis=-1, keepdims=True)
            m2 = jnp.mean(dxh * xh, axis=-1, keepdims=True)
            du1 = rstd * (dxh - m1 - xh * m2)
            dbuf[rows, :] = du1
            acc_cb[...] += _rows8(du1)

        for c0 in range(0, e, CONV_COLS):
            _shifted_copies(shifted, dbuf, c0, tt + halo, 1)
            for r0 in range(0, tt, CONV_ROWS):
                du0 = _conv_aligned(shifted, 0, cw_ref, width, r0, c0, 1)
                rows, cols = pl.ds(r0, CONV_ROWS), pl.ds(c0, CONV_COLS)
                val = p_ref[rows, cols].astype(F32)
                sg = _sigmoid(p_ref[rows, pl.ds(e + c0, CONV_COLS)].astype(F32))
                dp_ref[rows, cols] = (du0 * sg).astype(BF16)
                dp_ref[rows, pl.ds(e + c0, CONV_COLS)] = (du0 * val * sg * (1.0 - sg)).astype(BF16)
            _conv_weight_grad_aligned(dcw_ref, shifted, ubuf, halo, width, tt, c0)

        @pl.when(i == nt - 1)
        def _():
            dcb_ref[...] = jnp.sum(acc_cb[...], axis=0, keepdims=True)
            dlg_ref[...] = jnp.sum(acc_lg[...], axis=0, keepdims=True)
            dlb_ref[...] = jnp.sum(acc_lb[...], axis=0, keepdims=True)

    vec = pl.BlockSpec((1, e), lambda i: (0, 0))
    return pl.pallas_call(
        body, name=name, grid=(nt,),
        in_specs=[pl.BlockSpec((tt, e3), lambda i: (nt - 1 - i, 0)),
                  pl.BlockSpec((halo, e3), lambda i: (jnp.maximum((nt - 1 - i) * hb - 1, 0), 0)),
                  pl.BlockSpec((tt, e), lambda i: (nt - 1 - i, 0)), pl.BlockSpec((tt, e), lambda i: (nt - 1 - i, 0)),
                  pl.BlockSpec(cw.shape, lambda i: (0, 0)), vec, vec] + (ride.in_specs if ride else []),
        out_specs=[pl.BlockSpec((tt, e3), lambda i: (nt - 1 - i, 0)), pl.BlockSpec(cw.shape, lambda i: (0, 0)), vec, vec, vec]
        + (ride.out_specs if ride else []),
        out_shape=[SDS((t, e3), BF16), SDS(cw.shape, F32), SDS((1, e), F32), SDS((1, e), F32), SDS((1, e), F32)]
        + (ride.out_shape if ride else []),
        scratch_shapes=(ride.scratch if ride else [])
        + [pltpu.VMEM((halo + tt, e), F32), pltpu.VMEM((tt + halo, e), F32),
           pltpu.VMEM((SUBLANES, halo + tt, CONV_COLS), F32),
           pltpu.VMEM((SUBLANES, e), F32), pltpu.VMEM((SUBLANES, e), F32), pltpu.VMEM((SUBLANES, e), F32)],
        compiler_params=_cparams("arbitrary"),
    )(proj, proj, u1, dyz, cw, lg, lb, *(ride.arrays if ride else []))


def _c_mid_fwd(proj, cw, width, name, tt=256):
    t, e4 = proj.shape
    e = e4 // 4
    halo = SHORT_HALO

    def body(p_ref, cw_ref, y_ref, wbuf):
        i = pl.program_id(0)

        @pl.when(i == 0)
        def _():
            wbuf[pl.ds(0, halo), :] = jnp.zeros((halo, e), F32)

        @pl.when(i > 0)
        def _():
            wbuf[pl.ds(0, halo), :] = wbuf[pl.ds(tt, halo), :]

        for r0 in range(0, tt, CONV_ROWS):
            rows = pl.ds(r0, CONV_ROWS)
            wbuf[pl.ds(halo + r0, CONV_ROWS), :] = p_ref[rows, pl.ds(2 * e, e)].astype(F32) * p_ref[rows, pl.ds(0, e)].astype(F32)
        for c0 in range(0, e, CONV_COLS):
            for r0 in range(0, tt, CONV_ROWS):
                rows = pl.ds(r0, CONV_ROWS)
                cv = _conv_chunk(wbuf, halo, cw_ref, width, r0, c0, False)
                bg = p_ref[rows, pl.ds(e + c0, CONV_COLS)].astype(F32)
                z = p_ref[rows, pl.ds(3 * e + c0, CONV_COLS)].astype(F32)
                y_ref[rows, pl.ds(c0, CONV_COLS)] = ((bg * cv) * _silu(z)).astype(BF16)

    return pl.pallas_call(
        body, name=name, grid=(t // tt,),
        in_specs=[pl.BlockSpec((tt, e4), lambda i: (i, 0)), pl.BlockSpec(cw.shape, lambda i: (0, 0))],
        out_specs=pl.BlockSpec((tt, e), lambda i: (i, 0)),
        out_shape=SDS((t, e), BF16),
        scratch_shapes=[pltpu.VMEM((halo + tt, e), F32)],
        compiler_params=_cparams("arbitrary"),
    )(proj, cw)


def _c_mid_bwd(proj, dyz, cw, width, name, tt=256):
    t, e4 = proj.shape
    e = e4 // 4
    halo = SHORT_HALO
    nt = t // tt
    hb = tt // halo

    def body(p_ref, pp_ref, dy_ref, cw_ref, dp_ref, dcw_ref, wbuf, dbuf):
        i = pl.program_id(0)
        ti = nt - 1 - i

        @pl.when(i == 0)
        def _():
            dbuf[pl.ds(tt, halo), :] = jnp.zeros((halo, e), F32)
            dcw_ref[...] = jnp.zeros_like(dcw_ref)

        @pl.when(i > 0)
        def _():
            dbuf[pl.ds(tt, halo), :] = dbuf[pl.ds(0, halo), :]

        keep = (ti > 0).astype(F32)
        wbuf[pl.ds(0, halo), :] = keep * (pp_ref[:, pl.ds(2 * e, e)].astype(F32) * pp_ref[:, pl.ds(0, e)].astype(F32))
        for r0 in range(0, tt, CONV_ROWS):
            rows = pl.ds(r0, CONV_ROWS)
            wbuf[pl.ds(halo + r0, CONV_ROWS), :] = p_ref[rows, pl.ds(2 * e, e)].astype(F32) * p_ref[rows, pl.ds(0, e)].astype(F32)
        for c0 in range(0, e, CONV_COLS):
            for r0 in range(0, tt, CONV_ROWS):
                rows, cols = pl.ds(r0, CONV_ROWS), pl.ds(c0, CONV_COLS)
                cv = _conv_chunk(wbuf, halo, cw_ref, width, r0, c0, False)
                bg = p_ref[rows, pl.ds(e + c0, CONV_COLS)].astype(F32)
                z = p_ref[rows, pl.ds(3 * e + c0, CONV_COLS)].astype(F32)
                sz = _sigmoid(z)
                dyz_c = dy_ref[rows, cols].astype(F32)
                dy = dyz_c * (z * sz)
                dp_ref[rows, pl.ds(3 * e + c0, CONV_COLS)] = (dyz_c * (bg * cv) * (sz * (1.0 + z * (1.0 - sz)))).astype(BF16)
                dp_ref[rows, pl.ds(e + c0, CONV_COLS)] = (dy * cv).astype(BF16)
                dbuf[rows, cols] = dy * bg
        for c0 in range(0, e, CONV_COLS):
            for r0 in range(0, tt, CONV_ROWS):
                rows, cols = pl.ds(r0, CONV_ROWS), pl.ds(c0, CONV_COLS)
                dw = _conv_chunk(dbuf, 0, cw_ref, width, r0, c0, True)
                dp_ref[rows, pl.ds(2 * e + c0, CONV_COLS)] = (dw * p_ref[rows, cols].astype(F32)).astype(BF16)
                dp_ref[rows, cols] = (dw * p_ref[rows, pl.ds(2 * e + c0, CONV_COLS)].astype(F32)).astype(BF16)
        _conv_weight_grad(dcw_ref, dbuf, wbuf, halo, width, tt, e)

    return pl.pallas_call(
        body, name=name, grid=(nt,),
        in_specs=[pl.BlockSpec((tt, e4), lambda i: (nt - 1 - i, 0)),
                  pl.BlockSpec((halo, e4), lambda i: (jnp.maximum((nt - 1 - i) * hb - 1, 0), 0)),
                  pl.BlockSpec((tt, e), lambda i: (nt - 1 - i, 0)), pl.BlockSpec(cw.shape, lambda i: (0, 0))],
        out_specs=[pl.BlockSpec((tt, e4), lambda i: (nt - 1 - i, 0)), pl.BlockSpec(cw.shape, lambda i: (0, 0))],
        out_shape=[SDS((t, e4), BF16), SDS(cw.shape, F32)],
        scratch_shapes=[pltpu.VMEM((halo + tt, e), F32), pltpu.VMEM((tt + halo, e), F32)],
        compiler_params=_cparams("arbitrary"),
    )(proj, proj, dyz, cw)


def _b_prep_fwd(proj, flog, fbias, qg, kg, heads, name, tt=256):
    t, e4 = proj.shape
    e = e4 // 4
    scale = HEAD_DIM ** -0.5 * LOG2E

    def body(q_ref, k_ref, fl_ref, fb_ref, qg_ref, kg_ref, qs_ref, kn_ref, knt_ref, c_ref, ct_ref, carry):
        i = pl.program_id(0)

        @pl.when(i == 0)
        def _():
            carry[...] = jnp.zeros_like(carry)

        for h in range(heads):
            cols = pl.ds(h * HEAD_DIM, HEAD_DIM)
            qh = q_ref[:, cols].astype(F32)
            r = lax.rsqrt(jnp.mean(qh * qh, axis=-1, keepdims=True) + NORM_EPS)
            qs_ref[:, cols] = (((qh * r) * qg_ref[:, cols]) * scale).astype(BF16)
            kh = k_ref[:, cols].astype(F32)
            r = lax.rsqrt(jnp.mean(kh * kh, axis=-1, keepdims=True) + NORM_EPS)
            kn = (kh * r) * kg_ref[:, cols]
            kn_ref[:, cols] = kn.astype(BF16)
            knt_ref[cols, :] = kn.T.astype(BF16)

        a = fl_ref[...] + fb_ref[...]
        lf = jnp.minimum(a, 0.0) - jnp.log(1.0 + jnp.exp(-jnp.abs(a)))
        tri = (lax.broadcasted_iota(jnp.int32, (tt, tt), 0) >= lax.broadcasted_iota(jnp.int32, (tt, tt), 1)).astype(BF16)
        c = _tri_matmul(tri, lf) + carry[...]
        c_ref[...] = c
        ct_ref[...] = (c * LOG2E).T
        carry[...] = c_ref[pl.ds(tt - 1, 1), :]

    return pl.pallas_call(
        body, name=name, grid=(t // tt,),
        in_specs=[pl.BlockSpec((tt, e), lambda i: (i, 0)), pl.BlockSpec((tt, e), lambda i: (i, 1)),
                  pl.BlockSpec((tt, LANES), lambda i: (i, 0)), pl.BlockSpec((1, LANES), lambda i: (0, 0)),
                  pl.BlockSpec((1, e), lambda i: (0, 0)), pl.BlockSpec((1, e), lambda i: (0, 0))],
        out_specs=[pl.BlockSpec((tt, e), lambda i: (i, 0)), pl.BlockSpec((tt, e), lambda i: (i, 0)),
                   pl.BlockSpec((e, tt), lambda i: (0, i)),
                   pl.BlockSpec((tt, LANES), lambda i: (i, 0)), pl.BlockSpec((LANES, tt), lambda i: (0, i))],
        out_shape=[SDS((t, e), BF16), SDS((t, e), BF16), SDS((e, t), BF16), SDS((t, LANES), F32), SDS((LANES, t), F32)],
        scratch_shapes=[pltpu.VMEM((1, LANES), F32)],
        compiler_params=_cparams("arbitrary"),
    )(proj, proj, flog, fbias, qg, kg)


ATT_BLOCK = 1024
ATT_CHUNK_FWD = 512
ATT_CHUNK_BWD = 256
NEG_BIG = -1e30
LOG2E = 1.4426950408889634
LN2 = 0.6931471805599453


def _flash_fwd(qs, kn, proj, ck, heads, name):
    t, e = qs.shape
    blk = min(ATT_BLOCK, t)
    cw = min(ATT_CHUNK_FWD, blk // 2)
    nq, nch = t // blk, blk // cw
    assert nch % 2 == 0

    def body(q_ref, k_ref, v_ref, ck_ref, z_ref, o_ref, y_ref, m_ref, l_ref, s_a, s_b):
        i = pl.program_id(1)
        q = q_ref[...]
        bufs = (s_a, s_b)

        def key_rows(j, c):
            return pl.ds(pl.multiple_of(j * blk, blk) + c * cw, cw)

        def logits(j, c):
            bufs[c % 2][...] = (lax.dot_general(q, k_ref[key_rows(j, c), :], NT_DIMS, preferred_element_type=F32)
                            - ck_ref[j][:, c * cw:(c + 1) * cw])

        def weights(c, m, masked):
            s = bufs[c % 2][...]
            if masked:
                keep = lax.broadcasted_iota(jnp.int32, (blk, cw), 0) >= (lax.broadcasted_iota(jnp.int32, (blk, cw), 1) + c * cw)
                s = jnp.where(keep, s, NEG_BIG)
            m_new = jnp.maximum(m, jnp.ceil(jnp.max(s, axis=-1, keepdims=True)))
            return m_new, jnp.exp2(m - m_new), jnp.exp2(s - m_new).astype(BF16)

        ones = jnp.ones((cw, HEAD_DIM), BF16)

        def block(j, carry, masked):
            m, acc = carry
            for c in range(nch):
                if c + 1 < nch:
                    logits(j, c + 1)
                elif not masked:
                    logits(j + 1, 0)
                m, alpha, p = weights(c, m, masked)
                v1 = jnp.concatenate([v_ref[key_rows(j, c), :], ones], axis=1)
                acc = alpha * acc + jnp.dot(p, v1, preferred_element_type=F32)
            return m, acc

        logits(0, 0)
        carry = (jnp.full((blk, 1), NEG_BIG, F32), jnp.zeros((blk, 2 * HEAD_DIM), F32))
        carry = lax.fori_loop(0, i, lambda j, cr: block(j, cr, False), carry)
        m, acc = block(i, carry, True)
        l = acc[:, HEAD_DIM:HEAD_DIM + 1]
        o = acc[:, :HEAD_DIM] / l
        o_ref[...] = o
        y_ref[...] = (o * _silu(z_ref[...].astype(F32))).astype(BF16)
        m_ref[...] = jnp.broadcast_to(m, (blk, LANES))
        l_ref[...] = jnp.broadcast_to(l, (blk, LANES))

    head_all = pl.BlockSpec((t, HEAD_DIM), lambda h, i: (0, h))
    tile = pl.BlockSpec((blk, HEAD_DIM), lambda h, i: (i, h))
    stat = pl.BlockSpec((None, blk, LANES), lambda h, i: (h, i, 0))
    return pl.pallas_call(
        body, name=name, grid=(heads, nq),
        in_specs=[tile, head_all, pl.BlockSpec((t, HEAD_DIM), lambda h, i: (0, 2 * heads + h)),
                  pl.BlockSpec((None, nq, 1, blk), lambda h, i: (h, 0, 0, 0)),
                  pl.BlockSpec((blk, HEAD_DIM), lambda h, i: (i, 3 * heads + h))],
        out_specs=[tile, tile, stat, stat],
        out_shape=[SDS((t, e), F32), SDS((t, e), BF16), SDS((heads, t, LANES), F32), SDS((heads, t, LANES), F32)],
        scratch_shapes=[pltpu.VMEM((blk, cw), F32), pltpu.VMEM((blk, cw), F32)],
        compiler_params=_cparams("parallel", "arbitrary"),
    )(qs, kn, proj, ck, proj)


def _flash_bwd(qs, kn, knt, proj, ck, dyz, o, mstat, lstat, heads, name):
    t, e = qs.shape
    blk = min(ATT_BLOCK, t)
    cw = min(ATT_CHUNK_BWD, blk // 2)
    nq, nch = t // blk, blk // cw
    assert nch % 2 == 0

    def body(q_ref, dy_ref, o_ref, z_ref, m_ref, l_ref, k_ref, kt_ref, v_ref, ck_ref,
             dq_ref, dk_ref, dv_ref, dc_ref, dz_ref, s_a, s_b, d_a, d_b):
        i = pl.program_id(1)

        @pl.when(i == 0)
        def _():
            dk_ref[...] = jnp.zeros_like(dk_ref)
            dv_ref[...] = jnp.zeros_like(dv_ref)
            dc_ref[...] = jnp.zeros_like(dc_ref)

        z = z_ref[...].astype(F32)
        sz = _sigmoid(z)
        dy = dy_ref[...].astype(F32)
        of = o_ref[...]
        do = ((dy * (z * sz)) / l_ref[:, 0:1]).astype(BF16)
        dz_ref[...] = (dy * of * (sz * (1.0 + z * (1.0 - sz)))).astype(BF16)
        dl = jnp.sum(do.astype(F32) * of, axis=-1, keepdims=True)
        q = q_ref[...]
        q_t = q.astype(F32).T.astype(BF16)
        do_t = do.astype(F32).T.astype(BF16)
        mrow = m_ref[:, 0:1]
        sbuf, dbuf = (s_a, s_b), (d_a, d_b)

        def key_rows(j, c):
            return pl.ds(pl.multiple_of(j * blk, blk) + c * cw, cw)

        def products(j, c):
            rows = key_rows(j, c)
            sbuf[c % 2][...] = (lax.dot_general(q, k_ref[rows, :], NT_DIMS, preferred_element_type=F32)
                            - ck_ref[j][:, c * cw:(c + 1) * cw])
            dbuf[c % 2][...] = lax.dot_general(do, v_ref[rows, :], NT_DIMS, preferred_element_type=F32)

        def weights(c, masked):
            p = jnp.exp2(sbuf[c % 2][...] - mrow)
            if masked:
                keep = lax.broadcasted_iota(jnp.int32, (blk, cw), 0) >= (lax.broadcasted_iota(jnp.int32, (blk, cw), 1) + c * cw)
                p = jnp.where(keep, p, 0.0)
            p = p.astype(BF16)
            ds = p.astype(F32) * (dbuf[c % 2][...] - dl)
            return p, ds.astype(BF16), jnp.sum(ds, axis=0, keepdims=True)

        def outputs(j, c, p, dsb, colsum, dq):
            rows = key_rows(j, c)
            dv_ref[:, rows] += jnp.dot(do_t, p, preferred_element_type=F32)
            dk_ref[:, rows] += jnp.dot(q_t, dsb, preferred_element_type=F32)
            dc_ref[j, :, pl.ds(c * cw, cw)] -= colsum
            return dq + lax.dot_general(kt_ref[:, rows], dsb, NT_DIMS, preferred_element_type=F32)

        def block(j, dq, masked):
            products(j, 1)
            for c in range(nch):
                p, dsb, colsum = weights(c, masked)
                if c + 2 < nch:
                    products(j, c + 2)
                elif c + 2 == nch and not masked:
                    products(j + 1, 0)
                dq = outputs(j, c, p, dsb, colsum, dq)
            return dq

        products(0, 0)
        dq = lax.fori_loop(0, i, lambda j, acc: block(j, acc, False), jnp.zeros((HEAD_DIM, blk), F32))
        dq_ref[...] = block(i, dq, True).T

    tile = pl.BlockSpec((blk, HEAD_DIM), lambda h, i: (i, h))
    stat = pl.BlockSpec((None, blk, LANES), lambda h, i: (h, i, 0))
    head_all = pl.BlockSpec((t, HEAD_DIM), lambda h, i: (0, h))
    head_all_t = pl.BlockSpec((HEAD_DIM, t), lambda h, i: (h, 0))
    cspec = pl.BlockSpec((None, nq, 1, blk), lambda h, i: (h, 0, 0, 0))
    return pl.pallas_call(
        body, name=name, grid=(heads, nq),
        in_specs=[tile, tile, tile, pl.BlockSpec((blk, HEAD_DIM), lambda h, i: (i, 3 * heads + h)), stat, stat, head_all,
                  head_all_t, pl.BlockSpec((t, HEAD_DIM), lambda h, i: (0, 2 * heads + h)), cspec],
        out_specs=[tile, head_all_t, head_all_t, cspec, tile],
        out_shape=[SDS((t, e), F32), SDS((e, t), F32), SDS((e, t), F32), SDS((heads, nq, 1, blk), F32), SDS((t, e), BF16)],
        scratch_shapes=[pltpu.VMEM((blk, cw), F32)] * 4,
        compiler_params=_cparams("parallel", "arbitrary"),
    )(qs, dyz, o, proj, mstat, lstat, kn, knt, proj, ck)


def _b_prep_bwd(dqs, dknt, dvt, dz, proj, qg, kg, dct, flog, fbias, heads, name, tt=256):
    t, e4 = proj.shape
    e = e4 // 4
    nt = t // tt
    scale = HEAD_DIM ** -0.5

    def body(dq_ref, dkt_ref, dvt_ref, dz_ref, q_ref, k_ref, qg_ref, kg_ref, dc_ref, fl_ref, fb_ref,
             dp_ref, dqg_ref, dkg_ref, dfb_ref, carry, dlf, acc_q, acc_k, acc_f):
        i = pl.program_id(0)

        @pl.when(i == 0)
        def _():
            carry[...] = jnp.zeros_like(carry)
            acc_q[...] = jnp.zeros_like(acc_q)
            acc_k[...] = jnp.zeros_like(acc_k)
            acc_f[...] = jnp.zeros_like(acc_f)

        for h in range(heads):
            cols = pl.ds(h * HEAD_DIM, HEAD_DIM)
            for src_ref, d, g_ref, acc, mult, off in ((q_ref, dq_ref[:, cols], qg_ref, acc_q, scale, 0),
                                                      (k_ref, dkt_ref[cols, :].T, kg_ref, acc_k, LN2, e)):
                xf = src_ref[:, cols].astype(F32)
                r = lax.rsqrt(jnp.mean(xf * xf, axis=-1, keepdims=True) + NORM_EPS)
                xh = xf * r
                dn = d * mult
                acc[...] += _rows8(dn * xh)
                dxh = dn * g_ref[:, cols]
                dp_ref[:, pl.ds(off + h * HEAD_DIM, HEAD_DIM)] = (
                    r * (dxh - xh * jnp.mean(dxh * xh, axis=-1, keepdims=True))).astype(BF16)
                if off:
                    dp_ref[:, pl.ds(2 * e + h * HEAD_DIM, HEAD_DIM)] = dvt_ref[cols, :].T.astype(BF16)
        dp_ref[:, pl.ds(3 * e, e)] = dz_ref[...]

        tri = (lax.broadcasted_iota(jnp.int32, (tt, tt), 0) <= lax.broadcasted_iota(jnp.int32, (tt, tt), 1)).astype(BF16)
        dlf[...] = _tri_matmul(tri, dc_ref[...]) + carry[...]
        carry[...] = dlf[pl.ds(0, 1), :]
        a = fl_ref[...] + fb_ref[...]
        dfl = dlf[...] * _sigmoid(-a)
        dp_ref[:, pl.ds(4 * e, LANES)] = dfl.astype(BF16)
        acc_f[...] += _rows8(dfl)

        @pl.when(i == nt - 1)
        def _():
            dqg_ref[...] = jnp.sum(acc_q[...], axis=0, keepdims=True)
            dkg_ref[...] = jnp.sum(acc_k[...], axis=0, keepdims=True)
            dfb_ref[...] = jnp.sum(acc_f[...], axis=0, keepdims=True)

    rev = lambda i: (nt - 1 - i, 0)
    vec_e = pl.BlockSpec((1, e), lambda i: (0, 0))
    vec = pl.BlockSpec((1, LANES), lambda i: (0, 0))
    wide = pl.BlockSpec((tt, e), rev)
    tall = pl.BlockSpec((e, tt), lambda i: (0, nt - 1 - i))
    lane = pl.BlockSpec((tt, LANES), rev)
    return pl.pallas_call(
        body, name=name, grid=(nt,),
        in_specs=[wide, tall, tall, wide, wide, pl.BlockSpec((tt, e), lambda i: (nt - 1 - i, 1)), vec_e, vec_e, lane, lane, vec],
        out_specs=[pl.BlockSpec((tt, e4 + LANES), rev), vec, vec, vec],
        out_shape=[SDS((t, e4 + LANES), BF16), SDS((1, LANES), F32), SDS((1, LANES), F32), SDS((1, LANES), F32)],
        scratch_shapes=[pltpu.VMEM((1, LANES), F32), pltpu.VMEM((tt, LANES), F32), pltpu.VMEM((SUBLANES, LANES), F32),
                        pltpu.VMEM((SUBLANES, LANES), F32), pltpu.VMEM((SUBLANES, LANES), F32)],
        compiler_params=_cparams("arbitrary"),
    )(dqs, dknt, dvt, dz, proj, proj, qg, kg, dct, flog, fbias)


def _b_fwd(h, b_norm, wb_pad, wb_out, qg, kg, fbias, heads, tag, ride=None):
    t = h.shape[0]
    e = wb_out.shape[0]
    blk = min(ATT_BLOCK, t)
    hn, proj, *rode = _norm_matmul(h, b_norm, wb_pad, f"b_in_proj_{tag}", ride=ride, n_cols=4 * e)
    flog = _matmul_f32out(hn, wb_pad[:, 4 * e:], f"b_forget_proj_{tag}")
    qs, kn, knt, _, ct = _b_prep_fwd(proj, flog, fbias, qg, kg, heads, f"b_prep_fwd_{tag}")
    ck = ct.reshape(LANES, t // blk, 1, blk)
    o, y, mstat, lstat = _flash_fwd(qs, kn, proj, ck, heads, f"b_attention_fwd_{tag}")
    sv = dict(x=h, hn=hn, proj=proj, flog=flog, qs=qs, kn=kn, knt=knt, ck=ck, o=o, y=y, mstat=mstat, lstat=lstat)
    return y, sv, rode


def _b_bwd(dh, sv, b_norm, wb_pad, wb_out, qg, kg, fbias, heads, tag, ride=None):
    t = dh.shape[0]
    e = wb_out.shape[0]
    gb = {}
    dyz = _matmul_nt(dh, wb_out, f"b_out_bwd_{tag}")
    gb["w_out"] = _matmul_tn(sv["y"], dh, f"b_out_wgrad_{tag}", tm=2048)
    dqs, dknt, dvt, dc, dz = _flash_bwd(sv["qs"], sv["kn"], sv["knt"], sv["proj"], sv["ck"], dyz, sv["o"], sv["mstat"],
                                        sv["lstat"], heads, f"b_attention_bwd_{tag}")
    dct = jnp.pad(dc.reshape(heads, t).T, ((0, 0), (0, LANES - heads)))
    dproj, dqg, dkg, dfb = _b_prep_bwd(dqs, dknt, dvt, dz, sv["proj"], qg, kg, dct, sv["flog"], fbias, heads,
                                       f"b_prep_bwd_{tag}")
    gb["w_in"] = _matmul_tn(sv["hn"], dproj, f"b_in_wgrad_{tag}")[:, :4 * e + heads]
    dh, dg, *rode = _dproj_matmul_normbwd(dproj, wb_pad, sv["x"], b_norm, dh, f"b_in_bwd_{tag}", ride=ride)
    gb["norm"], gb["q_norm"], gb["k_norm"], gb["f_bias"] = dg, dqg, dkg, dfb[:, :heads]
    return dh, gb, rode


def _sum_adamw(recv, w, m, v, name, tr=256):
    nl, r, c = w.shape
    tr = min(tr, r)

    def body(g_ref, w_ref, m_ref, v_ref, go_ref, d_ref, mo_ref, vo_ref):
        g = g_ref[0].astype(F32)
        for s in range(1, N_DEV):
            g = g + g_ref[s].astype(F32)
        go_ref[...] = g
        mn = ADAM_B1 * m_ref[...] + (1.0 - ADAM_B1) * g
        vn = ADAM_B2 * v_ref[...] + (1.0 - ADAM_B2) * (g * g)
        m_hat = mn / (1.0 - ADAM_B1 ** ADAM_STEP)
        v_hat = vn / (1.0 - ADAM_B2 ** ADAM_STEP)
        d_ref[...] = -ADAM_LR * (m_hat / (jnp.sqrt(v_hat) + ADAM_EPS) + ADAM_WD * w_ref[...])
        mo_ref[...] = mn
        vo_ref[...] = vn

    blk = pl.BlockSpec((None, tr, c), lambda l, i: (l, i, 0))
    return pl.pallas_call(
        body, name=name, grid=(nl, r // tr),
        in_specs=[pl.BlockSpec((N_DEV, None, tr, c), lambda l, i: (0, l, i, 0)), blk, blk, blk],
        out_specs=[blk, blk, blk, blk],
        out_shape=[SDS(w.shape, F32)] * 4,
        compiler_params=_cparams("parallel", "parallel"),
    )(recv, w, m, v)


def _unshard(g, axis):
    g = jnp.moveaxis(g, 0, axis)
    return g.reshape(g.shape[:axis] + (g.shape[axis] * g.shape[axis + 1],) + g.shape[axis + 2:])


def _to_slabs(full, axis):
    n = full.shape[axis]
    s = full.reshape(full.shape[:axis] + (N_DEV, n // N_DEV) + full.shape[axis + 1:])
    return jnp.moveaxis(s, axis, 0)


def _pack_rows(parts, lead):
    flat = [p.reshape(p.shape[:lead] + (-1,)) for p in parts]
    cat = jnp.concatenate(flat, axis=-1)
    n = cat.shape[-1]
    pad = (-n) % (SUBLANES * LANES)
    cat = jnp.pad(cat, [(0, 0)] * lead + [(0, pad)])
    return cat.reshape(cat.shape[:lead] + ((n + pad) // LANES, LANES))


def _unpack_rows(packed, shapes, lead):
    flat = packed.reshape(packed.shape[:lead] + (-1,))
    out, off = [], 0
    for shp in shapes:
        size = int(np.prod(shp))
        out.append(flat[..., off:off + size].reshape(packed.shape[:lead] + tuple(shp)))
        off += size
    return out


def _pad_rows(w, rows):
    return jnp.pad(w, ((0, rows - w.shape[0]), (0, 0)))


def kernel(x, a_norm, a_w_in, a_conv_w, a_conv_b, a_ln_g, a_ln_b, a_w_out, b_norm, b_w_in, b_f_bias, b_q_norm, b_k_norm, b_w_out, c_norm, c_w_in, c_conv_w, c_w_out, loss_target, m_a_norm, m_a_w_in, m_a_conv_w, m_a_conv_b, m_a_ln_g, m_a_ln_b, m_a_w_out, m_b_norm, m_b_w_in, m_b_f_bias, m_b_q_norm, m_b_k_norm, m_b_w_out, m_c_norm, m_c_w_in, m_c_conv_w, m_c_w_out, v_a_norm, v_a_w_in, v_a_conv_w, v_a_conv_b, v_a_ln_g, v_a_ln_b, v_a_w_out, v_b_norm, v_b_w_in, v_b_f_bias, v_b_q_norm, v_b_k_norm, v_b_w_out, v_c_norm, v_c_w_in, v_c_conv_w, v_c_w_out):
    t, d = x.shape[1], x.shape[2]
    e = a_w_out.shape[1] * N_DEV
    heads = b_f_bias.shape[1]
    n_a, n_b, n_c = a_norm.shape[0], b_norm.shape[0], c_norm.shape[0]
    depth = n_a + n_b + n_c
    ka, kc = a_conv_w.shape[1], c_conv_w.shape[1]
    assert e == heads * HEAD_DIM and n_b == 1 and n_c == 1 and x.shape[0] == 1

    layers = [(i % 3, i // 3) for i in range(depth)]

    def mat_shards(kind, j):
        w_in, w_out = ((a_w_in, a_w_out), (b_w_in, b_w_out), (c_w_in, c_w_out))[kind]
        return [w_in[j].astype(BF16), w_out[j].astype(BF16)]

    def full_mats(kind, gathered):
        w_in, w_out = _unshard(gathered[0], 1), _unshard(gathered[1], 0)
        return (jnp.pad(w_in, ((0, 0), (0, LANES - heads))) if kind == 1 else w_in), w_out

    small_names = ["a_norm", "a_conv_w", "a_conv_b", "a_ln_g", "a_ln_b", "c_norm", "c_conv_w"]
    small = dict(a_norm=a_norm, a_conv_w=a_conv_w, a_conv_b=a_conv_b, a_ln_g=a_ln_g, a_ln_b=a_ln_b,
                 c_norm=c_norm, c_conv_w=c_conv_w)
    small_pack = _pack_rows([small[k] for k in small_names], 0)
    first = _all_gather(mat_shards(*layers[0]) + [small_pack], "all_gather_first_layer")
    sm = _unpack_rows(first[2], [small[k].shape for k in small_names], 1)
    g_a_norm = _unshard(sm[0], 1)
    g_a_conv_w = _unshard(sm[1], 2)
    g_a_conv_b = _unshard(sm[2], 1)
    g_a_ln_g = _unshard(sm[3], 1)
    g_a_ln_b = _unshard(sm[4], 1)
    g_c_norm = _unshard(sm[5], 1)
    g_c_conv_w = _unshard(sm[6], 2)

    cw_a = [_pad_rows(g_a_conv_w[j], CONF_HALO) for j in range(n_a)]
    cw_c = _pad_rows(g_c_conv_w[0], SHORT_HALO)
    qg = jnp.tile(b_q_norm, (1, heads))
    kg = jnp.tile(b_k_norm, (1, heads))
    fbias = jnp.pad(b_f_bias, ((0, 0), (0, LANES - heads)))

    h = x[0]
    saved, weights = [], [full_mats(layers[0][0], first[:2])]
    for i, (kind, j) in enumerate(layers):
        tag = f"l{i}"
        w_in, w_out = weights[i]
        ride = _Ride("gather", mat_shards(*layers[i + 1])) if i + 1 < depth else None
        if kind == 0:
            hn, proj = _norm_matmul(h, g_a_norm[j:j + 1], w_in, f"a_in_proj_{tag}")
            y, u1, *rode = _a_mid_fwd(proj, cw_a[j], g_a_conv_b[j:j + 1], g_a_ln_g[j:j + 1], g_a_ln_b[j:j + 1], ka,
                                      f"a_mid_fwd_{tag}", ride=ride)
            saved.append(dict(x=h, hn=hn, proj=proj, u1=u1, y=y))
        elif kind == 1:
            y, sv, rode = _b_fwd(h, b_norm, w_in, w_out, qg, kg, fbias, heads, tag, ride=ride)
            saved.append(sv)
        else:
            hn, proj, *rode = _norm_matmul(h, g_c_norm, w_in, f"c_in_proj_{tag}", ride=ride)
            y = _c_mid_fwd(proj, cw_c, kc, f"c_mid_fwd_{tag}")
            saved.append(dict(x=h, hn=hn, proj=proj, y=y))
        if ride:
            weights.append(full_mats(layers[i + 1][0], rode))
            h = _out_matmul_residual(y, w_out, h, f"{'abc'[kind]}_out_proj_{tag}")
        else:
            loss_part, dh = _out_matmul_loss(y, w_out, h, loss_target[0], f"{'abc'[kind]}_out_proj_loss_{tag}")
    loss_local = jnp.sum(loss_part).reshape(1, 1)

    ga = dict(norm=[None] * n_a, conv_w=[None] * n_a, conv_b=[None] * n_a, ln_g=[None] * n_a, ln_b=[None] * n_a)
    gb, gc = None, {}
    recv_mats = [None] * depth
    pending = None
    for i in reversed(range(depth)):
        kind, j = layers[i]
        tag = f"l{i}"
        sv = saved[i]
        w_in, w_out = weights[i]
        ride = _Ride("exchange", pending) if pending is not None else None
        if kind == 0:
            dyz = _matmul_nt(dh, w_out, f"a_out_bwd_{tag}")
            gw_out = _matmul_tn(sv["y"], dh, f"a_out_wgrad_{tag}", tm=2048)
            mid_ride, ride = (ride, None) if i == 0 else (None, ride)
            dproj, dcw, dcb, dlg, dlb, *mid_rode = _a_mid_bwd(sv["proj"], sv["u1"], dyz, cw_a[j], g_a_ln_g[j:j + 1],
                                                              g_a_ln_b[j:j + 1], ka, f"a_mid_bwd_{tag}", ride=mid_ride)
            gw_in = _matmul_tn(sv["hn"], dproj, f"a_in_wgrad_{tag}")
            if i == 0:
                ride = _Ride("exchange", [_to_slabs(gw_in, 1), _to_slabs(gw_out, 0)])
            dh, dg, *rode = _dproj_matmul_normbwd(dproj, w_in, sv["x"], g_a_norm[j:j + 1], dh, f"a_in_bwd_{tag}", ride=ride)
            if i == 0:
                if mid_ride:
                    recv_mats[1] = mid_rode
                recv_mats[0], ride = rode, None
            ga["norm"][j], ga["conv_w"][j], ga["conv_b"][j], ga["ln_g"][j], ga["ln_b"][j] = dg[0], dcw[:ka], dcb[0], dlg[0], dlb[0]
        elif kind == 1:
            dh, gb, rode = _b_bwd(dh, sv, b_norm, w_in, w_out, qg, kg, fbias, heads, tag, ride=ride)
            gw_in, gw_out = gb["w_in"], gb["w_out"]
        else:
            dyz = _matmul_nt(dh, w_out, f"c_out_bwd_{tag}")
            gw_out = _matmul_tn(sv["y"], dh, f"c_out_wgrad_{tag}", tm=2048)
            dproj, dcw = _c_mid_bwd(sv["proj"], dyz, cw_c, kc, f"c_mid_bwd_{tag}")
            gw_in = _matmul_tn(sv["hn"], dproj, f"c_in_wgrad_{tag}")
            dh, dg, *rode = _dproj_matmul_normbwd(dproj, w_in, sv["x"], g_c_norm, dh, f"c_in_bwd_{tag}", ride=ride)
            gc["norm"], gc["conv_w"] = dg, dcw[:kc][None]
        if ride:
            recv_mats[i + 1] = rode
        pending = [_to_slabs(gw_in, 1), _to_slabs(gw_out, 0)]
    grad_x = dh[None]

    sharded_small = [(jnp.stack(ga["norm"]), 1), (jnp.stack(ga["conv_w"]), 2), (jnp.stack(ga["conv_b"]), 1),
                     (jnp.stack(ga["ln_g"]), 1), (jnp.stack(ga["ln_b"]), 1), (gc["norm"], 1), (gc["conv_w"], 2)]
    repl_small = [gb["norm"], gb["f_bias"], gb["q_norm"], gb["k_norm"], loss_local]
    small_slabs = _pack_rows([_to_slabs(g, ax) for g, ax in sharded_small]
                             + [jnp.broadcast_to(g[None], (N_DEV,) + g.shape) for g in repl_small], 1)
    recv_small, = _exchange([small_slabs], "exchange_vector_gradients")

    outs = {}
    mat_w = dict(a_w_in=(a_w_in, m_a_w_in, v_a_w_in), a_w_out=(a_w_out, m_a_w_out, v_a_w_out),
                 b_w_in=(b_w_in, m_b_w_in, v_b_w_in), b_w_out=(b_w_out, m_b_w_out, v_b_w_out),
                 c_w_in=(c_w_in, m_c_w_in, v_c_w_in), c_w_out=(c_w_out, m_c_w_out, v_c_w_out))
    for kind, prefix in enumerate("abc"):
        members = [i for i, (k, _) in enumerate(layers) if k == kind]
        for which, name in enumerate((f"{prefix}_w_in", f"{prefix}_w_out")):
            recv = jnp.stack([recv_mats[i][which] for i in members], axis=1)
            outs[name] = _sum_adamw(recv, *mat_w[name], f"adamw_{name}")

    small_order = small_names + ["b_norm", "b_f_bias", "b_q_norm", "b_k_norm", "loss"]
    no_state = jnp.zeros((1, 1), F32)
    small_w = dict(a_norm=(a_norm, m_a_norm, v_a_norm), a_conv_w=(a_conv_w, m_a_conv_w, v_a_conv_w),
                   a_conv_b=(a_conv_b, m_a_conv_b, v_a_conv_b), a_ln_g=(a_ln_g, m_a_ln_g, v_a_ln_g),
                   a_ln_b=(a_ln_b, m_a_ln_b, v_a_ln_b), c_norm=(c_norm, m_c_norm, v_c_norm),
                   c_conv_w=(c_conv_w, m_c_conv_w, v_c_conv_w), b_norm=(b_norm, m_b_norm, v_b_norm),
                   b_f_bias=(b_f_bias, m_b_f_bias, v_b_f_bias), b_q_norm=(b_q_norm, m_b_q_norm, v_b_q_norm),
                   b_k_norm=(b_k_norm, m_b_k_norm, v_b_k_norm), loss=(no_state, no_state, no_state))
    packs = [_pack_rows([small_w[k][q] for k in small_order], 0)[None] for q in range(3)]
    small_out = _sum_adamw(recv_small[:, None], *packs, "adamw_vectors")
    shapes = [small_w[k][0].shape for k in small_order]
    unpacked = [_unpack_rows(o[0], shapes, 0) for o in small_out]
    for idx, name in enumerate(small_order):
        outs[name] = tuple(unpacked[q][idx] for q in range(4))

    loss = outs["loss"][0].reshape(())
    order = ["a_norm", "a_w_in", "a_conv_w", "a_conv_b", "a_ln_g", "a_ln_b", "a_w_out", "b_norm", "b_w_in", "b_f_bias",
             "b_q_norm", "b_k_norm", "b_w_out", "c_norm", "c_w_in", "c_conv_w", "c_w_out"]
    return (loss, grad_x, *[outs[k][0] for k in order], *[outs[k][1] for k in order],
            *[outs[k][2] for k in order], *[outs[k][3] for k in order])
```

```python
import jax
import jax.numpy as jnp
import numpy as np
from jax import lax
from jax.experimental import pallas as pl
from jax.experimental.pallas import tpu as pltpu

F32 = jnp.float32
BF16 = jnp.bfloat16
SDS = jax.ShapeDtypeStruct

NORM_EPS = 1e-6
ADAM_LR = 0.001
ADAM_B1 = 0.9
ADAM_B2 = 0.999
ADAM_EPS = 1e-08
ADAM_WD = 0.01
ADAM_STEP = 10

N_DEV = 8
LANES = 128
SUBLANES = 8
HEAD_DIM = 128
CONF_HALO = 32
SHORT_HALO = 8
VMEM_LIMIT = 56 * 1024 * 1024

NT_DIMS = (((1,), (1,)), ((), ()))
TN_DIMS = (((0,), (0,)), ((), ()))
MESH = pl.DeviceIdType.MESH
ANY = pl.BlockSpec(memory_space=pl.ANY)


def _cparams(*sem):
    return pltpu.CompilerParams(dimension_semantics=sem, vmem_limit_bytes=VMEM_LIMIT)


def _divisor_tile(n, cap):
    return max(m for m in range(LANES, min(n, cap) + 1, LANES) if n % m == 0)


def _sigmoid(x):
    return 0.5 * jnp.tanh(0.5 * x) + 0.5


def _silu(x):
    return x * _sigmoid(x)


def _dsilu(x):
    s = _sigmoid(x)
    return s * (1.0 + x * (1.0 - s))


def _rows8(v):
    out = v[0:SUBLANES]
    for a in range(1, v.shape[0] // SUBLANES):
        out = out + v[a * SUBLANES:(a + 1) * SUBLANES]
    return out


def _split3(v):
    hi = v.astype(BF16)
    r1 = v - hi.astype(F32)
    mid = r1.astype(BF16)
    lo = (r1 - mid.astype(F32)).astype(BF16)
    return hi, mid, lo


def _tri_matmul(tri, v):
    hi, mid, lo = _split3(v)
    return (jnp.dot(tri, hi, preferred_element_type=F32) + jnp.dot(tri, mid, preferred_element_type=F32)
            + jnp.dot(tri, lo, preferred_element_type=F32))


def _position():
    return lax.axis_index("x"), lax.axis_index("y"), lax.axis_index("c")


def _all_gather(shards, name):
    n = len(shards)

    def body(*refs):
        xs, outs = refs[:n], refs[n:2 * n]
        send_sems, recv_sems, local_sems = refs[2 * n:]
        x, y, c = _position()
        me, sibling = (x, y, c), (x, y, 1 - c)
        chips = [(1 - x, y), (x, 1 - y), (1 - x, 1 - y)]

        def slot(a, px, py, pc):
            return outs[a].at[4 * px + 2 * py + pc]

        def copy(a, k, block, to, src=None):
            return pltpu.make_async_remote_copy(
                src_ref=slot(a, *block) if src is None else src, dst_ref=slot(a, *block),
                send_sem=send_sems.at[a, k], recv_sem=recv_sems.at[a, k], device_id=to, device_id_type=MESH)

        started = []
        mine = []
        for a in range(n):
            cp = pltpu.make_async_copy(xs[a], slot(a, *me), local_sems.at[a])
            cp.start()
            mine.append(cp)
        for a in range(n):
            first = [copy(a, 0, me, sibling, src=xs[a])]
            first += [copy(a, 1 + j, me, (*chip, c), src=xs[a]) for j, chip in enumerate(chips)]
            for cp in first:
                cp.start()
            started += first
        for a in range(n):
            for j, chip in enumerate(chips):
                copy(a, 1 + j, (*chip, c), me).wait_recv()
                fwd = copy(a, 4 + j, (*chip, c), sibling)
                fwd.start()
                started.append(fwd)
        for a in range(n):
            copy(a, 0, sibling, me).wait_recv()
            for j, chip in enumerate(chips):
                copy(a, 4 + j, (*chip, 1 - c), me).wait_recv()
        for cp in started:
            cp.wait_send()
        for cp in mine:
            cp.wait()

    return pl.pallas_call(
        body, name=name,
        out_shape=[SDS((N_DEV,) + s.shape, s.dtype) for s in shards],
        in_specs=[ANY] * n, out_specs=[ANY] * n,
        scratch_shapes=[pltpu.SemaphoreType.DMA((n, 7)), pltpu.SemaphoreType.DMA((n, 7)), pltpu.SemaphoreType.DMA((n,))],
    )(*shards)


def _exchange(slabs, name):
    n = len(slabs)

    def body(*refs):
        ins, outs = refs[:n], refs[n:2 * n]
        send_sems, recv_sems, local_sems = refs[2 * n:]
        x, y, c = _position()
        me = 4 * x + 2 * y + c
        peers = [(x ^ bx, y ^ by, c ^ bc) for bx in (0, 1) for by in (0, 1) for bc in (0, 1)][1:]

        def copy(a, k, peer):
            pid = 4 * peer[0] + 2 * peer[1] + peer[2]
            return pltpu.make_async_remote_copy(
                src_ref=ins[a].at[pid], dst_ref=outs[a].at[me],
                send_sem=send_sems.at[a, k], recv_sem=recv_sems.at[a, k], device_id=peer, device_id_type=MESH)

        def arrival(a, k, peer):
            pid = 4 * peer[0] + 2 * peer[1] + peer[2]
            return pltpu.make_async_remote_copy(
                src_ref=ins[a].at[pid], dst_ref=outs[a].at[pid],
                send_sem=send_sems.at[a, k], recv_sem=recv_sems.at[a, k], device_id=peer, device_id_type=MESH)

        mine = []
        for a in range(n):
            cp = pltpu.make_async_copy(ins[a].at[me], outs[a].at[me], local_sems.at[a])
            cp.start()
            mine.append(cp)
        started = []
        for a in range(n):
            for k, peer in enumerate(peers):
                cp = copy(a, k, peer)
                cp.start()
                started.append(cp)
        for a in range(n):
            for k, peer in enumerate(peers):
                arrival(a, k, peer).wait_recv()
        for cp in started:
            cp.wait_send()
        for cp in mine:
            cp.wait()

    return pl.pallas_call(
        body, name=name,
        out_shape=[SDS(s.shape, s.dtype) for s in slabs],
        in_specs=[ANY] * n, out_specs=[ANY] * n,
        scratch_shapes=[pltpu.SemaphoreType.DMA((n, 7)), pltpu.SemaphoreType.DMA((n, 7)), pltpu.SemaphoreType.DMA((n,))],
    )(*slabs)


class _Ride:
    def __init__(self, kind, arrays):
        self.kind, self.arrays, self.n = kind, list(arrays), len(arrays)
        self.in_specs = [ANY] * self.n
        self.out_specs = [ANY] * self.n
        self.out_shape = [SDS(((N_DEV,) + a.shape) if kind == "gather" else a.shape, a.dtype) for a in self.arrays]
        self.scratch = [pltpu.SemaphoreType.DMA((self.n, 7)), pltpu.SemaphoreType.DMA((self.n, 7)),
                        pltpu.SemaphoreType.DMA((self.n,))]

    def _copies(self, ins, outs, sems, arriving):
        send_sems, recv_sems, local_sems = sems
        x, y, c = _position()
        me = 4 * x + 2 * y + c
        peers = [(x ^ bx, y ^ by, c ^ bc) for bx in (0, 1) for by in (0, 1) for bc in (0, 1)][1:]
        local, remote = [], []
        for a in range(self.n):
            own = ins[a] if self.kind == "gather" else ins[a].at[me]
            local.append(pltpu.make_async_copy(own, outs[a].at[me], local_sems.at[a]))
            for k, peer in enumerate(peers):
                pid = 4 * peer[0] + 2 * peer[1] + peer[2]
                remote.append(pltpu.make_async_remote_copy(
                    src_ref=ins[a] if self.kind == "gather" else ins[a].at[pid],
                    dst_ref=outs[a].at[pid if arriving else me],
                    send_sem=send_sems.at[a, k], recv_sem=recv_sems.at[a, k], device_id=peer, device_id_type=MESH))
        return local, remote

    def start(self, ins, outs, sems):
        local, sends = self._copies(ins, outs, sems, False)
        for cp in local + sends:
            cp.start()

    def wait(self, ins, outs, sems):
        local, arrivals = self._copies(ins, outs, sems, True)
        for cp in arrivals:
            cp.wait_recv()
        for cp in arrivals:
            cp.wait_send()
        for cp in local:
            cp.wait()


def _norm_matmul(x, g, w, name, tm=2048, tn=1024, ride=None, n_cols=None):
    t, d = x.shape
    n = n_cols or w.shape[1]
    tm, tn = min(tm, t), min(tn, n)
    ni, nj = t // tm, n // tn
    nr = ride.n if ride else 0

    def body(*refs):
        x_ref, g_ref, w_ref = refs[:3]
        hn_ref, o_ref = refs[3 + nr:5 + nr]
        rin, rout, sems = refs[3:3 + nr], refs[5 + nr:5 + 2 * nr], refs[5 + 2 * nr:]
        i, j = pl.program_id(0), pl.program_id(1)
        if ride:
            @pl.when((i == 0) & (j == 0))
            def _():
                ride.start(rin, rout, sems)

        @pl.when(j == 0)
        def _():
            xf = x_ref[...]
            r = lax.rsqrt(jnp.mean(xf * xf, axis=-1, keepdims=True) + NORM_EPS)
            hn_ref[...] = ((xf * r) * g_ref[...]).astype(BF16)

        o_ref[...] = jnp.dot(hn_ref[...], w_ref[...], preferred_element_type=F32).astype(o_ref.dtype)
        if ride:
            @pl.when((i == ni - 1) & (j == nj - 1))
            def _():
                ride.wait(rin, rout, sems)

    return pl.pallas_call(
        body, name=name, grid=(ni, nj),
        in_specs=[pl.BlockSpec((tm, d), lambda i, j: (i, 0)), pl.BlockSpec((1, d), lambda i, j: (0, 0)),
                  pl.BlockSpec((d, tn), lambda i, j: (0, j))] + (ride.in_specs if ride else []),
        out_specs=[pl.BlockSpec((tm, d), lambda i, j: (i, 0)), pl.BlockSpec((tm, tn), lambda i, j: (i, j))]
        + (ride.out_specs if ride else []),
        out_shape=[SDS((t, d), BF16), SDS((t, n), BF16)] + (ride.out_shape if ride else []),
        scratch_shapes=ride.scratch if ride else [],
        compiler_params=_cparams("arbitrary", "arbitrary") if ride else _cparams("parallel", "arbitrary"),
    )(x, g, w, *(ride.arrays if ride else []))


def _matmul_f32out(a, w, name, tm=512):
    t, k = a.shape
    n = w.shape[1]

    def body(a_ref, w_ref, o_ref):
        o_ref[...] = jnp.dot(a_ref[...], w_ref[...], preferred_element_type=F32)

    return pl.pallas_call(
        body, name=name, grid=(t // tm,),
        in_specs=[pl.BlockSpec((tm, k), lambda i: (i, 0)), pl.BlockSpec((k, n), lambda i: (0, 0))],
        out_specs=pl.BlockSpec((tm, n), lambda i: (i, 0)),
        out_shape=SDS((t, n), F32),
        compiler_params=_cparams("parallel"),
    )(a, w)


def _out_matmul_residual(y, w, x, name, tm=1024):
    t, e = y.shape
    d = w.shape[1]
    tm = min(tm, t)

    def body(y_ref, w_ref, x_ref, o_ref):
        o_ref[...] = x_ref[...] + jnp.dot(y_ref[...], w_ref[...], preferred_element_type=F32)

    return pl.pallas_call(
        body, name=name, grid=(t // tm,),
        in_specs=[pl.BlockSpec((tm, e), lambda i: (i, 0)), pl.BlockSpec((e, d), lambda i: (0, 0)),
                  pl.BlockSpec((tm, d), lambda i: (i, 0))],
        out_specs=pl.BlockSpec((tm, d), lambda i: (i, 0)),
        out_shape=SDS((t, d), F32),
        compiler_params=_cparams("parallel"),
    )(y, w, x)


def _matmul_nt(a, w, name, tm=1024):
    t, d = a.shape
    e = w.shape[0]
    tm = min(tm, t)

    def body(a_ref, w_ref, o_ref):
        o_ref[...] = lax.dot_general(a_ref[...].astype(BF16), w_ref[...], NT_DIMS,
                                     preferred_element_type=F32).astype(o_ref.dtype)

    return pl.pallas_call(
        body, name=name, grid=(t // tm,),
        in_specs=[pl.BlockSpec((tm, d), lambda i: (i, 0)), pl.BlockSpec((e, d), lambda i: (0, 0))],
        out_specs=pl.BlockSpec((tm, e), lambda i: (i, 0)),
        out_shape=SDS((t, e), BF16),
        compiler_params=_cparams("parallel"),
    )(a, w)


def _matmul_tn(a, b, name, out_dtype=BF16, tm=1024, tn=1024, tk=1024):
    t, m = a.shape
    n = b.shape[1]
    tm, tn, tk = min(tm, m), _divisor_tile(n, 2 * tn), min(tk, t)
    nk = t // tk

    def body(a_ref, b_ref, o_ref, acc_ref):
        k = pl.program_id(2)

        @pl.when(k == 0)
        def _():
            acc_ref[...] = jnp.zeros_like(acc_ref)

        acc_ref[...] += lax.dot_general(a_ref[...].astype(BF16), b_ref[...].astype(BF16), TN_DIMS,
                                        preferred_element_type=F32)

        @pl.when(k == nk - 1)
        def _():
            o_ref[...] = acc_ref[...].astype(o_ref.dtype)

    return pl.pallas_call(
        body, name=name, grid=(m // tm, n // tn, nk),
        in_specs=[pl.BlockSpec((tk, tm), lambda i, j, k: (k, i)), pl.BlockSpec((tk, tn), lambda i, j, k: (k, j))],
        out_specs=pl.BlockSpec((tm, tn), lambda i, j, k: (i, j)),
        out_shape=SDS((m, n), out_dtype),
        scratch_shapes=[pltpu.VMEM((tm, tn), F32)],
        compiler_params=_cparams("parallel", "parallel", "arbitrary"),
    )(a, b)


def _dproj_matmul_normbwd(dproj, w, x, g, dxn, name, tm=1024, tk=1024, ride=None):
    t, n = dproj.shape
    d = w.shape[0]
    tm, tk = min(tm, t), _divisor_tile(n, tk)
    nk = n // tk
    ni = t // tm
    nr = ride.n if ride else 0

    def body(*refs):
        dp_ref, w_ref, x_ref, g_ref, dxn_ref = refs[:5]
        dx_ref, dg_ref = refs[5 + nr:7 + nr]
        rin, rout = refs[5:5 + nr], refs[7 + nr:7 + 2 * nr]
        sems, acc_ref = refs[7 + 2 * nr:-1], refs[-1]
        i, k = pl.program_id(0), pl.program_id(1)
        if ride:
            @pl.when((i == 0) & (k == 0))
            def _():
                ride.start(rin, rout, sems)

        @pl.when(k == 0)
        def _():
            acc_ref[...] = jnp.zeros_like(acc_ref)

        acc_ref[...] += lax.dot_general(dp_ref[...], w_ref[...], NT_DIMS, preferred_element_type=F32)

        @pl.when(k == nk - 1)
        def _():
            dhn = acc_ref[...]
            xf = x_ref[...]
            r = lax.rsqrt(jnp.mean(xf * xf, axis=-1, keepdims=True) + NORM_EPS)
            xh = xf * r
            dy = dhn * g_ref[...]
            dx_ref[...] = dxn_ref[...] + r * (dy - xh * jnp.mean(dy * xh, axis=-1, keepdims=True))
            part = jnp.sum(dhn * xh, axis=0, keepdims=True)

            @pl.when(i == 0)
            def _():
                dg_ref[...] = part

            @pl.when(i > 0)
            def _():
                dg_ref[...] += part

        if ride:
            @pl.when((i == ni - 1) & (k == nk - 1))
            def _():
                ride.wait(rin, rout, sems)

    return pl.pallas_call(
        body, name=name, grid=(ni, nk),
        in_specs=[pl.BlockSpec((tm, tk), lambda i, k: (i, k)), pl.BlockSpec((d, tk), lambda i, k: (0, k)),
                  pl.BlockSpec((tm, d), lambda i, k: (i, 0)), pl.BlockSpec((1, d), lambda i, k: (0, 0)),
                  pl.BlockSpec((tm, d), lambda i, k: (i, 0))] + (ride.in_specs if ride else []),
        out_specs=[pl.BlockSpec((tm, d), lambda i, k: (i, 0)), pl.BlockSpec((1, d), lambda i, k: (0, 0))]
        + (ride.out_specs if ride else []),
        out_shape=[SDS((t, d), F32), SDS((1, d), F32)] + (ride.out_shape if ride else []),
        scratch_shapes=(ride.scratch if ride else []) + [pltpu.VMEM((tm, d), F32)],
        compiler_params=_cparams("arbitrary", "arbitrary"),
    )(dproj, w, x, g, dxn, *(ride.arrays if ride else []))


def _out_matmul_loss(y, w, x, target, name, tm=1024):
    t, e = y.shape
    d = w.shape[1]
    tm = min(tm, t)
    inv_d = 1.0 / d

    def body(y_ref, w_ref, x_ref, t_ref, part_ref, dy_ref):
        i = pl.program_id(0)
        err = (x_ref[...] + jnp.dot(y_ref[...], w_ref[...], preferred_element_type=F32)) - t_ref[...]
        dy_ref[...] = err * inv_d
        part = jnp.sum(err * err, axis=0, keepdims=True) * (0.5 * inv_d)

        @pl.when(i == 0)
        def _():
            part_ref[...] = part

        @pl.when(i > 0)
        def _():
            part_ref[...] += part

    return pl.pallas_call(
        body, name=name, grid=(t // tm,),
        in_specs=[pl.BlockSpec((tm, e), lambda i: (i, 0)), pl.BlockSpec((e, d), lambda i: (0, 0)),
                  pl.BlockSpec((tm, d), lambda i: (i, 0)), pl.BlockSpec((tm, d), lambda i: (i, 0))],
        out_specs=[pl.BlockSpec((1, d), lambda i: (0, 0)), pl.BlockSpec((tm, d), lambda i: (i, 0))],
        out_shape=[SDS((1, d), F32), SDS((t, d), F32)],
        compiler_params=_cparams("arbitrary"),
    )(y, w, x, target)


CONV_ROWS = 32
CONV_COLS = 256


def _conv_chunk(src_ref, base, w_ref, width, r0, c0, flip):
    acc = None
    for k in range(width):
        off = base + r0 + ((width - 1 - k) if flip else (k - (width - 1)))
        term = src_ref[pl.ds(off, CONV_ROWS), pl.ds(c0, CONV_COLS)] * w_ref[pl.ds(k, 1), pl.ds(c0, CONV_COLS)]
        acc = term if acc is None else acc + term
    return acc


def _conv_weight_grad(dw_ref, d_ref, src_ref, base, width, tt, e):
    for c0 in range(0, e, CONV_COLS):
        for k in range(width):
            acc = None
            for r0 in range(0, tt, CONV_ROWS):
                prod = (d_ref[pl.ds(r0, CONV_ROWS), pl.ds(c0, CONV_COLS)]
                        * src_ref[pl.ds(base + r0 - (width - 1) + k, CONV_ROWS), pl.ds(c0, CONV_COLS)])
                part = _rows8(prod)
                acc = part if acc is None else acc + part
            dw_ref[pl.ds(k, 1), pl.ds(c0, CONV_COLS)] += jnp.sum(acc, axis=0, keepdims=True)


def _shifted_copies(dst_ref, src_ref, c0, length, sign):
    lo, hi = (SUBLANES, length) if sign < 0 else (0, length - SUBLANES)
    for b in range(SUBLANES):
        for r0 in range(lo, hi, CONV_ROWS):
            n = min(CONV_ROWS, hi - r0)
            dst_ref[b, pl.ds(r0, n), :] = src_ref[pl.ds(r0 + sign * b, n), pl.ds(c0, CONV_COLS)]


def _conv_aligned(copies_ref, base, w_ref, width, r0, c0, sign):
    acc = None
    for d in range(width):
        a, b = divmod(d, SUBLANES)
        term = (copies_ref[b, pl.ds(base + r0 + sign * SUBLANES * a, CONV_ROWS), :]
                * w_ref[pl.ds(width - 1 - d, 1), pl.ds(c0, CONV_COLS)])
        acc = term if acc is None else acc + term
    return acc


def _conv_weight_grad_aligned(dw_ref, dcopies_ref, src_ref, base, width, tt, c0):
    for d in range(width):
        a, b = divmod(d, SUBLANES)
        acc = None
        for r0 in range(0, tt, CONV_ROWS):
            prod = (dcopies_ref[b, pl.ds(r0, CONV_ROWS), :]
                    * src_ref[pl.ds(base + r0 - SUBLANES * a, CONV_ROWS), pl.ds(c0, CONV_COLS)])
            part = _rows8(prod)
            acc = part if acc is None else acc + part
        dw_ref[pl.ds(width - 1 - d, 1), pl.ds(c0, CONV_COLS)] += jnp.sum(acc, axis=0, keepdims=True)


LN_ROWS = 16


def _a_mid_fwd(proj, cw, cb, lg, lb, width, name, tt=256, ride=None):
    t, e3 = proj.shape
    e = e3 // 3
    halo = CONF_HALO
    nt = t // tt
    nr = ride.n if ride else 0

    def body(*refs):
        rin, rout = refs[5:5 + nr], refs[7 + nr:7 + 2 * nr]
        scratch = refs[7 + 2 * nr:]
        sems, own = (scratch[:3], scratch[3:]) if ride else ((), scratch)
        if ride:
            @pl.when(pl.program_id(0) == 0)
            def _():
                ride.start(rin, rout, sems)

        tile(*refs[:5], *refs[5 + nr:7 + nr], *own)
        if ride:
            @pl.when(pl.program_id(0) == nt - 1)
            def _():
                ride.wait(rin, rout, sems)

    def tile(p_ref, cw_ref, cb_ref, lg_ref, lb_ref, y_ref, u1_ref, ubuf, shifted):
        i = pl.program_id(0)

        @pl.when(i == 0)
        def _():
            ubuf[pl.ds(0, halo), :] = jnp.zeros((halo, e), F32)

        @pl.when(i > 0)
        def _():
            ubuf[pl.ds(0, halo), :] = ubuf[pl.ds(tt, halo), :]

        for r0 in range(0, tt, CONV_ROWS):
            val = p_ref[pl.ds(r0, CONV_ROWS), pl.ds(0, e)].astype(F32)
            gate = p_ref[pl.ds(r0, CONV_ROWS), pl.ds(e, e)].astype(F32)
            ubuf[pl.ds(halo + r0, CONV_ROWS), :] = val * _sigmoid(gate)
        for c0 in range(0, e, CONV_COLS):
            _shifted_copies(shifted, ubuf, c0, halo + tt, -1)
            for r0 in range(0, tt, CONV_ROWS):
                acc = _conv_aligned(shifted, halo, cw_ref, width, r0, c0, -1)
                u1_ref[pl.ds(r0, CONV_ROWS), pl.ds(c0, CONV_COLS)] = acc + cb_ref[:, pl.ds(c0, CONV_COLS)]
        for r0 in range(0, tt, LN_ROWS):
            u = u1_ref[pl.ds(r0, LN_ROWS), :]
            mu = jnp.mean(u, axis=-1, keepdims=True)
            dlt = u - mu
            var = jnp.mean(dlt * dlt, axis=-1, keepdims=True)
            u2 = (dlt * lax.rsqrt(var + NORM_EPS)) * lg_ref[...] + lb_ref[...]
            z = p_ref[pl.ds(r0, LN_ROWS), pl.ds(2 * e, e)].astype(F32)
            y_ref[pl.ds(r0, LN_ROWS), :] = (_silu(u2) * _silu(z)).astype(BF16)

    return pl.pallas_call(
        body, name=name, grid=(nt,),
        in_specs=[pl.BlockSpec((tt, e3), lambda i: (i, 0)), pl.BlockSpec(cw.shape, lambda i: (0, 0)),
                  pl.BlockSpec((1, e), lambda i: (0, 0)), pl.BlockSpec((1, e), lambda i: (0, 0)),
                  pl.BlockSpec((1, e), lambda i: (0, 0))] + (ride.in_specs if ride else []),
        out_specs=[pl.BlockSpec((tt, e), lambda i: (i, 0)), pl.BlockSpec((tt, e), lambda i: (i, 0))]
        + (ride.out_specs if ride else []),
        out_shape=[SDS((t, e), BF16), SDS((t, e), F32)] + (ride.out_shape if ride else []),
        scratch_shapes=(ride.scratch if ride else [])
        + [pltpu.VMEM((halo + tt, e), F32), pltpu.VMEM((SUBLANES, halo + tt, CONV_COLS), F32)],
        compiler_params=_cparams("arbitrary"),
    )(proj, cw, cb, lg, lb, *(ride.arrays if ride else []))


def _a_mid_bwd(proj, u1, dyz, cw, lg, lb, width, name, tt=256, ride=None):
    t, e3 = proj.shape
    e = e3 // 3
    halo = CONF_HALO
    nt = t // tt
    hb = tt // halo
    nr = ride.n if ride else 0

    def body(*refs):
        rin, rout = refs[7:7 + nr], refs[12 + nr:12 + 2 * nr]
        scratch = refs[12 + 2 * nr:]
        sems, own = (scratch[:3], scratch[3:]) if ride else ((), scratch)
        if ride:
            @pl.when(pl.program_id(0) == 0)
            def _():
                ride.start(rin, rout, sems)

        tile(*refs[:7], *refs[7 + nr:12 + nr], *own)
        if ride:
            @pl.when(pl.program_id(0) == nt - 1)
            def _():
                ride.wait(rin, rout, sems)

    def tile(p_ref, pp_ref, u1_ref, dy_ref, cw_ref, lg_ref, lb_ref,
             dp_ref, dcw_ref, dcb_ref, dlg_ref, dlb_ref, ubuf, dbuf, shifted, acc_cb, acc_lg, acc_lb):
        i = pl.program_id(0)
        ti = nt - 1 - i

        @pl.when(i == 0)
        def _():
            dbuf[pl.ds(tt, halo), :] = jnp.zeros((halo, e), F32)
            dcw_ref[...] = jnp.zeros_like(dcw_ref)
            acc_cb[...] = jnp.zeros_like(acc_cb)
            acc_lg[...] = jnp.zeros_like(acc_lg)
            acc_lb[...] = jnp.zeros_like(acc_lb)

        @pl.when(i > 0)
        def _():
            dbuf[pl.ds(tt, halo), :] = dbuf[pl.ds(0, halo), :]

        keep = (ti > 0).astype(F32)
        ubuf[pl.ds(0, halo), :] = keep * (pp_ref[:, pl.ds(0, e)].astype(F32) * _sigmoid(pp_ref[:, pl.ds(e, e)].astype(F32)))
        for r0 in range(0, tt, CONV_ROWS):
            val = p_ref[pl.ds(r0, CONV_ROWS), pl.ds(0, e)].astype(F32)
            gate = p_ref[pl.ds(r0, CONV_ROWS), pl.ds(e, e)].astype(F32)
            ubuf[pl.ds(halo + r0, CONV_ROWS), :] = val * _sigmoid(gate)

        for r0 in range(0, tt, LN_ROWS):
            rows = pl.ds(r0, LN_ROWS)
            u = u1_ref[rows, :]
            mu = jnp.mean(u, axis=-1, keepdims=True)
            dlt = u - mu
            var = jnp.mean(dlt * dlt, axis=-1, keepdims=True)
            rstd = lax.rsqrt(var + NORM_EPS)
            xh = dlt * rstd
            u2 = xh * lg_ref[...] + lb_ref[...]
            s2 = _sigmoid(u2)
            u3 = u2 * s2
            z = p_ref[rows, pl.ds(2 * e, e)].astype(F32)
            sz = _sigmoid(z)
            dy = dy_ref[rows, :].astype(F32)
            dp_ref[rows, pl.ds(2 * e, e)] = (dy * u3 * (sz * (1.0 + z * (1.0 - sz)))).astype(BF16)
            du2 = (dy * (z * sz)) * (s2 * (1.0 + u2 * (1.0 - s2)))
            acc_lg[...] += _rows8(du2 * xh)
            acc_lb[...] += _rows8(du2)
            dxh = du2 * lg_ref[...]
            m1 = jnp.mean(dxh, axis=-1, keepdims=True)
            m2 = jnp.mean(dxh * xh, axis=-1, keepdims=True)
            du1 = rstd * (dxh - m1 - xh * m2)
            dbuf[rows, :] = du1
            acc_cb[...] += _rows8(du1)

        for c0 in range(0, e, CONV_COLS):
            _shifted_copies(shifted, dbuf, c0, tt + halo, 1)
            for r0 in range(0, tt, CONV_ROWS):
                du0 = _conv_aligned(shifted, 0, cw_ref, width, r0, c0, 1)
                rows, cols = pl.ds(r0, CONV_ROWS), pl.ds(c0, CONV_COLS)
                val = p_ref[rows, cols].astype(F32)
                sg = _sigmoid(p_ref[rows, pl.ds(e + c0, CONV_COLS)].astype(F32))
                dp_ref[rows, cols] = (du0 * sg).astype(BF16)
                dp_ref[rows, pl.ds(e + c0, CONV_COLS)] = (du0 * val * sg * (1.0 - sg)).astype(BF16)
            _conv_weight_grad_aligned(dcw_ref, shifted, ubuf, halo, width, tt, c0)

        @pl.when(i == nt - 1)
        def _():
            dcb_ref[...] = jnp.sum(acc_cb[...], axis=0, keepdims=True)
            dlg_ref[...] = jnp.sum(acc_lg[...], axis=0, keepdims=True)
            dlb_ref[...] = jnp.sum(acc_lb[...], axis=0, keepdims=True)

    vec = pl.BlockSpec((1, e), lambda i: (0, 0))
    return pl.pallas_call(
        body, name=name, grid=(nt,),
        in_specs=[pl.BlockSpec((tt, e3), lambda i: (nt - 1 - i, 0)),
                  pl.BlockSpec((halo, e3), lambda i: (jnp.maximum((nt - 1 - i) * hb - 1, 0), 0)),
                  pl.BlockSpec((tt, e), lambda i: (nt - 1 - i, 0)), pl.BlockSpec((tt, e), lambda i: (nt - 1 - i, 0)),
                  pl.BlockSpec(cw.shape, lambda i: (0, 0)), vec, vec] + (ride.in_specs if ride else []),
        out_specs=[pl.BlockSpec((tt, e3), lambda i: (nt - 1 - i, 0)), pl.BlockSpec(cw.shape, lambda i: (0, 0)), vec, vec, vec]
        + (ride.out_specs if ride else []),
        out_shape=[SDS((t, e3), BF16), SDS(cw.shape, F32), SDS((1, e), F32), SDS((1, e), F32), SDS((1, e), F32)]
        + (ride.out_shape if ride else []),
        scratch_shapes=(ride.scratch if ride else [])
        + [pltpu.VMEM((halo + tt, e), F32), pltpu.VMEM((tt + halo, e), F32),
           pltpu.VMEM((SUBLANES, halo + tt, CONV_COLS), F32),
           pltpu.VMEM((SUBLANES, e), F32), pltpu.VMEM((SUBLANES, e), F32), pltpu.VMEM((SUBLANES, e), F32)],
        compiler_params=_cparams("arbitrary"),
    )(proj, proj, u1, dyz, cw, lg, lb, *(ride.arrays if ride else []))


def _c_mid_fwd(proj, cw, width, name, tt=256):
    t, e4 = proj.shape
    e = e4 // 4
    halo = SHORT_HALO

    def body(p_ref, cw_ref, y_ref, wbuf):
        i = pl.program_id(0)

        @pl.when(i == 0)
        def _():
            wbuf[pl.ds(0, halo), :] = jnp.zeros((halo, e), F32)

        @pl.when(i > 0)
        def _():
            wbuf[pl.ds(0, halo), :] = wbuf[pl.ds(tt, halo), :]

        for r0 in range(0, tt, CONV_ROWS):
            rows = pl.ds(r0, CONV_ROWS)
            wbuf[pl.ds(halo + r0, CONV_ROWS), :] = p_ref[rows, pl.ds(2 * e, e)].astype(F32) * p_ref[rows, pl.ds(0, e)].astype(F32)
        for c0 in range(0, e, CONV_COLS):
            for r0 in range(0, tt, CONV_ROWS):
                rows = pl.ds(r0, CONV_ROWS)
                cv = _conv_chunk(wbuf, halo, cw_ref, width, r0, c0, False)
                bg = p_ref[rows, pl.ds(e + c0, CONV_COLS)].astype(F32)
                z = p_ref[rows, pl.ds(3 * e + c0, CONV_COLS)].astype(F32)
                y_ref[rows, pl.ds(c0, CONV_COLS)] = ((bg * cv) * _silu(z)).astype(BF16)

    return pl.pallas_call(
        body, name=name, grid=(t // tt,),
        in_specs=[pl.BlockSpec((tt, e4), lambda i: (i, 0)), pl.BlockSpec(cw.shape, lambda i: (0, 0))],
        out_specs=pl.BlockSpec((tt, e), lambda i: (i, 0)),
        out_shape=SDS((t, e), BF16),
        scratch_shapes=[pltpu.VMEM((halo + tt, e), F32)],
        compiler_params=_cparams("arbitrary"),
    )(proj, cw)


def _c_mid_bwd(proj, dyz, cw, width, name, tt=256):
    t, e4 = proj.shape
    e = e4 // 4
    halo = SHORT_HALO
    nt = t // tt
    hb = tt // halo

    def body(p_ref, pp_ref, dy_ref, cw_ref, dp_ref, dcw_ref, wbuf, dbuf):
        i = pl.program_id(0)
        ti = nt - 1 - i

        @pl.when(i == 0)
        def _():
            dbuf[pl.ds(tt, halo), :] = jnp.zeros((halo, e), F32)
            dcw_ref[...] = jnp.zeros_like(dcw_ref)

        @pl.when(i > 0)
        def _():
            dbuf[pl.ds(tt, halo), :] = dbuf[pl.ds(0, halo), :]

        keep = (ti > 0).astype(F32)
        wbuf[pl.ds(0, halo), :] = keep * (pp_ref[:, pl.ds(2 * e, e)].astype(F32) * pp_ref[:, pl.ds(0, e)].astype(F32))
        for r0 in range(0, tt, CONV_ROWS):
            rows = pl.ds(r0, CONV_ROWS)
            wbuf[pl.ds(halo + r0, CONV_ROWS), :] = p_ref[rows, pl.ds(2 * e, e)].astype(F32) * p_ref[rows, pl.ds(0, e)].astype(F32)
        for c0 in range(0, e, CONV_COLS):
            for r0 in range(0, tt, CONV_ROWS):
                rows, cols = pl.ds(r0, CONV_ROWS), pl.ds(c0, CONV_COLS)
                cv = _conv_chunk(wbuf, halo, cw_ref, width, r0, c0, False)
                bg = p_ref[rows, pl.ds(e + c0, CONV_COLS)].astype(F32)
                z = p_ref[rows, pl.ds(3 * e + c0, CONV_COLS)].astype(F32)
                sz = _sigmoid(z)
                dyz_c = dy_ref[rows, cols].astype(F32)
                dy = dyz_c * (z * sz)
                dp_ref[rows, pl.ds(3 * e + c0, CONV_COLS)] = (dyz_c * (bg * cv) * (sz * (1.0 + z * (1.0 - sz)))).astype(BF16)
                dp_ref[rows, pl.ds(e + c0, CONV_COLS)] = (dy * cv).astype(BF16)
                dbuf[rows, cols] = dy * bg
        for c0 in range(0, e, CONV_COLS):
            for r0 in range(0, tt, CONV_ROWS):
                rows, cols = pl.ds(r0, CONV_ROWS), pl.ds(c0, CONV_COLS)
                dw = _conv_chunk(dbuf, 0, cw_ref, width, r0, c0, True)
                dp_ref[rows, pl.ds(2 * e + c0, CONV_COLS)] = (dw * p_ref[rows, cols].astype(F32)).astype(BF16)
                dp_ref[rows, cols] = (dw * p_ref[rows, pl.ds(2 * e + c0, CONV_COLS)].astype(F32)).astype(BF16)
        _conv_weight_grad(dcw_ref, dbuf, wbuf, halo, width, tt, e)

    return pl.pallas_call(
        body, name=name, grid=(nt,),
        in_specs=[pl.BlockSpec((tt, e4), lambda i: (nt - 1 - i, 0)),
                  pl.BlockSpec((halo, e4), lambda i: (jnp.maximum((nt - 1 - i) * hb - 1, 0), 0)),
                  pl.BlockSpec((tt, e), lambda i: (nt - 1 - i, 0)), pl.BlockSpec(cw.shape, lambda i: (0, 0))],
        out_specs=[pl.BlockSpec((tt, e4), lambda i: (nt - 1 - i, 0)), pl.BlockSpec(cw.shape, lambda i: (0, 0))],
        out_shape=[SDS((t, e4), BF16), SDS(cw.shape, F32)],
        scratch_shapes=[pltpu.VMEM((halo + tt, e), F32), pltpu.VMEM((tt + halo, e), F32)],
        compiler_params=_cparams("arbitrary"),
    )(proj, proj, dyz, cw)


def _b_prep_fwd(proj, flog, fbias, qg, kg, heads, name, tt=256):
    t, e4 = proj.shape
    e = e4 // 4
    scale = HEAD_DIM ** -0.5 * LOG2E

    def body(q_ref, k_ref, fl_ref, fb_ref, qg_ref, kg_ref, qs_ref, kn_ref, knt_ref, c_ref, ct_ref, carry):
        i = pl.program_id(0)

        @pl.when(i == 0)
        def _():
            carry[...] = jnp.zeros_like(carry)

        for h in range(heads):
            cols = pl.ds(h * HEAD_DIM, HEAD_DIM)
            qh = q_ref[:, cols].astype(F32)
            r = lax.rsqrt(jnp.mean(qh * qh, axis=-1, keepdims=True) + NORM_EPS)
            qs_ref[:, cols] = (((qh * r) * qg_ref[:, cols]) * scale).astype(BF16)
            kh = k_ref[:, cols].astype(F32)
            r = lax.rsqrt(jnp.mean(kh * kh, axis=-1, keepdims=True) + NORM_EPS)
            kn = (kh * r) * kg_ref[:, cols]
            kn_ref[:, cols] = kn.astype(BF16)
            knt_ref[cols, :] = kn.T.astype(BF16)

        a = fl_ref[...] + fb_ref[...]
        lf = jnp.minimum(a, 0.0) - jnp.log(1.0 + jnp.exp(-jnp.abs(a)))
        tri = (lax.broadcasted_iota(jnp.int32, (tt, tt), 0) >= lax.broadcasted_iota(jnp.int32, (tt, tt), 1)).astype(BF16)
        c = _tri_matmul(tri, lf) + carry[...]
        c_ref[...] = c
        ct_ref[...] = (c * LOG2E).T
        carry[...] = c_ref[pl.ds(tt - 1, 1), :]

    return pl.pallas_call(
        body, name=name, grid=(t // tt,),
        in_specs=[pl.BlockSpec((tt, e), lambda i: (i, 0)), pl.BlockSpec((tt, e), lambda i: (i, 1)),
                  pl.BlockSpec((tt, LANES), lambda i: (i, 0)), pl.BlockSpec((1, LANES), lambda i: (0, 0)),
                  pl.BlockSpec((1, e), lambda i: (0, 0)), pl.BlockSpec((1, e), lambda i: (0, 0))],
        out_specs=[pl.BlockSpec((tt, e), lambda i: (i, 0)), pl.BlockSpec((tt, e), lambda i: (i, 0)),
                   pl.BlockSpec((e, tt), lambda i: (0, i)),
                   pl.BlockSpec((tt, LANES), lambda i: (i, 0)), pl.BlockSpec((LANES, tt), lambda i: (0, i))],
        out_shape=[SDS((t, e), BF16), SDS((t, e), BF16), SDS((e, t), BF16), SDS((t, LANES), F32), SDS((LANES, t), F32)],
        scratch_shapes=[pltpu.VMEM((1, LANES), F32)],
        compiler_params=_cparams("arbitrary"),
    )(proj, proj, flog, fbias, qg, kg)


ATT_BLOCK = 1024
ATT_CHUNK_FWD = 512
ATT_CHUNK_BWD = 256
NEG_BIG = -1e30
LOG2E = 1.4426950408889634
LN2 = 0.6931471805599453


def _flash_fwd(qs, kn, proj, ck, heads, name):
    t, e = qs.shape
    blk = min(ATT_BLOCK, t)
    cw = min(ATT_CHUNK_FWD, blk // 2)
    nq, nch = t // blk, blk // cw
    assert nch % 2 == 0

    def body(q_ref, k_ref, v_ref, ck_ref, z_ref, o_ref, y_ref, m_ref, l_ref, s_a, s_b):
        i = pl.program_id(1)
        q = q_ref[...]
        bufs = (s_a, s_b)

        def key_rows(j, c):
            return pl.ds(pl.multiple_of(j * blk, blk) + c * cw, cw)

        def logits(j, c):
            bufs[c % 2][...] = (lax.dot_general(q, k_ref[key_rows(j, c), :], NT_DIMS, preferred_element_type=F32)
                            - ck_ref[j][:, c * cw:(c + 1) * cw])

        def weights(c, m, masked):
            s = bufs[c % 2][...]
            if masked:
                keep = lax.broadcasted_iota(jnp.int32, (blk, cw), 0) >= (lax.broadcasted_iota(jnp.int32, (blk, cw), 1) + c * cw)
                s = jnp.where(keep, s, NEG_BIG)
            m_new = jnp.maximum(m, jnp.ceil(jnp.max(s, axis=-1, keepdims=True)))
            return m_new, jnp.exp2(m - m_new), jnp.exp2(s - m_new).astype(BF16)

        ones = jnp.ones((cw, HEAD_DIM), BF16)

        def block(j, carry, masked):
            m, acc = carry
            for c in range(nch):
                if c + 1 < nch:
                    logits(j, c + 1)
                elif not masked:
                    logits(j + 1, 0)
                m, alpha, p = weights(c, m, masked)
                v1 = jnp.concatenate([v_ref[key_rows(j, c), :], ones], axis=1)
                acc = alpha * acc + jnp.dot(p, v1, preferred_element_type=F32)
            return m, acc

        logits(0, 0)
        carry = (jnp.full((blk, 1), NEG_BIG, F32), jnp.zeros((blk, 2 * HEAD_DIM), F32))
        carry = lax.fori_loop(0, i, lambda j, cr: block(j, cr, False), carry)
        m, acc = block(i, carry, True)
        l = acc[:, HEAD_DIM:HEAD_DIM + 1]
        o = acc[:, :HEAD_DIM] / l
        o_ref[...] = o
        y_ref[...] = (o * _silu(z_ref[...].astype(F32))).astype(BF16)
        m_ref[...] = jnp.broadcast_to(m, (blk, LANES))
        l_ref[...] = jnp.broadcast_to(l, (blk, LANES))

    head_all = pl.BlockSpec((t, HEAD_DIM), lambda h, i: (0, h))
    tile = pl.BlockSpec((blk, HEAD_DIM), lambda h, i: (i, h))
    stat = pl.BlockSpec((None, blk, LANES), lambda h, i: (h, i, 0))
    return pl.pallas_call(
        body, name=name, grid=(heads, nq),
        in_specs=[tile, head_all, pl.BlockSpec((t, HEAD_DIM), lambda h, i: (0, 2 * heads + h)),
                  pl.BlockSpec((None, nq, 1, blk), lambda h, i: (h, 0, 0, 0)),
                  pl.BlockSpec((blk, HEAD_DIM), lambda h, i: (i, 3 * heads + h))],
        out_specs=[tile, tile, stat, stat],
        out_shape=[SDS((t, e), F32), SDS((t, e), BF16), SDS((heads, t, LANES), F32), SDS((heads, t, LANES), F32)],
        scratch_shapes=[pltpu.VMEM((blk, cw), F32), pltpu.VMEM((blk, cw), F32)],
        compiler_params=_cparams("parallel", "arbitrary"),
    )(qs, kn, proj, ck, proj)


def _flash_bwd(qs, kn, knt, proj, ck, dyz, o, mstat, lstat, heads, name):
    t, e = qs.shape
    blk = min(ATT_BLOCK, t)
    cw = min(ATT_CHUNK_BWD, blk // 2)
    nq, nch = t // blk, blk // cw
    assert nch % 2 == 0

    def body(q_ref, dy_ref, o_ref, z_ref, m_ref, l_ref, k_ref, kt_ref, v_ref, ck_ref,
             dq_ref, dk_ref, dv_ref, dc_ref, dz_ref, s_a, s_b, d_a, d_b):
        i = pl.program_id(1)

        @pl.when(i == 0)
        def _():
            dk_ref[...] = jnp.zeros_like(dk_ref)
            dv_ref[...] = jnp.zeros_like(dv_ref)
            dc_ref[...] = jnp.zeros_like(dc_ref)

        z = z_ref[...].astype(F32)
        sz = _sigmoid(z)
        dy = dy_ref[...].astype(F32)
        of = o_ref[...]
        do = ((dy * (z * sz)) / l_ref[:, 0:1]).astype(BF16)
        dz_ref[...] = (dy * of * (sz * (1.0 + z * (1.0 - sz)))).astype(BF16)
        dl = jnp.sum(do.astype(F32) * of, axis=-1, keepdims=True)
        q = q_ref[...]
        q_t = q.astype(F32).T.astype(BF16)
        do_t = do.astype(F32).T.astype(BF16)
        mrow = m_ref[:, 0:1]
        sbuf, dbuf = (s_a, s_b), (d_a, d_b)

        def key_rows(j, c):
            return pl.ds(pl.multiple_of(j * blk, blk) + c * cw, cw)

        def products(j, c):
            rows = key_rows(j, c)
            sbuf[c % 2][...] = (lax.dot_general(q, k_ref[rows, :], NT_DIMS, preferred_element_type=F32)
                            - ck_ref[j][:, c * cw:(c + 1) * cw])
            dbuf[c % 2][...] = lax.dot_general(do, v_ref[rows, :], NT_DIMS, preferred_element_type=F32)

        def weights(c, masked):
            p = jnp.exp2(sbuf[c % 2][...] - mrow)
            if masked:
                keep = lax.broadcasted_iota(jnp.int32, (blk, cw), 0) >= (lax.broadcasted_iota(jnp.int32, (blk, cw), 1) + c * cw)
                p = jnp.where(keep, p, 0.0)
            p = p.astype(BF16)
            ds = p.astype(F32) * (dbuf[c % 2][...] - dl)
            return p, ds.astype(BF16), jnp.sum(ds, axis=0, keepdims=True)

        def outputs(j, c, p, dsb, colsum, dq):
            rows = key_rows(j, c)
            dv_ref[:, rows] += jnp.dot(do_t, p, preferred_element_type=F32)
            dk_ref[:, rows] += jnp.dot(q_t, dsb, preferred_element_type=F32)
            dc_ref[j, :, pl.ds(c * cw, cw)] -= colsum
            return dq + lax.dot_general(kt_ref[:, rows], dsb, NT_DIMS, preferred_element_type=F32)

        def block(j, dq, masked):
            products(j, 1)
            for c in range(nch):
                p, dsb, colsum = weights(c, masked)
                if c + 2 < nch:
                    products(j, c + 2)
                elif c + 2 == nch and not masked:
                    products(j + 1, 0)
                dq = outputs(j, c, p, dsb, colsum, dq)
            return dq

        products(0, 0)
        dq = lax.fori_loop(0, i, lambda j, acc: block(j, acc, False), jnp.zeros((HEAD_DIM, blk), F32))
        dq_ref[...] = block(i, dq, True).T

    tile = pl.BlockSpec((blk, HEAD_DIM), lambda h, i: (i, h))
    stat = pl.BlockSpec((None, blk, LANES), lambda h, i: (h, i, 0))
    head_all = pl.BlockSpec((t, HEAD_DIM), lambda h, i: (0, h))
    head_all_t = pl.BlockSpec((HEAD_DIM, t), lambda h, i: (h, 0))
    cspec = pl.BlockSpec((None, nq, 1, blk), lambda h, i: (h, 0, 0, 0))
    return pl.pallas_call(
        body, name=name, grid=(heads, nq),
        in_specs=[tile, tile, tile, pl.BlockSpec((blk, HEAD_DIM), lambda h, i: (i, 3 * heads + h)), stat, stat, head_all,
                  head_all_t, pl.BlockSpec((t, HEAD_DIM), lambda h, i: (0, 2 * heads + h)), cspec],
        out_specs=[tile, head_all_t, head_all_t, cspec, tile],
        out_shape=[SDS((t, e), F32), SDS((e, t), F32), SDS((e, t), F32), SDS((heads, nq, 1, blk), F32), SDS((t, e), BF16)],
        scratch_shapes=[pltpu.VMEM((blk, cw), F32)] * 4,
        compiler_params=_cparams("parallel", "arbitrary"),
    )(qs, dyz, o, proj, mstat, lstat, kn, knt, proj, ck)


def _b_prep_bwd(dqs, dknt, dvt, dz, proj, qg, kg, dct, flog, fbias, heads, name, tt=256):
    t, e4 = proj.shape
    e = e4 // 4
    nt = t // tt
    scale = HEAD_DIM ** -0.5

    def body(dq_ref, dkt_ref, dvt_ref, dz_ref, q_ref, k_ref, qg_ref, kg_ref, dc_ref, fl_ref, fb_ref,
             dp_ref, dqg_ref, dkg_ref, dfb_ref, carry, dlf, acc_q, acc_k, acc_f):
        i = pl.program_id(0)

        @pl.when(i == 0)
        def _():
            carry[...] = jnp.zeros_like(carry)
            acc_q[...] = jnp.zeros_like(acc_q)
            acc_k[...] = jnp.zeros_like(acc_k)
            acc_f[...] = jnp.zeros_like(acc_f)

        for h in range(heads):
            cols = pl.ds(h * HEAD_DIM, HEAD_DIM)
            for src_ref, d, g_ref, acc, mult, off in ((q_ref, dq_ref[:, cols], qg_ref, acc_q, scale, 0),
                                                      (k_ref, dkt_ref[cols, :].T, kg_ref, acc_k, LN2, e)):
                xf = src_ref[:, cols].astype(F32)
                r = lax.rsqrt(jnp.mean(xf * xf, axis=-1, keepdims=True) + NORM_EPS)
                xh = xf * r
                dn = d * mult
                acc[...] += _rows8(dn * xh)
                dxh = dn * g_ref[:, cols]
                dp_ref[:, pl.ds(off + h * HEAD_DIM, HEAD_DIM)] = (
                    r * (dxh - xh * jnp.mean(dxh * xh, axis=-1, keepdims=True))).astype(BF16)
                if off:
                    dp_ref[:, pl.ds(2 * e + h * HEAD_DIM, HEAD_DIM)] = dvt_ref[cols, :].T.astype(BF16)
        dp_ref[:, pl.ds(3 * e, e)] = dz_ref[...]

        tri = (lax.broadcasted_iota(jnp.int32, (tt, tt), 0) <= lax.broadcasted_iota(jnp.int32, (tt, tt), 1)).astype(BF16)
        dlf[...] = _tri_matmul(tri, dc_ref[...]) + carry[...]
        carry[...] = dlf[pl.ds(0, 1), :]
        a = fl_ref[...] + fb_ref[...]
        dfl = dlf[...] * _sigmoid(-a)
        dp_ref[:, pl.ds(4 * e, LANES)] = dfl.astype(BF16)
        acc_f[...] += _rows8(dfl)

        @pl.when(i == nt - 1)
        def _():
            dqg_ref[...] = jnp.sum(acc_q[...], axis=0, keepdims=True)
            dkg_ref[...] = jnp.sum(acc_k[...], axis=0, keepdims=True)
            dfb_ref[...] = jnp.sum(acc_f[...], axis=0, keepdims=True)

    rev = lambda i: (nt - 1 - i, 0)
    vec_e = pl.BlockSpec((1, e), lambda i: (0, 0))
    vec = pl.BlockSpec((1, LANES), lambda i: (0, 0))
    wide = pl.BlockSpec((tt, e), rev)
    tall = pl.BlockSpec((e, tt), lambda i: (0, nt - 1 - i))
    lane = pl.BlockSpec((tt, LANES), rev)
    return pl.pallas_call(
        body, name=name, grid=(nt,),
        in_specs=[wide, tall, tall, wide, wide, pl.BlockSpec((tt, e), lambda i: (nt - 1 - i, 1)), vec_e, vec_e, lane, lane, vec],
        out_specs=[pl.BlockSpec((tt, e4 + LANES), rev), vec, vec, vec],
        out_shape=[SDS((t, e4 + LANES), BF16), SDS((1, LANES), F32), SDS((1, LANES), F32), SDS((1, LANES), F32)],
        scratch_shapes=[pltpu.VMEM((1, LANES), F32), pltpu.VMEM((tt, LANES), F32), pltpu.VMEM((SUBLANES, LANES), F32),
                        pltpu.VMEM((SUBLANES, LANES), F32), pltpu.VMEM((SUBLANES, LANES), F32)],
        compiler_params=_cparams("arbitrary"),
    )(dqs, dknt, dvt, dz, proj, proj, qg, kg, dct, flog, fbias)


def _b_fwd(h, b_norm, wb_pad, wb_out, qg, kg, fbias, heads, tag, ride=None):
    t = h.shape[0]
    e = wb_out.shape[0]
    blk = min(ATT_BLOCK, t)
    hn, proj, *rode = _norm_matmul(h, b_norm, wb_pad, f"b_in_proj_{tag}", ride=ride, n_cols=4 * e)
    flog = _matmul_f32out(hn, wb_pad[:, 4 * e:], f"b_forget_proj_{tag}")
    qs, kn, knt, _, ct = _b_prep_fwd(proj, flog, fbias, qg, kg, heads, f"b_prep_fwd_{tag}")
    ck = ct.reshape(LANES, t // blk, 1, blk)
    o, y, mstat, lstat = _flash_fwd(qs, kn, proj, ck, heads, f"b_attention_fwd_{tag}")
    sv = dict(x=h, hn=hn, proj=proj, flog=flog, qs=qs, kn=kn, knt=knt, ck=ck, o=o, y=y, mstat=mstat, lstat=lstat)
    return y, sv, rode


def _b_bwd(dh, sv, b_norm, wb_pad, wb_out, qg, kg, fbias, heads, tag, ride=None):
    t = dh.shape[0]
    e = wb_out.shape[0]
    gb = {}
    dyz = _matmul_nt(dh, wb_out, f"b_out_bwd_{tag}")
    gb["w_out"] = _matmul_tn(sv["y"], dh, f"b_out_wgrad_{tag}", tm=2048)
    dqs, dknt, dvt, dc, dz = _flash_bwd(sv["qs"], sv["kn"], sv["knt"], sv["proj"], sv["ck"], dyz, sv["o"], sv["mstat"],
                                        sv["lstat"], heads, f"b_attention_bwd_{tag}")
    dct = jnp.pad(dc.reshape(heads, t).T, ((0, 0), (0, LANES - heads)))
    dproj, dqg, dkg, dfb = _b_prep_bwd(dqs, dknt, dvt, dz, sv["proj"], qg, kg, dct, sv["flog"], fbias, heads,
                                       f"b_prep_bwd_{tag}")
    gb["w_in"] = _matmul_tn(sv["hn"], dproj, f"b_in_wgrad_{tag}")[:, :4 * e + heads]
    dh, dg, *rode = _dproj_matmul_normbwd(dproj, wb_pad, sv["x"], b_norm, dh, f"b_in_bwd_{tag}", ride=ride)
    gb["norm"], gb["q_norm"], gb["k_norm"], gb["f_bias"] = dg, dqg, dkg, dfb[:, :heads]
    return dh, gb, rode


def _sum_adamw(recv, w, m, v, name, tr=256):
    nl, r, c = w.shape
    tr = min(tr, r)

    def body(g_ref, w_ref, m_ref, v_ref, go_ref, d_ref, mo_ref, vo_ref):
        g = g_ref[0].astype(F32)
        for s in range(1, N_DEV):
            g = g + g_ref[s].astype(F32)
        go_ref[...] = g
        mn = ADAM_B1 * m_ref[...] + (1.0 - ADAM_B1) * g
        vn = ADAM_B2 * v_ref[...] + (1.0 - ADAM_B2) * (g * g)
        m_hat = mn / (1.0 - ADAM_B1 ** ADAM_STEP)
        v_hat = vn / (1.0 - ADAM_B2 ** ADAM_STEP)
        d_ref[...] = -ADAM_LR * (m_hat / (jnp.sqrt(v_hat) + ADAM_EPS) + ADAM_WD * w_ref[...])
        mo_ref[...] = mn
        vo_ref[...] = vn

    blk = pl.BlockSpec((None, tr, c), lambda l, i: (l, i, 0))
    return pl.pallas_call(
        body, name=name, grid=(nl, r // tr),
        in_specs=[pl.BlockSpec((N_DEV, None, tr, c), lambda l, i: (0, l, i, 0)), blk, blk, blk],
        out_specs=[blk, blk, blk, blk],
        out_shape=[SDS(w.shape, F32)] * 4,
        compiler_params=_cparams("parallel", "parallel"),
    )(recv, w, m, v)


def _unshard(g, axis):
    g = jnp.moveaxis(g, 0, axis)
    return g.reshape(g.shape[:axis] + (g.shape[axis] * g.shape[axis + 1],) + g.shape[axis + 2:])


def _to_slabs(full, axis):
    n = full.shape[axis]
    s = full.reshape(full.shape[:axis] + (N_DEV, n // N_DEV) + full.shape[axis + 1:])
    return jnp.moveaxis(s, axis, 0)


def _pack_rows(parts, lead):
    flat = [p.reshape(p.shape[:lead] + (-1,)) for p in parts]
    cat = jnp.concatenate(flat, axis=-1)
    n = cat.shape[-1]
    pad = (-n) % (SUBLANES * LANES)
    cat = jnp.pad(cat, [(0, 0)] * lead + [(0, pad)])
    return cat.reshape(cat.shape[:lead] + ((n + pad) // LANES, LANES))


def _unpack_rows(packed, shapes, lead):
    flat = packed.reshape(packed.shape[:lead] + (-1,))
    out, off = [], 0
    for shp in shapes:
        size = int(np.prod(shp))
        out.append(flat[..., off:off + size].reshape(packed.shape[:lead] + tuple(shp)))
        off += size
    return out


def _pad_rows(w, rows):
    return jnp.pad(w, ((0, rows - w.shape[0]), (0, 0)))


def kernel(x, a_norm, a_w_in, a_conv_w, a_conv_b, a_ln_g, a_ln_b, a_w_out, b_norm, b_w_in, b_f_bias, b_q_norm, b_k_norm, b_w_out, c_norm, c_w_in, c_conv_w, c_w_out, loss_target, m_a_norm, m_a_w_in, m_a_conv_w, m_a_conv_b, m_a_ln_g, m_a_ln_b, m_a_w_out, m_b_norm, m_b_w_in, m_b_f_bias, m_b_q_norm, m_b_k_norm, m_b_w_out, m_c_norm, m_c_w_in, m_c_conv_w, m_c_w_out, v_a_norm, v_a_w_in, v_a_conv_w, v_a_conv_b, v_a_ln_g, v_a_ln_b, v_a_w_out, v_b_norm, v_b_w_in, v_b_f_bias, v_b_q_norm, v_b_k_norm, v_b_w_out, v_c_norm, v_c_w_in, v_c_conv_w, v_c_w_out):
    t, d = x.shape[1], x.shape[2]
    e = a_w_out.shape[1] * N_DEV
    heads = b_f_bias.shape[1]
    n_a, n_b, n_c = a_norm.shape[0], b_norm.shape[0], c_norm.shape[0]
    depth = n_a + n_b + n_c
    ka, kc = a_conv_w.shape[1], c_conv_w.shape[1]
    assert e == heads * HEAD_DIM and n_b == 1 and n_c == 1 and x.shape[0] == 1

    layers = [(i % 3, i // 3) for i in range(depth)]

    def mat_shards(kind, j):
        w_in, w_out = ((a_w_in, a_w_out), (b_w_in, b_w_out), (c_w_in, c_w_out))[kind]
        return [w_in[j].astype(BF16), w_out[j].astype(BF16)]

    def full_mats(kind, gathered):
        w_in, w_out = _unshard(gathered[0], 1), _unshard(gathered[1], 0)
        return (jnp.pad(w_in, ((0, 0), (0, LANES - heads))) if kind == 1 else w_in), w_out

    small_names = ["a_norm", "a_conv_w", "a_conv_b", "a_ln_g", "a_ln_b", "c_norm", "c_conv_w"]
    small = dict(a_norm=a_norm, a_conv_w=a_conv_w, a_conv_b=a_conv_b, a_ln_g=a_ln_g, a_ln_b=a_ln_b,
                 c_norm=c_norm, c_conv_w=c_conv_w)
    small_pack = _pack_rows([small[k] for k in small_names], 0)
    first = _all_gather(mat_shards(*layers[0]) + [small_pack], "all_gather_first_layer")
    sm = _unpack_rows(first[2], [small[k].shape for k in small_names], 1)
    g_a_norm = _unshard(sm[0], 1)
    g_a_conv_w = _unshard(sm[1], 2)
    g_a_conv_b = _unshard(sm[2], 1)
    g_a_ln_g = _unshard(sm[3], 1)
    g_a_ln_b = _unshard(sm[4], 1)
    g_c_norm = _unshard(sm[5], 1)
    g_c_conv_w = _unshard(sm[6], 2)

    cw_a = [_pad_rows(g_a_conv_w[j], CONF_HALO) for j in range(n_a)]
    cw_c = _pad_rows(g_c_conv_w[0], SHORT_HALO)
    qg = jnp.tile(b_q_norm, (1, heads))
    kg = jnp.tile(b_k_norm, (1, heads))
    fbias = jnp.pad(b_f_bias, ((0, 0), (0, LANES - heads)))

    h = x[0]
    saved, weights = [], [full_mats(layers[0][0], first[:2])]
    for i, (kind, j) in enumerate(layers):
        tag = f"l{i}"
        w_in, w_out = weights[i]
        ride = _Ride("gather", mat_shards(*layers[i + 1])) if i + 1 < depth else None
        if kind == 0:
            hn, proj = _norm_matmul(h, g_a_norm[j:j + 1], w_in, f"a_in_proj_{tag}")
            y, u1, *rode = _a_mid_fwd(proj, cw_a[j], g_a_conv_b[j:j + 1], g_a_ln_g[j:j + 1], g_a_ln_b[j:j + 1], ka,
                                      f"a_mid_fwd_{tag}", ride=ride)
            saved.append(dict(x=h, hn=hn, proj=proj, u1=u1, y=y))
        elif kind == 1:
            y, sv, rode = _b_fwd(h, b_norm, w_in, w_out, qg, kg, fbias, heads, tag, ride=ride)
            saved.append(sv)
        else:
            hn, proj, *rode = _norm_matmul(h, g_c_norm, w_in, f"c_in_proj_{tag}", ride=ride)
            y = _c_mid_fwd(proj, cw_c, kc, f"c_mid_fwd_{tag}")
            saved.append(dict(x=h, hn=hn, proj=proj, y=y))
        if ride:
            weights.append(full_mats(layers[i + 1][0], rode))
            h = _out_matmul_residual(y, w_out, h, f"{'abc'[kind]}_out_proj_{tag}")
        else:
            loss_part, dh = _out_matmul_loss(y, w_out, h, loss_target[0], f"{'abc'[kind]}_out_proj_loss_{tag}")
    loss_local = jnp.sum(loss_part).reshape(1, 1)

    ga = dict(norm=[None] * n_a, conv_w=[None] * n_a, conv_b=[None] * n_a, ln_g=[None] * n_a, ln_b=[None] * n_a)
    gb, gc = None, {}
    recv_mats = [None] * depth
    pending = None
    for i in reversed(range(depth)):
        kind, j = layers[i]
        tag = f"l{i}"
        sv = saved[i]
        w_in, w_out = weights[i]
        ride = _Ride("exchange", pending) if pending is not None else None
        if kind == 0:
            dyz = _matmul_nt(dh, w_out, f"a_out_bwd_{tag}")
            gw_out = _matmul_tn(sv["y"], dh, f"a_out_wgrad_{tag}", tm=2048)
            mid_ride, ride = (ride, None) if i == 0 else (None, ride)
            dproj, dcw, dcb, dlg, dlb, *mid_rode = _a_mid_bwd(sv["proj"], sv["u1"], dyz, cw_a[j], g_a_ln_g[j:j + 1],
                                                              g_a_ln_b[j:j + 1], ka, f"a_mid_bwd_{tag}", ride=mid_ride)
            gw_in = _matmul_tn(sv["hn"], dproj, f"a_in_wgrad_{tag}")
            if i == 0:
                ride = _Ride("exchange", [_to_slabs(gw_in, 1), _to_slabs(gw_out, 0)])
            dh, dg, *rode = _dproj_matmul_normbwd(dproj, w_in, sv["x"], g_a_norm[j:j + 1], dh, f"a_in_bwd_{tag}", ride=ride)
            if i == 0:
                if mid_ride:
                    recv_mats[1] = mid_rode
                recv_mats[0], ride = rode, None
            ga["norm"][j], ga["conv_w"][j], ga["conv_b"][j], ga["ln_g"][j], ga["ln_b"][j] = dg[0], dcw[:ka], dcb[0], dlg[0], dlb[0]
        elif kind == 1:
            dh, gb, rode = _b_bwd(dh, sv, b_norm, w_in, w_out, qg, kg, fbias, heads, tag, ride=ride)
            gw_in, gw_out = gb["w_in"], gb["w_out"]
        else:
            dyz = _matmul_nt(dh, w_out, f"c_out_bwd_{tag}")
            gw_out = _matmul_tn(sv["y"], dh, f"c_out_wgrad_{tag}", tm=2048)
            dproj, dcw = _c_mid_bwd(sv["proj"], dyz, cw_c, kc, f"c_mid_bwd_{tag}")
            gw_in = _matmul_tn(sv["hn"], dproj, f"c_in_wgrad_{tag}")
            dh, dg, *rode = _dproj_matmul_normbwd(dproj, w_in, sv["x"], g_c_norm, dh, f"c_in_bwd_{tag}", ride=ride)
            gc["norm"], gc["conv_w"] = dg, dcw[:kc][None]
        if ride:
            recv_mats[i + 1] = rode
        pending = [_to_slabs(gw_in, 1), _to_slabs(gw_out, 0)]
    grad_x = dh[None]

    sharded_small = [(jnp.stack(ga["norm"]), 1), (jnp.stack(ga["conv_w"]), 2), (jnp.stack(ga["conv_b"]), 1),
                     (jnp.stack(ga["ln_g"]), 1), (jnp.stack(ga["ln_b"]), 1), (gc["norm"], 1), (gc["conv_w"], 2)]
    repl_small = [gb["norm"], gb["f_bias"], gb["q_norm"], gb["k_norm"], loss_local]
    small_slabs = _pack_rows([_to_slabs(g, ax) for g, ax in sharded_small]
                             + [jnp.broadcast_to(g[None], (N_DEV,) + g.shape) for g in repl_small], 1)
    recv_small, = _exchange([small_slabs], "exchange_vector_gradients")

    outs = {}
    mat_w = dict(a_w_in=(a_w_in, m_a_w_in, v_a_w_in), a_w_out=(a_w_out, m_a_w_out, v_a_w_out),
                 b_w_in=(b_w_in, m_b_w_in, v_b_w_in), b_w_out=(b_w_out, m_b_w_out, v_b_w_out),
                 c_w_in=(c_w_in, m_c_w_in, v_c_w_in), c_w_out=(c_w_out, m_c_w_out, v_c_w_out))
    for kind, prefix in enumerate("abc"):
        members = [i for i, (k, _) in enumerate(layers) if k == kind]
        for which, name in enumerate((f"{prefix}_w_in", f"{prefix}_w_out")):
            recv = jnp.stack([recv_mats[i][which] for i in members], axis=1)
            outs[name] = _sum_adamw(recv, *mat_w[name], f"adamw_{name}")

    small_order = small_names + ["b_norm", "b_f_bias", "b_q_norm", "b_k_norm", "loss"]
    no_state = jnp.zeros((1, 1), F32)
    small_w = dict(a_norm=(a_norm, m_a_norm, v_a_norm), a_conv_w=(a_conv_w, m_a_conv_w, v_a_conv_w),
                   a_conv_b=(a_conv_b, m_a_conv_b, v_a_conv_b), a_ln_g=(a_ln_g, m_a_ln_g, v_a_ln_g),
                   a_ln_b=(a_ln_b, m_a_ln_b, v_a_ln_b), c_norm=(c_norm, m_c_norm, v_c_norm),
                   c_conv_w=(c_conv_w, m_c_conv_w, v_c_conv_w), b_norm=(b_norm, m_b_norm, v_b_norm),
                   b_f_bias=(b_f_bias, m_b_f_bias, v_b_f_bias), b_q_norm=(b_q_norm, m_b_q_norm, v_b_q_norm),
                   b_k_norm=(b_k_norm, m_b_k_norm, v_b_k_norm), loss=(no_state, no_state, no_state))
    packs = [_pack_rows([small_w[k][q] for k in small_order], 0)[None] for q in range(3)]
    small_out = _sum_adamw(recv_small[:, None], *packs, "adamw_vectors")
    shapes = [small_w[k][0].shape for k in small_order]
    unpacked = [_unpack_rows(o[0], shapes, 0) for o in small_out]
    for idx, name in enumerate(small_order):
        outs[name] = tuple(unpacked[q][idx] for q in range(4))

    loss = outs["loss"][0].reshape(())
    order = ["a_norm", "a_w_in", "a_conv_w", "a_conv_b", "a_ln_g", "a_ln_b", "a_w_out", "b_norm", "b_w_in", "b_f_bias",
             "b_q_norm", "b_k_norm", "b_w_out", "c_norm", "c_w_in", "c_conv_w", "c_w_out"]
    return (loss, grad_x, *[outs[k][0] for k in order], *[outs[k][1] for k in order],
            *[outs[k][2] for k in order], *[outs[k][3] for k in order])
```
